```python
import jax, jax.numpy as jnp
from jax import lax
import numpy as np

D_MODEL = 1024
BATCH = 16
SEQ = 2048
DEPTH = 2

CHUNK = 64
N_MIXERS = 2
N_GLA_LAYERS = (DEPTH + 1) // 2
N_LRU_LAYERS = DEPTH // 2
EPS = 1e-6

GLA_HEADS = 4
GLA_DK = D_MODEL // 2 // GLA_HEADS
GLA_DV = D_MODEL // GLA_HEADS
GLA_QK = GLA_HEADS * GLA_DK
GLA_VD = GLA_HEADS * GLA_DV
GLA_GATE_RANK = 16
GLA_GATE_TAU = 16.0
GLA_IN = 2 * GLA_QK + 2 * GLA_VD + GLA_GATE_RANK

LRU_WIDTH = D_MODEL
LRU_BLOCKS = 4
LRU_BLOCK_W = LRU_WIDTH // LRU_BLOCKS
CONV_W = 4
LRU_C = 8.0

N_EXPERTS = 16
N_GROUPS = 4
EXPERTS_PER_GROUP = N_EXPERTS // N_GROUPS
TOP_K = 2
D_EXPERT = 512

kernel_name = "hybrid_gla_rglru_grouped_moe_adaln"


def rmsnorm(x, g):
    x32 = x.astype(jnp.float32)
    inv = lax.rsqrt(jnp.mean(x32 * x32, axis=-1, keepdims=True) + EPS)
    return (x32 * inv).astype(x.dtype) * g


def modulate(h, shift, scale):
    return h * (1 + scale[:, None]) + shift[:, None]


def gla_mixer(h, w_in, w_gate_up, b_gate, norm_g, w_out):
    B, S, _ = h.shape
    N = S // CHUNK
    proj = h @ w_in
    q, k, v, g, a_lr = jnp.split(
        proj, [GLA_QK, 2 * GLA_QK, 2 * GLA_QK + GLA_VD, 2 * GLA_QK + 2 * GLA_VD], axis=-1)
    log_a = jax.nn.log_sigmoid((a_lr @ w_gate_up + b_gate).astype(jnp.float32)) / GLA_GATE_TAU

    def chunked(t, d):
        return t.reshape(B, N, CHUNK, GLA_HEADS, d)

    q = chunked(q, GLA_DK) * (GLA_DK ** -0.5)
    k = chunked(k, GLA_DK)
    v = chunked(v, GLA_DV)
    cum = jnp.cumsum(chunked(log_a, GLA_DK), axis=2)
    total = cum[:, :, -1]
    k_dec = k * jnp.exp(total[:, :, None] - cum).astype(k.dtype)
    gamma = jnp.exp(total).astype(h.dtype)

    def step(s_prev, xs):
        qn, kn, vn, gn = xs
        inter = jnp.einsum('bchk,bhkv->bchv', qn * gn[:, None], s_prev)
        scores = jnp.einsum('bchk,bshk->bhcs', qn, kn)
        intra = jnp.einsum('bhcs,bshv->bchv', scores, vn)
        s_new = gn[..., None] * s_prev + jnp.einsum('bshk,bshv->bhkv', kn, vn)
        return s_new, inter + intra

    s0 = jnp.zeros((B, GLA_HEADS, GLA_DK, GLA_DV), h.dtype)
    xs = (jnp.moveaxis(q, 1, 0), jnp.moveaxis(k_dec, 1, 0),
          jnp.moveaxis(v, 1, 0), jnp.moveaxis(gamma, 1, 0))
    _, o = lax.scan(step, s0, xs)
    o = jnp.moveaxis(o, 0, 1).reshape(B, S, GLA_HEADS, GLA_DV)
    o = rmsnorm(o, norm_g) * jax.nn.silu(g).reshape(B, S, GLA_HEADS, GLA_DV)
    return o.reshape(B, S, GLA_VD) @ w_out


def lru_mixer(h, w_in, conv_w, conv_b, w_r, b_r, w_i, b_i, lam, w_out):
    B, S, _ = h.shape
    proj = h @ w_in
    gate_br, xb = jnp.split(proj, 2, axis=-1)
    xp = jnp.pad(xb, ((0, 0), (CONV_W - 1, 0), (0, 0)))
    xc = conv_b + sum(xp[:, j:j + S] * conv_w[j] for j in range(CONV_W))
    xg = xc.reshape(B, S, LRU_BLOCKS, LRU_BLOCK_W)
    r = jax.nn.sigmoid(jnp.einsum('bshi,hij->bshj', xg, w_r) + b_r).reshape(B, S, LRU_WIDTH)
    i = jax.nn.sigmoid(jnp.einsum('bshi,hij->bshj', xg, w_i) + b_i).reshape(B, S, LRU_WIDTH)
    log_a = (-LRU_C * r.astype(jnp.float32)) * jax.nn.softplus(-lam.astype(jnp.float32))
    a = jnp.exp(log_a)
    mult = jnp.sqrt(-jnp.expm1(2.0 * log_a))
    b = (xc * i).astype(jnp.float32) * mult

    def combine(left, right):
        a1, b1 = left
        a2, b2 = right
        return a1 * a2, a2 * b1 + b2

    _, hs = lax.associative_scan(combine, (a, b), axis=1)
    y = hs.astype(h.dtype) * jax.nn.gelu(gate_br)
    return y @ w_out


def grouped_moe(h, router_w, router_bias, w_gate, w_up, w_down):
    B, S, D = h.shape
    t = h.reshape(-1, D)
    scores = jax.nn.sigmoid((t @ router_w).astype(jnp.float32))
    sel = scores + router_bias.astype(jnp.float32)
    grouped = sel.reshape(-1, N_GROUPS, EXPERTS_PER_GROUP)
    group_score = lax.top_k(grouped, TOP_K)[0].sum(-1)
    grp = jnp.argmax(group_score, axis=-1)
    in_group = jnp.take_along_axis(grouped, grp[:, None, None], axis=1)[:, 0]
    _, local = lax.top_k(in_group, TOP_K)
    expert_idx = grp[:, None] * EXPERTS_PER_GROUP + local
    w = jnp.take_along_axis(scores, expert_idx, axis=1)
    w = w / jnp.sum(w, axis=-1, keepdims=True)
    combine = jnp.sum(jax.nn.one_hot(expert_idx, N_EXPERTS, dtype=jnp.float32) * w[..., None], axis=1)
    combine = combine.astype(t.dtype)
    out = jnp.zeros_like(t)
    for e in range(N_EXPERTS):
        he = jax.nn.silu(t @ w_gate[e]) * (t @ w_up[e])
        out = out + combine[:, e:e + 1] * (he @ w_down[e])
    return out.reshape(B, S, D)


def setup_inputs(seed: int = 0) -> dict:
    key = jax.random.key(seed)
    ks = iter(jax.random.split(key, 40))
    f32 = jnp.float32

    def nrm(shape, scale):
        return jax.random.normal(next(ks), shape, f32) * scale

    def gain(shape):
        return 1.0 + nrm(shape, 0.05)

    D = D_MODEL
    p = {}
    p["x"] = nrm((BATCH, SEQ, D), 1.0)
    p["c"] = nrm((BATCH, D), 1.0)
    p["gla_w_in"] = nrm((N_GLA_LAYERS, D, GLA_IN), D ** -0.5)
    p["gla_w_gate_up"] = nrm((N_GLA_LAYERS, GLA_GATE_RANK, GLA_QK), GLA_GATE_RANK ** -0.5)
    p["gla_b_gate"] = nrm((N_GLA_LAYERS, GLA_QK), 0.1)
    p["gla_norm_g"] = gain((N_GLA_LAYERS, GLA_DV))
    p["gla_w_out"] = nrm((N_GLA_LAYERS, GLA_VD, D), GLA_VD ** -0.5)
    p["lru_w_in"] = nrm((N_LRU_LAYERS, D, 2 * LRU_WIDTH), D ** -0.5)
    p["lru_conv_w"] = nrm((N_LRU_LAYERS, CONV_W, LRU_WIDTH), CONV_W ** -0.5)
    p["lru_conv_b"] = nrm((N_LRU_LAYERS, LRU_WIDTH), 0.02)
    p["lru_w_r"] = nrm((N_LRU_LAYERS, LRU_BLOCKS, LRU_BLOCK_W, LRU_BLOCK_W), LRU_BLOCK_W ** -0.5)
    p["lru_b_r"] = nrm((N_LRU_LAYERS, LRU_BLOCKS, LRU_BLOCK_W), 0.02)
    p["lru_w_i"] = nrm((N_LRU_LAYERS, LRU_BLOCKS, LRU_BLOCK_W, LRU_BLOCK_W), LRU_BLOCK_W ** -0.5)
    p["lru_b_i"] = nrm((N_LRU_LAYERS, LRU_BLOCKS, LRU_BLOCK_W), 0.02)
    u = jax.random.uniform(next(ks), (N_LRU_LAYERS, LRU_WIDTH), f32, 0.9, 0.999)
    a_base = u ** (1.0 / LRU_C)
    p["lru_lambda"] = jnp.log(a_base) - jnp.log1p(-a_base)
    p["lru_w_out"] = nrm((N_LRU_LAYERS, LRU_WIDTH, D), LRU_WIDTH ** -0.5)
    p["router_w"] = nrm((D, N_EXPERTS), D ** -0.5)
    p["router_bias"] = nrm((N_EXPERTS,), 0.01)
    p["moe_w_gate"] = nrm((DEPTH, N_EXPERTS, D, D_EXPERT), D ** -0.5)
    p["moe_w_up"] = nrm((DEPTH, N_EXPERTS, D, D_EXPERT), D ** -0.5)
    p["moe_w_down"] = nrm((DEPTH, N_EXPERTS, D_EXPERT, D), D_EXPERT ** -0.5)
    p["norm_mix_g"] = gain((DEPTH, D))
    p["norm_ffn_g"] = gain((DEPTH, D))
    p["ada_w"] = nrm((DEPTH, D, 6 * D), 0.5 * D ** -0.5)
    p["ada_b"] = nrm((DEPTH, 6 * D), 0.02)
    p["final_norm_g"] = gain((D,))
    return p


def reference(x, c, gla_w_in, gla_w_gate_up, gla_b_gate, gla_norm_g, gla_w_out,
              lru_w_in, lru_conv_w, lru_conv_b, lru_w_r, lru_b_r, lru_w_i, lru_b_i,
              lru_lambda, lru_w_out, router_w, router_bias, moe_w_gate, moe_w_up,
              moe_w_down, norm_mix_g, norm_ffn_g, ada_w, ada_b, final_norm_g):
    cond = jax.nn.silu(c)
    for i in range(DEPTH):
        sh1, sc1, g1, sh2, sc2, g2 = jnp.split(cond @ ada_w[i] + ada_b[i], 6, axis=-1)
        h = modulate(rmsnorm(x, norm_mix_g[i]), sh1, sc1)
        j = i // N_MIXERS
        if i % N_MIXERS == 0:
            mix = gla_mixer(h, gla_w_in[j], gla_w_gate_up[j], gla_b_gate[j],
                            gla_norm_g[j], gla_w_out[j])
        else:
            mix = lru_mixer(h, lru_w_in[j], lru_conv_w[j], lru_conv_b[j], lru_w_r[j],
                            lru_b_r[j], lru_w_i[j], lru_b_i[j], lru_lambda[j], lru_w_out[j])
        x = x + g1[:, None] * mix
        h = modulate(rmsnorm(x, norm_ffn_g[i]), sh2, sc2)
        x = x + g2[:, None] * grouped_moe(h, router_w, router_bias,
                                          moe_w_gate[i], moe_w_up[i], moe_w_down[i])
    return rmsnorm(x, final_norm_g)
```

```python
import functools

import jax
import jax.numpy as jnp
from jax import lax
from jax.experimental import pallas as pl
from jax.experimental.pallas import tpu as pltpu

D_MODEL = 1024
CHUNK = 64
EPS = 1e-6

GLA_HEADS = 4
GLA_DK = 128
GLA_DV = 256
GLA_QK = GLA_HEADS * GLA_DK
GLA_VD = GLA_HEADS * GLA_DV
GLA_GATE_RANK = 16
GLA_GATE_TAU = 16.0
GLA_RANK_PAD = 128
GLA_IN_PAD = 2 * GLA_QK + 2 * GLA_VD + GLA_RANK_PAD

LRU_BLOCKS = 4
LRU_BLOCK_W = D_MODEL // LRU_BLOCKS
CONV_W = 4
LRU_C = 8.0

N_EXPERTS = 16
N_GROUPS = 4
EPG = N_EXPERTS // N_GROUPS
D_EXPERT = 512

LANES = 128
SUBLANES = 8
ROWS_PER_TOKEN = D_MODEL // LANES

MIX_TS = 256
MOE_CHUNK = 512
MOE_TM = 256
VMEM_LIMIT = 56 * 1024 * 1024

_HI = lax.Precision.HIGHEST


def _dot(a, b):
    return jnp.dot(a, b, preferred_element_type=jnp.float32)


def _dot_nt(a, b):
    return lax.dot_general(a, b, (((1,), (1,)), ((), ())),
                           preferred_element_type=jnp.float32)


def _dot_tn(a, b):
    return lax.dot_general(a, b, (((0,), (0,)), ((), ())),
                           preferred_element_type=jnp.float32)


def _sigmoid(x):
    return 1.0 / (1.0 + jnp.exp(-x))


def _silu(x):
    return x * _sigmoid(x)


def _rms(x, g):
    inv = lax.rsqrt(jnp.mean(x * x, axis=-1, keepdims=True) + EPS)
    return x * inv * g


def _ada_body(c_ref, w_ref, b_ref, o_ref):
    cond = _silu(c_ref[...])
    o_ref[0] = jnp.dot(cond, w_ref[0], preferred_element_type=jnp.float32,
                       precision=_HI) + b_ref[0]


def _ada_call(c, ada_w, ada_b):
    depth, d, n = ada_w.shape
    bsz = c.shape[0]
    tn = 1024
    return pl.pallas_call(
        _ada_body,
        grid=(depth, n // tn),
        in_specs=[
            pl.BlockSpec((bsz, d), lambda l, j: (0, 0)),
            pl.BlockSpec((1, d, tn), lambda l, j: (l, 0, j)),
            pl.BlockSpec((1, 1, tn), lambda l, j: (l, 0, j)),
        ],
        out_specs=pl.BlockSpec((1, bsz, tn), lambda l, j: (l, 0, j)),
        out_shape=jax.ShapeDtypeStruct((depth, bsz, n), jnp.float32),
        compiler_params=pltpu.CompilerParams(
            dimension_semantics=("arbitrary", "arbitrary"),
            vmem_limit_bytes=VMEM_LIMIT),
        name="ada",
    )(c, ada_w, ada_b.reshape(depth, 1, n))


def _pre_norm(x, mod_ref, g_ref, shift_row, scale_row):
    shift = mod_ref[0, shift_row:shift_row + 1, :]
    scale = mod_ref[0, scale_row:scale_row + 1, :]
    return _rms(x, g_ref[...]) * (1.0 + scale) + shift


def _group_partner(x, k, sub):
    n = x.shape[0]
    fwd = pltpu.roll(x, n - k, 0)
    back = pltpu.roll(x, EPG - k, 0)
    wrapped = (sub % EPG) + k >= EPG
    return jnp.where(wrapped, back, fwd), wrapped


def _route_and_emit(h2, rw_cat_ref, rw_hi_ref, rbias_ref, cnt_scr,
                    ri_ref, rwt_ref, cnt_ref, first_tile):
    ts = h2.shape[0]
    h_hi = h2.astype(jnp.bfloat16)
    h_lo = (h2 - h_hi.astype(jnp.float32)).astype(jnp.bfloat16)
    p1 = _dot_nt(rw_cat_ref[...], h_hi)
    p2 = _dot_nt(rw_hi_ref[...], h_lo)
    logits = p1[0:N_EXPERTS] + p1[N_EXPERTS:2 * N_EXPERTS] + p2
    s = _sigmoid(logits)
    sel = s + rbias_ref[...]
    sub = lax.broadcasted_iota(jnp.int32, (N_EXPERTS, ts), 0)

    pair_best = None
    rank = jnp.zeros((N_EXPERTS, ts), jnp.float32)
    for k in range(1, EPG):
        p, wrapped = _group_partner(sel, k, sub)
        ps = sel + p
        pair_best = ps if pair_best is None else jnp.maximum(pair_best, ps)
        ahead = (p > sel) | ((p == sel) & wrapped)
        rank = rank + ahead.astype(jnp.float32)
    gscore = pair_best
    for k in range(1, EPG):
        p, _ = _group_partner(pair_best, k, sub)
        gscore = jnp.maximum(gscore, p)
    chosen = jnp.ones((N_EXPERTS, ts), jnp.bool_)
    grp = sub // EPG
    for j in range(1, N_GROUPS):
        other = pltpu.roll(gscore, N_EXPERTS - EPG * j, 0)
        other_is_later = grp + j < N_GROUPS
        chosen = chosen & ((gscore > other) | ((gscore == other) & other_is_later))
    m = chosen & (rank < 2.0)
    mf = m.astype(jnp.float32)
    sm = s * mf
    wgt = sm / jnp.sum(sm, axis=0, keepdims=True)

    @pl.when(first_tile)
    def _():
        cnt_scr[...] = jnp.zeros_like(cnt_scr)

    row = lax.broadcasted_iota(jnp.int32, (ts, ts), 0)
    col = lax.broadcasted_iota(jnp.int32, (ts, ts), 1)
    before = (row < col).astype(jnp.bfloat16)
    carry = cnt_scr[...]
    rnk = _dot(mf.astype(jnp.bfloat16), before) + carry[:, 0:1]
    new_carry = carry + jnp.sum(mf, axis=1, keepdims=True)
    cnt_scr[...] = new_carry
    cnt_ref[0] = new_carry.astype(jnp.int32)

    eidx = sub.astype(jnp.float32)
    e_lo = jnp.min(jnp.where(m, eidx, 99.0), axis=0, keepdims=True)
    e_hi = jnp.max(jnp.where(m, eidx, -1.0), axis=0, keepdims=True)
    is_lo = m & (eidx == e_lo)
    is_hi = m & (eidx == e_hi)
    r_lo = jnp.sum(jnp.where(is_lo, rnk, 0.0), axis=0, keepdims=True)
    r_hi = jnp.sum(jnp.where(is_hi, rnk, 0.0), axis=0, keepdims=True)
    w_lo = jnp.sum(jnp.where(is_lo, wgt, 0.0), axis=0, keepdims=True)
    w_hi = jnp.sum(jnp.where(is_hi, wgt, 0.0), axis=0, keepdims=True)
    ri_ref[0, 0:1, :] = e_lo.astype(jnp.int32)
    ri_ref[0, 1:2, :] = r_lo.astype(jnp.int32)
    ri_ref[0, 2:3, :] = e_hi.astype(jnp.int32)
    ri_ref[0, 3:4, :] = r_hi.astype(jnp.int32)
    rwt_ref[0, 0:1, :] = w_lo
    rwt_ref[0, 1:2, :] = w_hi


def _post_mix(x, mix, mod_ref, gffn_ref, rw_cat_ref, rw_hi_ref, rbias_ref, cnt_scr,
              xo_ref, h2_ref, ri_ref, rwt_ref, cnt_ref, first_tile):
    ts = x.shape[0]
    x_new = x + mod_ref[0, 2:3, :] * mix
    xo_ref[0] = x_new
    h2 = _pre_norm(x_new, mod_ref, gffn_ref, 3, 4)
    for c in range(ROWS_PER_TOKEN):
        h2_ref[pl.ds(c, ts, stride=ROWS_PER_TOKEN), :] = h2[:, c * LANES:(c + 1) * LANES]
    _route_and_emit(h2, rw_cat_ref, rw_hi_ref, rbias_ref, cnt_scr,
                    ri_ref, rwt_ref, cnt_ref, first_tile)


def _mixer_out_specs(bsz, seq, ts):
    nt = seq // ts
    specs = [
        pl.BlockSpec((1, ts, D_MODEL), lambda b, t: (b, t, 0)),
        pl.BlockSpec((ts * ROWS_PER_TOKEN, LANES), lambda b, t: (b * nt + t, 0)),
        pl.BlockSpec((1, 4, ts), lambda b, t: (b, 0, t)),
        pl.BlockSpec((1, 2, ts), lambda b, t: (b, 0, t)),
        pl.BlockSpec((1, N_EXPERTS, LANES), lambda b, t: (b, 0, 0)),
    ]
    shapes = [
        jax.ShapeDtypeStruct((bsz, seq, D_MODEL), jnp.float32),
        jax.ShapeDtypeStruct((bsz * seq * ROWS_PER_TOKEN, LANES), jnp.float32),
        jax.ShapeDtypeStruct((bsz, 4, seq), jnp.int32),
        jax.ShapeDtypeStruct((bsz, 2, seq), jnp.float32),
        jax.ShapeDtypeStruct((bsz, N_EXPERTS, LANES), jnp.int32),
    ]
    return specs, shapes


def _const_spec(shape):
    nd = len(shape)
    return pl.BlockSpec(shape, lambda b, t: (0,) * nd)


def _gla_body(x_ref, mod_ref, gmix_ref, gffn_ref, win_ref, wgu_ref, bg_ref, ng_ref,
              wout_ref, rw_cat_ref, rw_hi_ref, rbias_ref,
              xo_ref, h2_ref, ri_ref, rwt_ref, cnt_ref,
              state_scr, o_scr, cnt_scr):
    t = pl.program_id(1)
    first_tile = t == 0
    ts = x_ref.shape[1]
    nchunk = ts // CHUNK

    @pl.when(first_tile)
    def _():
        state_scr[...] = jnp.zeros_like(state_scr)

    x = x_ref[0]
    h = _pre_norm(x, mod_ref, gmix_ref, 0, 1).astype(jnp.bfloat16)

    o_q, o_k, o_v, o_g, o_a = 0, GLA_QK, 2 * GLA_QK, 2 * GLA_QK + GLA_VD, 2 * GLA_QK + 2 * GLA_VD
    q = _dot(h, win_ref[:, o_q:o_k]) * (GLA_DK ** -0.5)
    k = _dot(h, win_ref[:, o_k:o_v])
    v = _dot(h, win_ref[:, o_v:o_g]).astype(jnp.bfloat16)
    a_lr = _dot(h, win_ref[:, o_a:o_a + GLA_RANK_PAD])

    z = jnp.dot(a_lr, wgu_ref[...], preferred_element_type=jnp.float32,
                precision=_HI) + bg_ref[...]
    log_a = -(jnp.maximum(-z, 0.0) + jnp.log1p(jnp.exp(-jnp.abs(z)))) * (1.0 / GLA_GATE_TAU)

    row = lax.broadcasted_iota(jnp.int32, (ts, ts), 0)
    col = lax.broadcasted_iota(jnp.int32, (ts, ts), 1)
    tri = ((row // CHUNK == col // CHUNK) & (col <= row)).astype(jnp.float32)
    cum = jnp.dot(tri, log_a, preferred_element_type=jnp.float32, precision=_HI)

    q = q.astype(jnp.bfloat16)
    ng = ng_ref[...]
    for j in range(nchunk):
        r0 = j * CHUNK
        cum_j = cum[r0:r0 + CHUNK]
        total = cum_j[CHUNK - 1:CHUNK]
        k_dec = (k[r0:r0 + CHUNK] * jnp.exp(total - cum_j)).astype(jnp.bfloat16)
        gamma = jnp.exp(total)
        for hd in range(GLA_HEADS):
            ks = slice(hd * GLA_DK, (hd + 1) * GLA_DK)
            vs = slice(hd * GLA_DV, (hd + 1) * GLA_DV)
            kv_t = _dot_tn(v[r0:r0 + CHUNK, vs], k_dec[:, ks])
            s_new = state_scr[hd] * gamma[:, ks] + kv_t
            state_scr[hd] = s_new
            o = _dot_nt(q[r0:r0 + CHUNK, ks], s_new.astype(jnp.bfloat16))
            o_scr[r0:r0 + CHUNK, vs] = _rms(o, ng)

    g = _dot(h, win_ref[:, o_g:o_a])
    og = (o_scr[...] * _silu(g)).astype(jnp.bfloat16)
    mix = _dot(og, wout_ref[...])
    _post_mix(x, mix, mod_ref, gffn_ref, rw_cat_ref, rw_hi_ref, rbias_ref, cnt_scr,
              xo_ref, h2_ref, ri_ref, rwt_ref, cnt_ref, first_tile)


def _gla_layer_call(x, mod, gmix, gffn, w_in, w_gu, b_g, n_g, w_out, rw_cat, rw_hi, rbias):
    bsz, seq, d = x.shape
    ts = MIX_TS
    out_specs, out_shapes = _mixer_out_specs(bsz, seq, ts)
    return pl.pallas_call(
        _gla_body,
        grid=(bsz, seq // ts),
        in_specs=[
            pl.BlockSpec((1, ts, d), lambda b, t: (b, t, 0)),
            pl.BlockSpec((1, 6, d), lambda b, t: (b, 0, 0)),
            _const_spec((1, d)), _const_spec((1, d)),
            _const_spec(w_in.shape), _const_spec(w_gu.shape), _const_spec(b_g.shape),
            _const_spec(n_g.shape), _const_spec(w_out.shape),
            _const_spec(rw_cat.shape), _const_spec(rw_hi.shape), _const_spec(rbias.shape),
        ],
        out_specs=out_specs,
        out_shape=out_shapes,
        scratch_shapes=[
            pltpu.VMEM((GLA_HEADS, GLA_DV, GLA_DK), jnp.float32),
            pltpu.VMEM((ts, GLA_VD), jnp.float32),
            pltpu.VMEM((N_EXPERTS, LANES), jnp.float32),
        ],
        compiler_params=pltpu.CompilerParams(
            dimension_semantics=("arbitrary", "arbitrary"),
            vmem_limit_bytes=VMEM_LIMIT),
        name="gla_layer",
    )(x, mod, gmix, gffn, w_in, w_gu, b_g, n_g, w_out, rw_cat, rw_hi, rbias)


def _lru_body(x_ref, mod_ref, gmix_ref, gffn_ref, win_ref, cw_ref, cb_ref, wr_ref, br_ref,
              wi_ref, bi_ref, lam_ref, wout_ref, rw_cat_ref, rw_hi_ref, rbias_ref,
              xo_ref, h2_ref, ri_ref, rwt_ref, cnt_ref,
              conv_scr, hstate_scr, a_scr, b_scr, cnt_scr):
    t = pl.program_id(1)
    first_tile = t == 0
    ts = x_ref.shape[1]
    w = D_MODEL

    @pl.when(first_tile)
    def _():
        conv_scr[0:SUBLANES, :] = jnp.zeros((SUBLANES, w), jnp.float32)
        hstate_scr[...] = jnp.zeros_like(hstate_scr)

    x = x_ref[0]
    h = _pre_norm(x, mod_ref, gmix_ref, 0, 1).astype(jnp.bfloat16)
    xb = _dot(h, win_ref[:, w:2 * w])

    conv_scr[SUBLANES:SUBLANES + ts, :] = xb
    xc = cb_ref[...] + xb * cw_ref[CONV_W - 1:CONV_W, :]
    for j in range(CONV_W - 1):
        back = CONV_W - 1 - j
        xc = xc + conv_scr[SUBLANES - back:SUBLANES - back + ts, :] * cw_ref[j:j + 1, :]
    conv_scr[0:SUBLANES, :] = conv_scr[ts:ts + SUBLANES, :]

    xcb = xc.astype(jnp.bfloat16)
    rs, iis = [], []
    for hd in range(LRU_BLOCKS):
        sl = slice(hd * LRU_BLOCK_W, (hd + 1) * LRU_BLOCK_W)
        rs.append(_dot(xcb[:, sl], wr_ref[hd]))
        iis.append(_dot(xcb[:, sl], wi_ref[hd]))
    r = _sigmoid(jnp.concatenate(rs, axis=1) + br_ref[...])
    ig = _sigmoid(jnp.concatenate(iis, axis=1) + bi_ref[...])

    lam = lam_ref[...]
    softplus_neg_lam = jnp.maximum(-lam, 0.0) + jnp.log1p(jnp.exp(-jnp.abs(lam)))
    log_a = (-LRU_C * r) * softplus_neg_lam
    a = jnp.exp(log_a)
    mult = jnp.sqrt(-jnp.tanh(log_a) * (a * a + 1.0))
    bb = (xc * ig) * mult

    rowi = lax.broadcasted_iota(jnp.int32, (ts, w), 0) % SUBLANES
    for d in (1, 2, 4):
        keep = rowi >= d
        a_sh = jnp.where(keep, pltpu.roll(a, d, 0), 1.0)
        b_sh = jnp.where(keep, pltpu.roll(bb, d, 0), 0.0)
        bb = a * b_sh + bb
        a = a * a_sh
    a_scr[...] = a
    b_scr[...] = bb

    def group_step(gi, hprev):
        r0 = pl.multiple_of(gi * SUBLANES, SUBLANES)
        hs = a_scr[pl.ds(r0, SUBLANES), :] * hprev + b_scr[pl.ds(r0, SUBLANES), :]
        b_scr[pl.ds(r0, SUBLANES), :] = hs
        return hs[SUBLANES - 1:SUBLANES, :]

    hlast = lax.fori_loop(0, ts // SUBLANES, group_step, hstate_scr[...])
    hstate_scr[...] = hlast

    gate_br = _dot(h, win_ref[:, 0:w])
    gelu = 0.5 * gate_br * (1.0 + jnp.tanh(0.7978845608028654 *
                                           (gate_br + 0.044715 * gate_br * gate_br * gate_br)))
    y = (b_scr[...] * gelu).astype(jnp.bfloat16)
    mix = _dot(y, wout_ref[...])
    _post_mix(x, mix, mod_ref, gffn_ref, rw_cat_ref, rw_hi_ref, rbias_ref, cnt_scr,
              xo_ref, h2_ref, ri_ref, rwt_ref, cnt_ref, first_tile)


def _lru_layer_call(x, mod, gmix, gffn, w_in, conv_w, conv_b, w_r, b_r, w_i, b_i, lam, w_out,
                    rw_cat, rw_hi, rbias):
    bsz, seq, d = x.shape
    ts = MIX_TS
    out_specs, out_shapes = _mixer_out_specs(bsz, seq, ts)
    consts = (w_in, conv_w, conv_b, w_r, b_r, w_i, b_i, lam, w_out, rw_cat, rw_hi, rbias)
    return pl.pallas_call(
        _lru_body,
        grid=(bsz, seq // ts),
        in_specs=[
            pl.BlockSpec((1, ts, d), lambda b, t: (b, t, 0)),
            pl.BlockSpec((1, 6, d), lambda b, t: (b, 0, 0)),
            _const_spec((1, d)), _const_spec((1, d)),
        ] + [_const_spec(a.shape) for a in consts],
        out_specs=out_specs,
        out_shape=out_shapes,
        scratch_shapes=[
            pltpu.VMEM((SUBLANES + ts, d), jnp.float32),
            pltpu.VMEM((1, d), jnp.float32),
            pltpu.VMEM((ts, d), jnp.float32),
            pltpu.VMEM((ts, d), jnp.float32),
            pltpu.VMEM((N_EXPERTS, LANES), jnp.float32),
        ],
        compiler_params=pltpu.CompilerParams(
            dimension_semantics=("arbitrary", "arbitrary"),
            vmem_limit_bytes=VMEM_LIMIT),
        name="lru_layer",
    )(x, mod, gmix, gffn, *consts)


def _pos_body(ri_ref, cnt_ref, pos_ref, off_ref):
    cnt = cnt_ref[0].astype(jnp.float32)
    seq = ri_ref.shape[2]
    e_lo = ri_ref[0, 0:1, :].astype(jnp.float32)
    e_hi = ri_ref[0, 2:3, :].astype(jnp.float32)
    p_lo = ri_ref[0, 1:2, :].astype(jnp.float32)
    p_hi = ri_ref[0, 3:4, :].astype(jnp.float32)
    off = jnp.zeros((1, LANES), jnp.float32)
    for e in range(N_EXPERTS):
        off_ref[0, e:e + 1, :] = off.astype(jnp.int32)
        off_b = jnp.broadcast_to(off[:, 0:1], (1, seq))
        p_lo = p_lo + jnp.where(e_lo == float(e), off_b, 0.0)
        p_hi = p_hi + jnp.where(e_hi == float(e), off_b, 0.0)
        off = off + cnt[e:e + 1, :]
    pos_ref[0, 0:1, :] = p_lo.astype(jnp.int32)
    pos_ref[0, 1:2, :] = p_hi.astype(jnp.int32)


def _pos_call(ri, cnt):
    bsz, _, seq = ri.shape
    return pl.pallas_call(
        _pos_body,
        grid=(bsz,),
        in_specs=[
            pl.BlockSpec((1, 4, seq), lambda b: (b, 0, 0)),
            pl.BlockSpec((1, N_EXPERTS, LANES), lambda b: (b, 0, 0)),
        ],
        out_specs=[
            pl.BlockSpec((1, 2, seq), lambda b: (b, 0, 0)),
            pl.BlockSpec((1, N_EXPERTS, LANES), lambda b: (b, 0, 0)),
        ],
        out_shape=[
            jax.ShapeDtypeStruct((bsz, 2, seq), jnp.int32),
            jax.ShapeDtypeStruct((bsz, N_EXPERTS, LANES), jnp.int32),
        ],
        compiler_params=pltpu.CompilerParams(dimension_semantics=("arbitrary",)),
        name="moe_pos",
    )(ri, cnt)


def _moe_body(off_sm, cnt_sm, pos_ref, wts_ref, h2_ref, x_ref, mod_ref, fin_ref,
              wg_ref, wu_ref, wd_ref, o_ref, slots_scr, tm_scr, *, n_chunks, final_norm):
    b = pl.program_id(0)
    s = pl.program_id(1)
    chunk = h2_ref.shape[0] // ROWS_PER_TOKEN
    tm = MOE_TM
    n_slots = 2 * n_chunks * chunk
    R = ROWS_PER_TOKEN

    @pl.when((b == 0) & (s == 0))
    def _():
        slots_scr[n_slots * R:(n_slots + tm) * R, :] = jnp.zeros((tm * R, LANES), jnp.float32)

    @pl.when(s < n_chunks)
    def _scatter():
        base = s * chunk

        def body(i, carry):
            tok = base + i
            p0 = pos_ref[0, 0, tok]
            p1 = pos_ref[0, 1, tok]
            val = h2_ref[pl.ds(pl.multiple_of(i * R, R), R), :]
            slots_scr[pl.ds(pl.multiple_of(p0 * R, R), R), :] = val
            slots_scr[pl.ds(pl.multiple_of(p1 * R, R), R), :] = val
            return carry

        lax.fori_loop(0, chunk, body, 0)

    @pl.when((s >= n_chunks) & (s < n_chunks + N_EXPERTS))
    def _experts():
        e = s - n_chunks
        off = off_sm[b, e]
        n = cnt_sm[b, e]
        n_tiles = (n + tm - 1) // tm

        def tile_body(j, carry):
            slot0 = off + j * tm
            row0 = pl.multiple_of(slot0 * R, R)
            n_valid = n - j * tm
            xs = [slots_scr[pl.ds(row0 + c, tm, stride=R), :] for c in range(R)]
            xt = jnp.concatenate(xs, axis=1).astype(jnp.bfloat16)
            gate = _dot(xt, wg_ref[0])
            up = _dot(xt, wu_ref[0])
            he = (_silu(gate) * up).astype(jnp.bfloat16)
            y = _dot(he, wd_ref[0])

            @pl.when(n_valid >= tm)
            def _():
                for c in range(R):
                    slots_scr[pl.ds(row0 + c, tm, stride=R), :] = y[:, c * LANES:(c + 1) * LANES]

            @pl.when(n_valid < tm)
            def _():
                ok = lax.broadcasted_iota(jnp.int32, (tm, LANES), 0) < n_valid
                for c in range(R):
                    old = slots_scr[pl.ds(row0 + c, tm, stride=R), :]
                    slots_scr[pl.ds(row0 + c, tm, stride=R), :] = jnp.where(
                        ok, y[:, c * LANES:(c + 1) * LANES], old)

            return carry

        lax.fori_loop(0, n_tiles, tile_body, 0)

    @pl.when(s >= n_chunks + N_EXPERTS)
    def _combine():
        base = (s - n_chunks - N_EXPERTS) * chunk

        def body(i, carry):
            tok = base + i
            p0 = pos_ref[0, 0, tok]
            p1 = pos_ref[0, 1, tok]
            w0 = wts_ref[0, 0, tok]
            w1 = wts_ref[0, 1, tok]
            z = (w0 * slots_scr[pl.ds(pl.multiple_of(p0 * R, R), R), :]
                 + w1 * slots_scr[pl.ds(pl.multiple_of(p1 * R, R), R), :])
            tm_scr[pl.ds(pl.multiple_of(i * R, R), R), :] = z
            return carry

        lax.fori_loop(0, chunk, body, 0)
        moe = jnp.concatenate(
            [tm_scr[pl.ds(c, chunk, stride=R), :] for c in range(R)], axis=1)
        out = x_ref[...] + mod_ref[0, 5:6, :] * moe
        if final_norm:
            out = _rms(out, fin_ref[...])
        o_ref[...] = out


def _moe_call(off, cnt, pos, wts, h2_tm, x, mod, fin_g, w_gate, w_up, w_down, final_norm):
    bsz, seq, d = x.shape
    chunk = MOE_CHUNK
    n_chunks = seq // chunk
    n_steps = 2 * n_chunks + N_EXPERTS
    n_slots = 2 * seq

    def chunk_in(b, s, *_):
        return (b * n_chunks + jnp.clip(s, 0, n_chunks - 1), 0)

    def chunk_out(b, s, *_):
        return (b * n_chunks + jnp.clip(s - n_chunks - N_EXPERTS, 0, n_chunks - 1), 0)

    def expert_idx(b, s, *_):
        return (jnp.clip(s - n_chunks, 0, N_EXPERTS - 1), 0, 0)

    body = functools.partial(_moe_body, n_chunks=n_chunks, final_norm=final_norm)
    out = pl.pallas_call(
        body,
        grid_spec=pltpu.PrefetchScalarGridSpec(
            num_scalar_prefetch=2,
            grid=(bsz, n_steps),
            in_specs=[
                pl.BlockSpec((1, 2, seq), lambda b, s, *_: (b, 0, 0), memory_space=pltpu.SMEM),
                pl.BlockSpec((1, 2, seq), lambda b, s, *_: (b, 0, 0), memory_space=pltpu.SMEM),
                pl.BlockSpec((chunk * ROWS_PER_TOKEN, LANES), chunk_in),
                pl.BlockSpec((chunk, d), chunk_out),
                pl.BlockSpec((1, 6, d), lambda b, s, *_: (b, 0, 0)),
                pl.BlockSpec((1, d), lambda b, s, *_: (0, 0)),
                pl.BlockSpec((1, d, D_EXPERT), expert_idx),
                pl.BlockSpec((1, d, D_EXPERT), expert_idx),
                pl.BlockSpec((1, D_EXPERT, d), expert_idx),
            ],
            out_specs=pl.BlockSpec((chunk, d), chunk_out),
            scratch_shapes=[
                pltpu.VMEM(((n_slots + MOE_TM) * ROWS_PER_TOKEN, LANES), jnp.float32),
                pltpu.VMEM((chunk * ROWS_PER_TOKEN, LANES), jnp.float32),
            ],
        ),
        out_shape=jax.ShapeDtypeStruct((bsz * seq, d), jnp.float32),
        compiler_params=pltpu.CompilerParams(
            dimension_semantics=("arbitrary", "arbitrary"),
            vmem_limit_bytes=VMEM_LIMIT),
        name="moe",
    )(off, cnt, pos, wts, h2_tm, x.reshape(bsz * seq, d), mod, fin_g, w_gate, w_up, w_down)
    return out.reshape(bsz, seq, d)


def _moe_layer(x, h2_tm, ri, rwt, cnt, mod, fin_g, w_gate, w_up, w_down, final_norm):
    pos, off = _pos_call(ri, cnt)
    return _moe_call(off[:, :, 0], cnt[:, :, 0], pos, rwt, h2_tm, x, mod, fin_g,
                     w_gate, w_up, w_down, final_norm)


def kernel(x, c, gla_w_in, gla_w_gate_up, gla_b_gate, gla_norm_g, gla_w_out, lru_w_in, lru_conv_w, lru_conv_b, lru_w_r, lru_b_r, lru_w_i, lru_b_i, lru_lambda, lru_w_out, router_w, router_bias, moe_w_gate, moe_w_up, moe_w_down, norm_mix_g, norm_ffn_g, ada_w, ada_b, final_norm_g):
    bf = jnp.bfloat16
    depth = ada_w.shape[0]
    bsz = x.shape[0]
    d = D_MODEL
    mod_all = _ada_call(c, ada_w, ada_b).reshape(depth, bsz, 6, d)

    rw_t = router_w.T
    rw_hi = rw_t.astype(bf)
    rw_lo = (rw_t - rw_hi.astype(jnp.float32)).astype(bf)
    rw_cat = jnp.concatenate([rw_hi, rw_lo], axis=0)
    rbias = router_bias.reshape(N_EXPERTS, 1)
    fin_g = final_norm_g.reshape(1, d)

    for i in range(depth):
        j = i // 2
        mod = mod_all[i]
        gmix = norm_mix_g[i].reshape(1, d)
        gffn = norm_ffn_g[i].reshape(1, d)
        if i % 2 == 0:
            w_in = jnp.pad(gla_w_in[j], ((0, 0), (0, GLA_RANK_PAD - GLA_GATE_RANK))).astype(bf)
            w_gu = jnp.pad(gla_w_gate_up[j], ((0, GLA_RANK_PAD - GLA_GATE_RANK), (0, 0)))
            x, h2_tm, ri, rwt, cnt = _gla_layer_call(
                x, mod, gmix, gffn, w_in, w_gu, gla_b_gate[j].reshape(1, GLA_QK),
                gla_norm_g[j].reshape(1, GLA_DV), gla_w_out[j].astype(bf),
                rw_cat, rw_hi, rbias)
        else:
            x, h2_tm, ri, rwt, cnt = _lru_layer_call(
                x, mod, gmix, gffn, lru_w_in[j].astype(bf), lru_conv_w[j],
                lru_conv_b[j].reshape(1, d), lru_w_r[j].astype(bf),
                lru_b_r[j].reshape(1, d), lru_w_i[j].astype(bf), lru_b_i[j].reshape(1, d),
                lru_lambda[j].reshape(1, d), lru_w_out[j].astype(bf), rw_cat, rw_hi, rbias)
        x = _moe_layer(x, h2_tm, ri, rwt, cnt, mod, fin_g, moe_w_gate[i].astype(bf),
                       moe_w_up[i].astype(bf), moe_w_down[i].astype(bf),
                       final_norm=(i == depth - 1))
    return x
```

```python
import functools

import jax
import jax.numpy as jnp
from jax import lax
from jax.experimental import pallas as pl
from jax.experimental.pallas import tpu as pltpu

D_MODEL = 1024
CHUNK = 64
EPS = 1e-6

GLA_HEADS = 4
GLA_DK = 128
GLA_DV = 256
GLA_QK = GLA_HEADS * GLA_DK
GLA_VD = GLA_HEADS * GLA_DV
GLA_GATE_RANK = 16
GLA_GATE_TAU = 16.0
GLA_RANK_PAD = 128
GLA_IN_PAD = 2 * GLA_QK + 2 * GLA_VD + GLA_RANK_PAD

LRU_BLOCKS = 4
LRU_BLOCK_W = D_MODEL // LRU_BLOCKS
CONV_W = 4
LRU_C = 8.0

N_EXPERTS = 16
N_GROUPS = 4
EPG = N_EXPERTS // N_GROUPS
D_EXPERT = 512

LANES = 128
SUBLANES = 8
ROWS_PER_TOKEN = D_MODEL // LANES

MIX_TS = 256
MOE_CHUNK = 512
MOE_TM = 256
MOE_UNROLL = 8
VMEM_LIMIT = 56 * 1024 * 1024

_HI = lax.Precision.HIGHEST


def _dot(a, b):
    return jnp.dot(a, b, preferred_element_type=jnp.float32)


def _dot_nt(a, b):
    return lax.dot_general(a, b, (((1,), (1,)), ((), ())),
                           preferred_element_type=jnp.float32)


def _dot_tn(a, b):
    return lax.dot_general(a, b, (((0,), (0,)), ((), ())),
                           preferred_element_type=jnp.float32)


def _split_bf16(x):
    hi = x.astype(jnp.bfloat16)
    lo = (x - hi.astype(jnp.float32)).astype(jnp.bfloat16)
    return hi, lo


def _sigmoid(x):
    return 1.0 / (1.0 + jnp.exp(-x))


def _silu(x):
    return x * _sigmoid(x)


def _rms(x, g):
    inv = lax.rsqrt(jnp.mean(x * x, axis=-1, keepdims=True) + EPS)
    return x * inv * g


def _ada_body(c_ref, w_ref, b_ref, o_ref):
    cond = _silu(c_ref[...])
    o_ref[0] = jnp.dot(cond, w_ref[0], preferred_element_type=jnp.float32,
                       precision=_HI) + b_ref[0]


def _ada_call(c, ada_w, ada_b):
    depth, d, n = ada_w.shape
    bsz = c.shape[0]
    tn = 1024
    return pl.pallas_call(
        _ada_body,
        grid=(depth, n // tn),
        in_specs=[
            pl.BlockSpec((bsz, d), lambda l, j: (0, 0)),
            pl.BlockSpec((1, d, tn), lambda l, j: (l, 0, j)),
            pl.BlockSpec((1, 1, tn), lambda l, j: (l, 0, j)),
        ],
        out_specs=pl.BlockSpec((1, bsz, tn), lambda l, j: (l, 0, j)),
        out_shape=jax.ShapeDtypeStruct((depth, bsz, n), jnp.float32),
        compiler_params=pltpu.CompilerParams(
            dimension_semantics=("arbitrary", "arbitrary"),
            vmem_limit_bytes=VMEM_LIMIT),
        name="ada",
    )(c, ada_w, ada_b.reshape(depth, 1, n))


def _pre_norm(x, mod_ref, g_ref, shift_row, scale_row):
    shift = mod_ref[0, shift_row:shift_row + 1, :]
    scale = mod_ref[0, scale_row:scale_row + 1, :]
    return _rms(x, g_ref[...]) * (1.0 + scale) + shift


def _group_partner(x, k, sub):
    n = x.shape[0]
    fwd = pltpu.roll(x, n - k, 0)
    back = pltpu.roll(x, EPG - k, 0)
    wrapped = (sub % EPG) + k >= EPG
    return jnp.where(wrapped, back, fwd), wrapped


def _route_and_emit(h2, rw_cat_ref, rw_hi_ref, rbias_ref, cnt_scr,
                    ri_ref, rwt_ref, cnt_ref, first_tile):
    ts = h2.shape[0]
    h_hi, h_lo = _split_bf16(h2)
    p1 = _dot_nt(rw_cat_ref[...], h_hi)
    p2 = _dot_nt(rw_hi_ref[...], h_lo)
    logits = p1[0:N_EXPERTS] + p1[N_EXPERTS:2 * N_EXPERTS] + p2
    s = _sigmoid(logits)
    sel = s + rbias_ref[...]
    sub = lax.broadcasted_iota(jnp.int32, (N_EXPERTS, ts), 0)

    pair_best = None
    rank = jnp.zeros((N_EXPERTS, ts), jnp.float32)
    for k in range(1, EPG):
        p, wrapped = _group_partner(sel, k, sub)
        ps = sel + p
        pair_best = ps if pair_best is None else jnp.maximum(pair_best, ps)
        ahead = (p > sel) | ((p == sel) & wrapped)
        rank = rank + ahead.astype(jnp.float32)
    gscore = pair_best
    for k in range(1, EPG):
        p, _ = _group_partner(pair_best, k, sub)
        gscore = jnp.maximum(gscore, p)
    chosen = jnp.ones((N_EXPERTS, ts), jnp.bool_)
    grp = sub // EPG
    for j in range(1, N_GROUPS):
        other = pltpu.roll(gscore, N_EXPERTS - EPG * j, 0)
        other_is_later = grp + j < N_GROUPS
        chosen = chosen & ((gscore > other) | ((gscore == other) & other_is_later))
    m = chosen & (rank < 2.0)
    mf = m.astype(jnp.float32)
    sm = s * mf
    wgt = sm / jnp.sum(sm, axis=0, keepdims=True)

    @pl.when(first_tile)
    def _():
        cnt_scr[...] = jnp.zeros_like(cnt_scr)

    row = lax.broadcasted_iota(jnp.int32, (ts, ts), 0)
    col = lax.broadcasted_iota(jnp.int32, (ts, ts), 1)
    before = (row < col).astype(jnp.bfloat16)
    carry = cnt_scr[...]
    rnk = _dot(mf.astype(jnp.bfloat16), before) + carry[:, 0:1]
    new_carry = carry + jnp.sum(mf, axis=1, keepdims=True)
    cnt_scr[...] = new_carry
    cnt_ref[0] = new_carry.astype(jnp.int32)

    eidx = sub.astype(jnp.float32)
    e_lo = jnp.min(jnp.where(m, eidx, 99.0), axis=0, keepdims=True)
    e_hi = jnp.max(jnp.where(m, eidx, -1.0), axis=0, keepdims=True)
    is_lo = m & (eidx == e_lo)
    is_hi = m & (eidx == e_hi)
    r_lo = jnp.sum(jnp.where(is_lo, rnk, 0.0), axis=0, keepdims=True)
    r_hi = jnp.sum(jnp.where(is_hi, rnk, 0.0), axis=0, keepdims=True)
    w_lo = jnp.sum(jnp.where(is_lo, wgt, 0.0), axis=0, keepdims=True)
    w_hi = jnp.sum(jnp.where(is_hi, wgt, 0.0), axis=0, keepdims=True)
    ri_ref[0, 0:1, :] = e_lo.astype(jnp.int32)
    ri_ref[0, 1:2, :] = r_lo.astype(jnp.int32)
    ri_ref[0, 2:3, :] = e_hi.astype(jnp.int32)
    ri_ref[0, 3:4, :] = r_hi.astype(jnp.int32)
    rwt_ref[0, 0:1, :] = w_lo
    rwt_ref[0, 1:2, :] = w_hi


def _post_mix(x, mix, mod_ref, gffn_ref, rw_cat_ref, rw_hi_ref, rbias_ref, cnt_scr,
              xo_ref, h2_ref, ri_ref, rwt_ref, cnt_ref, first_tile):
    ts = x.shape[0]
    x_new = x + mod_ref[0, 2:3, :] * mix
    xo_ref[0] = x_new
    h2 = _pre_norm(x_new, mod_ref, gffn_ref, 3, 4)
    for c in range(ROWS_PER_TOKEN):
        h2_ref[pl.ds(c, ts, stride=ROWS_PER_TOKEN), :] = h2[:, c * LANES:(c + 1) * LANES]
    _route_and_emit(h2, rw_cat_ref, rw_hi_ref, rbias_ref, cnt_scr,
                    ri_ref, rwt_ref, cnt_ref, first_tile)


def _mixer_out_specs(bsz, seq, ts):
    nt = seq // ts
    specs = [
        pl.BlockSpec((1, ts, D_MODEL), lambda b, t: (b, t, 0)),
        pl.BlockSpec((ts * ROWS_PER_TOKEN, LANES), lambda b, t: (b * nt + t, 0)),
        pl.BlockSpec((1, 4, ts), lambda b, t: (b, 0, t)),
        pl.BlockSpec((1, 2, ts), lambda b, t: (b, 0, t)),
        pl.BlockSpec((1, N_EXPERTS, LANES), lambda b, t: (b, 0, 0)),
    ]
    shapes = [
        jax.ShapeDtypeStruct((bsz, seq, D_MODEL), jnp.float32),
        jax.ShapeDtypeStruct((bsz * seq * ROWS_PER_TOKEN, LANES), jnp.float32),
        jax.ShapeDtypeStruct((bsz, 4, seq), jnp.int32),
        jax.ShapeDtypeStruct((bsz, 2, seq), jnp.float32),
        jax.ShapeDtypeStruct((bsz, N_EXPERTS, LANES), jnp.int32),
    ]
    return specs, shapes


def _const_spec(shape):
    nd = len(shape)
    return pl.BlockSpec(shape, lambda b, t: (0,) * nd)


def _gla_body(x_ref, mod_ref, gmix_ref, gffn_ref, win_ref, wgu_ref, bg_ref, ng_ref,
              wout_ref, rw_cat_ref, rw_hi_ref, rbias_ref,
              xo_ref, h2_ref, ri_ref, rwt_ref, cnt_ref,
              state_scr, o_scr, cnt_scr):
    t = pl.program_id(1)
    first_tile = t == 0
    ts = x_ref.shape[1]
    nchunk = ts // CHUNK

    @pl.when(first_tile)
    def _():
        state_scr[...] = jnp.zeros_like(state_scr)

    x = x_ref[0]
    h = _pre_norm(x, mod_ref, gmix_ref, 0, 1).astype(jnp.bfloat16)

    o_q, o_k, o_v, o_g, o_a = 0, GLA_QK, 2 * GLA_QK, 2 * GLA_QK + GLA_VD, 2 * GLA_QK + 2 * GLA_VD
    q = _dot(h, win_ref[:, o_q:o_k]) * (GLA_DK ** -0.5)
    k = _dot(h, win_ref[:, o_k:o_v])
    v = _dot(h, win_ref[:, o_v:o_g]).astype(jnp.bfloat16)
    a_lr = _dot(h, win_ref[:, o_a:o_a + GLA_RANK_PAD])

    a_hi, a_lo = _split_bf16(a_lr)
    z2 = _dot(jnp.concatenate([a_hi, a_lo], axis=0), wgu_ref[...])
    z = z2[0:ts] + z2[ts:2 * ts] + bg_ref[...]
    log_a = -(jnp.maximum(-z, 0.0) + jnp.log1p(jnp.exp(-jnp.abs(z)))) * (1.0 / GLA_GATE_TAU)

    row = lax.broadcasted_iota(jnp.int32, (ts, ts), 0)
    col = lax.broadcasted_iota(jnp.int32, (ts, ts), 1)
    tri = ((row // CHUNK == col // CHUNK) & (col <= row)).astype(jnp.bfloat16)
    l_hi, l_lo = _split_bf16(log_a)
    cum2 = _dot(tri, jnp.concatenate([l_hi, l_lo], axis=1))
    cum = cum2[:, 0:GLA_QK] + cum2[:, GLA_QK:2 * GLA_QK]

    q = q.astype(jnp.bfloat16)
    ng = ng_ref[...]
    for j in range(nchunk):
        r0 = j * CHUNK
        cum_j = cum[r0:r0 + CHUNK]
        total = cum_j[CHUNK - 1:CHUNK]
        k_dec = (k[r0:r0 + CHUNK] * jnp.exp(total - cum_j)).astype(jnp.bfloat16)
        gamma = jnp.exp(total)
        for hd in range(GLA_HEADS):
            ks = slice(hd * GLA_DK, (hd + 1) * GLA_DK)
            vs = slice(hd * GLA_DV, (hd + 1) * GLA_DV)
            kv_t = _dot_tn(v[r0:r0 + CHUNK, vs], k_dec[:, ks])
            s_new = state_scr[hd] * gamma[:, ks] + kv_t
            state_scr[hd] = s_new
            o = _dot_nt(q[r0:r0 + CHUNK, ks], s_new.astype(jnp.bfloat16))
            o_scr[r0:r0 + CHUNK, vs] = _rms(o, ng)

    g = _dot(h, win_ref[:, o_g:o_a])
    og = (o_scr[...] * _silu(g)).astype(jnp.bfloat16)
    mix = _dot(og, wout_ref[...])
    _post_mix(x, mix, mod_ref, gffn_ref, rw_cat_ref, rw_hi_ref, rbias_ref, cnt_scr,
              xo_ref, h2_ref, ri_ref, rwt_ref, cnt_ref, first_tile)


def _gla_layer_call(x, mod, gmix, gffn, w_in, w_gu, b_g, n_g, w_out, rw_cat, rw_hi, rbias):
    bsz, seq, d = x.shape
    ts = MIX_TS
    out_specs, out_shapes = _mixer_out_specs(bsz, seq, ts)
    return pl.pallas_call(
        _gla_body,
        grid=(bsz, seq // ts),
        in_specs=[
            pl.BlockSpec((1, ts, d), lambda b, t: (b, t, 0)),
            pl.BlockSpec((1, 6, d), lambda b, t: (b, 0, 0)),
            _const_spec((1, d)), _const_spec((1, d)),
            _const_spec(w_in.shape), _const_spec(w_gu.shape), _const_spec(b_g.shape),
            _const_spec(n_g.shape), _const_spec(w_out.shape),
            _const_spec(rw_cat.shape), _const_spec(rw_hi.shape), _const_spec(rbias.shape),
        ],
        out_specs=out_specs,
        out_shape=out_shapes,
        scratch_shapes=[
            pltpu.VMEM((GLA_HEADS, GLA_DV, GLA_DK), jnp.float32),
            pltpu.VMEM((ts, GLA_VD), jnp.float32),
            pltpu.VMEM((N_EXPERTS, LANES), jnp.float32),
        ],
        compiler_params=pltpu.CompilerParams(
            dimension_semantics=("arbitrary", "arbitrary"),
            vmem_limit_bytes=VMEM_LIMIT),
        name="gla_layer",
    )(x, mod, gmix, gffn, w_in, w_gu, b_g, n_g, w_out, rw_cat, rw_hi, rbias)


def _lru_body(x_ref, mod_ref, gmix_ref, gffn_ref, win_ref, cw_ref, cb_ref, wr_ref, br_ref,
              wi_ref, bi_ref, lam_ref, wout_ref, rw_cat_ref, rw_hi_ref, rbias_ref,
              xo_ref, h2_ref, ri_ref, rwt_ref, cnt_ref,
              conv_scr, hstate_scr, a_scr, b_scr, cnt_scr):
    t = pl.program_id(1)
    first_tile = t == 0
    ts = x_ref.shape[1]
    w = D_MODEL

    @pl.when(first_tile)
    def _():
        conv_scr[0:SUBLANES, :] = jnp.zeros((SUBLANES, w), jnp.float32)
        hstate_scr[...] = jnp.zeros_like(hstate_scr)

    x = x_ref[0]
    h = _pre_norm(x, mod_ref, gmix_ref, 0, 1).astype(jnp.bfloat16)
    xb = _dot(h, win_ref[:, w:2 * w])

    conv_scr[SUBLANES:SUBLANES + ts, :] = xb
    xc = cb_ref[...] + xb * cw_ref[CONV_W - 1:CONV_W, :]
    for j in range(CONV_W - 1):
        back = CONV_W - 1 - j
        xc = xc + conv_scr[SUBLANES - back:SUBLANES - back + ts, :] * cw_ref[j:j + 1, :]
    conv_scr[0:SUBLANES, :] = conv_scr[ts:ts + SUBLANES, :]

    xcb = xc.astype(jnp.bfloat16)
    rs, iis = [], []
    for hd in range(LRU_BLOCKS):
        sl = slice(hd * LRU_BLOCK_W, (hd + 1) * LRU_BLOCK_W)
        rs.append(_dot(xcb[:, sl], wr_ref[hd]))
        iis.append(_dot(xcb[:, sl], wi_ref[hd]))
    r = _sigmoid(jnp.concatenate(rs, axis=1) + br_ref[...])
    ig = _sigmoid(jnp.concatenate(iis, axis=1) + bi_ref[...])

    lam = lam_ref[...]
    softplus_neg_lam = jnp.maximum(-lam, 0.0) + jnp.log1p(jnp.exp(-jnp.abs(lam)))
    log_a = (-LRU_C * r) * softplus_neg_lam
    a = jnp.exp(log_a)
    mult = jnp.sqrt(-jnp.tanh(log_a) * (a * a + 1.0))
    bb = (xc * ig) * mult

    a = a.reshape(ts // SUBLANES, SUBLANES, w)
    bb = bb.reshape(ts // SUBLANES, SUBLANES, w)
    rowi = lax.broadcasted_iota(jnp.int32, a.shape, 1)
    for d in (1, 2, 4):
        keep = rowi >= d
        a_sh = jnp.where(keep, pltpu.roll(a, d, 1), 1.0)
        b_sh = jnp.where(keep, pltpu.roll(bb, d, 1), 0.0)
        bb = a * b_sh + bb
        a = a * a_sh
    a_scr[...] = a.reshape(ts, w)
    b_scr[...] = bb.reshape(ts, w)

    def group_step(gi, hprev):
        r0 = pl.multiple_of(gi * SUBLANES, SUBLANES)
        hs = a_scr[pl.ds(r0, SUBLANES), :] * hprev + b_scr[pl.ds(r0, SUBLANES), :]
        b_scr[pl.ds(r0, SUBLANES), :] = hs
        return hs[SUBLANES - 1:SUBLANES, :]

    hlast = lax.fori_loop(0, ts // SUBLANES, group_step, hstate_scr[...])
    hstate_scr[...] = hlast

    gate_br = _dot(h, win_ref[:, 0:w])
    gelu = 0.5 * gate_br * (1.0 + jnp.tanh(0.7978845608028654 *
                                           (gate_br + 0.044715 * gate_br * gate_br * gate_br)))
    y = (b_scr[...] * gelu).astype(jnp.bfloat16)
    mix = _dot(y, wout_ref[...])
    _post_mix(x, mix, mod_ref, gffn_ref, rw_cat_ref, rw_hi_ref, rbias_ref, cnt_scr,
              xo_ref, h2_ref, ri_ref, rwt_ref, cnt_ref, first_tile)


def _lru_layer_call(x, mod, gmix, gffn, w_in, conv_w, conv_b, w_r, b_r, w_i, b_i, lam, w_out,
                    rw_cat, rw_hi, rbias):
    bsz, seq, d = x.shape
    ts = MIX_TS
    out_specs, out_shapes = _mixer_out_specs(bsz, seq, ts)
    consts = (w_in, conv_w, conv_b, w_r, b_r, w_i, b_i, lam, w_out, rw_cat, rw_hi, rbias)
    return pl.pallas_call(
        _lru_body,
        grid=(bsz, seq // ts),
        in_specs=[
            pl.BlockSpec((1, ts, d), lambda b, t: (b, t, 0)),
            pl.BlockSpec((1, 6, d), lambda b, t: (b, 0, 0)),
            _const_spec((1, d)), _const_spec((1, d)),
        ] + [_const_spec(a.shape) for a in consts],
        out_specs=out_specs,
        out_shape=out_shapes,
        scratch_shapes=[
            pltpu.VMEM((SUBLANES + ts, d), jnp.float32),
            pltpu.VMEM((1, d), jnp.float32),
            pltpu.VMEM((ts, d), jnp.float32),
            pltpu.VMEM((ts, d), jnp.float32),
            pltpu.VMEM((N_EXPERTS, LANES), jnp.float32),
        ],
        compiler_params=pltpu.CompilerParams(
            dimension_semantics=("arbitrary", "arbitrary"),
            vmem_limit_bytes=VMEM_LIMIT),
        name="lru_layer",
    )(x, mod, gmix, gffn, *consts)


def _pos_body(ri_ref, cnt_ref, pos_ref, off_ref):
    cnt = cnt_ref[0].astype(jnp.float32)
    seq = ri_ref.shape[2]
    e_lo = ri_ref[0, 0:1, :].astype(jnp.float32)
    e_hi = ri_ref[0, 2:3, :].astype(jnp.float32)
    p_lo = ri_ref[0, 1:2, :].astype(jnp.float32)
    p_hi = ri_ref[0, 3:4, :].astype(jnp.float32)
    off = jnp.zeros((1, LANES), jnp.float32)
    for e in range(N_EXPERTS):
        off_ref[0, e:e + 1, :] = off.astype(jnp.int32)
        off_b = jnp.broadcast_to(off[:, 0:1], (1, seq))
        p_lo = p_lo + jnp.where(e_lo == float(e), off_b, 0.0)
        p_hi = p_hi + jnp.where(e_hi == float(e), off_b, 0.0)
        off = off + cnt[e:e + 1, :]
    pos_ref[0, 0:1, :] = p_lo.astype(jnp.int32)
    pos_ref[0, 1:2, :] = p_hi.astype(jnp.int32)


def _pos_call(ri, cnt):
    bsz, _, seq = ri.shape
    return pl.pallas_call(
        _pos_body,
        grid=(bsz,),
        in_specs=[
            pl.BlockSpec((1, 4, seq), lambda b: (b, 0, 0)),
            pl.BlockSpec((1, N_EXPERTS, LANES), lambda b: (b, 0, 0)),
        ],
        out_specs=[
            pl.BlockSpec((1, 2, seq), lambda b: (b, 0, 0)),
            pl.BlockSpec((1, N_EXPERTS, LANES), lambda b: (b, 0, 0)),
        ],
        out_shape=[
            jax.ShapeDtypeStruct((bsz, 2, seq), jnp.int32),
            jax.ShapeDtypeStruct((bsz, N_EXPERTS, LANES), jnp.int32),
        ],
        compiler_params=pltpu.CompilerParams(dimension_semantics=("arbitrary",)),
        name="moe_pos",
    )(ri, cnt)


def _expert_tile(slots_scr, wg_ref, wu_ref, wd_ref, slot0, n_valid, tm, masked):
    R = ROWS_PER_TOKEN
    row0 = pl.multiple_of(slot0 * R, R)
    xs = [slots_scr[pl.ds(row0 + c, tm, stride=R), :] for c in range(R)]
    xt = jnp.concatenate(xs, axis=1).astype(jnp.bfloat16)
    gate = _dot(xt, wg_ref[0])
    up = _dot(xt, wu_ref[0])
    he = (_silu(gate) * up).astype(jnp.bfloat16)
    y = _dot(he, wd_ref[0])
    if masked:
        ok = lax.broadcasted_iota(jnp.int32, (tm, LANES), 0) < n_valid
    for c in range(R):
        yc = y[:, c * LANES:(c + 1) * LANES]
        if masked:
            yc = jnp.where(ok, yc, xs[c])
        slots_scr[pl.ds(row0 + c, tm, stride=R), :] = yc


def _moe_body(off_sm, cnt_sm, pos_ref, wts_ref, h2_ref, x_ref, mod_ref, fin_ref,
              wg_ref, wu_ref, wd_ref, o_ref, slots_scr, tm_scr, *, n_chunks, final_norm):
    b = pl.program_id(0)
    s = pl.program_id(1)
    chunk = h2_ref.shape[0] // ROWS_PER_TOKEN
    tm = MOE_TM
    n_slots = 2 * n_chunks * chunk
    R = ROWS_PER_TOKEN
    U = MOE_UNROLL

    @pl.when((b == 0) & (s == 0))
    def _():
        slots_scr[n_slots * R:(n_slots + tm) * R, :] = jnp.zeros((tm * R, LANES), jnp.float32)

    @pl.when(s < n_chunks)
    def _scatter():
        base = s * chunk

        def body(i, carry):
            for u in range(U):
                loc = i * U + u
                tok2 = 2 * (base + loc)
                p0 = pos_ref[0, 0, tok2]
                p1 = pos_ref[0, 0, tok2 + 1]
                val = h2_ref[pl.ds(pl.multiple_of(loc * R, R), R), :]
                slots_scr[pl.ds(pl.multiple_of(p0 * R, R), R), :] = val
                slots_scr[pl.ds(pl.multiple_of(p1 * R, R), R), :] = val
            return carry

        lax.fori_loop(0, chunk // U, body, 0)

    @pl.when((s >= n_chunks) & (s < n_chunks + N_EXPERTS))
    def _experts():
        e = s - n_chunks
        off = off_sm[b, e]
        n = cnt_sm[b, e]
        n_full = n // tm
        rem = n - n_full * tm
        tile = functools.partial(_expert_tile, slots_scr, wg_ref, wu_ref, wd_ref)

        def tile_body(j, carry):
            tile(off + j * tm, tm, tm, False)
            return carry

        lax.fori_loop(0, n_full, tile_body, 0)
        tail0 = off + n_full * tm

        @pl.when(rem > tm // 2)
        def _():
            tile(tail0, rem, tm, True)

        @pl.when((rem > 0) & (rem <= tm // 2))
        def _():
            tile(tail0, rem, tm // 2, True)

    @pl.when(s >= n_chunks + N_EXPERTS)
    def _combine():
        base = (s - n_chunks - N_EXPERTS) * chunk

        def body(i, carry):
            for u in range(U):
                loc = i * U + u
                tok2 = 2 * (base + loc)
                p0 = pos_ref[0, 0, tok2]
                p1 = pos_ref[0, 0, tok2 + 1]
                w0 = wts_ref[0, 0, tok2]
                w1 = wts_ref[0, 0, tok2 + 1]
                z = (w0 * slots_scr[pl.ds(pl.multiple_of(p0 * R, R), R), :]
                     + w1 * slots_scr[pl.ds(pl.multiple_of(p1 * R, R), R), :])
                tm_scr[pl.ds(pl.multiple_of(loc * R, R), R), :] = z
            return carry

        lax.fori_loop(0, chunk // U, body, 0)
        moe = jnp.concatenate(
            [tm_scr[pl.ds(c, chunk, stride=R), :] for c in range(R)], axis=1)
        out = x_ref[...] + mod_ref[0, 5:6, :] * moe
        if final_norm:
            out = _rms(out, fin_ref[...])
        o_ref[...] = out


def _moe_call(off, cnt, pos, wts, h2_tm, x, mod, fin_g, w_gate, w_up, w_down, final_norm):
    bsz, seq, d = x.shape
    chunk = MOE_CHUNK
    n_chunks = seq // chunk
    n_steps = 2 * n_chunks + N_EXPERTS
    n_slots = 2 * seq

    def chunk_in(b, s, *_):
        return (b * n_chunks + jnp.clip(s, 0, n_chunks - 1), 0)

    def chunk_out(b, s, *_):
        return (b * n_chunks + jnp.clip(s - n_chunks - N_EXPERTS, 0, n_chunks - 1), 0)

    def expert_idx(b, s, *_):
        return (jnp.clip(s - n_chunks, 0, N_EXPERTS - 1), 0, 0)

    body = functools.partial(_moe_body, n_chunks=n_chunks, final_norm=final_norm)
    out = pl.pallas_call(
        body,
        grid_spec=pltpu.PrefetchScalarGridSpec(
            num_scalar_prefetch=2,
            grid=(bsz, n_steps),
            in_specs=[
                pl.BlockSpec((1, 1, 2 * seq), lambda b, s, *_: (b, 0, 0), memory_space=pltpu.SMEM),
                pl.BlockSpec((1, 1, 2 * seq), lambda b, s, *_: (b, 0, 0), memory_space=pltpu.SMEM),
                pl.BlockSpec((chunk * ROWS_PER_TOKEN, LANES), chunk_in),
                pl.BlockSpec((chunk, d), chunk_out),
                pl.BlockSpec((1, 6, d), lambda b, s, *_: (b, 0, 0)),
                pl.BlockSpec((1, d), lambda b, s, *_: (0, 0)),
                pl.BlockSpec((1, d, D_EXPERT), expert_idx),
                pl.BlockSpec((1, d, D_EXPERT), expert_idx),
                pl.BlockSpec((1, D_EXPERT, d), expert_idx),
            ],
            out_specs=pl.BlockSpec((chunk, d), chunk_out),
            scratch_shapes=[
                pltpu.VMEM(((n_slots + MOE_TM) * ROWS_PER_TOKEN, LANES), jnp.float32),
                pltpu.VMEM((chunk * ROWS_PER_TOKEN, LANES), jnp.float32),
            ],
        ),
        out_shape=jax.ShapeDtypeStruct((bsz * seq, d), jnp.float32),
        compiler_params=pltpu.CompilerParams(
            dimension_semantics=("arbitrary", "arbitrary"),
            vmem_limit_bytes=VMEM_LIMIT),
        name="moe",
    )(off, cnt, pos, wts, h2_tm, x.reshape(bsz * seq, d), mod, fin_g, w_gate, w_up, w_down)
    return out.reshape(bsz, seq, d)


def _moe_layer(x, h2_tm, ri, rwt, cnt, mod, fin_g, w_gate, w_up, w_down, final_norm):
    pos, off = _pos_call(ri, cnt)
    bsz, _, seq = pos.shape
    pos = pos.transpose(0, 2, 1).reshape(bsz, 1, 2 * seq)
    rwt = rwt.transpose(0, 2, 1).reshape(bsz, 1, 2 * seq)
    return _moe_call(off[:, :, 0], cnt[:, :, 0], pos, rwt, h2_tm, x, mod, fin_g,
                     w_gate, w_up, w_down, final_norm)


def kernel(x, c, gla_w_in, gla_w_gate_up, gla_b_gate, gla_norm_g, gla_w_out, lru_w_in, lru_conv_w, lru_conv_b, lru_w_r, lru_b_r, lru_w_i, lru_b_i, lru_lambda, lru_w_out, router_w, router_bias, moe_w_gate, moe_w_up, moe_w_down, norm_mix_g, norm_ffn_g, ada_w, ada_b, final_norm_g):
    bf = jnp.bfloat16
    depth = ada_w.shape[0]
    bsz = x.shape[0]
    d = D_MODEL
    mod_all = _ada_call(c, ada_w, ada_b).reshape(depth, bsz, 6, d)

    rw_t = router_w.T
    rw_hi = rw_t.astype(bf)
    rw_lo = (rw_t - rw_hi.astype(jnp.float32)).astype(bf)
    rw_cat = jnp.concatenate([rw_hi, rw_lo], axis=0)
    rbias = router_bias.reshape(N_EXPERTS, 1)
    fin_g = final_norm_g.reshape(1, d)

    for i in range(depth):
        j = i // 2
        mod = mod_all[i]
        gmix = norm_mix_g[i].reshape(1, d)
        gffn = norm_ffn_g[i].reshape(1, d)
        if i % 2 == 0:
            w_in = jnp.pad(gla_w_in[j], ((0, 0), (0, GLA_RANK_PAD - GLA_GATE_RANK))).astype(bf)
            w_gu = jnp.pad(gla_w_gate_up[j], ((0, GLA_RANK_PAD - GLA_GATE_RANK), (0, 0))).astype(bf)
            x, h2_tm, ri, rwt, cnt = _gla_layer_call(
                x, mod, gmix, gffn, w_in, w_gu, gla_b_gate[j].reshape(1, GLA_QK),
                gla_norm_g[j].reshape(1, GLA_DV), gla_w_out[j].astype(bf),
                rw_cat, rw_hi, rbias)
        else:
            x, h2_tm, ri, rwt, cnt = _lru_layer_call(
                x, mod, gmix, gffn, lru_w_in[j].astype(bf), lru_conv_w[j],
                lru_conv_b[j].reshape(1, d), lru_w_r[j].astype(bf),
                lru_b_r[j].reshape(1, d), lru_w_i[j].astype(bf), lru_b_i[j].reshape(1, d),
                lru_lambda[j].reshape(1, d), lru_w_out[j].astype(bf), rw_cat, rw_hi, rbias)
        x = _moe_layer(x, h2_tm, ri, rwt, cnt, mod, fin_g, moe_w_gate[i].astype(bf),
                       moe_w_up[i].astype(bf), moe_w_down[i].astype(bf),
                       final_norm=(i == depth - 1))
    return x
```

```python
import functools

import jax
import jax.numpy as jnp
from jax import lax
from jax.experimental import pallas as pl
from jax.experimental.pallas import tpu as pltpu

D_MODEL = 1024
CHUNK = 64
EPS = 1e-6

GLA_HEADS = 4
GLA_DK = 128
GLA_DV = 256
GLA_QK = GLA_HEADS * GLA_DK
GLA_VD = GLA_HEADS * GLA_DV
GLA_GATE_RANK = 16
GLA_GATE_TAU = 16.0
GLA_RANK_PAD = 128
GLA_IN_PAD = 2 * GLA_QK + 2 * GLA_VD + GLA_RANK_PAD

LRU_BLOCKS = 4
LRU_BLOCK_W = D_MODEL // LRU_BLOCKS
CONV_W = 4
LRU_C = 8.0

N_EXPERTS = 16
N_GROUPS = 4
EPG = N_EXPERTS // N_GROUPS
D_EXPERT = 512

LANES = 128
SUBLANES = 8
ROWS_PER_TOKEN = D_MODEL // LANES

MIX_TS = 256
MIX_STREAMS = 2
MOE_CHUNK = 512
MOE_TM = 288
MOE_UNROLL = 8
VMEM_LIMIT = 56 * 1024 * 1024

def _dot(a, b):
    return jnp.dot(a, b, preferred_element_type=jnp.float32)


def _dot_nt(a, b):
    return lax.dot_general(a, b, (((1,), (1,)), ((), ())),
                           preferred_element_type=jnp.float32)


def _dot_tn(a, b):
    return lax.dot_general(a, b, (((0,), (0,)), ((), ())),
                           preferred_element_type=jnp.float32)


def _split_bf16(x):
    hi = x.astype(jnp.bfloat16)
    lo = (x - hi.astype(jnp.float32)).astype(jnp.bfloat16)
    return hi, lo


def _sigmoid(x):
    return 1.0 / (1.0 + jnp.exp(-x))


def _silu(x):
    return x * _sigmoid(x)


def _rms(x, g):
    inv = lax.rsqrt(jnp.mean(x * x, axis=-1, keepdims=True) + EPS)
    return x * inv * g


def _ada_body(c_ref, w_ref, b_ref, o_ref):
    c_hi, c_lo = _split_bf16(_silu(c_ref[...]))
    w_hi, w_lo = _split_bf16(w_ref[0])
    o_ref[0] = _dot(c_hi, w_hi) + (_dot(c_lo, w_hi) + _dot(c_hi, w_lo)) + b_ref[0]


def _ada_call(c, ada_w, ada_b):
    depth, d, n = ada_w.shape
    bsz = c.shape[0]
    tn = 1024
    return pl.pallas_call(
        _ada_body,
        grid=(depth, n // tn),
        in_specs=[
            pl.BlockSpec((bsz, d), lambda l, j: (0, 0)),
            pl.BlockSpec((1, d, tn), lambda l, j: (l, 0, j)),
            pl.BlockSpec((1, 1, tn), lambda l, j: (l, 0, j)),
        ],
        out_specs=pl.BlockSpec((1, bsz, tn), lambda l, j: (l, 0, j)),
        out_shape=jax.ShapeDtypeStruct((depth, bsz, n), jnp.float32),
        compiler_params=pltpu.CompilerParams(
            dimension_semantics=("arbitrary", "arbitrary"),
            vmem_limit_bytes=VMEM_LIMIT),
        name="ada",
    )(c, ada_w, ada_b.reshape(depth, 1, n))


def _pre_norm(x, mod_ref, g_ref, shift_row, scale_row):
    shift = mod_ref[0, shift_row:shift_row + 1, :]
    scale = mod_ref[0, scale_row:scale_row + 1, :]
    return _rms(x, g_ref[...]) * (1.0 + scale) + shift


def _group_partner(x, k, sub):
    n = x.shape[0]
    fwd = pltpu.roll(x, n - k, 0)
    back = pltpu.roll(x, EPG - k, 0)
    wrapped = (sub % EPG) + k >= EPG
    return jnp.where(wrapped, back, fwd), wrapped


def _route_and_emit(h2, rw_cat_ref, rw_hi_ref, rbias_ref, cnt_scr,
                    ri_ref, rwt_ref, cnt_ref):
    ts = h2.shape[0]
    h_hi, h_lo = _split_bf16(h2)
    p1 = _dot_nt(rw_cat_ref[...], h_hi)
    p2 = _dot_nt(rw_hi_ref[...], h_lo)
    logits = p1[0:N_EXPERTS] + p1[N_EXPERTS:2 * N_EXPERTS] + p2
    s = _sigmoid(logits)
    sel = s + rbias_ref[...]
    sub = lax.broadcasted_iota(jnp.int32, (N_EXPERTS, ts), 0)

    pair_best = None
    rank = jnp.zeros((N_EXPERTS, ts), jnp.float32)
    for k in range(1, EPG):
        p, wrapped = _group_partner(sel, k, sub)
        ps = sel + p
        pair_best = ps if pair_best is None else jnp.maximum(pair_best, ps)
        ahead = (p > sel) | ((p == sel) & wrapped)
        rank = rank + ahead.astype(jnp.float32)
    gscore = pair_best
    for k in range(1, EPG):
        p, _ = _group_partner(pair_best, k, sub)
        gscore = jnp.maximum(gscore, p)
    chosen = jnp.ones((N_EXPERTS, ts), jnp.bool_)
    grp = sub // EPG
    for j in range(1, N_GROUPS):
        other = pltpu.roll(gscore, N_EXPERTS - EPG * j, 0)
        other_is_later = grp + j < N_GROUPS
        chosen = chosen & ((gscore > other) | ((gscore == other) & other_is_later))
    m = chosen & (rank < 2.0)
    mf = m.astype(jnp.float32)
    sm = s * mf
    wgt = sm / jnp.sum(sm, axis=0, keepdims=True)

    row = lax.broadcasted_iota(jnp.int32, (ts, ts), 0)
    col = lax.broadcasted_iota(jnp.int32, (ts, ts), 1)
    before = (row < col).astype(jnp.bfloat16)
    carry = cnt_scr[...]
    rnk = _dot(mf.astype(jnp.bfloat16), before) + carry[:, 0:1]
    new_carry = carry + jnp.sum(mf, axis=1, keepdims=True)
    cnt_scr[...] = new_carry
    cnt_ref[0] = new_carry.astype(jnp.int32)

    eidx = sub.astype(jnp.float32)
    e_lo = jnp.min(jnp.where(m, eidx, 99.0), axis=0, keepdims=True)
    e_hi = jnp.max(jnp.where(m, eidx, -1.0), axis=0, keepdims=True)
    is_lo = m & (eidx == e_lo)
    is_hi = m & (eidx == e_hi)
    r_lo = jnp.sum(jnp.where(is_lo, rnk, 0.0), axis=0, keepdims=True)
    r_hi = jnp.sum(jnp.where(is_hi, rnk, 0.0), axis=0, keepdims=True)
    w_lo = jnp.sum(jnp.where(is_lo, wgt, 0.0), axis=0, keepdims=True)
    w_hi = jnp.sum(jnp.where(is_hi, wgt, 0.0), axis=0, keepdims=True)
    ri_ref[0, 0:1, :] = e_lo.astype(jnp.int32)
    ri_ref[0, 1:2, :] = r_lo.astype(jnp.int32)
    ri_ref[0, 2:3, :] = e_hi.astype(jnp.int32)
    ri_ref[0, 3:4, :] = r_hi.astype(jnp.int32)
    rwt_ref[0, 0:1, :] = w_lo
    rwt_ref[0, 1:2, :] = w_hi


def _post_mix(x, mix, mod_ref, gffn_ref, rw_cat_ref, rw_hi_ref, rbias_ref, cnt_scr,
              xo_ref, h2_ref, ri_ref, rwt_ref, cnt_ref):
    ts = x.shape[0]
    x_new = x + mod_ref[0, 2:3, :] * mix
    xo_ref[0] = x_new
    h2 = _pre_norm(x_new, mod_ref, gffn_ref, 3, 4)
    for c in range(ROWS_PER_TOKEN):
        h2_ref[pl.ds(c, ts, stride=ROWS_PER_TOKEN), :] = h2[:, c * LANES:(c + 1) * LANES]
    _route_and_emit(h2, rw_cat_ref, rw_hi_ref, rbias_ref, cnt_scr,
                    ri_ref, rwt_ref, cnt_ref)


def _mixer_io_specs(bsz, seq, ts):
    ns = MIX_STREAMS
    hb = bsz // ns
    nt = seq // ts
    in_specs = [
        pl.BlockSpec((ns, 1, ts, D_MODEL), lambda b, t: (0, b, t, 0)),
        pl.BlockSpec((ns, 1, 6, D_MODEL), lambda b, t: (0, b, 0, 0)),
    ]
    out_specs = [
        pl.BlockSpec((ns, 1, ts, D_MODEL), lambda b, t: (0, b, t, 0)),
        pl.BlockSpec((ns, ts * ROWS_PER_TOKEN, LANES), lambda b, t: (0, b * nt + t, 0)),
        pl.BlockSpec((ns, 1, 4, ts), lambda b, t: (0, b, 0, t)),
        pl.BlockSpec((ns, 1, 2, ts), lambda b, t: (0, b, 0, t)),
        pl.BlockSpec((ns, 1, N_EXPERTS, LANES), lambda b, t: (0, b, 0, 0)),
    ]
    out_shapes = [
        jax.ShapeDtypeStruct((ns, hb, seq, D_MODEL), jnp.float32),
        jax.ShapeDtypeStruct((ns, hb * seq * ROWS_PER_TOKEN, LANES), jnp.float32),
        jax.ShapeDtypeStruct((ns, hb, 4, seq), jnp.int32),
        jax.ShapeDtypeStruct((ns, hb, 2, seq), jnp.float32),
        jax.ShapeDtypeStruct((ns, hb, N_EXPERTS, LANES), jnp.int32),
    ]
    return in_specs, out_specs, out_shapes


def _mixer_unstream(outs, bsz, seq):
    x, h2_tm, ri, rwt, cnt = outs
    return (x.reshape(bsz, seq, D_MODEL), h2_tm.reshape(bsz * seq * ROWS_PER_TOKEN, LANES),
            ri.reshape(bsz, 4, seq), rwt.reshape(bsz, 2, seq),
            cnt.reshape(bsz, N_EXPERTS, LANES))


def _const_spec(shape):
    nd = len(shape)
    return pl.BlockSpec(shape, lambda b, t: (0,) * nd)


def _gla_body(x_ref, mod_ref, gmix_ref, gffn_ref, win_ref, wgu_ref, bg_ref, ng_ref,
              wout_ref, rw_cat_ref, rw_hi_ref, rbias_ref,
              xo_ref, h2_ref, ri_ref, rwt_ref, cnt_ref,
              state_scr, o_scr, cnt_scr):
    @pl.when(pl.program_id(1) == 0)
    def _():
        state_scr[...] = jnp.zeros_like(state_scr)
        cnt_scr[...] = jnp.zeros_like(cnt_scr)

    for st in range(MIX_STREAMS):
        _gla_stream(x_ref.at[st], mod_ref.at[st], gmix_ref, gffn_ref, win_ref, wgu_ref, bg_ref,
                    ng_ref, wout_ref, rw_cat_ref, rw_hi_ref, rbias_ref,
                    xo_ref.at[st], h2_ref.at[st], ri_ref.at[st], rwt_ref.at[st], cnt_ref.at[st],
                    state_scr.at[st], o_scr.at[st], cnt_scr.at[st])


def _gla_stream(x_ref, mod_ref, gmix_ref, gffn_ref, win_ref, wgu_ref, bg_ref, ng_ref,
                wout_ref, rw_cat_ref, rw_hi_ref, rbias_ref,
                xo_ref, h2_ref, ri_ref, rwt_ref, cnt_ref,
                state_scr, o_scr, cnt_scr):
    ts = x_ref.shape[1]
    nchunk = ts // CHUNK
    x = x_ref[0]
    h = _pre_norm(x, mod_ref, gmix_ref, 0, 1).astype(jnp.bfloat16)

    o_q, o_k, o_v, o_g, o_a = 0, GLA_QK, 2 * GLA_QK, 2 * GLA_QK + GLA_VD, 2 * GLA_QK + 2 * GLA_VD
    q = _dot(h, win_ref[:, o_q:o_k]) * (GLA_DK ** -0.5)
    k = _dot(h, win_ref[:, o_k:o_v])
    v = _dot(h, win_ref[:, o_v:o_g]).astype(jnp.bfloat16)
    a_lr = _dot(h, win_ref[:, o_a:o_a + GLA_RANK_PAD])

    a_hi, a_lo = _split_bf16(a_lr)
    z2 = _dot(jnp.concatenate([a_hi, a_lo], axis=0), wgu_ref[...])
    z = z2[0:ts] + z2[ts:2 * ts] + bg_ref[...]
    log_a = -(jnp.maximum(-z, 0.0) + jnp.log1p(jnp.exp(-jnp.abs(z)))) * (1.0 / GLA_GATE_TAU)

    row = lax.broadcasted_iota(jnp.int32, (ts, ts), 0)
    col = lax.broadcasted_iota(jnp.int32, (ts, ts), 1)
    tri = ((row // CHUNK == col // CHUNK) & (col <= row)).astype(jnp.bfloat16)
    l_hi, l_lo = _split_bf16(log_a)
    cum2 = _dot(tri, jnp.concatenate([l_hi, l_lo], axis=1))
    cum = cum2[:, 0:GLA_QK] + cum2[:, GLA_QK:2 * GLA_QK]

    q = q.astype(jnp.bfloat16)
    ng = ng_ref[...]
    kv_t, gammas = [], []
    for j in range(nchunk):
        r0 = j * CHUNK
        cum_j = cum[r0:r0 + CHUNK]
        total = cum_j[CHUNK - 1:CHUNK]
        k_dec = (k[r0:r0 + CHUNK] * jnp.exp(total - cum_j)).astype(jnp.bfloat16)
        gammas.append(jnp.exp(total))
        for hd in range(GLA_HEADS):
            ks = slice(hd * GLA_DK, (hd + 1) * GLA_DK)
            vs = slice(hd * GLA_DV, (hd + 1) * GLA_DV)
            kv_t.append(_dot_tn(v[r0:r0 + CHUNK, vs], k_dec[:, ks]))
    states = []
    for hd in range(GLA_HEADS):
        ks = slice(hd * GLA_DK, (hd + 1) * GLA_DK)
        s_cur = state_scr[hd]
        for j in range(nchunk):
            s_cur = s_cur * gammas[j][:, ks] + kv_t[j * GLA_HEADS + hd]
            states.append(s_cur.astype(jnp.bfloat16))
        state_scr[hd] = s_cur
    for hd in range(GLA_HEADS):
        ks = slice(hd * GLA_DK, (hd + 1) * GLA_DK)
        vs = slice(hd * GLA_DV, (hd + 1) * GLA_DV)
        for j in range(nchunk):
            r0 = j * CHUNK
            o = _dot_nt(q[r0:r0 + CHUNK, ks], states[hd * nchunk + j])
            o_scr[r0:r0 + CHUNK, vs] = _rms(o, ng)

    g = _dot(h, win_ref[:, o_g:o_a])
    og = (o_scr[...] * _silu(g)).astype(jnp.bfloat16)
    mix = _dot(og, wout_ref[...])
    _post_mix(x, mix, mod_ref, gffn_ref, rw_cat_ref, rw_hi_ref, rbias_ref, cnt_scr,
              xo_ref, h2_ref, ri_ref, rwt_ref, cnt_ref)


def _gla_layer_call(x, mod, gmix, gffn, w_in, w_gu, b_g, n_g, w_out, rw_cat, rw_hi, rbias):
    bsz, seq, d = x.shape
    ts = MIX_TS
    ns = MIX_STREAMS
    hb = bsz // ns
    io_in, out_specs, out_shapes = _mixer_io_specs(bsz, seq, ts)
    consts = (gmix, gffn, w_in, w_gu, b_g, n_g, w_out, rw_cat, rw_hi, rbias)
    outs = pl.pallas_call(
        _gla_body,
        grid=(hb, seq // ts),
        in_specs=io_in + [_const_spec(a.shape) for a in consts],
        out_specs=out_specs,
        out_shape=out_shapes,
        scratch_shapes=[
            pltpu.VMEM((ns, GLA_HEADS, GLA_DV, GLA_DK), jnp.float32),
            pltpu.VMEM((ns, ts, GLA_VD), jnp.float32),
            pltpu.VMEM((ns, N_EXPERTS, LANES), jnp.float32),
        ],
        compiler_params=pltpu.CompilerParams(
            dimension_semantics=("arbitrary", "arbitrary"),
            vmem_limit_bytes=VMEM_LIMIT),
        name="gla_layer",
    )(x.reshape(ns, hb, seq, d), mod.reshape(ns, hb, 6, d), *consts)
    return _mixer_unstream(outs, bsz, seq)


def _lru_body(x_ref, mod_ref, gmix_ref, gffn_ref, win_ref, cw_ref, cb_ref, wr_ref, br_ref,
              wi_ref, bi_ref, lam_ref, wout_ref, rw_cat_ref, rw_hi_ref, rbias_ref,
              xo_ref, h2_ref, ri_ref, rwt_ref, cnt_ref,
              conv_scr, hstate_scr, a_scr, b_scr, gelu_scr, cnt_scr):
    ns = MIX_STREAMS
    ts = x_ref.shape[2]
    w = D_MODEL

    @pl.when(pl.program_id(1) == 0)
    def _():
        conv_scr[:, 0:SUBLANES, :] = jnp.zeros((ns, SUBLANES, w), jnp.float32)
        hstate_scr[...] = jnp.zeros_like(hstate_scr)
        cnt_scr[...] = jnp.zeros_like(cnt_scr)

    for st in range(ns):
        _lru_pre_scan(x_ref.at[st], mod_ref.at[st], gmix_ref, win_ref, cw_ref, cb_ref, wr_ref,
                      br_ref, wi_ref, bi_ref, lam_ref,
                      conv_scr.at[st], a_scr.at[st], b_scr.at[st], gelu_scr.at[st])

    unroll = 4

    def group_step(gi, hprev):
        for u in range(unroll):
            r0 = pl.multiple_of(gi * (unroll * SUBLANES), unroll * SUBLANES) + u * SUBLANES
            nxt = []
            for st in range(ns):
                hs = (a_scr[st, pl.ds(r0, SUBLANES), :] * hprev[st]
                      + b_scr[st, pl.ds(r0, SUBLANES), :])
                b_scr[st, pl.ds(r0, SUBLANES), :] = hs
                nxt.append(hs[SUBLANES - 1:SUBLANES, :])
            hprev = tuple(nxt)
        return hprev

    hlast = lax.fori_loop(0, ts // (unroll * SUBLANES), group_step,
                          tuple(hstate_scr[st] for st in range(ns)))
    for st in range(ns):
        hstate_scr[st] = hlast[st]

    for st in range(ns):
        x = x_ref[st, 0]
        y = (b_scr[st] * gelu_scr[st]).astype(jnp.bfloat16)
        mix = _dot(y, wout_ref[...])
        _post_mix(x, mix, mod_ref.at[st], gffn_ref, rw_cat_ref, rw_hi_ref, rbias_ref,
                  cnt_scr.at[st], xo_ref.at[st], h2_ref.at[st], ri_ref.at[st], rwt_ref.at[st],
                  cnt_ref.at[st])


def _lru_pre_scan(x_ref, mod_ref, gmix_ref, win_ref, cw_ref, cb_ref, wr_ref, br_ref,
                  wi_ref, bi_ref, lam_ref, conv_scr, a_scr, b_scr, gelu_scr):
    ts = x_ref.shape[1]
    w = D_MODEL
    x = x_ref[0]
    h = _pre_norm(x, mod_ref, gmix_ref, 0, 1).astype(jnp.bfloat16)
    gate_br = _dot(h, win_ref[:, 0:w])
    gelu_scr[...] = 0.5 * gate_br * (1.0 + jnp.tanh(
        0.7978845608028654 * (gate_br + 0.044715 * gate_br * gate_br * gate_br)))
    xb = _dot(h, win_ref[:, w:2 * w])

    conv_scr[SUBLANES:SUBLANES + ts, :] = xb
    xc = cb_ref[...] + xb * cw_ref[CONV_W - 1:CONV_W, :]
    for j in range(CONV_W - 1):
        back = CONV_W - 1 - j
        xc = xc + conv_scr[SUBLANES - back:SUBLANES - back + ts, :] * cw_ref[j:j + 1, :]
    conv_scr[0:SUBLANES, :] = conv_scr[ts:ts + SUBLANES, :]

    xcb = xc.astype(jnp.bfloat16)
    rs, iis = [], []
    for hd in range(LRU_BLOCKS):
        sl = slice(hd * LRU_BLOCK_W, (hd + 1) * LRU_BLOCK_W)
        rs.append(_dot(xcb[:, sl], wr_ref[hd]))
        iis.append(_dot(xcb[:, sl], wi_ref[hd]))
    r = _sigmoid(jnp.concatenate(rs, axis=1) + br_ref[...])
    ig = _sigmoid(jnp.concatenate(iis, axis=1) + bi_ref[...])

    lam = lam_ref[...]
    softplus_neg_lam = jnp.maximum(-lam, 0.0) + jnp.log1p(jnp.exp(-jnp.abs(lam)))
    log_a = (-LRU_C * r) * softplus_neg_lam
    a = jnp.exp(log_a)
    mult = jnp.sqrt(-jnp.tanh(log_a) * (a * a + 1.0))
    bb = (xc * ig) * mult

    a = a.reshape(ts // SUBLANES, SUBLANES, w)
    bb = bb.reshape(ts // SUBLANES, SUBLANES, w)
    rowi = lax.broadcasted_iota(jnp.int32, a.shape, 1)
    for d in (1, 2, 4):
        keep = rowi >= d
        a_sh = jnp.where(keep, pltpu.roll(a, d, 1), 1.0)
        b_sh = jnp.where(keep, pltpu.roll(bb, d, 1), 0.0)
        bb = a * b_sh + bb
        a = a * a_sh
    a_scr[...] = a.reshape(ts, w)
    b_scr[...] = bb.reshape(ts, w)


def _lru_layer_call(x, mod, gmix, gffn, w_in, conv_w, conv_b, w_r, b_r, w_i, b_i, lam, w_out,
                    rw_cat, rw_hi, rbias):
    bsz, seq, d = x.shape
    ts = MIX_TS
    ns = MIX_STREAMS
    hb = bsz // ns
    io_in, out_specs, out_shapes = _mixer_io_specs(bsz, seq, ts)
    consts = (gmix, gffn, w_in, conv_w, conv_b, w_r, b_r, w_i, b_i, lam, w_out,
              rw_cat, rw_hi, rbias)
    outs = pl.pallas_call(
        _lru_body,
        grid=(hb, seq // ts),
        in_specs=io_in + [_const_spec(a.shape) for a in consts],
        out_specs=out_specs,
        out_shape=out_shapes,
        scratch_shapes=[
            pltpu.VMEM((ns, SUBLANES + ts, d), jnp.float32),
            pltpu.VMEM((ns, 1, d), jnp.float32),
            pltpu.VMEM((ns, ts, d), jnp.float32),
            pltpu.VMEM((ns, ts, d), jnp.float32),
            pltpu.VMEM((ns, ts, d), jnp.float32),
            pltpu.VMEM((ns, N_EXPERTS, LANES), jnp.float32),
        ],
        compiler_params=pltpu.CompilerParams(
            dimension_semantics=("arbitrary", "arbitrary"),
            vmem_limit_bytes=VMEM_LIMIT),
        name="lru_layer",
    )(x.reshape(ns, hb, seq, d), mod.reshape(ns, hb, 6, d), *consts)
    return _mixer_unstream(outs, bsz, seq)


def _pos_body(ri_ref, cnt_ref, pos_ref, off_ref):
    cnt = cnt_ref[0].astype(jnp.float32)
    seq = ri_ref.shape[2]
    e_lo = ri_ref[0, 0:1, :].astype(jnp.float32)
    e_hi = ri_ref[0, 2:3, :].astype(jnp.float32)
    p_lo = ri_ref[0, 1:2, :].astype(jnp.float32)
    p_hi = ri_ref[0, 3:4, :].astype(jnp.float32)
    off = jnp.zeros((1, LANES), jnp.float32)
    for e in range(N_EXPERTS):
        off_ref[0, e:e + 1, :] = off.astype(jnp.int32)
        off_b = jnp.broadcast_to(off[:, 0:1], (1, seq))
        p_lo = p_lo + jnp.where(e_lo == float(e), off_b, 0.0)
        p_hi = p_hi + jnp.where(e_hi == float(e), off_b, 0.0)
        off = off + cnt[e:e + 1, :]
    pos_ref[0, 0:1, :] = p_lo.astype(jnp.int32) * ROWS_PER_TOKEN
    pos_ref[0, 1:2, :] = p_hi.astype(jnp.int32) * ROWS_PER_TOKEN


def _pos_call(ri, cnt):
    bsz, _, seq = ri.shape
    return pl.pallas_call(
        _pos_body,
        grid=(bsz,),
        in_specs=[
            pl.BlockSpec((1, 4, seq), lambda b: (b, 0, 0)),
            pl.BlockSpec((1, N_EXPERTS, LANES), lambda b: (b, 0, 0)),
        ],
        out_specs=[
            pl.BlockSpec((1, 2, seq), lambda b: (b, 0, 0)),
            pl.BlockSpec((1, N_EXPERTS, LANES), lambda b: (b, 0, 0)),
        ],
        out_shape=[
            jax.ShapeDtypeStruct((bsz, 2, seq), jnp.int32),
            jax.ShapeDtypeStruct((bsz, N_EXPERTS, LANES), jnp.int32),
        ],
        compiler_params=pltpu.CompilerParams(dimension_semantics=("arbitrary",)),
        name="moe_pos",
    )(ri, cnt)


def _expert_tile(slots_scr, wg_ref, wu_ref, wd_ref, slot0, n_valid, tm, masked):
    R = ROWS_PER_TOKEN
    row0 = pl.multiple_of(slot0 * R, R)
    xs = [slots_scr[pl.ds(row0 + c, tm, stride=R), :] for c in range(R)]
    xt = jnp.concatenate(xs, axis=1).astype(jnp.bfloat16)
    gate = _dot(xt, wg_ref[0])
    up = _dot(xt, wu_ref[0])
    he = (_silu(gate) * up).astype(jnp.bfloat16)
    y = _dot(he, wd_ref[0])
    if masked:
        ok = lax.broadcasted_iota(jnp.int32, (tm, LANES), 0) < n_valid
    for c in range(R):
        yc = y[:, c * LANES:(c + 1) * LANES]
        if masked:
            yc = jnp.where(ok, yc, xs[c])
        slots_scr[pl.ds(row0 + c, tm, stride=R), :] = yc


def _moe_body(off_sm, cnt_sm, pos_ref, wts_ref, h2_ref, x_ref, mod_ref, fin_ref,
              wg_ref, wu_ref, wd_ref, o_ref, slots_scr, tm_scr, *, n_chunks, final_norm):
    b = pl.program_id(0)
    s = pl.program_id(1)
    chunk = h2_ref.shape[0] // ROWS_PER_TOKEN
    tm = MOE_TM
    n_slots = 2 * n_chunks * chunk
    R = ROWS_PER_TOKEN
    U = MOE_UNROLL

    @pl.when((b == 0) & (s == 0))
    def _():
        slots_scr[n_slots * R:(n_slots + tm) * R, :] = jnp.zeros((tm * R, LANES), jnp.float32)

    @pl.when(s < n_chunks)
    def _scatter():
        base = s * chunk

        def body(i, carry):
            t0 = 2 * (base + i * U)
            r0 = pl.multiple_of(i * (U * R), U * R)
            for u in range(U):
                p0 = pos_ref[0, 0, t0 + 2 * u]
                p1 = pos_ref[0, 0, t0 + 2 * u + 1]
                val = h2_ref[pl.ds(r0 + u * R, R), :]
                slots_scr[pl.ds(pl.multiple_of(p0, R), R), :] = val
                slots_scr[pl.ds(pl.multiple_of(p1, R), R), :] = val
            return carry

        lax.fori_loop(0, chunk // U, body, 0)

    @pl.when((s >= n_chunks) & (s < n_chunks + N_EXPERTS))
    def _experts():
        e = s - n_chunks
        off = off_sm[b, e]
        n = cnt_sm[b, e]
        n_full = n // tm
        rem = n - n_full * tm
        tile = functools.partial(_expert_tile, slots_scr, wg_ref, wu_ref, wd_ref)

        def tile_body(j, carry):
            tile(off + j * tm, tm, tm, False)
            return carry

        lax.fori_loop(0, n_full, tile_body, 0)
        tail0 = off + n_full * tm

        @pl.when(rem > tm // 2)
        def _():
            tile(tail0, rem, tm, True)

        @pl.when((rem > 0) & (rem <= tm // 2))
        def _():
            tile(tail0, rem, tm // 2, True)

    @pl.when(s >= n_chunks + N_EXPERTS)
    def _combine():
        base = (s - n_chunks - N_EXPERTS) * chunk

        def body(i, carry):
            t0 = 2 * (base + i * U)
            r0 = pl.multiple_of(i * (U * R), U * R)
            for u in range(U):
                p0 = pos_ref[0, 0, t0 + 2 * u]
                p1 = pos_ref[0, 0, t0 + 2 * u + 1]
                w0 = wts_ref[0, 0, t0 + 2 * u]
                za = slots_scr[pl.ds(pl.multiple_of(p0, R), R), :]
                zb = slots_scr[pl.ds(pl.multiple_of(p1, R), R), :]
                tm_scr[pl.ds(r0 + u * R, R), :] = zb + w0 * (za - zb)
            return carry

        lax.fori_loop(0, chunk // U, body, 0)
        moe = jnp.concatenate(
            [tm_scr[pl.ds(c, chunk, stride=R), :] for c in range(R)], axis=1)
        out = x_ref[...] + mod_ref[0, 5:6, :] * moe
        if final_norm:
            out = _rms(out, fin_ref[...])
        o_ref[...] = out


def _moe_call(off, cnt, pos, wts, h2_tm, x, mod, fin_g, w_gate, w_up, w_down, final_norm):
    bsz, seq, d = x.shape
    chunk = MOE_CHUNK
    n_chunks = seq // chunk
    n_steps = 2 * n_chunks + N_EXPERTS
    n_slots = 2 * seq

    def chunk_in(b, s, *_):
        return (b * n_chunks + jnp.clip(s, 0, n_chunks - 1), 0)

    def chunk_out(b, s, *_):
        return (b * n_chunks + jnp.clip(s - n_chunks - N_EXPERTS, 0, n_chunks - 1), 0)

    def expert_idx(b, s, *_):
        return (jnp.clip(s - n_chunks, 0, N_EXPERTS - 1), 0, 0)

    body = functools.partial(_moe_body, n_chunks=n_chunks, final_norm=final_norm)
    out = pl.pallas_call(
        body,
        grid_spec=pltpu.PrefetchScalarGridSpec(
            num_scalar_prefetch=2,
            grid=(bsz, n_steps),
            in_specs=[
                pl.BlockSpec((1, 1, 2 * seq), lambda b, s, *_: (b, 0, 0), memory_space=pltpu.SMEM),
                pl.BlockSpec((1, 1, 2 * seq), lambda b, s, *_: (b, 0, 0), memory_space=pltpu.SMEM),
                pl.BlockSpec((chunk * ROWS_PER_TOKEN, LANES), chunk_in),
                pl.BlockSpec((chunk, d), chunk_out),
                pl.BlockSpec((1, 6, d), lambda b, s, *_: (b, 0, 0)),
                pl.BlockSpec((1, d), lambda b, s, *_: (0, 0)),
                pl.BlockSpec((1, d, D_EXPERT), expert_idx),
                pl.BlockSpec((1, d, D_EXPERT), expert_idx),
                pl.BlockSpec((1, D_EXPERT, d), expert_idx),
            ],
            out_specs=pl.BlockSpec((chunk, d), chunk_out),
            scratch_shapes=[
                pltpu.VMEM(((n_slots + MOE_TM) * ROWS_PER_TOKEN, LANES), jnp.float32),
                pltpu.VMEM((chunk * ROWS_PER_TOKEN, LANES), jnp.float32),
            ],
        ),
        out_shape=jax.ShapeDtypeStruct((bsz * seq, d), jnp.float32),
        compiler_params=pltpu.CompilerParams(
            dimension_semantics=("arbitrary", "arbitrary"),
            vmem_limit_bytes=VMEM_LIMIT),
        name="moe",
    )(off, cnt, pos, wts, h2_tm, x.reshape(bsz * seq, d), mod, fin_g, w_gate, w_up, w_down)
    return out.reshape(bsz, seq, d)


def _moe_layer(x, h2_tm, ri, rwt, cnt, mod, fin_g, w_gate, w_up, w_down, final_norm):
    pos, off = _pos_call(ri, cnt)
    bsz, _, seq = pos.shape
    pos = pos.transpose(0, 2, 1).reshape(bsz, 1, 2 * seq)
    rwt = rwt.transpose(0, 2, 1).reshape(bsz, 1, 2 * seq)
    return _moe_call(off[:, :, 0], cnt[:, :, 0], pos, rwt, h2_tm, x, mod, fin_g,
                     w_gate, w_up, w_down, final_norm)


def kernel(x, c, gla_w_in, gla_w_gate_up, gla_b_gate, gla_norm_g, gla_w_out, lru_w_in, lru_conv_w, lru_conv_b, lru_w_r, lru_b_r, lru_w_i, lru_b_i, lru_lambda, lru_w_out, router_w, router_bias, moe_w_gate, moe_w_up, moe_w_down, norm_mix_g, norm_ffn_g, ada_w, ada_b, final_norm_g):
    bf = jnp.bfloat16
    depth = ada_w.shape[0]
    bsz = x.shape[0]
    d = D_MODEL
    mod_all = _ada_call(c, ada_w, ada_b).reshape(depth, bsz, 6, d)

    rw_t = router_w.T
    rw_hi = rw_t.astype(bf)
    rw_lo = (rw_t - rw_hi.astype(jnp.float32)).astype(bf)
    rw_cat = jnp.concatenate([rw_hi, rw_lo], axis=0)
    rbias = router_bias.reshape(N_EXPERTS, 1)
    fin_g = final_norm_g.reshape(1, d)

    for i in range(depth):
        j = i // 2
        mod = mod_all[i]
        gmix = norm_mix_g[i].reshape(1, d)
        gffn = norm_ffn_g[i].reshape(1, d)
        if i % 2 == 0:
            w_in = jnp.pad(gla_w_in[j], ((0, 0), (0, GLA_RANK_PAD - GLA_GATE_RANK))).astype(bf)
            w_gu = jnp.pad(gla_w_gate_up[j], ((0, GLA_RANK_PAD - GLA_GATE_RANK), (0, 0))).astype(bf)
            x, h2_tm, ri, rwt, cnt = _gla_layer_call(
                x, mod, gmix, gffn, w_in, w_gu, gla_b_gate[j].reshape(1, GLA_QK),
                gla_norm_g[j].reshape(1, GLA_DV), gla_w_out[j].astype(bf),
                rw_cat, rw_hi, rbias)
        else:
            x, h2_tm, ri, rwt, cnt = _lru_layer_call(
                x, mod, gmix, gffn, lru_w_in[j].astype(bf), lru_conv_w[j],
                lru_conv_b[j].reshape(1, d), lru_w_r[j].astype(bf),
                lru_b_r[j].reshape(1, d), lru_w_i[j].astype(bf), lru_b_i[j].reshape(1, d),
                lru_lambda[j].reshape(1, d), lru_w_out[j].astype(bf), rw_cat, rw_hi, rbias)
        x = _moe_layer(x, h2_tm, ri, rwt, cnt, mod, fin_g, moe_w_gate[i].astype(bf),
                       moe_w_up[i].astype(bf), moe_w_down[i].astype(bf),
                       final_norm=(i == depth - 1))
    return x
```

```python
import functools

import jax
import jax.numpy as jnp
from jax import lax
from jax.experimental import pallas as pl
from jax.experimental.pallas import tpu as pltpu

D_MODEL = 1024
CHUNK = 64
EPS = 1e-6

GLA_HEADS = 4
GLA_DK = 128
GLA_DV = 256
GLA_QK = GLA_HEADS * GLA_DK
GLA_VD = GLA_HEADS * GLA_DV
GLA_GATE_RANK = 16
GLA_GATE_TAU = 16.0
GLA_RANK_PAD = 128
GLA_IN_PAD = 2 * GLA_QK + 2 * GLA_VD + GLA_RANK_PAD

LRU_BLOCKS = 4
LRU_BLOCK_W = D_MODEL // LRU_BLOCKS
CONV_W = 4
LRU_C = 8.0

N_EXPERTS = 16
N_GROUPS = 4
EPG = N_EXPERTS // N_GROUPS
D_EXPERT = 512

LANES = 128
SUBLANES = 8
ROWS_PER_TOKEN = D_MODEL // LANES

MIX_TS = 256
MIX_STREAMS = 2
MOE_TM = 288
MOE_UNROLL = 8
MOE_MOVE = 256
VMEM_LIMIT = 56 * 1024 * 1024
MOE_VMEM_LIMIT = 60 * 1024 * 1024

def _dot(a, b):
    return jnp.dot(a, b, preferred_element_type=jnp.float32)


def _dot_nt(a, b):
    return lax.dot_general(a, b, (((1,), (1,)), ((), ())),
                           preferred_element_type=jnp.float32)


def _dot_tn(a, b):
    return lax.dot_general(a, b, (((0,), (0,)), ((), ())),
                           preferred_element_type=jnp.float32)


def _split_bf16(x):
    hi = x.astype(jnp.bfloat16)
    lo = (x - hi.astype(jnp.float32)).astype(jnp.bfloat16)
    return hi, lo


def _sigmoid(x):
    return 1.0 / (1.0 + jnp.exp(-x))


def _silu(x):
    return x * _sigmoid(x)


def _rms(x, g):
    inv = lax.rsqrt(jnp.mean(x * x, axis=-1, keepdims=True) + EPS)
    return x * inv * g


def _ada_body(c_ref, w_ref, b_ref, o_ref):
    c_hi, c_lo = _split_bf16(_silu(c_ref[...]))
    w_hi, w_lo = _split_bf16(w_ref[0])
    o_ref[0] = _dot(c_hi, w_hi) + (_dot(c_lo, w_hi) + _dot(c_hi, w_lo)) + b_ref[0]


def _ada_call(c, ada_w, ada_b):
    depth, d, n = ada_w.shape
    bsz = c.shape[0]
    tn = 1024
    return pl.pallas_call(
        _ada_body,
        grid=(depth, n // tn),
        in_specs=[
            pl.BlockSpec((bsz, d), lambda l, j: (0, 0)),
            pl.BlockSpec((1, d, tn), lambda l, j: (l, 0, j)),
            pl.BlockSpec((1, 1, tn), lambda l, j: (l, 0, j)),
        ],
        out_specs=pl.BlockSpec((1, bsz, tn), lambda l, j: (l, 0, j)),
        out_shape=jax.ShapeDtypeStruct((depth, bsz, n), jnp.float32),
        compiler_params=pltpu.CompilerParams(
            dimension_semantics=("arbitrary", "arbitrary"),
            vmem_limit_bytes=VMEM_LIMIT),
        name="ada",
    )(c, ada_w, ada_b.reshape(depth, 1, n))


def _pre_norm(x, mod_ref, g_ref, shift_row, scale_row):
    shift = mod_ref[0, shift_row:shift_row + 1, :]
    scale = mod_ref[0, scale_row:scale_row + 1, :]
    return _rms(x, g_ref[...]) * (1.0 + scale) + shift


def _group_partner(x, k, sub):
    n = x.shape[0]
    fwd = pltpu.roll(x, n - k, 0)
    back = pltpu.roll(x, EPG - k, 0)
    wrapped = (sub % EPG) + k >= EPG
    return jnp.where(wrapped, back, fwd), wrapped


def _route_and_emit(h2, rw_cat_ref, rw_hi_ref, rbias_ref, cnt_scr,
                    ri_ref, rwt_ref, cnt_ref):
    ts = h2.shape[0]
    h_hi, h_lo = _split_bf16(h2)
    p1 = _dot_nt(rw_cat_ref[...], h_hi)
    p2 = _dot_nt(rw_hi_ref[...], h_lo)
    logits = p1[0:N_EXPERTS] + p1[N_EXPERTS:2 * N_EXPERTS] + p2
    s = _sigmoid(logits)
    sel = s + rbias_ref[...]
    sub = lax.broadcasted_iota(jnp.int32, (N_EXPERTS, ts), 0)

    pair_best = None
    rank = jnp.zeros((N_EXPERTS, ts), jnp.float32)
    for k in range(1, EPG):
        p, wrapped = _group_partner(sel, k, sub)
        ps = sel + p
        pair_best = ps if pair_best is None else jnp.maximum(pair_best, ps)
        ahead = (p > sel) | ((p == sel) & wrapped)
        rank = rank + ahead.astype(jnp.float32)
    gscore = pair_best
    for k in range(1, EPG):
        p, _ = _group_partner(pair_best, k, sub)
        gscore = jnp.maximum(gscore, p)
    chosen = jnp.ones((N_EXPERTS, ts), jnp.bool_)
    grp = sub // EPG
    for j in range(1, N_GROUPS):
        other = pltpu.roll(gscore, N_EXPERTS - EPG * j, 0)
        other_is_later = grp + j < N_GROUPS
        chosen = chosen & ((gscore > other) | ((gscore == other) & other_is_later))
    m = chosen & (rank < 2.0)
    mf = m.astype(jnp.float32)
    sm = s * mf
    wgt = sm / jnp.sum(sm, axis=0, keepdims=True)

    row = lax.broadcasted_iota(jnp.int32, (ts, ts), 0)
    col = lax.broadcasted_iota(jnp.int32, (ts, ts), 1)
    before = (row < col).astype(jnp.bfloat16)
    carry = cnt_scr[...]
    rnk = _dot(mf.astype(jnp.bfloat16), before) + carry[:, 0:1]
    new_carry = carry + jnp.sum(mf, axis=1, keepdims=True)
    cnt_scr[...] = new_carry
    cnt_ref[0] = new_carry.astype(jnp.int32)

    eidx = sub.astype(jnp.float32)
    e_lo = jnp.min(jnp.where(m, eidx, 99.0), axis=0, keepdims=True)
    e_hi = jnp.max(jnp.where(m, eidx, -1.0), axis=0, keepdims=True)
    is_lo = m & (eidx == e_lo)
    is_hi = m & (eidx == e_hi)
    r_lo = jnp.sum(jnp.where(is_lo, rnk, 0.0), axis=0, keepdims=True)
    r_hi = jnp.sum(jnp.where(is_hi, rnk, 0.0), axis=0, keepdims=True)
    w_lo = jnp.sum(jnp.where(is_lo, wgt, 0.0), axis=0, keepdims=True)
    w_hi = jnp.sum(jnp.where(is_hi, wgt, 0.0), axis=0, keepdims=True)
    ri_ref[0, 0:1, :] = e_lo.astype(jnp.int32)
    ri_ref[0, 1:2, :] = r_lo.astype(jnp.int32)
    ri_ref[0, 2:3, :] = e_hi.astype(jnp.int32)
    ri_ref[0, 3:4, :] = r_hi.astype(jnp.int32)
    rwt_ref[0, 0:1, :] = w_lo
    rwt_ref[0, 1:2, :] = w_hi


def _post_mix(x, mix, mod_ref, gffn_ref, rw_cat_ref, rw_hi_ref, rbias_ref, cnt_scr,
              xo_ref, h2_ref, ri_ref, rwt_ref, cnt_ref):
    ts = x.shape[0]
    x_new = x + mod_ref[0, 2:3, :] * mix
    xo_ref[0] = x_new
    h2 = _pre_norm(x_new, mod_ref, gffn_ref, 3, 4)
    for c in range(ROWS_PER_TOKEN):
        h2_ref[pl.ds(c, ts, stride=ROWS_PER_TOKEN), :] = h2[:, c * LANES:(c + 1) * LANES]
    _route_and_emit(h2, rw_cat_ref, rw_hi_ref, rbias_ref, cnt_scr,
                    ri_ref, rwt_ref, cnt_ref)


def _mixer_io_specs(bsz, seq, ts):
    ns = MIX_STREAMS
    hb = bsz // ns
    nt = seq // ts
    in_specs = [
        pl.BlockSpec((ns, 1, ts, D_MODEL), lambda b, t: (0, b, t, 0)),
        pl.BlockSpec((ns, 1, 6, D_MODEL), lambda b, t: (0, b, 0, 0)),
    ]
    out_specs = [
        pl.BlockSpec((ns, 1, ts, D_MODEL), lambda b, t: (0, b, t, 0)),
        pl.BlockSpec((ns, ts * ROWS_PER_TOKEN, LANES), lambda b, t: (0, b * nt + t, 0)),
        pl.BlockSpec((ns, 1, 4, ts), lambda b, t: (0, b, 0, t)),
        pl.BlockSpec((ns, 1, 2, ts), lambda b, t: (0, b, 0, t)),
        pl.BlockSpec((ns, 1, N_EXPERTS, LANES), lambda b, t: (0, b, 0, 0)),
    ]
    out_shapes = [
        jax.ShapeDtypeStruct((ns, hb, seq, D_MODEL), jnp.float32),
        jax.ShapeDtypeStruct((ns, hb * seq * ROWS_PER_TOKEN, LANES), jnp.float32),
        jax.ShapeDtypeStruct((ns, hb, 4, seq), jnp.int32),
        jax.ShapeDtypeStruct((ns, hb, 2, seq), jnp.float32),
        jax.ShapeDtypeStruct((ns, hb, N_EXPERTS, LANES), jnp.int32),
    ]
    return in_specs, out_specs, out_shapes


def _mixer_unstream(outs, bsz, seq):
    x, h2_tm, ri, rwt, cnt = outs
    return (x.reshape(bsz, seq, D_MODEL), h2_tm.reshape(bsz * seq * ROWS_PER_TOKEN, LANES),
            ri.reshape(bsz, 4, seq), rwt.reshape(bsz, 2, seq),
            cnt.reshape(bsz, N_EXPERTS, LANES))


def _const_spec(shape):
    nd = len(shape)
    return pl.BlockSpec(shape, lambda b, t: (0,) * nd)


def _gla_body(x_ref, mod_ref, gmix_ref, gffn_ref, win_ref, wgu_ref, bg_ref, ng_ref,
              wout_ref, rw_cat_ref, rw_hi_ref, rbias_ref,
              xo_ref, h2_ref, ri_ref, rwt_ref, cnt_ref,
              state_scr, o_scr, cnt_scr):
    @pl.when(pl.program_id(1) == 0)
    def _():
        state_scr[...] = jnp.zeros_like(state_scr)
        cnt_scr[...] = jnp.zeros_like(cnt_scr)

    for st in range(MIX_STREAMS):
        _gla_stream(x_ref.at[st], mod_ref.at[st], gmix_ref, gffn_ref, win_ref, wgu_ref, bg_ref,
                    ng_ref, wout_ref, rw_cat_ref, rw_hi_ref, rbias_ref,
                    xo_ref.at[st], h2_ref.at[st], ri_ref.at[st], rwt_ref.at[st], cnt_ref.at[st],
                    state_scr.at[st], o_scr.at[st], cnt_scr.at[st])


def _gla_stream(x_ref, mod_ref, gmix_ref, gffn_ref, win_ref, wgu_ref, bg_ref, ng_ref,
                wout_ref, rw_cat_ref, rw_hi_ref, rbias_ref,
                xo_ref, h2_ref, ri_ref, rwt_ref, cnt_ref,
                state_scr, o_scr, cnt_scr):
    ts = x_ref.shape[1]
    nchunk = ts // CHUNK
    x = x_ref[0]
    h = _pre_norm(x, mod_ref, gmix_ref, 0, 1).astype(jnp.bfloat16)

    o_q, o_k, o_v, o_g, o_a = 0, GLA_QK, 2 * GLA_QK, 2 * GLA_QK + GLA_VD, 2 * GLA_QK + 2 * GLA_VD
    q = _dot(h, win_ref[:, o_q:o_k]) * (GLA_DK ** -0.5)
    k = _dot(h, win_ref[:, o_k:o_v])
    v = _dot(h, win_ref[:, o_v:o_g]).astype(jnp.bfloat16)
    a_lr = _dot(h, win_ref[:, o_a:o_a + GLA_RANK_PAD])

    a_hi, a_lo = _split_bf16(a_lr)
    z2 = _dot(jnp.concatenate([a_hi, a_lo], axis=0), wgu_ref[...])
    z = z2[0:ts] + z2[ts:2 * ts] + bg_ref[...]
    log_a = -(jnp.maximum(-z, 0.0) + jnp.log1p(jnp.exp(-jnp.abs(z)))) * (1.0 / GLA_GATE_TAU)

    row = lax.broadcasted_iota(jnp.int32, (ts, ts), 0)
    col = lax.broadcasted_iota(jnp.int32, (ts, ts), 1)
    tri = ((row // CHUNK == col // CHUNK) & (col <= row)).astype(jnp.bfloat16)
    l_hi, l_lo = _split_bf16(log_a)
    cum2 = _dot(tri, jnp.concatenate([l_hi, l_lo], axis=1))
    cum = cum2[:, 0:GLA_QK] + cum2[:, GLA_QK:2 * GLA_QK]

    q = q.astype(jnp.bfloat16)
    ng = ng_ref[...]
    kv_t, gammas = [], []
    for j in range(nchunk):
        r0 = j * CHUNK
        cum_j = cum[r0:r0 + CHUNK]
        total = cum_j[CHUNK - 1:CHUNK]
        k_dec = (k[r0:r0 + CHUNK] * jnp.exp(total - cum_j)).astype(jnp.bfloat16)
        gammas.append(jnp.exp(total))
        for hd in range(GLA_HEADS):
            ks = slice(hd * GLA_DK, (hd + 1) * GLA_DK)
            vs = slice(hd * GLA_DV, (hd + 1) * GLA_DV)
            kv_t.append(_dot_tn(v[r0:r0 + CHUNK, vs], k_dec[:, ks]))
    states = []
    for hd in range(GLA_HEADS):
        ks = slice(hd * GLA_DK, (hd + 1) * GLA_DK)
        s_cur = state_scr[hd]
        for j in range(nchunk):
            s_cur = s_cur * gammas[j][:, ks] + kv_t[j * GLA_HEADS + hd]
            states.append(s_cur.astype(jnp.bfloat16))
        state_scr[hd] = s_cur
    for hd in range(GLA_HEADS):
        ks = slice(hd * GLA_DK, (hd + 1) * GLA_DK)
        vs = slice(hd * GLA_DV, (hd + 1) * GLA_DV)
        for j in range(nchunk):
            r0 = j * CHUNK
            o = _dot_nt(q[r0:r0 + CHUNK, ks], states[hd * nchunk + j])
            o_scr[r0:r0 + CHUNK, vs] = _rms(o, ng)

    g = _dot(h, win_ref[:, o_g:o_a])
    og = (o_scr[...] * _silu(g)).astype(jnp.bfloat16)
    mix = _dot(og, wout_ref[...])
    _post_mix(x, mix, mod_ref, gffn_ref, rw_cat_ref, rw_hi_ref, rbias_ref, cnt_scr,
              xo_ref, h2_ref, ri_ref, rwt_ref, cnt_ref)


def _gla_layer_call(x, mod, gmix, gffn, w_in, w_gu, b_g, n_g, w_out, rw_cat, rw_hi, rbias):
    bsz, seq, d = x.shape
    ts = MIX_TS
    ns = MIX_STREAMS
    hb = bsz // ns
    io_in, out_specs, out_shapes = _mixer_io_specs(bsz, seq, ts)
    consts = (gmix, gffn, w_in, w_gu, b_g, n_g, w_out, rw_cat, rw_hi, rbias)
    outs = pl.pallas_call(
        _gla_body,
        grid=(hb, seq // ts),
        in_specs=io_in + [_const_spec(a.shape) for a in consts],
        out_specs=out_specs,
        out_shape=out_shapes,
        scratch_shapes=[
            pltpu.VMEM((ns, GLA_HEADS, GLA_DV, GLA_DK), jnp.float32),
            pltpu.VMEM((ns, ts, GLA_VD), jnp.float32),
            pltpu.VMEM((ns, N_EXPERTS, LANES), jnp.float32),
        ],
        compiler_params=pltpu.CompilerParams(
            dimension_semantics=("arbitrary", "arbitrary"),
            vmem_limit_bytes=VMEM_LIMIT),
        name="gla_layer",
    )(x.reshape(ns, hb, seq, d), mod.reshape(ns, hb, 6, d), *consts)
    return _mixer_unstream(outs, bsz, seq)


def _lru_body(x_ref, mod_ref, gmix_ref, gffn_ref, win_ref, cw_ref, cb_ref, wr_ref, br_ref,
              wi_ref, bi_ref, lam_ref, wout_ref, rw_cat_ref, rw_hi_ref, rbias_ref,
              xo_ref, h2_ref, ri_ref, rwt_ref, cnt_ref,
              conv_scr, hstate_scr, a_scr, b_scr, gelu_scr, cnt_scr):
    ns = MIX_STREAMS
    ts = x_ref.shape[2]
    w = D_MODEL

    @pl.when(pl.program_id(1) == 0)
    def _():
        conv_scr[:, 0:SUBLANES, :] = jnp.zeros((ns, SUBLANES, w), jnp.float32)
        hstate_scr[...] = jnp.zeros_like(hstate_scr)
        cnt_scr[...] = jnp.zeros_like(cnt_scr)

    for st in range(ns):
        _lru_pre_scan(x_ref.at[st], mod_ref.at[st], gmix_ref, win_ref, cw_ref, cb_ref, wr_ref,
                      br_ref, wi_ref, bi_ref, lam_ref,
                      conv_scr.at[st], a_scr.at[st], b_scr.at[st], gelu_scr.at[st])

    unroll = 4

    def group_step(gi, hprev):
        for u in range(unroll):
            r0 = pl.multiple_of(gi * (unroll * SUBLANES), unroll * SUBLANES) + u * SUBLANES
            nxt = []
            for st in range(ns):
                hs = (a_scr[st, pl.ds(r0, SUBLANES), :] * hprev[st]
                      + b_scr[st, pl.ds(r0, SUBLANES), :])
                b_scr[st, pl.ds(r0, SUBLANES), :] = hs
                nxt.append(hs[SUBLANES - 1:SUBLANES, :])
            hprev = tuple(nxt)
        return hprev

    hlast = lax.fori_loop(0, ts // (unroll * SUBLANES), group_step,
                          tuple(hstate_scr[st] for st in range(ns)))
    for st in range(ns):
        hstate_scr[st] = hlast[st]

    for st in range(ns):
        x = x_ref[st, 0]
        y = (b_scr[st] * gelu_scr[st]).astype(jnp.bfloat16)
        mix = _dot(y, wout_ref[...])
        _post_mix(x, mix, mod_ref.at[st], gffn_ref, rw_cat_ref, rw_hi_ref, rbias_ref,
                  cnt_scr.at[st], xo_ref.at[st], h2_ref.at[st], ri_ref.at[st], rwt_ref.at[st],
                  cnt_ref.at[st])


def _lru_pre_scan(x_ref, mod_ref, gmix_ref, win_ref, cw_ref, cb_ref, wr_ref, br_ref,
                  wi_ref, bi_ref, lam_ref, conv_scr, a_scr, b_scr, gelu_scr):
    ts = x_ref.shape[1]
    w = D_MODEL
    x = x_ref[0]
    h = _pre_norm(x, mod_ref, gmix_ref, 0, 1).astype(jnp.bfloat16)
    gate_br = _dot(h, win_ref[:, 0:w])
    gelu_scr[...] = 0.5 * gate_br * (1.0 + jnp.tanh(
        0.7978845608028654 * (gate_br + 0.044715 * gate_br * gate_br * gate_br)))
    xb = _dot(h, win_ref[:, w:2 * w])

    conv_scr[SUBLANES:SUBLANES + ts, :] = xb
    xc = cb_ref[...] + xb * cw_ref[CONV_W - 1:CONV_W, :]
    for j in range(CONV_W - 1):
        back = CONV_W - 1 - j
        xc = xc + conv_scr[SUBLANES - back:SUBLANES - back + ts, :] * cw_ref[j:j + 1, :]
    conv_scr[0:SUBLANES, :] = conv_scr[ts:ts + SUBLANES, :]

    xcb = xc.astype(jnp.bfloat16)
    rs, iis = [], []
    for hd in range(LRU_BLOCKS):
        sl = slice(hd * LRU_BLOCK_W, (hd + 1) * LRU_BLOCK_W)
        rs.append(_dot(xcb[:, sl], wr_ref[hd]))
        iis.append(_dot(xcb[:, sl], wi_ref[hd]))
    r = _sigmoid(jnp.concatenate(rs, axis=1) + br_ref[...])
    ig = _sigmoid(jnp.concatenate(iis, axis=1) + bi_ref[...])

    lam = lam_ref[...]
    softplus_neg_lam = jnp.maximum(-lam, 0.0) + jnp.log1p(jnp.exp(-jnp.abs(lam)))
    log_a = (-LRU_C * r) * softplus_neg_lam
    a = jnp.exp(log_a)
    mult = jnp.sqrt(-jnp.tanh(log_a) * (a * a + 1.0))
    bb = (xc * ig) * mult

    a = a.reshape(ts // SUBLANES, SUBLANES, w)
    bb = bb.reshape(ts // SUBLANES, SUBLANES, w)
    rowi = lax.broadcasted_iota(jnp.int32, a.shape, 1)
    for d in (1, 2, 4):
        keep = rowi >= d
        a_sh = jnp.where(keep, pltpu.roll(a, d, 1), 1.0)
        b_sh = jnp.where(keep, pltpu.roll(bb, d, 1), 0.0)
        bb = a * b_sh + bb
        a = a * a_sh
    a_scr[...] = a.reshape(ts, w)
    b_scr[...] = bb.reshape(ts, w)


def _lru_layer_call(x, mod, gmix, gffn, w_in, conv_w, conv_b, w_r, b_r, w_i, b_i, lam, w_out,
                    rw_cat, rw_hi, rbias):
    bsz, seq, d = x.shape
    ts = MIX_TS
    ns = MIX_STREAMS
    hb = bsz // ns
    io_in, out_specs, out_shapes = _mixer_io_specs(bsz, seq, ts)
    consts = (gmix, gffn, w_in, conv_w, conv_b, w_r, b_r, w_i, b_i, lam, w_out,
              rw_cat, rw_hi, rbias)
    outs = pl.pallas_call(
        _lru_body,
        grid=(hb, seq // ts),
        in_specs=io_in + [_const_spec(a.shape) for a in consts],
        out_specs=out_specs,
        out_shape=out_shapes,
        scratch_shapes=[
            pltpu.VMEM((ns, SUBLANES + ts, d), jnp.float32),
            pltpu.VMEM((ns, 1, d), jnp.float32),
            pltpu.VMEM((ns, ts, d), jnp.float32),
            pltpu.VMEM((ns, ts, d), jnp.float32),
            pltpu.VMEM((ns, ts, d), jnp.float32),
            pltpu.VMEM((ns, N_EXPERTS, LANES), jnp.float32),
        ],
        compiler_params=pltpu.CompilerParams(
            dimension_semantics=("arbitrary", "arbitrary"),
            vmem_limit_bytes=VMEM_LIMIT),
        name="lru_layer",
    )(x.reshape(ns, hb, seq, d), mod.reshape(ns, hb, 6, d), *consts)
    return _mixer_unstream(outs, bsz, seq)


def _pos_body(ri_ref, cnt_ref, pos_ref, off_ref):
    cnt = cnt_ref[0].astype(jnp.float32)
    seq = ri_ref.shape[2]
    e_lo = ri_ref[0, 0:1, :].astype(jnp.float32)
    e_hi = ri_ref[0, 2:3, :].astype(jnp.float32)
    p_lo = ri_ref[0, 1:2, :].astype(jnp.float32)
    p_hi = ri_ref[0, 3:4, :].astype(jnp.float32)
    off = jnp.zeros((1, LANES), jnp.float32)
    for e in range(N_EXPERTS):
        off_ref[0, e:e + 1, :] = off.astype(jnp.int32)
        off_b = jnp.broadcast_to(off[:, 0:1], (1, seq))
        p_lo = p_lo + jnp.where(e_lo == float(e), off_b, 0.0)
        p_hi = p_hi + jnp.where(e_hi == float(e), off_b, 0.0)
        off = off + cnt[e:e + 1, :]
    pos_ref[0, 0:1, :] = p_lo.astype(jnp.int32) * ROWS_PER_TOKEN
    pos_ref[0, 1:2, :] = p_hi.astype(jnp.int32) * ROWS_PER_TOKEN


def _pos_call(ri, cnt):
    bsz, _, seq = ri.shape
    return pl.pallas_call(
        _pos_body,
        grid=(bsz,),
        in_specs=[
            pl.BlockSpec((1, 4, seq), lambda b: (b, 0, 0)),
            pl.BlockSpec((1, N_EXPERTS, LANES), lambda b: (b, 0, 0)),
        ],
        out_specs=[
            pl.BlockSpec((1, 2, seq), lambda b: (b, 0, 0)),
            pl.BlockSpec((1, N_EXPERTS, LANES), lambda b: (b, 0, 0)),
        ],
        out_shape=[
            jax.ShapeDtypeStruct((bsz, 2, seq), jnp.int32),
            jax.ShapeDtypeStruct((bsz, N_EXPERTS, LANES), jnp.int32),
        ],
        compiler_params=pltpu.CompilerParams(dimension_semantics=("arbitrary",)),
        name="moe_pos",
    )(ri, cnt)


def _scatter_tokens(pos_ref, h2_ref, slots, idx0, lo, cnt):
    R = ROWS_PER_TOKEN
    for u in range(lo, lo + cnt):
        p0 = pos_ref[0, 0, idx0 + 2 * u]
        p1 = pos_ref[0, 0, idx0 + 2 * u + 1]
        val = h2_ref[u * R:(u + 1) * R, :]
        slots[pl.ds(pl.multiple_of(p0, R), R), :] = val
        slots[pl.ds(pl.multiple_of(p1, R), R), :] = val


def _combine_tokens(pos_ref, wts_ref, slots, tm_scr, idx0, lo, cnt):
    R = ROWS_PER_TOKEN
    for u in range(lo, lo + cnt):
        p0 = pos_ref[0, 0, idx0 + 2 * u]
        p1 = pos_ref[0, 0, idx0 + 2 * u + 1]
        w0 = wts_ref[0, 0, idx0 + 2 * u]
        za = slots[pl.ds(pl.multiple_of(p0, R), R), :]
        zb = slots[pl.ds(pl.multiple_of(p1, R), R), :]
        tm_scr[u * R:(u + 1) * R, :] = zb + w0 * (za - zb)


def _combine_epilogue(tm_scr, x_ref, mod_ref, fin_ref, o_ref, lo, cnt, final_norm):
    R = ROWS_PER_TOKEN
    moe = jnp.concatenate(
        [tm_scr[pl.ds(lo * R + c, cnt, stride=R), :] for c in range(R)], axis=1)
    out = x_ref[lo:lo + cnt, :] + mod_ref[0, 5:6, :] * moe
    if final_norm:
        out = _rms(out, fin_ref[...])
    o_ref[lo:lo + cnt, :] = out


def _expert_tile_hooked(slots, wg_ref, wu_ref, wd_ref, slot0, n_valid, tm, hooks):
    R = ROWS_PER_TOKEN
    nw = 256
    row0 = pl.multiple_of(slot0 * R, R)
    xs = [slots[pl.ds(row0 + c, tm, stride=R), :] for c in range(R)]
    xt = jnp.concatenate(xs, axis=1).astype(jnp.bfloat16)
    ok = lax.broadcasted_iota(jnp.int32, (tm, LANES), 0) < n_valid
    hooks = list(hooks)

    def run_hook():
        if hooks:
            hooks.pop(0)()

    gate, up = [], []
    for i in range(D_EXPERT // nw):
        gate.append(_dot(xt, wg_ref[0, :, i * nw:(i + 1) * nw]))
        run_hook()
    for i in range(D_EXPERT // nw):
        up.append(_dot(xt, wu_ref[0, :, i * nw:(i + 1) * nw]))
        run_hook()
    he = (_silu(jnp.concatenate(gate, axis=1)) * jnp.concatenate(up, axis=1)).astype(jnp.bfloat16)
    for i in range(D_MODEL // nw):
        y = _dot(he, wd_ref[0, :, i * nw:(i + 1) * nw])
        for cc in range(nw // LANES):
            c = i * (nw // LANES) + cc
            slots[pl.ds(row0 + c, tm, stride=R), :] = jnp.where(
                ok, y[:, cc * LANES:(cc + 1) * LANES], xs[c])
        run_hook()
    while hooks:
        run_hook()


def _moe_body(off_sm, cnt_sm, pos_sc_ref, pos_cb_ref, wts_ref, h2_ref, x_ref, mod_ref, fin_ref,
              wg_ref, wu_ref, wd_ref, o_ref, slots_a, slots_b, tm_scr, *, n_blocks, final_norm):
    r = pl.program_id(0)
    e = pl.program_id(1)
    R = ROWS_PER_TOKEN
    U = MOE_UNROLL
    tm = MOE_TM
    mv = MOE_MOVE
    half = N_EXPERTS // 2
    n_slots = 2 * half * mv
    n_hooks = 2 * (D_EXPERT // 256) + D_MODEL // 256
    per_hook = mv // n_hooks

    @pl.when((r == 0) & (e == 0))
    def _():
        zeros = jnp.zeros((tm * R, LANES), jnp.float32)
        slots_a[n_slots * R:(n_slots + tm) * R, :] = zeros
        slots_b[n_slots * R:(n_slots + tm) * R, :] = zeros

    do_exp = (r >= 1) & (r <= n_blocks)
    do_comb = (r >= 2) & (e < half)
    do_scat = (r < n_blocks) & (e >= half)
    b_exp = jnp.clip(r - 1, 0, n_blocks - 1)
    off = off_sm[b_exp, e]
    n = cnt_sm[b_exp, e]
    one_tile = do_exp & (n > tm // 2) & (n <= tm)
    comb_idx0 = 2 * (e * mv)
    scat_idx0 = 2 * ((e - half) * mv)

    def round_body(exp_buf, mov_buf):
        tile = functools.partial(_expert_tile_hooked, exp_buf, wg_ref, wu_ref, wd_ref)

        @pl.when(one_tile & do_comb)
        def _():
            def hook(k):
                def run():
                    if k < n_hooks:
                        _combine_tokens(pos_cb_ref, wts_ref, mov_buf, tm_scr, comb_idx0,
                                        k * per_hook, per_hook)
                    if k >= 1:
                        _combine_epilogue(tm_scr, x_ref, mod_ref, fin_ref, o_ref,
                                          (k - 1) * per_hook, per_hook, final_norm)
                return run
            tile(off, n, tm, [hook(k) for k in range(n_hooks + 1)])

        @pl.when(one_tile & do_scat)
        def _():
            def hook(k):
                return lambda: _scatter_tokens(pos_sc_ref, h2_ref, mov_buf, scat_idx0,
                                               k * per_hook, per_hook)
            tile(off, n, tm, [hook(k) for k in range(n_hooks)])

        @pl.when(jnp.logical_not(one_tile & (do_comb | do_scat)))
        def _():
            @pl.when(do_comb)
            def _():
                def body(i, carry):
                    i0 = comb_idx0 + 2 * U * i
                    for u in range(U):
                        p0 = pos_cb_ref[0, 0, i0 + 2 * u]
                        p1 = pos_cb_ref[0, 0, i0 + 2 * u + 1]
                        w0 = wts_ref[0, 0, i0 + 2 * u]
                        za = mov_buf[pl.ds(pl.multiple_of(p0, R), R), :]
                        zb = mov_buf[pl.ds(pl.multiple_of(p1, R), R), :]
                        r0 = pl.multiple_of(i * (U * R), U * R) + u * R
                        tm_scr[pl.ds(r0, R), :] = zb + w0 * (za - zb)
                    return carry

                lax.fori_loop(0, mv // U, body, 0)
                _combine_epilogue(tm_scr, x_ref, mod_ref, fin_ref, o_ref, 0, mv, final_norm)

            @pl.when(do_scat)
            def _():
                def body(i, carry):
                    i0 = scat_idx0 + 2 * U * i
                    for u in range(U):
                        p0 = pos_sc_ref[0, 0, i0 + 2 * u]
                        p1 = pos_sc_ref[0, 0, i0 + 2 * u + 1]
                        r0 = pl.multiple_of(i * (U * R), U * R) + u * R
                        val = h2_ref[pl.ds(r0, R), :]
                        mov_buf[pl.ds(pl.multiple_of(p0, R), R), :] = val
                        mov_buf[pl.ds(pl.multiple_of(p1, R), R), :] = val
                    return carry

                lax.fori_loop(0, mv // U, body, 0)

            @pl.when(do_exp)
            def _():
                def tile_body(j, carry):
                    tile(off + j * tm, n - j * tm, tm, [])
                    return carry

                lax.fori_loop(0, (n + tm - 1) // tm, tile_body, 0)

    @pl.when(r % 2 == 0)
    def _():
        round_body(slots_b, slots_a)

    @pl.when(r % 2 == 1)
    def _():
        round_body(slots_a, slots_b)


def _moe_call(off, cnt, pos, wts, h2_tm, x, mod, fin_g, w_gate, w_up, w_down, final_norm):
    bsz, seq, d = x.shape
    mv = MOE_MOVE
    half = N_EXPERTS // 2
    assert seq == half * mv
    n_slots = 2 * seq
    last = bsz - 1

    def scat_blk(r, e, *_):
        return (jnp.minimum(r, last) * half + jnp.clip(e - half, 0, half - 1), 0)

    def comb_blk(r, e, *_):
        return (jnp.where(r < 2, 0, (r - 2) * half + jnp.minimum(e, half - 1)), 0)

    def expert_idx(r, e, *_):
        return (jnp.where(r == 0, 0, jnp.where(r == bsz + 1, N_EXPERTS - 1, e)), 0, 0)

    def scat_row(r, e, *_):
        return (jnp.minimum(r, last), 0, 0)

    def comb_row(r, e, *_):
        return (jnp.clip(r - 2, 0, last), 0, 0)

    body = functools.partial(_moe_body, n_blocks=bsz, final_norm=final_norm)
    slot_rows = (n_slots + MOE_TM) * ROWS_PER_TOKEN
    out = pl.pallas_call(
        body,
        grid_spec=pltpu.PrefetchScalarGridSpec(
            num_scalar_prefetch=2,
            grid=(bsz + 2, N_EXPERTS),
            in_specs=[
                pl.BlockSpec((1, 1, 2 * seq), scat_row, memory_space=pltpu.SMEM),
                pl.BlockSpec((1, 1, 2 * seq), comb_row, memory_space=pltpu.SMEM),
                pl.BlockSpec((1, 1, 2 * seq), comb_row, memory_space=pltpu.SMEM),
                pl.BlockSpec((mv * ROWS_PER_TOKEN, LANES), scat_blk),
                pl.BlockSpec((mv, d), comb_blk),
                pl.BlockSpec((1, 6, d), comb_row),
                pl.BlockSpec((1, d), lambda r, e, *_: (0, 0)),
                pl.BlockSpec((1, d, D_EXPERT), expert_idx),
                pl.BlockSpec((1, d, D_EXPERT), expert_idx),
                pl.BlockSpec((1, D_EXPERT, d), expert_idx),
            ],
            out_specs=pl.BlockSpec((mv, d), comb_blk),
            scratch_shapes=[
                pltpu.VMEM((slot_rows, LANES), jnp.float32),
                pltpu.VMEM((slot_rows, LANES), jnp.float32),
                pltpu.VMEM((mv * ROWS_PER_TOKEN, LANES), jnp.float32),
            ],
        ),
        out_shape=jax.ShapeDtypeStruct((bsz * seq, d), jnp.float32),
        compiler_params=pltpu.CompilerParams(
            dimension_semantics=("arbitrary", "arbitrary"),
            vmem_limit_bytes=MOE_VMEM_LIMIT),
        name="moe",
    )(off, cnt, pos, pos, wts, h2_tm, x.reshape(bsz * seq, d), mod, fin_g, w_gate, w_up, w_down)
    return out.reshape(bsz, seq, d)


def _moe_layer(x, h2_tm, ri, rwt, cnt, mod, fin_g, w_gate, w_up, w_down, final_norm):
    pos, off = _pos_call(ri, cnt)
    bsz, _, seq = pos.shape
    pos = pos.transpose(0, 2, 1).reshape(bsz, 1, 2 * seq)
    rwt = rwt.transpose(0, 2, 1).reshape(bsz, 1, 2 * seq)
    return _moe_call(off[:, :, 0], cnt[:, :, 0], pos, rwt, h2_tm, x, mod, fin_g,
                     w_gate, w_up, w_down, final_norm)


def kernel(x, c, gla_w_in, gla_w_gate_up, gla_b_gate, gla_norm_g, gla_w_out, lru_w_in, lru_conv_w, lru_conv_b, lru_w_r, lru_b_r, lru_w_i, lru_b_i, lru_lambda, lru_w_out, router_w, router_bias, moe_w_gate, moe_w_up, moe_w_down, norm_mix_g, norm_ffn_g, ada_w, ada_b, final_norm_g):
    bf = jnp.bfloat16
    depth = ada_w.shape[0]
    bsz = x.shape[0]
    d = D_MODEL
    mod_all = _ada_call(c, ada_w, ada_b).reshape(depth, bsz, 6, d)

    rw_t = router_w.T
    rw_hi = rw_t.astype(bf)
    rw_lo = (rw_t - rw_hi.astype(jnp.float32)).astype(bf)
    rw_cat = jnp.concatenate([rw_hi, rw_lo], axis=0)
    rbias = router_bias.reshape(N_EXPERTS, 1)
    fin_g = final_norm_g.reshape(1, d)

    for i in range(depth):
        j = i // 2
        mod = mod_all[i]
        gmix = norm_mix_g[i].reshape(1, d)
        gffn = norm_ffn_g[i].reshape(1, d)
        if i % 2 == 0:
            w_in = jnp.pad(gla_w_in[j], ((0, 0), (0, GLA_RANK_PAD - GLA_GATE_RANK))).astype(bf)
            w_gu = jnp.pad(gla_w_gate_up[j], ((0, GLA_RANK_PAD - GLA_GATE_RANK), (0, 0))).astype(bf)
            x, h2_tm, ri, rwt, cnt = _gla_layer_call(
                x, mod, gmix, gffn, w_in, w_gu, gla_b_gate[j].reshape(1, GLA_QK),
                gla_norm_g[j].reshape(1, GLA_DV), gla_w_out[j].astype(bf),
                rw_cat, rw_hi, rbias)
        else:
            x, h2_tm, ri, rwt, cnt = _lru_layer_call(
                x, mod, gmix, gffn, lru_w_in[j].astype(bf), lru_conv_w[j],
                lru_conv_b[j].reshape(1, d), lru_w_r[j].astype(bf),
                lru_b_r[j].reshape(1, d), lru_w_i[j].astype(bf), lru_b_i[j].reshape(1, d),
                lru_lambda[j].reshape(1, d), lru_w_out[j].astype(bf), rw_cat, rw_hi, rbias)
        x = _moe_layer(x, h2_tm, ri, rwt, cnt, mod, fin_g, moe_w_gate[i].astype(bf),
                       moe_w_up[i].astype(bf), moe_w_down[i].astype(bf),
                       final_norm=(i == depth - 1))
    return x
```

```python
import functools

import jax
import jax.numpy as jnp
from jax import lax
from jax.experimental import pallas as pl
from jax.experimental.pallas import tpu as pltpu

D_MODEL = 1024
CHUNK = 64
EPS = 1e-6

GLA_HEADS = 4
GLA_DK = 128
GLA_DV = 256
GLA_QK = GLA_HEADS * GLA_DK
GLA_VD = GLA_HEADS * GLA_DV
GLA_GATE_RANK = 16
GLA_GATE_TAU = 16.0
GLA_RANK_PAD = 128
GLA_IN_PAD = 2 * GLA_QK + 2 * GLA_VD + GLA_RANK_PAD

LRU_BLOCKS = 4
LRU_BLOCK_W = D_MODEL // LRU_BLOCKS
CONV_W = 4
LRU_C = 8.0

N_EXPERTS = 16
N_GROUPS = 4
EPG = N_EXPERTS // N_GROUPS
D_EXPERT = 512

LANES = 128
SUBLANES = 8
ROWS_PER_TOKEN = D_MODEL // LANES

MIX_TS = 256
MIX_STREAMS = 2
MOE_TM = 288
MOE_UNROLL = 8
MOE_MOVE = 256
MOE_WRING = 3
VMEM_LIMIT = 56 * 1024 * 1024
MOE_VMEM_LIMIT = 60 * 1024 * 1024

def _dot(a, b):
    return jnp.dot(a, b, preferred_element_type=jnp.float32)


def _dot_nt(a, b):
    return lax.dot_general(a, b, (((1,), (1,)), ((), ())),
                           preferred_element_type=jnp.float32)


def _dot_tn(a, b):
    return lax.dot_general(a, b, (((0,), (0,)), ((), ())),
                           preferred_element_type=jnp.float32)


def _split_bf16(x):
    hi = x.astype(jnp.bfloat16)
    lo = (x - hi.astype(jnp.float32)).astype(jnp.bfloat16)
    return hi, lo


def _sigmoid(x):
    return 1.0 / (1.0 + jnp.exp(-x))


def _silu(x):
    return x * _sigmoid(x)


def _rms(x, g):
    inv = lax.rsqrt(jnp.mean(x * x, axis=-1, keepdims=True) + EPS)
    return x * inv * g


def _ada_body(c_ref, w_ref, b_ref, o_ref):
    c_hi, c_lo = _split_bf16(_silu(c_ref[...]))
    w_hi, w_lo = _split_bf16(w_ref[0])
    o_ref[0] = _dot(c_hi, w_hi) + (_dot(c_lo, w_hi) + _dot(c_hi, w_lo)) + b_ref[0]


def _ada_call(c, ada_w, ada_b):
    depth, d, n = ada_w.shape
    bsz = c.shape[0]
    tn = 1024
    return pl.pallas_call(
        _ada_body,
        grid=(depth, n // tn),
        in_specs=[
            pl.BlockSpec((bsz, d), lambda l, j: (0, 0)),
            pl.BlockSpec((1, d, tn), lambda l, j: (l, 0, j)),
            pl.BlockSpec((1, 1, tn), lambda l, j: (l, 0, j)),
        ],
        out_specs=pl.BlockSpec((1, bsz, tn), lambda l, j: (l, 0, j)),
        out_shape=jax.ShapeDtypeStruct((depth, bsz, n), jnp.float32),
        compiler_params=pltpu.CompilerParams(
            dimension_semantics=("arbitrary", "arbitrary"),
            vmem_limit_bytes=VMEM_LIMIT),
        name="ada",
    )(c, ada_w, ada_b.reshape(depth, 1, n))


def _pre_norm(x, mod_ref, g_ref, shift_row, scale_row):
    shift = mod_ref[0, shift_row:shift_row + 1, :]
    scale = mod_ref[0, scale_row:scale_row + 1, :]
    return _rms(x, g_ref[...]) * (1.0 + scale) + shift


def _group_partner(x, k, sub):
    n = x.shape[0]
    fwd = pltpu.roll(x, n - k, 0)
    back = pltpu.roll(x, EPG - k, 0)
    wrapped = (sub % EPG) + k >= EPG
    return jnp.where(wrapped, back, fwd), wrapped


def _route_and_emit(h2, rw_cat_ref, rw_hi_ref, rbias_ref, cnt_scr,
                    ri_ref, rwt_ref, cnt_ref):
    ts = h2.shape[0]
    h_hi, h_lo = _split_bf16(h2)
    p1 = _dot_nt(rw_cat_ref[...], h_hi)
    p2 = _dot_nt(rw_hi_ref[...], h_lo)
    logits = p1[0:N_EXPERTS] + p1[N_EXPERTS:2 * N_EXPERTS] + p2
    s = _sigmoid(logits)
    sel = s + rbias_ref[...]
    sub = lax.broadcasted_iota(jnp.int32, (N_EXPERTS, ts), 0)

    pair_best = None
    rank = jnp.zeros((N_EXPERTS, ts), jnp.float32)
    for k in range(1, EPG):
        p, wrapped = _group_partner(sel, k, sub)
        ps = sel + p
        pair_best = ps if pair_best is None else jnp.maximum(pair_best, ps)
        ahead = (p > sel) | ((p == sel) & wrapped)
        rank = rank + ahead.astype(jnp.float32)
    gscore = pair_best
    for k in range(1, EPG):
        p, _ = _group_partner(pair_best, k, sub)
        gscore = jnp.maximum(gscore, p)
    chosen = jnp.ones((N_EXPERTS, ts), jnp.bool_)
    grp = sub // EPG
    for j in range(1, N_GROUPS):
        other = pltpu.roll(gscore, N_EXPERTS - EPG * j, 0)
        other_is_later = grp + j < N_GROUPS
        chosen = chosen & ((gscore > other) | ((gscore == other) & other_is_later))
    m = chosen & (rank < 2.0)
    mf = m.astype(jnp.float32)
    sm = s * mf
    wgt = sm / jnp.sum(sm, axis=0, keepdims=True)

    row = lax.broadcasted_iota(jnp.int32, (ts, ts), 0)
    col = lax.broadcasted_iota(jnp.int32, (ts, ts), 1)
    before = (row < col).astype(jnp.bfloat16)
    carry = cnt_scr[...]
    rnk = _dot(mf.astype(jnp.bfloat16), before) + carry[:, 0:1]
    new_carry = carry + jnp.sum(mf, axis=1, keepdims=True)
    cnt_scr[...] = new_carry
    cnt_ref[0] = new_carry.astype(jnp.int32)

    eidx = sub.astype(jnp.float32)
    e_lo = jnp.min(jnp.where(m, eidx, 99.0), axis=0, keepdims=True)
    e_hi = jnp.max(jnp.where(m, eidx, -1.0), axis=0, keepdims=True)
    is_lo = m & (eidx == e_lo)
    is_hi = m & (eidx == e_hi)
    r_lo = jnp.sum(jnp.where(is_lo, rnk, 0.0), axis=0, keepdims=True)
    r_hi = jnp.sum(jnp.where(is_hi, rnk, 0.0), axis=0, keepdims=True)
    w_lo = jnp.sum(jnp.where(is_lo, wgt, 0.0), axis=0, keepdims=True)
    w_hi = jnp.sum(jnp.where(is_hi, wgt, 0.0), axis=0, keepdims=True)
    ri_ref[0, 0:1, :] = e_lo.astype(jnp.int32)
    ri_ref[0, 1:2, :] = r_lo.astype(jnp.int32)
    ri_ref[0, 2:3, :] = e_hi.astype(jnp.int32)
    ri_ref[0, 3:4, :] = r_hi.astype(jnp.int32)
    rwt_ref[0, 0:1, :] = w_lo
    rwt_ref[0, 1:2, :] = w_hi


def _post_mix(x, mix, mod_ref, gffn_ref, rw_cat_ref, rw_hi_ref, rbias_ref, cnt_scr,
              xo_ref, h2_ref, ri_ref, rwt_ref, cnt_ref):
    ts = x.shape[0]
    x_new = x + mod_ref[0, 2:3, :] * mix
    xo_ref[0] = x_new
    h2 = _pre_norm(x_new, mod_ref, gffn_ref, 3, 4)
    for c in range(ROWS_PER_TOKEN):
        h2_ref[pl.ds(c, ts, stride=ROWS_PER_TOKEN), :] = h2[:, c * LANES:(c + 1) * LANES]
    _route_and_emit(h2, rw_cat_ref, rw_hi_ref, rbias_ref, cnt_scr,
                    ri_ref, rwt_ref, cnt_ref)


def _mixer_io_specs(bsz, seq, ts):
    ns = MIX_STREAMS
    hb = bsz // ns
    nt = seq // ts
    in_specs = [
        pl.BlockSpec((ns, 1, ts, D_MODEL), lambda b, t: (0, b, t, 0)),
        pl.BlockSpec((ns, 1, 6, D_MODEL), lambda b, t: (0, b, 0, 0)),
    ]
    out_specs = [
        pl.BlockSpec((ns, 1, ts, D_MODEL), lambda b, t: (0, b, t, 0)),
        pl.BlockSpec((ns, ts * ROWS_PER_TOKEN, LANES), lambda b, t: (0, b * nt + t, 0)),
        pl.BlockSpec((ns, 1, 4, ts), lambda b, t: (0, b, 0, t)),
        pl.BlockSpec((ns, 1, 2, ts), lambda b, t: (0, b, 0, t)),
        pl.BlockSpec((ns, 1, N_EXPERTS, LANES), lambda b, t: (0, b, 0, 0)),
    ]
    out_shapes = [
        jax.ShapeDtypeStruct((ns, hb, seq, D_MODEL), jnp.float32),
        jax.ShapeDtypeStruct((ns, hb * seq * ROWS_PER_TOKEN, LANES), jnp.float32),
        jax.ShapeDtypeStruct((ns, hb, 4, seq), jnp.int32),
        jax.ShapeDtypeStruct((ns, hb, 2, seq), jnp.float32),
        jax.ShapeDtypeStruct((ns, hb, N_EXPERTS, LANES), jnp.int32),
    ]
    return in_specs, out_specs, out_shapes


def _mixer_unstream(outs, bsz, seq):
    x, h2_tm, ri, rwt, cnt = outs
    return (x.reshape(bsz, seq, D_MODEL), h2_tm.reshape(bsz * seq * ROWS_PER_TOKEN, LANES),
            ri.reshape(bsz, 4, seq), rwt.reshape(bsz, 2, seq),
            cnt.reshape(bsz, N_EXPERTS, LANES))


def _const_spec(shape):
    nd = len(shape)
    return pl.BlockSpec(shape, lambda b, t: (0,) * nd)


def _gla_body(x_ref, mod_ref, gmix_ref, gffn_ref, win_ref, wgu_ref, bg_ref, ng_ref,
              wout_ref, rw_cat_ref, rw_hi_ref, rbias_ref,
              xo_ref, h2_ref, ri_ref, rwt_ref, cnt_ref,
              state_scr, o_scr, cnt_scr):
    @pl.when(pl.program_id(1) == 0)
    def _():
        state_scr[...] = jnp.zeros_like(state_scr)
        cnt_scr[...] = jnp.zeros_like(cnt_scr)

    for st in range(MIX_STREAMS):
        _gla_stream(x_ref.at[st], mod_ref.at[st], gmix_ref, gffn_ref, win_ref, wgu_ref, bg_ref,
                    ng_ref, wout_ref, rw_cat_ref, rw_hi_ref, rbias_ref,
                    xo_ref.at[st], h2_ref.at[st], ri_ref.at[st], rwt_ref.at[st], cnt_ref.at[st],
                    state_scr.at[st], o_scr.at[st], cnt_scr.at[st])


def _gla_stream(x_ref, mod_ref, gmix_ref, gffn_ref, win_ref, wgu_ref, bg_ref, ng_ref,
                wout_ref, rw_cat_ref, rw_hi_ref, rbias_ref,
                xo_ref, h2_ref, ri_ref, rwt_ref, cnt_ref,
                state_scr, o_scr, cnt_scr):
    ts = x_ref.shape[1]
    nchunk = ts // CHUNK
    x = x_ref[0]
    h = _pre_norm(x, mod_ref, gmix_ref, 0, 1).astype(jnp.bfloat16)

    o_q, o_k, o_v, o_g, o_a = 0, GLA_QK, 2 * GLA_QK, 2 * GLA_QK + GLA_VD, 2 * GLA_QK + 2 * GLA_VD
    q = _dot(h, win_ref[:, o_q:o_k]) * (GLA_DK ** -0.5)
    k = _dot(h, win_ref[:, o_k:o_v])
    v = _dot(h, win_ref[:, o_v:o_g]).astype(jnp.bfloat16)
    a_lr = _dot(h, win_ref[:, o_a:o_a + GLA_RANK_PAD])

    a_hi, a_lo = _split_bf16(a_lr)
    z2 = _dot(jnp.concatenate([a_hi, a_lo], axis=0), wgu_ref[...])
    z = z2[0:ts] + z2[ts:2 * ts] + bg_ref[...]
    log_a = -(jnp.maximum(-z, 0.0) + jnp.log1p(jnp.exp(-jnp.abs(z)))) * (1.0 / GLA_GATE_TAU)

    row = lax.broadcasted_iota(jnp.int32, (ts, ts), 0)
    col = lax.broadcasted_iota(jnp.int32, (ts, ts), 1)
    tri = ((row // CHUNK == col // CHUNK) & (col <= row)).astype(jnp.bfloat16)
    l_hi, l_lo = _split_bf16(log_a)
    cum2 = _dot(tri, jnp.concatenate([l_hi, l_lo], axis=1))
    cum = cum2[:, 0:GLA_QK] + cum2[:, GLA_QK:2 * GLA_QK]

    q = q.astype(jnp.bfloat16)
    ng = ng_ref[...]
    kv_t, gammas = [], []
    for j in range(nchunk):
        r0 = j * CHUNK
        cum_j = cum[r0:r0 + CHUNK]
        total = cum_j[CHUNK - 1:CHUNK]
        k_dec = (k[r0:r0 + CHUNK] * jnp.exp(total - cum_j)).astype(jnp.bfloat16)
        gammas.append(jnp.exp(total))
        for hd in range(GLA_HEADS):
            ks = slice(hd * GLA_DK, (hd + 1) * GLA_DK)
            vs = slice(hd * GLA_DV, (hd + 1) * GLA_DV)
            kv_t.append(_dot_tn(v[r0:r0 + CHUNK, vs], k_dec[:, ks]))
    states = []
    for hd in range(GLA_HEADS):
        ks = slice(hd * GLA_DK, (hd + 1) * GLA_DK)
        s_cur = state_scr[hd]
        for j in range(nchunk):
            s_cur = s_cur * gammas[j][:, ks] + kv_t[j * GLA_HEADS + hd]
            states.append(s_cur.astype(jnp.bfloat16))
        state_scr[hd] = s_cur
    for hd in range(GLA_HEADS):
        ks = slice(hd * GLA_DK, (hd + 1) * GLA_DK)
        vs = slice(hd * GLA_DV, (hd + 1) * GLA_DV)
        for j in range(nchunk):
            r0 = j * CHUNK
            o = _dot_nt(q[r0:r0 + CHUNK, ks], states[hd * nchunk + j])
            o_scr[r0:r0 + CHUNK, vs] = _rms(o, ng)

    g = _dot(h, win_ref[:, o_g:o_a])
    og = (o_scr[...] * _silu(g)).astype(jnp.bfloat16)
    mix = _dot(og, wout_ref[...])
    _post_mix(x, mix, mod_ref, gffn_ref, rw_cat_ref, rw_hi_ref, rbias_ref, cnt_scr,
              xo_ref, h2_ref, ri_ref, rwt_ref, cnt_ref)


def _gla_layer_call(x, mod, gmix, gffn, w_in, w_gu, b_g, n_g, w_out, rw_cat, rw_hi, rbias):
    bsz, seq, d = x.shape
    ts = MIX_TS
    ns = MIX_STREAMS
    hb = bsz // ns
    io_in, out_specs, out_shapes = _mixer_io_specs(bsz, seq, ts)
    consts = (gmix, gffn, w_in, w_gu, b_g, n_g, w_out, rw_cat, rw_hi, rbias)
    outs = pl.pallas_call(
        _gla_body,
        grid=(hb, seq // ts),
        in_specs=io_in + [_const_spec(a.shape) for a in consts],
        out_specs=out_specs,
        out_shape=out_shapes,
        scratch_shapes=[
            pltpu.VMEM((ns, GLA_HEADS, GLA_DV, GLA_DK), jnp.float32),
            pltpu.VMEM((ns, ts, GLA_VD), jnp.float32),
            pltpu.VMEM((ns, N_EXPERTS, LANES), jnp.float32),
        ],
        compiler_params=pltpu.CompilerParams(
            dimension_semantics=("arbitrary", "arbitrary"),
            vmem_limit_bytes=VMEM_LIMIT),
        name="gla_layer",
    )(x.reshape(ns, hb, seq, d), mod.reshape(ns, hb, 6, d), *consts)
    return _mixer_unstream(outs, bsz, seq)


def _lru_body(x_ref, mod_ref, gmix_ref, gffn_ref, win_ref, cw_ref, cb_ref, wr_ref, br_ref,
              wi_ref, bi_ref, lam_ref, wout_ref, rw_cat_ref, rw_hi_ref, rbias_ref,
              xo_ref, h2_ref, ri_ref, rwt_ref, cnt_ref,
              conv_scr, hstate_scr, a_scr, b_scr, gelu_scr, cnt_scr):
    ns = MIX_STREAMS
    ts = x_ref.shape[2]
    w = D_MODEL

    @pl.when(pl.program_id(1) == 0)
    def _():
        conv_scr[:, 0:SUBLANES, :] = jnp.zeros((ns, SUBLANES, w), jnp.float32)
        hstate_scr[...] = jnp.zeros_like(hstate_scr)
        cnt_scr[...] = jnp.zeros_like(cnt_scr)

    for st in range(ns):
        _lru_pre_scan(x_ref.at[st], mod_ref.at[st], gmix_ref, win_ref, cw_ref, cb_ref, wr_ref,
                      br_ref, wi_ref, bi_ref, lam_ref,
                      conv_scr.at[st], a_scr.at[st], b_scr.at[st], gelu_scr.at[st])

    unroll = 4

    def group_step(gi, hprev):
        for u in range(unroll):
            r0 = pl.multiple_of(gi * (unroll * SUBLANES), unroll * SUBLANES) + u * SUBLANES
            nxt = []
            for st in range(ns):
                hs = (a_scr[st, pl.ds(r0, SUBLANES), :] * hprev[st]
                      + b_scr[st, pl.ds(r0, SUBLANES), :])
                b_scr[st, pl.ds(r0, SUBLANES), :] = hs
                nxt.append(hs[SUBLANES - 1:SUBLANES, :])
            hprev = tuple(nxt)
        return hprev

    hlast = lax.fori_loop(0, ts // (unroll * SUBLANES), group_step,
                          tuple(hstate_scr[st] for st in range(ns)))
    for st in range(ns):
        hstate_scr[st] = hlast[st]

    for st in range(ns):
        x = x_ref[st, 0]
        y = (b_scr[st] * gelu_scr[st]).astype(jnp.bfloat16)
        mix = _dot(y, wout_ref[...])
        _post_mix(x, mix, mod_ref.at[st], gffn_ref, rw_cat_ref, rw_hi_ref, rbias_ref,
                  cnt_scr.at[st], xo_ref.at[st], h2_ref.at[st], ri_ref.at[st], rwt_ref.at[st],
                  cnt_ref.at[st])


def _lru_pre_scan(x_ref, mod_ref, gmix_ref, win_ref, cw_ref, cb_ref, wr_ref, br_ref,
                  wi_ref, bi_ref, lam_ref, conv_scr, a_scr, b_scr, gelu_scr):
    ts = x_ref.shape[1]
    w = D_MODEL
    x = x_ref[0]
    h = _pre_norm(x, mod_ref, gmix_ref, 0, 1).astype(jnp.bfloat16)
    gate_br = _dot(h, win_ref[:, 0:w])
    gelu_scr[...] = 0.5 * gate_br * (1.0 + jnp.tanh(
        0.7978845608028654 * (gate_br + 0.044715 * gate_br * gate_br * gate_br)))
    xb = _dot(h, win_ref[:, w:2 * w])

    conv_scr[SUBLANES:SUBLANES + ts, :] = xb
    xc = cb_ref[...] + xb * cw_ref[CONV_W - 1:CONV_W, :]
    for j in range(CONV_W - 1):
        back = CONV_W - 1 - j
        xc = xc + conv_scr[SUBLANES - back:SUBLANES - back + ts, :] * cw_ref[j:j + 1, :]
    conv_scr[0:SUBLANES, :] = conv_scr[ts:ts + SUBLANES, :]

    xcb = xc.astype(jnp.bfloat16)
    rs, iis = [], []
    for hd in range(LRU_BLOCKS):
        sl = slice(hd * LRU_BLOCK_W, (hd + 1) * LRU_BLOCK_W)
        rs.append(_dot(xcb[:, sl], wr_ref[hd]))
        iis.append(_dot(xcb[:, sl], wi_ref[hd]))
    r = _sigmoid(jnp.concatenate(rs, axis=1) + br_ref[...])
    ig = _sigmoid(jnp.concatenate(iis, axis=1) + bi_ref[...])

    lam = lam_ref[...]
    softplus_neg_lam = jnp.maximum(-lam, 0.0) + jnp.log1p(jnp.exp(-jnp.abs(lam)))
    log_a = (-LRU_C * r) * softplus_neg_lam
    a = jnp.exp(log_a)
    mult = jnp.sqrt(-jnp.tanh(log_a) * (a * a + 1.0))
    bb = (xc * ig) * mult

    a = a.reshape(ts // SUBLANES, SUBLANES, w)
    bb = bb.reshape(ts // SUBLANES, SUBLANES, w)
    rowi = lax.broadcasted_iota(jnp.int32, a.shape, 1)
    for d in (1, 2, 4):
        keep = rowi >= d
        a_sh = jnp.where(keep, pltpu.roll(a, d, 1), 1.0)
        b_sh = jnp.where(keep, pltpu.roll(bb, d, 1), 0.0)
        bb = a * b_sh + bb
        a = a * a_sh
    a_scr[...] = a.reshape(ts, w)
    b_scr[...] = bb.reshape(ts, w)


def _lru_layer_call(x, mod, gmix, gffn, w_in, conv_w, conv_b, w_r, b_r, w_i, b_i, lam, w_out,
                    rw_cat, rw_hi, rbias):
    bsz, seq, d = x.shape
    ts = MIX_TS
    ns = MIX_STREAMS
    hb = bsz // ns
    io_in, out_specs, out_shapes = _mixer_io_specs(bsz, seq, ts)
    consts = (gmix, gffn, w_in, conv_w, conv_b, w_r, b_r, w_i, b_i, lam, w_out,
              rw_cat, rw_hi, rbias)
    outs = pl.pallas_call(
        _lru_body,
        grid=(hb, seq // ts),
        in_specs=io_in + [_const_spec(a.shape) for a in consts],
        out_specs=out_specs,
        out_shape=out_shapes,
        scratch_shapes=[
            pltpu.VMEM((ns, SUBLANES + ts, d), jnp.float32),
            pltpu.VMEM((ns, 1, d), jnp.float32),
            pltpu.VMEM((ns, ts, d), jnp.float32),
            pltpu.VMEM((ns, ts, d), jnp.float32),
            pltpu.VMEM((ns, ts, d), jnp.float32),
            pltpu.VMEM((ns, N_EXPERTS, LANES), jnp.float32),
        ],
        compiler_params=pltpu.CompilerParams(
            dimension_semantics=("arbitrary", "arbitrary"),
            vmem_limit_bytes=VMEM_LIMIT),
        name="lru_layer",
    )(x.reshape(ns, hb, seq, d), mod.reshape(ns, hb, 6, d), *consts)
    return _mixer_unstream(outs, bsz, seq)


def _pos_body(ri_ref, cnt_ref, pos_ref, off_ref):
    cnt = cnt_ref[0].astype(jnp.float32)
    seq = ri_ref.shape[2]
    e_lo = ri_ref[0, 0:1, :].astype(jnp.float32)
    e_hi = ri_ref[0, 2:3, :].astype(jnp.float32)
    p_lo = ri_ref[0, 1:2, :].astype(jnp.float32)
    p_hi = ri_ref[0, 3:4, :].astype(jnp.float32)
    off = jnp.zeros((1, LANES), jnp.float32)
    for e in range(N_EXPERTS):
        off_ref[0, e:e + 1, :] = off.astype(jnp.int32)
        off_b = jnp.broadcast_to(off[:, 0:1], (1, seq))
        p_lo = p_lo + jnp.where(e_lo == float(e), off_b, 0.0)
        p_hi = p_hi + jnp.where(e_hi == float(e), off_b, 0.0)
        off = off + cnt[e:e + 1, :]
    pos_ref[0, 0:1, :] = p_lo.astype(jnp.int32) * ROWS_PER_TOKEN
    pos_ref[0, 1:2, :] = p_hi.astype(jnp.int32) * ROWS_PER_TOKEN


def _pos_call(ri, cnt):
    bsz, _, seq = ri.shape
    return pl.pallas_call(
        _pos_body,
        grid=(bsz,),
        in_specs=[
            pl.BlockSpec((1, 4, seq), lambda b: (b, 0, 0)),
            pl.BlockSpec((1, N_EXPERTS, LANES), lambda b: (b, 0, 0)),
        ],
        out_specs=[
            pl.BlockSpec((1, 2, seq), lambda b: (b, 0, 0)),
            pl.BlockSpec((1, N_EXPERTS, LANES), lambda b: (b, 0, 0)),
        ],
        out_shape=[
            jax.ShapeDtypeStruct((bsz, 2, seq), jnp.int32),
            jax.ShapeDtypeStruct((bsz, N_EXPERTS, LANES), jnp.int32),
        ],
        compiler_params=pltpu.CompilerParams(dimension_semantics=("arbitrary",)),
        name="moe_pos",
    )(ri, cnt)


def _scatter_tokens(pos_ref, h2_ref, slots, idx0, lo, cnt):
    R = ROWS_PER_TOKEN
    for u in range(lo, lo + cnt):
        p0 = pos_ref[0, 0, idx0 + 2 * u]
        p1 = pos_ref[0, 0, idx0 + 2 * u + 1]
        val = h2_ref[u * R:(u + 1) * R, :]
        slots[pl.ds(pl.multiple_of(p0, R), R), :] = val
        slots[pl.ds(pl.multiple_of(p1, R), R), :] = val


def _combine_tokens(pos_ref, wts_ref, slots, tm_scr, idx0, lo, cnt):
    R = ROWS_PER_TOKEN
    for u in range(lo, lo + cnt):
        p0 = pos_ref[0, 0, idx0 + 2 * u]
        p1 = pos_ref[0, 0, idx0 + 2 * u + 1]
        w0 = wts_ref[0, 0, idx0 + 2 * u]
        za = slots[pl.ds(pl.multiple_of(p0, R), R), :]
        zb = slots[pl.ds(pl.multiple_of(p1, R), R), :]
        tm_scr[u * R:(u + 1) * R, :] = zb + w0 * (za - zb)


def _combine_epilogue(tm_scr, x_ref, mod_ref, fin_ref, o_ref, lo, cnt, final_norm):
    R = ROWS_PER_TOKEN
    moe = jnp.concatenate(
        [tm_scr[pl.ds(lo * R + c, cnt, stride=R), :] for c in range(R)], axis=1)
    out = x_ref[lo:lo + cnt, :] + mod_ref[0, 5:6, :] * moe
    if final_norm:
        out = _rms(out, fin_ref[...])
    o_ref[lo:lo + cnt, :] = out


def _expert_tile_hooked(slots, wg_ref, wu_ref, wd_ref, ws, slot0, n_valid, tm, hooks):
    R = ROWS_PER_TOKEN
    nw = 256
    row0 = pl.multiple_of(slot0 * R, R)
    xs = [slots[pl.ds(row0 + c, tm, stride=R), :] for c in range(R)]
    xt = jnp.concatenate(xs, axis=1).astype(jnp.bfloat16)
    ok = lax.broadcasted_iota(jnp.int32, (tm, LANES), 0) < n_valid
    hooks = list(hooks)

    def run_hook():
        if hooks:
            hooks.pop(0)()

    gate, up = [], []
    for i in range(D_EXPERT // nw):
        gate.append(_dot(xt, wg_ref[ws, :, i * nw:(i + 1) * nw]))
        run_hook()
    for i in range(D_EXPERT // nw):
        up.append(_dot(xt, wu_ref[ws, :, i * nw:(i + 1) * nw]))
        run_hook()
    he = (_silu(jnp.concatenate(gate, axis=1)) * jnp.concatenate(up, axis=1)).astype(jnp.bfloat16)
    for i in range(D_MODEL // nw):
        y = _dot(he, wd_ref[ws, :, i * nw:(i + 1) * nw])
        for cc in range(nw // LANES):
            c = i * (nw // LANES) + cc
            slots[pl.ds(row0 + c, tm, stride=R), :] = jnp.where(
                ok, y[:, cc * LANES:(cc + 1) * LANES], xs[c])
        run_hook()
    while hooks:
        run_hook()


def _moe_body(off_sm, cnt_sm, pos_sc_ref, pos_cb_ref, wts_ref, h2_ref, x_ref, mod_ref, fin_ref,
              wg_hbm, wu_hbm, wd_hbm, o_ref, slots_a, slots_b, tm_scr,
              wg_ref, wu_ref, wd_ref, w_sem, *, n_blocks, final_norm):
    r = pl.program_id(0)
    e = pl.program_id(1)

    g = r * N_EXPERTS + e
    n_steps = (n_blocks + 2) * N_EXPERTS

    def w_needed(step):
        rnd = step // N_EXPERTS
        return (rnd >= 1) & (rnd <= n_blocks) & (step < n_steps)

    def w_copies(step):
        ex = step % N_EXPERTS
        sl = step % MOE_WRING
        return [pltpu.make_async_copy(src.at[ex], dst.at[sl], w_sem.at[i, sl])
                for i, (src, dst) in enumerate(((wg_hbm, wg_ref), (wu_hbm, wu_ref),
                                                (wd_hbm, wd_ref)))]

    def w_start(step):
        @pl.when(w_needed(step))
        def _():
            for cp in w_copies(step):
                cp.start()

    @pl.when(g == 0)
    def _():
        for ahead in range(MOE_WRING - 1):
            w_start(g + ahead)

    w_start(g + MOE_WRING - 1)

    @pl.when(w_needed(g))
    def _():
        for cp in w_copies(g):
            cp.wait()

    ws = g % MOE_WRING
    R = ROWS_PER_TOKEN
    U = MOE_UNROLL
    tm = MOE_TM
    mv = MOE_MOVE
    half = N_EXPERTS // 2
    n_slots = 2 * half * mv
    n_hooks = 2 * (D_EXPERT // 256) + D_MODEL // 256
    per_hook = mv // n_hooks

    @pl.when((r == 0) & (e == 0))
    def _():
        zeros = jnp.zeros((tm * R, LANES), jnp.float32)
        slots_a[n_slots * R:(n_slots + tm) * R, :] = zeros
        slots_b[n_slots * R:(n_slots + tm) * R, :] = zeros

    do_exp = (r >= 1) & (r <= n_blocks)
    do_comb = (r >= 2) & (e < half)
    do_scat = (r < n_blocks) & (e >= half)
    b_exp = jnp.clip(r - 1, 0, n_blocks - 1)
    off = off_sm[b_exp, e]
    n = cnt_sm[b_exp, e]
    one_tile = do_exp & (n > tm // 2) & (n <= tm)
    comb_idx0 = 2 * (e * mv)
    scat_idx0 = 2 * ((e - half) * mv)

    def round_body(exp_buf, mov_buf):
        tile = functools.partial(_expert_tile_hooked, exp_buf, wg_ref, wu_ref, wd_ref, ws)

        @pl.when(one_tile & do_comb)
        def _():
            def hook(k):
                def run():
                    if k < n_hooks:
                        _combine_tokens(pos_cb_ref, wts_ref, mov_buf, tm_scr, comb_idx0,
                                        k * per_hook, per_hook)
                    if k >= 1:
                        _combine_epilogue(tm_scr, x_ref, mod_ref, fin_ref, o_ref,
                                          (k - 1) * per_hook, per_hook, final_norm)
                return run
            tile(off, n, tm, [hook(k) for k in range(n_hooks + 1)])

        @pl.when(one_tile & do_scat)
        def _():
            def hook(k):
                return lambda: _scatter_tokens(pos_sc_ref, h2_ref, mov_buf, scat_idx0,
                                               k * per_hook, per_hook)
            tile(off, n, tm, [hook(k) for k in range(n_hooks)])

        @pl.when(jnp.logical_not(one_tile & (do_comb | do_scat)))
        def _():
            @pl.when(do_comb)
            def _():
                def body(i, carry):
                    i0 = comb_idx0 + 2 * U * i
                    for u in range(U):
                        p0 = pos_cb_ref[0, 0, i0 + 2 * u]
                        p1 = pos_cb_ref[0, 0, i0 + 2 * u + 1]
                        w0 = wts_ref[0, 0, i0 + 2 * u]
                        za = mov_buf[pl.ds(pl.multiple_of(p0, R), R), :]
                        zb = mov_buf[pl.ds(pl.multiple_of(p1, R), R), :]
                        r0 = pl.multiple_of(i * (U * R), U * R) + u * R
                        tm_scr[pl.ds(r0, R), :] = zb + w0 * (za - zb)
                    return carry

                lax.fori_loop(0, mv // U, body, 0)
                _combine_epilogue(tm_scr, x_ref, mod_ref, fin_ref, o_ref, 0, mv, final_norm)

            @pl.when(do_scat)
            def _():
                def body(i, carry):
                    i0 = scat_idx0 + 2 * U * i
                    for u in range(U):
                        p0 = pos_sc_ref[0, 0, i0 + 2 * u]
                        p1 = pos_sc_ref[0, 0, i0 + 2 * u + 1]
                        r0 = pl.multiple_of(i * (U * R), U * R) + u * R
                        val = h2_ref[pl.ds(r0, R), :]
                        mov_buf[pl.ds(pl.multiple_of(p0, R), R), :] = val
                        mov_buf[pl.ds(pl.multiple_of(p1, R), R), :] = val
                    return carry

                lax.fori_loop(0, mv // U, body, 0)

            @pl.when(do_exp)
            def _():
                def tile_body(j, carry):
                    tile(off + j * tm, n - j * tm, tm, [])
                    return carry

                lax.fori_loop(0, (n + tm - 1) // tm, tile_body, 0)

    @pl.when(r % 2 == 0)
    def _():
        round_body(slots_b, slots_a)

    @pl.when(r % 2 == 1)
    def _():
        round_body(slots_a, slots_b)


def _moe_call(off, cnt, pos, wts, h2_tm, x, mod, fin_g, w_gate, w_up, w_down, final_norm):
    bsz, seq, d = x.shape
    mv = MOE_MOVE
    half = N_EXPERTS // 2
    assert seq == half * mv
    n_slots = 2 * seq
    last = bsz - 1

    def scat_blk(r, e, *_):
        return (jnp.minimum(r, last) * half + jnp.clip(e - half, 0, half - 1), 0)

    def comb_blk(r, e, *_):
        return (jnp.where(r < 2, 0, (r - 2) * half + jnp.minimum(e, half - 1)), 0)

    def scat_row(r, e, *_):
        return (jnp.minimum(r, last), 0, 0)

    def comb_row(r, e, *_):
        return (jnp.clip(r - 2, 0, last), 0, 0)

    body = functools.partial(_moe_body, n_blocks=bsz, final_norm=final_norm)
    slot_rows = (n_slots + MOE_TM) * ROWS_PER_TOKEN
    out = pl.pallas_call(
        body,
        grid_spec=pltpu.PrefetchScalarGridSpec(
            num_scalar_prefetch=2,
            grid=(bsz + 2, N_EXPERTS),
            in_specs=[
                pl.BlockSpec((1, 1, 2 * seq), scat_row, memory_space=pltpu.SMEM),
                pl.BlockSpec((1, 1, 2 * seq), comb_row, memory_space=pltpu.SMEM),
                pl.BlockSpec((1, 1, 2 * seq), comb_row, memory_space=pltpu.SMEM),
                pl.BlockSpec((mv * ROWS_PER_TOKEN, LANES), scat_blk),
                pl.BlockSpec((mv, d), comb_blk),
                pl.BlockSpec((1, 6, d), comb_row),
                pl.BlockSpec((1, d), lambda r, e, *_: (0, 0)),
                pl.BlockSpec(memory_space=pl.ANY),
                pl.BlockSpec(memory_space=pl.ANY),
                pl.BlockSpec(memory_space=pl.ANY),
            ],
            out_specs=pl.BlockSpec((mv, d), comb_blk),
            scratch_shapes=[
                pltpu.VMEM((slot_rows, LANES), jnp.float32),
                pltpu.VMEM((slot_rows, LANES), jnp.float32),
                pltpu.VMEM((mv * ROWS_PER_TOKEN, LANES), jnp.float32),
                pltpu.VMEM((MOE_WRING, d, D_EXPERT), jnp.bfloat16),
                pltpu.VMEM((MOE_WRING, d, D_EXPERT), jnp.bfloat16),
                pltpu.VMEM((MOE_WRING, D_EXPERT, d), jnp.bfloat16),
                pltpu.SemaphoreType.DMA((3, MOE_WRING)),
            ],
        ),
        out_shape=jax.ShapeDtypeStruct((bsz * seq, d), jnp.float32),
        compiler_params=pltpu.CompilerParams(
            dimension_semantics=("arbitrary", "arbitrary"),
            vmem_limit_bytes=MOE_VMEM_LIMIT),
        name="moe",
    )(off, cnt, pos, pos, wts, h2_tm, x.reshape(bsz * seq, d), mod, fin_g, w_gate, w_up, w_down)
    return out.reshape(bsz, seq, d)


def _moe_layer(x, h2_tm, ri, rwt, cnt, mod, fin_g, w_gate, w_up, w_down, final_norm):
    pos, off = _pos_call(ri, cnt)
    bsz, _, seq = pos.shape
    pos = pos.transpose(0, 2, 1).reshape(bsz, 1, 2 * seq)
    rwt = rwt.transpose(0, 2, 1).reshape(bsz, 1, 2 * seq)
    return _moe_call(off[:, :, 0], cnt[:, :, 0], pos, rwt, h2_tm, x, mod, fin_g,
                     w_gate, w_up, w_down, final_norm)


def kernel(x, c, gla_w_in, gla_w_gate_up, gla_b_gate, gla_norm_g, gla_w_out, lru_w_in, lru_conv_w, lru_conv_b, lru_w_r, lru_b_r, lru_w_i, lru_b_i, lru_lambda, lru_w_out, router_w, router_bias, moe_w_gate, moe_w_up, moe_w_down, norm_mix_g, norm_ffn_g, ada_w, ada_b, final_norm_g):
    bf = jnp.bfloat16
    depth = ada_w.shape[0]
    bsz = x.shape[0]
    d = D_MODEL
    mod_all = _ada_call(c, ada_w, ada_b).reshape(depth, bsz, 6, d)

    rw_t = router_w.T
    rw_hi = rw_t.astype(bf)
    rw_lo = (rw_t - rw_hi.astype(jnp.float32)).astype(bf)
    rw_cat = jnp.concatenate([rw_hi, rw_lo], axis=0)
    rbias = router_bias.reshape(N_EXPERTS, 1)
    fin_g = final_norm_g.reshape(1, d)

    for i in range(depth):
        j = i // 2
        mod = mod_all[i]
        gmix = norm_mix_g[i].reshape(1, d)
        gffn = norm_ffn_g[i].reshape(1, d)
        if i % 2 == 0:
            w_in = jnp.pad(gla_w_in[j], ((0, 0), (0, GLA_RANK_PAD - GLA_GATE_RANK))).astype(bf)
            w_gu = jnp.pad(gla_w_gate_up[j], ((0, GLA_RANK_PAD - GLA_GATE_RANK), (0, 0))).astype(bf)
            x, h2_tm, ri, rwt, cnt = _gla_layer_call(
                x, mod, gmix, gffn, w_in, w_gu, gla_b_gate[j].reshape(1, GLA_QK),
                gla_norm_g[j].reshape(1, GLA_DV), gla_w_out[j].astype(bf),
                rw_cat, rw_hi, rbias)
        else:
            x, h2_tm, ri, rwt, cnt = _lru_layer_call(
                x, mod, gmix, gffn, lru_w_in[j].astype(bf), lru_conv_w[j],
                lru_conv_b[j].reshape(1, d), lru_w_r[j].astype(bf),
                lru_b_r[j].reshape(1, d), lru_w_i[j].astype(bf), lru_b_i[j].reshape(1, d),
                lru_lambda[j].reshape(1, d), lru_w_out[j].astype(bf), rw_cat, rw_hi, rbias)
        x = _moe_layer(x, h2_tm, ri, rwt, cnt, mod, fin_g, moe_w_gate[i].astype(bf),
                       moe_w_up[i].astype(bf), moe_w_down[i].astype(bf),
                       final_norm=(i == depth - 1))
    return x
```

```python
import functools

import jax
import jax.numpy as jnp
from jax import lax
from jax.experimental import pallas as pl
from jax.experimental.pallas import tpu as pltpu

D_MODEL = 1024
CHUNK = 64
EPS = 1e-6

GLA_HEADS = 4
GLA_DK = 128
GLA_DV = 256
GLA_QK = GLA_HEADS * GLA_DK
GLA_VD = GLA_HEADS * GLA_DV
GLA_GATE_RANK = 16
GLA_GATE_TAU = 16.0
GLA_RANK_PAD = 128
GLA_IN_PAD = 2 * GLA_QK + 2 * GLA_VD + GLA_RANK_PAD

LRU_BLOCKS = 4
LRU_BLOCK_W = D_MODEL // LRU_BLOCKS
CONV_W = 4
LRU_C = 8.0

N_EXPERTS = 16
N_GROUPS = 4
EPG = N_EXPERTS // N_GROUPS
D_EXPERT = 512

LANES = 128
SUBLANES = 8
ROWS_PER_TOKEN = D_MODEL // LANES

MIX_TS = 256
MIX_STREAMS = 2
MOE_TM = 256
MOE_UNROLL = 8
MOE_MOVE = 256
MOE_WRING = 3
VMEM_LIMIT = 56 * 1024 * 1024
MOE_VMEM_LIMIT = 60 * 1024 * 1024

def _dot(a, b):
    return jnp.dot(a, b, preferred_element_type=jnp.float32)


def _dot_nt(a, b):
    return lax.dot_general(a, b, (((1,), (1,)), ((), ())),
                           preferred_element_type=jnp.float32)


def _dot_tn(a, b):
    return lax.dot_general(a, b, (((0,), (0,)), ((), ())),
                           preferred_element_type=jnp.float32)


def _split_bf16(x):
    hi = x.astype(jnp.bfloat16)
    lo = (x - hi.astype(jnp.float32)).astype(jnp.bfloat16)
    return hi, lo


def _sigmoid(x):
    return 1.0 / (1.0 + jnp.exp(-x))


def _silu(x):
    return x * _sigmoid(x)


def _rms(x, g):
    inv = lax.rsqrt(jnp.mean(x * x, axis=-1, keepdims=True) + EPS)
    return x * inv * g


def _ada_body(c_ref, w_ref, b_ref, o_ref):
    c_hi, c_lo = _split_bf16(_silu(c_ref[...]))
    w_hi, w_lo = _split_bf16(w_ref[0])
    o_ref[0] = _dot(c_hi, w_hi) + (_dot(c_lo, w_hi) + _dot(c_hi, w_lo)) + b_ref[0]


def _ada_call(c, ada_w, ada_b):
    depth, d, n = ada_w.shape
    bsz = c.shape[0]
    tn = 1024
    return pl.pallas_call(
        _ada_body,
        grid=(depth, n // tn),
        in_specs=[
            pl.BlockSpec((bsz, d), lambda l, j: (0, 0)),
            pl.BlockSpec((1, d, tn), lambda l, j: (l, 0, j)),
            pl.BlockSpec((1, 1, tn), lambda l, j: (l, 0, j)),
        ],
        out_specs=pl.BlockSpec((1, bsz, tn), lambda l, j: (l, 0, j)),
        out_shape=jax.ShapeDtypeStruct((depth, bsz, n), jnp.float32),
        compiler_params=pltpu.CompilerParams(
            dimension_semantics=("arbitrary", "arbitrary"),
            vmem_limit_bytes=VMEM_LIMIT),
        name="ada",
    )(c, ada_w, ada_b.reshape(depth, 1, n))


def _pre_norm(x, mod_ref, g_ref, shift_row, scale_row):
    shift = mod_ref[0, shift_row:shift_row + 1, :]
    scale = mod_ref[0, scale_row:scale_row + 1, :]
    return _rms(x, g_ref[...]) * (1.0 + scale) + shift


def _group_partner(x, k, sub):
    n = x.shape[0]
    fwd = pltpu.roll(x, n - k, 0)
    back = pltpu.roll(x, EPG - k, 0)
    wrapped = (sub % EPG) + k >= EPG
    return jnp.where(wrapped, back, fwd), wrapped


def _route_and_emit(h2, rw_cat_ref, rw_hi_ref, rbias_ref, cnt_scr,
                    ri_ref, rwt_ref, cnt_ref):
    ts = h2.shape[0]
    h_hi, h_lo = _split_bf16(h2)
    p1 = _dot_nt(rw_cat_ref[...], h_hi)
    p2 = _dot_nt(rw_hi_ref[...], h_lo)
    logits = p1[0:N_EXPERTS] + p1[N_EXPERTS:2 * N_EXPERTS] + p2
    s = _sigmoid(logits)
    sel = s + rbias_ref[...]
    sub = lax.broadcasted_iota(jnp.int32, (N_EXPERTS, ts), 0)

    pair_best = None
    rank = jnp.zeros((N_EXPERTS, ts), jnp.float32)
    for k in range(1, EPG):
        p, wrapped = _group_partner(sel, k, sub)
        ps = sel + p
        pair_best = ps if pair_best is None else jnp.maximum(pair_best, ps)
        ahead = (p > sel) | ((p == sel) & wrapped)
        rank = rank + ahead.astype(jnp.float32)
    gscore = pair_best
    for k in range(1, EPG):
        p, _ = _group_partner(pair_best, k, sub)
        gscore = jnp.maximum(gscore, p)
    chosen = jnp.ones((N_EXPERTS, ts), jnp.bool_)
    grp = sub // EPG
    for j in range(1, N_GROUPS):
        other = pltpu.roll(gscore, N_EXPERTS - EPG * j, 0)
        other_is_later = grp + j < N_GROUPS
        chosen = chosen & ((gscore > other) | ((gscore == other) & other_is_later))
    m = chosen & (rank < 2.0)
    mf = m.astype(jnp.float32)
    sm = s * mf
    wgt = sm / jnp.sum(sm, axis=0, keepdims=True)

    row = lax.broadcasted_iota(jnp.int32, (ts, ts), 0)
    col = lax.broadcasted_iota(jnp.int32, (ts, ts), 1)
    before = (row < col).astype(jnp.bfloat16)
    carry = cnt_scr[...]
    rnk = _dot(mf.astype(jnp.bfloat16), before) + carry[:, 0:1]
    new_carry = carry + jnp.sum(mf, axis=1, keepdims=True)
    cnt_scr[...] = new_carry
    cnt_ref[0] = new_carry.astype(jnp.int32)

    eidx = sub.astype(jnp.float32)
    e_lo = jnp.min(jnp.where(m, eidx, 99.0), axis=0, keepdims=True)
    e_hi = jnp.max(jnp.where(m, eidx, -1.0), axis=0, keepdims=True)
    is_lo = m & (eidx == e_lo)
    is_hi = m & (eidx == e_hi)
    r_lo = jnp.sum(jnp.where(is_lo, rnk, 0.0), axis=0, keepdims=True)
    r_hi = jnp.sum(jnp.where(is_hi, rnk, 0.0), axis=0, keepdims=True)
    w_lo = jnp.sum(jnp.where(is_lo, wgt, 0.0), axis=0, keepdims=True)
    w_hi = jnp.sum(jnp.where(is_hi, wgt, 0.0), axis=0, keepdims=True)
    ri_ref[0, 0:1, :] = e_lo.astype(jnp.int32)
    ri_ref[0, 1:2, :] = r_lo.astype(jnp.int32)
    ri_ref[0, 2:3, :] = e_hi.astype(jnp.int32)
    ri_ref[0, 3:4, :] = r_hi.astype(jnp.int32)
    rwt_ref[0, 0:1, :] = w_lo
    rwt_ref[0, 1:2, :] = w_hi


def _post_mix(x, mix, mod_ref, gffn_ref, rw_cat_ref, rw_hi_ref, rbias_ref, cnt_scr,
              xo_ref, h2_ref, ri_ref, rwt_ref, cnt_ref):
    ts = x.shape[0]
    x_new = x + mod_ref[0, 2:3, :] * mix
    xo_ref[0] = x_new
    h2 = _pre_norm(x_new, mod_ref, gffn_ref, 3, 4)
    for c in range(ROWS_PER_TOKEN):
        h2_ref[pl.ds(c, ts, stride=ROWS_PER_TOKEN), :] = h2[:, c * LANES:(c + 1) * LANES]
    _route_and_emit(h2, rw_cat_ref, rw_hi_ref, rbias_ref, cnt_scr,
                    ri_ref, rwt_ref, cnt_ref)


def _mixer_io_specs(bsz, seq, ts):
    ns = MIX_STREAMS
    hb = bsz // ns
    nt = seq // ts
    in_specs = [
        pl.BlockSpec((ns, 1, ts, D_MODEL), lambda b, t: (0, b, t, 0)),
        pl.BlockSpec((ns, 1, 6, D_MODEL), lambda b, t: (0, b, 0, 0)),
    ]
    out_specs = [
        pl.BlockSpec((ns, 1, ts, D_MODEL), lambda b, t: (0, b, t, 0)),
        pl.BlockSpec((ns, ts * ROWS_PER_TOKEN, LANES), lambda b, t: (0, b * nt + t, 0)),
        pl.BlockSpec((ns, 1, 4, ts), lambda b, t: (0, b, 0, t)),
        pl.BlockSpec((ns, 1, 2, ts), lambda b, t: (0, b, 0, t)),
        pl.BlockSpec((ns, 1, N_EXPERTS, LANES), lambda b, t: (0, b, 0, 0)),
    ]
    out_shapes = [
        jax.ShapeDtypeStruct((ns, hb, seq, D_MODEL), jnp.float32),
        jax.ShapeDtypeStruct((ns, hb * seq * ROWS_PER_TOKEN, LANES), jnp.float32),
        jax.ShapeDtypeStruct((ns, hb, 4, seq), jnp.int32),
        jax.ShapeDtypeStruct((ns, hb, 2, seq), jnp.float32),
        jax.ShapeDtypeStruct((ns, hb, N_EXPERTS, LANES), jnp.int32),
    ]
    return in_specs, out_specs, out_shapes


def _mixer_unstream(outs, bsz, seq):
    x, h2_tm, ri, rwt, cnt = outs
    return (x.reshape(bsz, seq, D_MODEL), h2_tm.reshape(bsz * seq * ROWS_PER_TOKEN, LANES),
            ri.reshape(bsz, 4, seq), rwt.reshape(bsz, 2, seq),
            cnt.reshape(bsz, N_EXPERTS, LANES))


def _const_spec(shape):
    nd = len(shape)
    return pl.BlockSpec(shape, lambda b, t: (0,) * nd)


def _gla_body(x_ref, mod_ref, gmix_ref, gffn_ref, win_ref, wgu_ref, bg_ref, ng_ref,
              wout_ref, rw_cat_ref, rw_hi_ref, rbias_ref,
              xo_ref, h2_ref, ri_ref, rwt_ref, cnt_ref,
              state_scr, o_scr, cnt_scr):
    @pl.when(pl.program_id(1) == 0)
    def _():
        state_scr[...] = jnp.zeros_like(state_scr)
        cnt_scr[...] = jnp.zeros_like(cnt_scr)

    for st in range(MIX_STREAMS):
        _gla_stream(x_ref.at[st], mod_ref.at[st], gmix_ref, gffn_ref, win_ref, wgu_ref, bg_ref,
                    ng_ref, wout_ref, rw_cat_ref, rw_hi_ref, rbias_ref,
                    xo_ref.at[st], h2_ref.at[st], ri_ref.at[st], rwt_ref.at[st], cnt_ref.at[st],
                    state_scr.at[st], o_scr.at[st], cnt_scr.at[st])


def _gla_stream(x_ref, mod_ref, gmix_ref, gffn_ref, win_ref, wgu_ref, bg_ref, ng_ref,
                wout_ref, rw_cat_ref, rw_hi_ref, rbias_ref,
                xo_ref, h2_ref, ri_ref, rwt_ref, cnt_ref,
                state_scr, o_scr, cnt_scr):
    ts = x_ref.shape[1]
    nchunk = ts // CHUNK
    x = x_ref[0]
    h = _pre_norm(x, mod_ref, gmix_ref, 0, 1).astype(jnp.bfloat16)

    o_q, o_k, o_v, o_g, o_a = 0, GLA_QK, 2 * GLA_QK, 2 * GLA_QK + GLA_VD, 2 * GLA_QK + 2 * GLA_VD
    q = _dot(h, win_ref[:, o_q:o_k]) * (GLA_DK ** -0.5)
    k = _dot(h, win_ref[:, o_k:o_v])
    v = _dot(h, win_ref[:, o_v:o_g]).astype(jnp.bfloat16)
    a_lr = _dot(h, win_ref[:, o_a:o_a + GLA_RANK_PAD])

    a_hi, a_lo = _split_bf16(a_lr)
    z2 = _dot(jnp.concatenate([a_hi, a_lo], axis=0), wgu_ref[...])
    z = z2[0:ts] + z2[ts:2 * ts] + bg_ref[...]
    log_a = -(jnp.maximum(-z, 0.0) + jnp.log1p(jnp.exp(-jnp.abs(z)))) * (1.0 / GLA_GATE_TAU)

    row = lax.broadcasted_iota(jnp.int32, (ts, ts), 0)
    col = lax.broadcasted_iota(jnp.int32, (ts, ts), 1)
    tri = ((row // CHUNK == col // CHUNK) & (col <= row)).astype(jnp.bfloat16)
    l_hi, l_lo = _split_bf16(log_a)
    cum2 = _dot(tri, jnp.concatenate([l_hi, l_lo], axis=1))
    cum = cum2[:, 0:GLA_QK] + cum2[:, GLA_QK:2 * GLA_QK]

    q = q.astype(jnp.bfloat16)
    ng = ng_ref[...]
    kv_t, gammas = [], []
    for j in range(nchunk):
        r0 = j * CHUNK
        cum_j = cum[r0:r0 + CHUNK]
        total = cum_j[CHUNK - 1:CHUNK]
        k_dec = (k[r0:r0 + CHUNK] * jnp.exp(total - cum_j)).astype(jnp.bfloat16)
        gammas.append(jnp.exp(total))
        for hd in range(GLA_HEADS):
            ks = slice(hd * GLA_DK, (hd + 1) * GLA_DK)
            vs = slice(hd * GLA_DV, (hd + 1) * GLA_DV)
            kv_t.append(_dot_tn(v[r0:r0 + CHUNK, vs], k_dec[:, ks]))
    states = []
    for hd in range(GLA_HEADS):
        ks = slice(hd * GLA_DK, (hd + 1) * GLA_DK)
        s_cur = state_scr[hd]
        for j in range(nchunk):
            s_cur = s_cur * gammas[j][:, ks] + kv_t[j * GLA_HEADS + hd]
            states.append(s_cur.astype(jnp.bfloat16))
        state_scr[hd] = s_cur
    for hd in range(GLA_HEADS):
        ks = slice(hd * GLA_DK, (hd + 1) * GLA_DK)
        vs = slice(hd * GLA_DV, (hd + 1) * GLA_DV)
        for j in range(nchunk):
            r0 = j * CHUNK
            o = _dot_nt(q[r0:r0 + CHUNK, ks], states[hd * nchunk + j])
            o_scr[r0:r0 + CHUNK, vs] = _rms(o, ng)

    g = _dot(h, win_ref[:, o_g:o_a])
    og = (o_scr[...] * _silu(g)).astype(jnp.bfloat16)
    mix = _dot(og, wout_ref[...])
    _post_mix(x, mix, mod_ref, gffn_ref, rw_cat_ref, rw_hi_ref, rbias_ref, cnt_scr,
              xo_ref, h2_ref, ri_ref, rwt_ref, cnt_ref)


def _gla_layer_call(x, mod, gmix, gffn, w_in, w_gu, b_g, n_g, w_out, rw_cat, rw_hi, rbias):
    bsz, seq, d = x.shape
    ts = MIX_TS
    ns = MIX_STREAMS
    hb = bsz // ns
    io_in, out_specs, out_shapes = _mixer_io_specs(bsz, seq, ts)
    consts = (gmix, gffn, w_in, w_gu, b_g, n_g, w_out, rw_cat, rw_hi, rbias)
    outs = pl.pallas_call(
        _gla_body,
        grid=(hb, seq // ts),
        in_specs=io_in + [_const_spec(a.shape) for a in consts],
        out_specs=out_specs,
        out_shape=out_shapes,
        scratch_shapes=[
            pltpu.VMEM((ns, GLA_HEADS, GLA_DV, GLA_DK), jnp.float32),
            pltpu.VMEM((ns, ts, GLA_VD), jnp.float32),
            pltpu.VMEM((ns, N_EXPERTS, LANES), jnp.float32),
        ],
        compiler_params=pltpu.CompilerParams(
            dimension_semantics=("arbitrary", "arbitrary"),
            vmem_limit_bytes=VMEM_LIMIT),
        name="gla_layer",
    )(x.reshape(ns, hb, seq, d), mod.reshape(ns, hb, 6, d), *consts)
    return _mixer_unstream(outs, bsz, seq)


def _lru_body(x_ref, mod_ref, gmix_ref, gffn_ref, win_ref, cw_ref, cb_ref, wr_ref, br_ref,
              wi_ref, bi_ref, lam_ref, wout_ref, rw_cat_ref, rw_hi_ref, rbias_ref,
              xo_ref, h2_ref, ri_ref, rwt_ref, cnt_ref,
              conv_scr, hstate_scr, a_scr, b_scr, gelu_scr, cnt_scr):
    ns = MIX_STREAMS
    ts = x_ref.shape[2]
    w = D_MODEL

    @pl.when(pl.program_id(1) == 0)
    def _():
        conv_scr[:, 0:SUBLANES, :] = jnp.zeros((ns, SUBLANES, w), jnp.float32)
        hstate_scr[...] = jnp.zeros_like(hstate_scr)
        cnt_scr[...] = jnp.zeros_like(cnt_scr)

    for st in range(ns):
        _lru_pre_scan(x_ref.at[st], mod_ref.at[st], gmix_ref, win_ref, cw_ref, cb_ref, wr_ref,
                      br_ref, wi_ref, bi_ref, lam_ref,
                      conv_scr.at[st], a_scr.at[st], b_scr.at[st], gelu_scr.at[st])

    unroll = 4

    def group_step(gi, hprev):
        for u in range(unroll):
            r0 = pl.multiple_of(gi * (unroll * SUBLANES), unroll * SUBLANES) + u * SUBLANES
            nxt = []
            for st in range(ns):
                hs = (a_scr[st, pl.ds(r0, SUBLANES), :] * hprev[st]
                      + b_scr[st, pl.ds(r0, SUBLANES), :])
                b_scr[st, pl.ds(r0, SUBLANES), :] = hs
                nxt.append(hs[SUBLANES - 1:SUBLANES, :])
            hprev = tuple(nxt)
        return hprev

    hlast = lax.fori_loop(0, ts // (unroll * SUBLANES), group_step,
                          tuple(hstate_scr[st] for st in range(ns)))
    for st in range(ns):
        hstate_scr[st] = hlast[st]

    for st in range(ns):
        x = x_ref[st, 0]
        y = (b_scr[st] * gelu_scr[st]).astype(jnp.bfloat16)
        mix = _dot(y, wout_ref[...])
        _post_mix(x, mix, mod_ref.at[st], gffn_ref, rw_cat_ref, rw_hi_ref, rbias_ref,
                  cnt_scr.at[st], xo_ref.at[st], h2_ref.at[st], ri_ref.at[st], rwt_ref.at[st],
                  cnt_ref.at[st])


def _lru_pre_scan(x_ref, mod_ref, gmix_ref, win_ref, cw_ref, cb_ref, wr_ref, br_ref,
                  wi_ref, bi_ref, lam_ref, conv_scr, a_scr, b_scr, gelu_scr):
    ts = x_ref.shape[1]
    w = D_MODEL
    x = x_ref[0]
    h = _pre_norm(x, mod_ref, gmix_ref, 0, 1).astype(jnp.bfloat16)
    gate_br = _dot(h, win_ref[:, 0:w])
    gelu_scr[...] = 0.5 * gate_br * (1.0 + jnp.tanh(
        0.7978845608028654 * (gate_br + 0.044715 * gate_br * gate_br * gate_br)))
    xb = _dot(h, win_ref[:, w:2 * w])

    conv_scr[SUBLANES:SUBLANES + ts, :] = xb
    xc = cb_ref[...] + xb * cw_ref[CONV_W - 1:CONV_W, :]
    for j in range(CONV_W - 1):
        back = CONV_W - 1 - j
        xc = xc + conv_scr[SUBLANES - back:SUBLANES - back + ts, :] * cw_ref[j:j + 1, :]
    conv_scr[0:SUBLANES, :] = conv_scr[ts:ts + SUBLANES, :]

    xcb = xc.astype(jnp.bfloat16)
    rs, iis = [], []
    for hd in range(LRU_BLOCKS):
        sl = slice(hd * LRU_BLOCK_W, (hd + 1) * LRU_BLOCK_W)
        rs.append(_dot(xcb[:, sl], wr_ref[hd]))
        iis.append(_dot(xcb[:, sl], wi_ref[hd]))
    r = _sigmoid(jnp.concatenate(rs, axis=1) + br_ref[...])
    ig = _sigmoid(jnp.concatenate(iis, axis=1) + bi_ref[...])

    lam = lam_ref[...]
    softplus_neg_lam = jnp.maximum(-lam, 0.0) + jnp.log1p(jnp.exp(-jnp.abs(lam)))
    log_a = (-LRU_C * r) * softplus_neg_lam
    a = jnp.exp(log_a)
    mult = jnp.sqrt(-jnp.tanh(log_a) * (a * a + 1.0))
    bb = (xc * ig) * mult

    a = a.reshape(ts // SUBLANES, SUBLANES, w)
    bb = bb.reshape(ts // SUBLANES, SUBLANES, w)
    rowi = lax.broadcasted_iota(jnp.int32, a.shape, 1)
    for d in (1, 2, 4):
        keep = rowi >= d
        a_sh = jnp.where(keep, pltpu.roll(a, d, 1), 1.0)
        b_sh = jnp.where(keep, pltpu.roll(bb, d, 1), 0.0)
        bb = a * b_sh + bb
        a = a * a_sh
    a_scr[...] = a.reshape(ts, w)
    b_scr[...] = bb.reshape(ts, w)


def _lru_layer_call(x, mod, gmix, gffn, w_in, conv_w, conv_b, w_r, b_r, w_i, b_i, lam, w_out,
                    rw_cat, rw_hi, rbias):
    bsz, seq, d = x.shape
    ts = MIX_TS
    ns = MIX_STREAMS
    hb = bsz // ns
    io_in, out_specs, out_shapes = _mixer_io_specs(bsz, seq, ts)
    consts = (gmix, gffn, w_in, conv_w, conv_b, w_r, b_r, w_i, b_i, lam, w_out,
              rw_cat, rw_hi, rbias)
    outs = pl.pallas_call(
        _lru_body,
        grid=(hb, seq // ts),
        in_specs=io_in + [_const_spec(a.shape) for a in consts],
        out_specs=out_specs,
        out_shape=out_shapes,
        scratch_shapes=[
            pltpu.VMEM((ns, SUBLANES + ts, d), jnp.float32),
            pltpu.VMEM((ns, 1, d), jnp.float32),
            pltpu.VMEM((ns, ts, d), jnp.float32),
            pltpu.VMEM((ns, ts, d), jnp.float32),
            pltpu.VMEM((ns, ts, d), jnp.float32),
            pltpu.VMEM((ns, N_EXPERTS, LANES), jnp.float32),
        ],
        compiler_params=pltpu.CompilerParams(
            dimension_semantics=("arbitrary", "arbitrary"),
            vmem_limit_bytes=VMEM_LIMIT),
        name="lru_layer",
    )(x.reshape(ns, hb, seq, d), mod.reshape(ns, hb, 6, d), *consts)
    return _mixer_unstream(outs, bsz, seq)


def _pos_body(ri_ref, cnt_ref, pos_ref, off_ref):
    cnt = cnt_ref[0].astype(jnp.float32)
    seq = ri_ref.shape[2]
    e_lo = ri_ref[0, 0:1, :].astype(jnp.float32)
    e_hi = ri_ref[0, 2:3, :].astype(jnp.float32)
    p_lo = ri_ref[0, 1:2, :].astype(jnp.float32)
    p_hi = ri_ref[0, 3:4, :].astype(jnp.float32)
    off = jnp.zeros((1, LANES), jnp.float32)
    for e in range(N_EXPERTS):
        off_ref[0, e:e + 1, :] = off.astype(jnp.int32)
        off_b = jnp.broadcast_to(off[:, 0:1], (1, seq))
        p_lo = p_lo + jnp.where(e_lo == float(e), off_b, 0.0)
        p_hi = p_hi + jnp.where(e_hi == float(e), off_b, 0.0)
        off = off + cnt[e:e + 1, :]
    pos_ref[0, 0:1, :] = p_lo.astype(jnp.int32) * ROWS_PER_TOKEN
    pos_ref[0, 1:2, :] = p_hi.astype(jnp.int32) * ROWS_PER_TOKEN


def _pos_call(ri, cnt):
    bsz, _, seq = ri.shape
    return pl.pallas_call(
        _pos_body,
        grid=(bsz,),
        in_specs=[
            pl.BlockSpec((1, 4, seq), lambda b: (b, 0, 0)),
            pl.BlockSpec((1, N_EXPERTS, LANES), lambda b: (b, 0, 0)),
        ],
        out_specs=[
            pl.BlockSpec((1, 2, seq), lambda b: (b, 0, 0)),
            pl.BlockSpec((1, N_EXPERTS, LANES), lambda b: (b, 0, 0)),
        ],
        out_shape=[
            jax.ShapeDtypeStruct((bsz, 2, seq), jnp.int32),
            jax.ShapeDtypeStruct((bsz, N_EXPERTS, LANES), jnp.int32),
        ],
        compiler_params=pltpu.CompilerParams(dimension_semantics=("arbitrary",)),
        name="moe_pos",
    )(ri, cnt)


def _scatter_tokens(pos_ref, h2_ref, slots, idx0, lo, cnt):
    R = ROWS_PER_TOKEN
    for u in range(lo, lo + cnt):
        p0 = pos_ref[0, 0, idx0 + 2 * u]
        p1 = pos_ref[0, 0, idx0 + 2 * u + 1]
        val = h2_ref[u * R:(u + 1) * R, :]
        slots[pl.ds(pl.multiple_of(p0, R), R), :] = val
        slots[pl.ds(pl.multiple_of(p1, R), R), :] = val


def _combine_tokens(pos_ref, wts_ref, slots, tm_scr, idx0, lo, cnt):
    R = ROWS_PER_TOKEN
    for u in range(lo, lo + cnt):
        p0 = pos_ref[0, 0, idx0 + 2 * u]
        p1 = pos_ref[0, 0, idx0 + 2 * u + 1]
        w0 = wts_ref[0, 0, idx0 + 2 * u]
        za = slots[pl.ds(pl.multiple_of(p0, R), R), :]
        zb = slots[pl.ds(pl.multiple_of(p1, R), R), :]
        tm_scr[u * R:(u + 1) * R, :] = zb + w0 * (za - zb)


def _combine_epilogue(tm_scr, x_ref, mod_ref, fin_ref, o_ref, lo, cnt, final_norm):
    R = ROWS_PER_TOKEN
    moe = jnp.concatenate(
        [tm_scr[pl.ds(lo * R + c, cnt, stride=R), :] for c in range(R)], axis=1)
    out = x_ref[lo:lo + cnt, :] + mod_ref[0, 5:6, :] * moe
    if final_norm:
        out = _rms(out, fin_ref[...])
    o_ref[lo:lo + cnt, :] = out


def _expert_tile_hooked(slots, wg_ref, wu_ref, wd_ref, ws, slot0, n_valid, tm, hooks):
    R = ROWS_PER_TOKEN
    nw = 256
    row0 = pl.multiple_of(slot0 * R, R)
    xs = [slots[pl.ds(row0 + c, tm, stride=R), :] for c in range(R)]
    xt = jnp.concatenate(xs, axis=1).astype(jnp.bfloat16)
    ok = lax.broadcasted_iota(jnp.int32, (tm, LANES), 0) < n_valid
    hooks = list(hooks)

    def run_hook():
        if hooks:
            hooks.pop(0)()

    gate, up = [], []
    for i in range(D_EXPERT // nw):
        gate.append(_dot(xt, wg_ref[ws, :, i * nw:(i + 1) * nw]))
        run_hook()
    for i in range(D_EXPERT // nw):
        up.append(_dot(xt, wu_ref[ws, :, i * nw:(i + 1) * nw]))
        run_hook()
    he = (_silu(jnp.concatenate(gate, axis=1)) * jnp.concatenate(up, axis=1)).astype(jnp.bfloat16)
    for i in range(D_MODEL // nw):
        y = _dot(he, wd_ref[ws, :, i * nw:(i + 1) * nw])
        for cc in range(nw // LANES):
            c = i * (nw // LANES) + cc
            slots[pl.ds(row0 + c, tm, stride=R), :] = jnp.where(
                ok, y[:, cc * LANES:(cc + 1) * LANES], xs[c])
        run_hook()
    while hooks:
        run_hook()


def _moe_body(off_sm, cnt_sm, pos_sc_ref, pos_cb_ref, wts_ref, h2_ref, x_ref, mod_ref, fin_ref,
              wg_hbm, wu_hbm, wd_hbm, o_ref, slots_a, slots_b, tm_scr,
              wg_ref, wu_ref, wd_ref, w_sem, *, n_blocks, final_norm):
    r = pl.program_id(0)
    e = pl.program_id(1)

    g = r * N_EXPERTS + e
    n_steps = (n_blocks + 2) * N_EXPERTS

    def w_needed(step):
        rnd = step // N_EXPERTS
        return (rnd >= 1) & (rnd <= n_blocks) & (step < n_steps)

    def w_copies(step):
        ex = step % N_EXPERTS
        sl = step % MOE_WRING
        return [pltpu.make_async_copy(src.at[ex], dst.at[sl], w_sem.at[i, sl])
                for i, (src, dst) in enumerate(((wg_hbm, wg_ref), (wu_hbm, wu_ref),
                                                (wd_hbm, wd_ref)))]

    def w_start(step):
        @pl.when(w_needed(step))
        def _():
            for cp in w_copies(step):
                cp.start()

    @pl.when(g == 0)
    def _():
        for ahead in range(MOE_WRING - 1):
            w_start(g + ahead)

    w_start(g + MOE_WRING - 1)

    @pl.when(w_needed(g))
    def _():
        for cp in w_copies(g):
            cp.wait()

    ws = g % MOE_WRING
    R = ROWS_PER_TOKEN
    U = MOE_UNROLL
    tm = MOE_TM
    mv = MOE_MOVE
    half = N_EXPERTS // 2
    n_slots = 2 * half * mv
    n_hooks = 2 * (D_EXPERT // 256) + D_MODEL // 256
    per_hook = mv // n_hooks

    @pl.when((r == 0) & (e == 0))
    def _():
        zeros = jnp.zeros((tm * R, LANES), jnp.float32)
        slots_a[n_slots * R:(n_slots + tm) * R, :] = zeros
        slots_b[n_slots * R:(n_slots + tm) * R, :] = zeros

    do_exp = (r >= 1) & (r <= n_blocks)
    do_comb = (r >= 2) & (e < half)
    do_scat = (r < n_blocks) & (e >= half)
    b_exp = jnp.clip(r - 1, 0, n_blocks - 1)
    off = off_sm[b_exp, e]
    n = jnp.where(do_exp, cnt_sm[b_exp, e], 0)
    comb_idx0 = 2 * (e * mv)
    scat_idx0 = 2 * ((e - half) * mv)
    small = tm // 2
    first_rows = jnp.where(n > small, tm, jnp.where(n > 0, small, 0))
    fused = do_comb | do_scat
    start = jnp.where(fused, first_rows, 0)
    rest = jnp.maximum(n - start, 0)
    rest_big = rest // tm + ((rest % tm) > small).astype(jnp.int32)
    rest_tail = rest - rest_big * tm

    def round_body(exp_buf, mov_buf):
        tile = functools.partial(_expert_tile_hooked, exp_buf, wg_ref, wu_ref, wd_ref, ws)

        def comb_hooks():
            def hook(k):
                def run():
                    if k < n_hooks:
                        _combine_tokens(pos_cb_ref, wts_ref, mov_buf, tm_scr, comb_idx0,
                                        k * per_hook, per_hook)
                    if k >= 1:
                        _combine_epilogue(tm_scr, x_ref, mod_ref, fin_ref, o_ref,
                                          (k - 1) * per_hook, per_hook, final_norm)
                return run
            return [hook(k) for k in range(n_hooks + 1)]

        def scat_hooks():
            def hook(k):
                return lambda: _scatter_tokens(pos_sc_ref, h2_ref, mov_buf, scat_idx0,
                                               k * per_hook, per_hook)
            return [hook(k) for k in range(n_hooks)]

        for rows in (tm, small):
            @pl.when((first_rows == rows) & do_comb)
            def _():
                tile(off, n, rows, comb_hooks())

            @pl.when((first_rows == rows) & do_scat)
            def _():
                tile(off, n, rows, scat_hooks())

        @pl.when(first_rows == 0)
        def _():
            @pl.when(do_comb)
            def _():
                def body(i, carry):
                    i0 = comb_idx0 + 2 * U * i
                    for u in range(U):
                        p0 = pos_cb_ref[0, 0, i0 + 2 * u]
                        p1 = pos_cb_ref[0, 0, i0 + 2 * u + 1]
                        w0 = wts_ref[0, 0, i0 + 2 * u]
                        za = mov_buf[pl.ds(pl.multiple_of(p0, R), R), :]
                        zb = mov_buf[pl.ds(pl.multiple_of(p1, R), R), :]
                        r0 = pl.multiple_of(i * (U * R), U * R) + u * R
                        tm_scr[pl.ds(r0, R), :] = zb + w0 * (za - zb)
                    return carry

                lax.fori_loop(0, mv // U, body, 0)
                _combine_epilogue(tm_scr, x_ref, mod_ref, fin_ref, o_ref, 0, mv, final_norm)

            @pl.when(do_scat)
            def _():
                def body(i, carry):
                    i0 = scat_idx0 + 2 * U * i
                    for u in range(U):
                        p0 = pos_sc_ref[0, 0, i0 + 2 * u]
                        p1 = pos_sc_ref[0, 0, i0 + 2 * u + 1]
                        r0 = pl.multiple_of(i * (U * R), U * R) + u * R
                        val = h2_ref[pl.ds(r0, R), :]
                        mov_buf[pl.ds(pl.multiple_of(p0, R), R), :] = val
                        mov_buf[pl.ds(pl.multiple_of(p1, R), R), :] = val
                    return carry

                lax.fori_loop(0, mv // U, body, 0)

        def tile_body(j, carry):
            done = start + j * tm
            tile(off + done, n - done, tm, [])
            return carry

        lax.fori_loop(0, rest_big, tile_body, 0)

        @pl.when(rest_tail > 0)
        def _():
            done = start + rest_big * tm
            tile(off + done, n - done, small, [])

    @pl.when(r % 2 == 0)
    def _():
        round_body(slots_b, slots_a)

    @pl.when(r % 2 == 1)
    def _():
        round_body(slots_a, slots_b)


def _moe_call(off, cnt, pos, wts, h2_tm, x, mod, fin_g, w_gate, w_up, w_down, final_norm):
    bsz, seq, d = x.shape
    mv = MOE_MOVE
    half = N_EXPERTS // 2
    assert seq == half * mv
    n_slots = 2 * seq
    last = bsz - 1

    def scat_blk(r, e, *_):
        return (jnp.minimum(r, last) * half + jnp.clip(e - half, 0, half - 1), 0)

    def comb_blk(r, e, *_):
        return (jnp.where(r < 2, 0, (r - 2) * half + jnp.minimum(e, half - 1)), 0)

    def scat_row(r, e, *_):
        return (jnp.minimum(r, last), 0, 0)

    def comb_row(r, e, *_):
        return (jnp.clip(r - 2, 0, last), 0, 0)

    body = functools.partial(_moe_body, n_blocks=bsz, final_norm=final_norm)
    slot_rows = (n_slots + MOE_TM) * ROWS_PER_TOKEN
    out = pl.pallas_call(
        body,
        grid_spec=pltpu.PrefetchScalarGridSpec(
            num_scalar_prefetch=2,
            grid=(bsz + 2, N_EXPERTS),
            in_specs=[
                pl.BlockSpec((1, 1, 2 * seq), scat_row, memory_space=pltpu.SMEM),
                pl.BlockSpec((1, 1, 2 * seq), comb_row, memory_space=pltpu.SMEM),
                pl.BlockSpec((1, 1, 2 * seq), comb_row, memory_space=pltpu.SMEM),
                pl.BlockSpec((mv * ROWS_PER_TOKEN, LANES), scat_blk),
                pl.BlockSpec((mv, d), comb_blk),
                pl.BlockSpec((1, 6, d), comb_row),
                pl.BlockSpec((1, d), lambda r, e, *_: (0, 0)),
                pl.BlockSpec(memory_space=pl.ANY),
                pl.BlockSpec(memory_space=pl.ANY),
                pl.BlockSpec(memory_space=pl.ANY),
            ],
            out_specs=pl.BlockSpec((mv, d), comb_blk),
            scratch_shapes=[
                pltpu.VMEM((slot_rows, LANES), jnp.float32),
                pltpu.VMEM((slot_rows, LANES), jnp.float32),
                pltpu.VMEM((mv * ROWS_PER_TOKEN, LANES), jnp.float32),
                pltpu.VMEM((MOE_WRING, d, D_EXPERT), jnp.bfloat16),
                pltpu.VMEM((MOE_WRING, d, D_EXPERT), jnp.bfloat16),
                pltpu.VMEM((MOE_WRING, D_EXPERT, d), jnp.bfloat16),
                pltpu.SemaphoreType.DMA((3, MOE_WRING)),
            ],
        ),
        out_shape=jax.ShapeDtypeStruct((bsz * seq, d), jnp.float32),
        compiler_params=pltpu.CompilerParams(
            dimension_semantics=("arbitrary", "arbitrary"),
            vmem_limit_bytes=MOE_VMEM_LIMIT),
        name="moe",
    )(off, cnt, pos, pos, wts, h2_tm, x.reshape(bsz * seq, d), mod, fin_g, w_gate, w_up, w_down)
    return out.reshape(bsz, seq, d)


def _moe_layer(x, h2_tm, ri, rwt, cnt, mod, fin_g, w_gate, w_up, w_down, final_norm):
    pos, off = _pos_call(ri, cnt)
    bsz, _, seq = pos.shape
    pos = pos.transpose(0, 2, 1).reshape(bsz, 1, 2 * seq)
    rwt = rwt.transpose(0, 2, 1).reshape(bsz, 1, 2 * seq)
    return _moe_call(off[:, :, 0], cnt[:, :, 0], pos, rwt, h2_tm, x, mod, fin_g,
                     w_gate, w_up, w_down, final_norm)


def kernel(x, c, gla_w_in, gla_w_gate_up, gla_b_gate, gla_norm_g, gla_w_out, lru_w_in, lru_conv_w, lru_conv_b, lru_w_r, lru_b_r, lru_w_i, lru_b_i, lru_lambda, lru_w_out, router_w, router_bias, moe_w_gate, moe_w_up, moe_w_down, norm_mix_g, norm_ffn_g, ada_w, ada_b, final_norm_g):
    bf = jnp.bfloat16
    depth = ada_w.shape[0]
    bsz = x.shape[0]
    d = D_MODEL
    mod_all = _ada_call(c, ada_w, ada_b).reshape(depth, bsz, 6, d)

    rw_t = router_w.T
    rw_hi = rw_t.astype(bf)
    rw_lo = (rw_t - rw_hi.astype(jnp.float32)).astype(bf)
    rw_cat = jnp.concatenate([rw_hi, rw_lo], axis=0)
    rbias = router_bias.reshape(N_EXPERTS, 1)
    fin_g = final_norm_g.reshape(1, d)

    for i in range(depth):
        j = i // 2
        mod = mod_all[i]
        gmix = norm_mix_g[i].reshape(1, d)
        gffn = norm_ffn_g[i].reshape(1, d)
        if i % 2 == 0:
            w_in = jnp.pad(gla_w_in[j], ((0, 0), (0, GLA_RANK_PAD - GLA_GATE_RANK))).astype(bf)
            w_gu = jnp.pad(gla_w_gate_up[j], ((0, GLA_RANK_PAD - GLA_GATE_RANK), (0, 0))).astype(bf)
            x, h2_tm, ri, rwt, cnt = _gla_layer_call(
                x, mod, gmix, gffn, w_in, w_gu, gla_b_gate[j].reshape(1, GLA_QK),
                gla_norm_g[j].reshape(1, GLA_DV), gla_w_out[j].astype(bf),
                rw_cat, rw_hi, rbias)
        else:
            x, h2_tm, ri, rwt, cnt = _lru_layer_call(
                x, mod, gmix, gffn, lru_w_in[j].astype(bf), lru_conv_w[j],
                lru_conv_b[j].reshape(1, d), lru_w_r[j].astype(bf),
                lru_b_r[j].reshape(1, d), lru_w_i[j].astype(bf), lru_b_i[j].reshape(1, d),
                lru_lambda[j].reshape(1, d), lru_w_out[j].astype(bf), rw_cat, rw_hi, rbias)
        x = _moe_layer(x, h2_tm, ri, rwt, cnt, mod, fin_g, moe_w_gate[i].astype(bf),
                       moe_w_up[i].astype(bf), moe_w_down[i].astype(bf),
                       final_norm=(i == depth - 1))
    return x
```

```python
import functools

import jax
import jax.numpy as jnp
from jax import lax
from jax.experimental import pallas as pl
from jax.experimental.pallas import tpu as pltpu

D_MODEL = 1024
CHUNK = 64
EPS = 1e-6

GLA_HEADS = 4
GLA_DK = 128
GLA_DV = 256
GLA_QK = GLA_HEADS * GLA_DK
GLA_VD = GLA_HEADS * GLA_DV
GLA_GATE_RANK = 16
GLA_GATE_TAU = 16.0
GLA_RANK_PAD = 128
GLA_IN_PAD = 2 * GLA_QK + 2 * GLA_VD + GLA_RANK_PAD

LRU_BLOCKS = 4
LRU_BLOCK_W = D_MODEL // LRU_BLOCKS
CONV_W = 4
LRU_C = 8.0

N_EXPERTS = 16
N_GROUPS = 4
EPG = N_EXPERTS // N_GROUPS
D_EXPERT = 512

LANES = 128
SUBLANES = 8
ROWS_PER_TOKEN = D_MODEL // LANES

MIX_TS = 256
MIX_STREAMS = 2
ROUTE_BLOCK = 256
MOE_TM = 256
MOE_UNROLL = 8
MOE_MOVE = 256
MOE_WRING = 3
VMEM_LIMIT = 56 * 1024 * 1024
MOE_VMEM_LIMIT = 60 * 1024 * 1024


def _dot(a, b):
    return jnp.dot(a, b, preferred_element_type=jnp.float32)


def _dot_nt(a, b):
    return lax.dot_general(a, b, (((1,), (1,)), ((), ())),
                           preferred_element_type=jnp.float32)


def _dot_tn(a, b):
    return lax.dot_general(a, b, (((0,), (0,)), ((), ())),
                           preferred_element_type=jnp.float32)


def _split_bf16(x):
    hi = x.astype(jnp.bfloat16)
    lo = (x - hi.astype(jnp.float32)).astype(jnp.bfloat16)
    return hi, lo


def _sigmoid(x):
    return 0.5 * jnp.tanh(0.5 * x) + 0.5


def _sigmoid_rel(x):
    return 1.0 / (1.0 + jnp.exp(-x))


def _silu(x):
    return x * _sigmoid(x)


def _rms(x, g):
    inv = lax.rsqrt(jnp.mean(x * x, axis=-1, keepdims=True) + EPS)
    return x * inv * g


def _ada_body(c_ref, w_ref, b_ref, o_ref):
    c_hi, c_lo = _split_bf16(_silu(c_ref[...]))
    w_hi, w_lo = _split_bf16(w_ref[0])
    o_ref[0] = _dot(c_hi, w_hi) + (_dot(c_lo, w_hi) + _dot(c_hi, w_lo)) + b_ref[0]


def _ada_call(c, ada_w, ada_b):
    depth, d, n = ada_w.shape
    bsz = c.shape[0]
    tn = 1024
    return pl.pallas_call(
        _ada_body,
        grid=(depth, n // tn),
        in_specs=[
            pl.BlockSpec((bsz, d), lambda l, j: (0, 0)),
            pl.BlockSpec((1, d, tn), lambda l, j: (l, 0, j)),
            pl.BlockSpec((1, 1, tn), lambda l, j: (l, 0, j)),
        ],
        out_specs=pl.BlockSpec((1, bsz, tn), lambda l, j: (l, 0, j)),
        out_shape=jax.ShapeDtypeStruct((depth, bsz, n), jnp.float32),
        compiler_params=pltpu.CompilerParams(
            dimension_semantics=("arbitrary", "arbitrary"),
            vmem_limit_bytes=VMEM_LIMIT),
        name="ada",
    )(c, ada_w, ada_b.reshape(depth, 1, n))


def _pre_norm(x, mod_ref, g_ref, shift_row, scale_row):
    shift = mod_ref[0, shift_row:shift_row + 1, :]
    scale = mod_ref[0, scale_row:scale_row + 1, :]
    return _rms(x, g_ref[...]) * (1.0 + scale) + shift


def _group_partner(x, k, sub):
    n = x.shape[0]
    fwd = pltpu.roll(x, n - k, 0)
    back = pltpu.roll(x, EPG - k, 0)
    wrapped = (sub % EPG) + k >= EPG
    return jnp.where(wrapped, back, fwd), wrapped


def _router_logits(h2, rw_cat_ref, rw_hi_ref):
    h_hi, h_lo = _split_bf16(h2)
    p1 = _dot_nt(rw_cat_ref[...], h_hi)
    p2 = _dot_nt(rw_hi_ref[...], h_lo)
    return p1[0:N_EXPERTS] + p1[N_EXPERTS:2 * N_EXPERTS] + p2


def _route_select(s, rbias):
    n = s.shape[1]
    sel = s + rbias
    sub = lax.broadcasted_iota(jnp.int32, (N_EXPERTS, n), 0)

    pair_best = None
    rank = jnp.zeros((N_EXPERTS, n), jnp.float32)
    for k in range(1, EPG):
        p, wrapped = _group_partner(sel, k, sub)
        ps = sel + p
        pair_best = ps if pair_best is None else jnp.maximum(pair_best, ps)
        ahead = (p > sel) | ((p == sel) & wrapped)
        rank = rank + ahead.astype(jnp.float32)
    gscore = pair_best
    for k in range(1, EPG):
        p, _ = _group_partner(pair_best, k, sub)
        gscore = jnp.maximum(gscore, p)
    chosen = jnp.ones((N_EXPERTS, n), jnp.bool_)
    grp = sub // EPG
    for j in range(1, N_GROUPS):
        other = pltpu.roll(gscore, N_EXPERTS - EPG * j, 0)
        other_is_later = grp + j < N_GROUPS
        chosen = chosen & ((gscore > other) | ((gscore == other) & other_is_later))
    m = chosen & (rank < 2.0)
    sm = s * m.astype(jnp.float32)
    wgt = sm / jnp.sum(sm, axis=0, keepdims=True)
    return m, wgt, sub


def _post_mix(x, mix, mod_ref, gffn_ref, rw_cat_ref, rw_hi_ref, xo_ref, h2_ref, lg_ref):
    ts = x.shape[0]
    x_new = x + mod_ref[0, 2:3, :] * mix
    xo_ref[0] = x_new
    h2 = _pre_norm(x_new, mod_ref, gffn_ref, 3, 4)
    for c in range(ROWS_PER_TOKEN):
        h2_ref[pl.ds(c, ts, stride=ROWS_PER_TOKEN), :] = h2[:, c * LANES:(c + 1) * LANES]
    lg_ref[0] = _router_logits(h2, rw_cat_ref, rw_hi_ref)


def _mixer_io_specs(bsz, seq, ts):
    ns = MIX_STREAMS
    hb = bsz // ns
    nt = seq // ts
    in_specs = [
        pl.BlockSpec((ns, 1, ts, D_MODEL), lambda b, t: (0, b, t, 0)),
        pl.BlockSpec((ns, 1, 6, D_MODEL), lambda b, t: (0, b, 0, 0)),
    ]
    out_specs = [
        pl.BlockSpec((ns, 1, ts, D_MODEL), lambda b, t: (0, b, t, 0)),
        pl.BlockSpec((ns, ts * ROWS_PER_TOKEN, LANES), lambda b, t: (0, b * nt + t, 0)),
        pl.BlockSpec((ns, 1, N_EXPERTS, ts), lambda b, t: (0, b, 0, t)),
    ]
    out_shapes = [
        jax.ShapeDtypeStruct((ns, hb, seq, D_MODEL), jnp.float32),
        jax.ShapeDtypeStruct((ns, hb * seq * ROWS_PER_TOKEN, LANES), jnp.float32),
        jax.ShapeDtypeStruct((ns, hb, N_EXPERTS, seq), jnp.float32),
    ]
    return in_specs, out_specs, out_shapes


def _mixer_unstream(outs, bsz, seq):
    x, h2_tm, logits = outs
    return (x.reshape(bsz, seq, D_MODEL), h2_tm.reshape(bsz * seq * ROWS_PER_TOKEN, LANES),
            logits.reshape(bsz, N_EXPERTS, seq))


def _const_spec(shape):
    nd = len(shape)
    return pl.BlockSpec(shape, lambda b, t: (0,) * nd)


def _gla_body(x_ref, mod_ref, gmix_ref, gffn_ref, win_ref, wgu_ref, bg_ref, ng_ref,
              wout_ref, rw_cat_ref, rw_hi_ref,
              xo_ref, h2_ref, lg_ref, state_scr, o_scr):
    @pl.when(pl.program_id(1) == 0)
    def _():
        state_scr[...] = jnp.zeros_like(state_scr)

    for st in range(MIX_STREAMS):
        _gla_stream(x_ref.at[st], mod_ref.at[st], gmix_ref, gffn_ref, win_ref, wgu_ref, bg_ref,
                    ng_ref, wout_ref, rw_cat_ref, rw_hi_ref,
                    xo_ref.at[st], h2_ref.at[st], lg_ref.at[st],
                    state_scr.at[st], o_scr.at[st])


def _gla_stream(x_ref, mod_ref, gmix_ref, gffn_ref, win_ref, wgu_ref, bg_ref, ng_ref,
                wout_ref, rw_cat_ref, rw_hi_ref,
                xo_ref, h2_ref, lg_ref, state_scr, o_scr):
    ts = x_ref.shape[1]
    nchunk = ts // CHUNK
    x = x_ref[0]
    h = _pre_norm(x, mod_ref, gmix_ref, 0, 1).astype(jnp.bfloat16)

    o_q, o_k, o_v, o_g, o_a = 0, GLA_QK, 2 * GLA_QK, 2 * GLA_QK + GLA_VD, 2 * GLA_QK + 2 * GLA_VD
    q = _dot(h, win_ref[:, o_q:o_k]) * (GLA_DK ** -0.5)
    k = _dot(h, win_ref[:, o_k:o_v])
    v = _dot(h, win_ref[:, o_v:o_g]).astype(jnp.bfloat16)
    a_lr = _dot(h, win_ref[:, o_a:o_a + GLA_RANK_PAD])

    a_hi, a_lo = _split_bf16(a_lr)
    z2 = _dot(jnp.concatenate([a_hi, a_lo], axis=0), wgu_ref[...])
    z = z2[0:ts] + z2[ts:2 * ts] + bg_ref[...]
    log_a = -(jnp.maximum(-z, 0.0) + jnp.log1p(jnp.exp(-jnp.abs(z)))) * (1.0 / GLA_GATE_TAU)

    row = lax.broadcasted_iota(jnp.int32, (ts, ts), 0)
    col = lax.broadcasted_iota(jnp.int32, (ts, ts), 1)
    tri = ((row // CHUNK == col // CHUNK) & (col <= row)).astype(jnp.bfloat16)
    l_hi, l_lo = _split_bf16(log_a)
    cum2 = _dot(tri, jnp.concatenate([l_hi, l_lo], axis=1))
    cum = cum2[:, 0:GLA_QK] + cum2[:, GLA_QK:2 * GLA_QK]

    q = q.astype(jnp.bfloat16)
    ng = ng_ref[...]
    kv_t, gammas = [], []
    for j in range(nchunk):
        r0 = j * CHUNK
        cum_j = cum[r0:r0 + CHUNK]
        total = cum_j[CHUNK - 1:CHUNK]
        k_dec = (k[r0:r0 + CHUNK] * jnp.exp(total - cum_j)).astype(jnp.bfloat16)
        gammas.append(jnp.exp(total))
        for hd in range(GLA_HEADS):
            ks = slice(hd * GLA_DK, (hd + 1) * GLA_DK)
            vs = slice(hd * GLA_DV, (hd + 1) * GLA_DV)
            kv_t.append(_dot_tn(v[r0:r0 + CHUNK, vs], k_dec[:, ks]))
    states = []
    for hd in range(GLA_HEADS):
        ks = slice(hd * GLA_DK, (hd + 1) * GLA_DK)
        s_cur = state_scr[hd]
        for j in range(nchunk):
            s_cur = s_cur * gammas[j][:, ks] + kv_t[j * GLA_HEADS + hd]
            states.append(s_cur.astype(jnp.bfloat16))
        state_scr[hd] = s_cur
    for hd in range(GLA_HEADS):
        ks = slice(hd * GLA_DK, (hd + 1) * GLA_DK)
        vs = slice(hd * GLA_DV, (hd + 1) * GLA_DV)
        for j in range(nchunk):
            r0 = j * CHUNK
            o = _dot_nt(q[r0:r0 + CHUNK, ks], states[hd * nchunk + j])
            o_scr[r0:r0 + CHUNK, vs] = _rms(o, ng)

    g = _dot(h, win_ref[:, o_g:o_a])
    og = (o_scr[...] * _silu(g)).astype(jnp.bfloat16)
    mix = _dot(og, wout_ref[...])
    _post_mix(x, mix, mod_ref, gffn_ref, rw_cat_ref, rw_hi_ref, xo_ref, h2_ref, lg_ref)


def _gla_layer_call(x, mod, gmix, gffn, w_in, w_gu, b_g, n_g, w_out, rw_cat, rw_hi):
    bsz, seq, d = x.shape
    ts = MIX_TS
    ns = MIX_STREAMS
    hb = bsz // ns
    io_in, out_specs, out_shapes = _mixer_io_specs(bsz, seq, ts)
    consts = (gmix, gffn, w_in, w_gu, b_g, n_g, w_out, rw_cat, rw_hi)
    outs = pl.pallas_call(
        _gla_body,
        grid=(hb, seq // ts),
        in_specs=io_in + [_const_spec(a.shape) for a in consts],
        out_specs=out_specs,
        out_shape=out_shapes,
        scratch_shapes=[
            pltpu.VMEM((ns, GLA_HEADS, GLA_DV, GLA_DK), jnp.float32),
            pltpu.VMEM((ns, ts, GLA_VD), jnp.float32),
        ],
        compiler_params=pltpu.CompilerParams(
            dimension_semantics=("arbitrary", "arbitrary"),
            vmem_limit_bytes=VMEM_LIMIT),
        name="gla_layer",
    )(x.reshape(ns, hb, seq, d), mod.reshape(ns, hb, 6, d), *consts)
    return _mixer_unstream(outs, bsz, seq)


def _lru_body(x_ref, mod_ref, gmix_ref, gffn_ref, win_ref, cw_ref, cb_ref, wr_ref, br_ref,
              wi_ref, bi_ref, lam_ref, wout_ref, rw_cat_ref, rw_hi_ref,
              xo_ref, h2_ref, lg_ref,
              conv_scr, hstate_scr, a_scr, b_scr, gelu_scr):
    ns = MIX_STREAMS
    ts = x_ref.shape[2]
    w = D_MODEL

    @pl.when(pl.program_id(1) == 0)
    def _():
        conv_scr[:, 0:SUBLANES, :] = jnp.zeros((ns, SUBLANES, w), jnp.float32)
        hstate_scr[...] = jnp.zeros_like(hstate_scr)

    for st in range(ns):
        _lru_pre_scan(x_ref.at[st], mod_ref.at[st], gmix_ref, win_ref, cw_ref, cb_ref, wr_ref,
                      br_ref, wi_ref, bi_ref, lam_ref,
                      conv_scr.at[st], a_scr.at[st], b_scr.at[st], gelu_scr.at[st])

    unroll = 4

    def group_step(gi, hprev):
        for u in range(unroll):
            r0 = pl.multiple_of(gi * (unroll * SUBLANES), unroll * SUBLANES) + u * SUBLANES
            nxt = []
            for st in range(ns):
                hs = (a_scr[st, pl.ds(r0, SUBLANES), :] * hprev[st]
                      + b_scr[st, pl.ds(r0, SUBLANES), :])
                b_scr[st, pl.ds(r0, SUBLANES), :] = hs
                nxt.append(hs[SUBLANES - 1:SUBLANES, :])
            hprev = tuple(nxt)
        return hprev

    hlast = lax.fori_loop(0, ts // (unroll * SUBLANES), group_step,
                          tuple(hstate_scr[st] for st in range(ns)))
    for st in range(ns):
        hstate_scr[st] = hlast[st]

    for st in range(ns):
        x = x_ref[st, 0]
        y = (b_scr[st] * gelu_scr[st]).astype(jnp.bfloat16)
        mix = _dot(y, wout_ref[...])
        _post_mix(x, mix, mod_ref.at[st], gffn_ref, rw_cat_ref, rw_hi_ref,
                  xo_ref.at[st], h2_ref.at[st], lg_ref.at[st])


def _lru_pre_scan(x_ref, mod_ref, gmix_ref, win_ref, cw_ref, cb_ref, wr_ref, br_ref,
                  wi_ref, bi_ref, lam_ref, conv_scr, a_scr, b_scr, gelu_scr):
    ts = x_ref.shape[1]
    w = D_MODEL
    x = x_ref[0]
    h = _pre_norm(x, mod_ref, gmix_ref, 0, 1).astype(jnp.bfloat16)
    gate_br = _dot(h, win_ref[:, 0:w])
    gelu_scr[...] = 0.5 * gate_br * (1.0 + jnp.tanh(
        0.7978845608028654 * (gate_br + 0.044715 * gate_br * gate_br * gate_br)))
    xb = _dot(h, win_ref[:, w:2 * w])

    conv_scr[SUBLANES:SUBLANES + ts, :] = xb
    xc = cb_ref[...] + xb * cw_ref[CONV_W - 1:CONV_W, :]
    for j in range(CONV_W - 1):
        back = CONV_W - 1 - j
        xc = xc + conv_scr[SUBLANES - back:SUBLANES - back + ts, :] * cw_ref[j:j + 1, :]
    conv_scr[0:SUBLANES, :] = conv_scr[ts:ts + SUBLANES, :]

    xcb = xc.astype(jnp.bfloat16)
    rs, iis = [], []
    for hd in range(LRU_BLOCKS):
        sl = slice(hd * LRU_BLOCK_W, (hd + 1) * LRU_BLOCK_W)
        rs.append(_dot(xcb[:, sl], wr_ref[hd]))
        iis.append(_dot(xcb[:, sl], wi_ref[hd]))
    r = _sigmoid(jnp.concatenate(rs, axis=1) + br_ref[...])
    ig = _sigmoid(jnp.concatenate(iis, axis=1) + bi_ref[...])

    lam = lam_ref[...]
    softplus_neg_lam = jnp.maximum(-lam, 0.0) + jnp.log1p(jnp.exp(-jnp.abs(lam)))
    log_a = (-LRU_C * r) * softplus_neg_lam
    a = jnp.exp(log_a)
    mult = jnp.sqrt(-jnp.tanh(log_a) * (a * a + 1.0))
    bb = (xc * ig) * mult

    a = a.reshape(ts // SUBLANES, SUBLANES, w)
    bb = bb.reshape(ts // SUBLANES, SUBLANES, w)
    rowi = lax.broadcasted_iota(jnp.int32, a.shape, 1)
    for d in (1, 2, 4):
        keep = rowi >= d
        a_sh = jnp.where(keep, pltpu.roll(a, d, 1), 1.0)
        b_sh = jnp.where(keep, pltpu.roll(bb, d, 1), 0.0)
        bb = a * b_sh + bb
        a = a * a_sh
    a_scr[...] = a.reshape(ts, w)
    b_scr[...] = bb.reshape(ts, w)


def _lru_layer_call(x, mod, gmix, gffn, w_in, conv_w, conv_b, w_r, b_r, w_i, b_i, lam, w_out,
                    rw_cat, rw_hi):
    bsz, seq, d = x.shape
    ts = MIX_TS
    ns = MIX_STREAMS
    hb = bsz // ns
    io_in, out_specs, out_shapes = _mixer_io_specs(bsz, seq, ts)
    consts = (gmix, gffn, w_in, conv_w, conv_b, w_r, b_r, w_i, b_i, lam, w_out,
              rw_cat, rw_hi)
    outs = pl.pallas_call(
        _lru_body,
        grid=(hb, seq // ts),
        in_specs=io_in + [_const_spec(a.shape) for a in consts],
        out_specs=out_specs,
        out_shape=out_shapes,
        scratch_shapes=[
            pltpu.VMEM((ns, SUBLANES + ts, d), jnp.float32),
            pltpu.VMEM((ns, 1, d), jnp.float32),
            pltpu.VMEM((ns, ts, d), jnp.float32),
            pltpu.VMEM((ns, ts, d), jnp.float32),
            pltpu.VMEM((ns, ts, d), jnp.float32),
        ],
        compiler_params=pltpu.CompilerParams(
            dimension_semantics=("arbitrary", "arbitrary"),
            vmem_limit_bytes=VMEM_LIMIT),
        name="lru_layer",
    )(x.reshape(ns, hb, seq, d), mod.reshape(ns, hb, 6, d), *consts)
    return _mixer_unstream(outs, bsz, seq)


def _route_body(lg_ref, rbias_ref, pos0_ref, pos1_ref, w0_ref, off_ref, cnt_ref):
    seq = lg_ref.shape[2]
    blk = ROUTE_BLOCK
    s = _sigmoid_rel(lg_ref[0])
    m, wgt, sub = _route_select(s, rbias_ref[...])
    mf = m.astype(jnp.float32)

    row = lax.broadcasted_iota(jnp.int32, (blk, blk), 0)
    col = lax.broadcasted_iota(jnp.int32, (blk, blk), 1)
    before = (row < col).astype(jnp.bfloat16)
    carry = jnp.zeros((N_EXPERTS, 1), jnp.float32)
    ranks = []
    for i in range(seq // blk):
        mb = mf[:, i * blk:(i + 1) * blk]
        ranks.append(_dot(mb.astype(jnp.bfloat16), before) + carry)
        carry = carry + jnp.sum(mb, axis=1, keepdims=True)
    rnk = jnp.concatenate(ranks, axis=1)
    cnt_ref[0] = jnp.broadcast_to(carry, (N_EXPERTS, LANES)).astype(jnp.int32)

    eidx = sub.astype(jnp.float32)
    e_lo = jnp.min(jnp.where(m, eidx, 99.0), axis=0, keepdims=True)
    e_hi = jnp.max(jnp.where(m, eidx, -1.0), axis=0, keepdims=True)
    is_lo = m & (eidx == e_lo)
    is_hi = m & (eidx == e_hi)
    p_lo = jnp.sum(jnp.where(is_lo, rnk, 0.0), axis=0, keepdims=True)
    p_hi = jnp.sum(jnp.where(is_hi, rnk, 0.0), axis=0, keepdims=True)
    w0_ref[0] = jnp.sum(jnp.where(is_lo, wgt, 0.0), axis=0, keepdims=True)

    off = jnp.zeros((1, 1), jnp.float32)
    for e in range(N_EXPERTS):
        off_ref[0, e:e + 1, :] = jnp.broadcast_to(off, (1, LANES)).astype(jnp.int32)
        p_lo = p_lo + jnp.where(e_lo == float(e), off, 0.0)
        p_hi = p_hi + jnp.where(e_hi == float(e), off, 0.0)
        off = off + carry[e:e + 1, :]
    pos0_ref[0] = p_lo.astype(jnp.int32) * ROWS_PER_TOKEN
    pos1_ref[0] = p_hi.astype(jnp.int32) * ROWS_PER_TOKEN


def _route_call(logits, rbias):
    bsz, _, seq = logits.shape
    row_spec = pl.BlockSpec((1, 1, seq), lambda b: (b, 0, 0))
    tab_spec = pl.BlockSpec((1, N_EXPERTS, LANES), lambda b: (b, 0, 0))
    return pl.pallas_call(
        _route_body,
        grid=(bsz,),
        in_specs=[
            pl.BlockSpec((1, N_EXPERTS, seq), lambda b: (b, 0, 0)),
            pl.BlockSpec((N_EXPERTS, 1), lambda b: (0, 0)),
        ],
        out_specs=[row_spec, row_spec, row_spec, tab_spec, tab_spec],
        out_shape=[
            jax.ShapeDtypeStruct((bsz, 1, seq), jnp.int32),
            jax.ShapeDtypeStruct((bsz, 1, seq), jnp.int32),
            jax.ShapeDtypeStruct((bsz, 1, seq), jnp.float32),
            jax.ShapeDtypeStruct((bsz, N_EXPERTS, LANES), jnp.int32),
            jax.ShapeDtypeStruct((bsz, N_EXPERTS, LANES), jnp.int32),
        ],
        compiler_params=pltpu.CompilerParams(dimension_semantics=("arbitrary",)),
        name="moe_route",
    )(logits, rbias)


def _scatter_tokens(pos0_ref, pos1_ref, h2_ref, slots, idx0, lo, cnt):
    R = ROWS_PER_TOKEN
    for u in range(lo, lo + cnt):
        p0 = pos0_ref[0, 0, idx0 + u]
        p1 = pos1_ref[0, 0, idx0 + u]
        val = h2_ref[u * R:(u + 1) * R, :]
        slots[pl.ds(pl.multiple_of(p0, R), R), :] = val
        slots[pl.ds(pl.multiple_of(p1, R), R), :] = val


def _combine_tokens(pos0_ref, pos1_ref, wts_ref, slots, tm_scr, idx0, lo, cnt):
    R = ROWS_PER_TOKEN
    for u in range(lo, lo + cnt):
        p0 = pos0_ref[0, 0, idx0 + u]
        p1 = pos1_ref[0, 0, idx0 + u]
        w0 = wts_ref[0, 0, idx0 + u]
        za = slots[pl.ds(pl.multiple_of(p0, R), R), :]
        zb = slots[pl.ds(pl.multiple_of(p1, R), R), :]
        tm_scr[u * R:(u + 1) * R, :] = zb + w0 * (za - zb)


def _combine_epilogue(tm_scr, x_ref, mod_ref, fin_ref, o_ref, lo, cnt, final_norm):
    R = ROWS_PER_TOKEN
    moe = jnp.concatenate(
        [tm_scr[pl.ds(lo * R + c, cnt, stride=R), :] for c in range(R)], axis=1)
    out = x_ref[lo:lo + cnt, :] + mod_ref[0, 5:6, :] * moe
    if final_norm:
        out = _rms(out, fin_ref[...])
    o_ref[lo:lo + cnt, :] = out


def _expert_tile_hooked(slots, wg_ref, wu_ref, wd_ref, ws, slot0, n_valid, tm, hooks):
    R = ROWS_PER_TOKEN
    nw = 256
    row0 = pl.multiple_of(slot0 * R, R)
    xs = [slots[pl.ds(row0 + c, tm, stride=R), :] for c in range(R)]
    xt = jnp.concatenate(xs, axis=1).astype(jnp.bfloat16)
    ok = lax.broadcasted_iota(jnp.int32, (tm, LANES), 0) < n_valid
    hooks = list(hooks)

    def run_hook():
        if hooks:
            hooks.pop(0)()

    gate, up = [], []
    for i in range(D_EXPERT // nw):
        gate.append(_dot(xt, wg_ref[ws, :, i * nw:(i + 1) * nw]))
        run_hook()
    for i in range(D_EXPERT // nw):
        up.append(_dot(xt, wu_ref[ws, :, i * nw:(i + 1) * nw]))
        run_hook()
    he = (_silu(jnp.concatenate(gate, axis=1)) * jnp.concatenate(up, axis=1)).astype(jnp.bfloat16)
    for i in range(D_MODEL // nw):
        y = _dot(he, wd_ref[ws, :, i * nw:(i + 1) * nw])
        for cc in range(nw // LANES):
            c = i * (nw // LANES) + cc
            slots[pl.ds(row0 + c, tm, stride=R), :] = jnp.where(
                ok, y[:, cc * LANES:(cc + 1) * LANES], xs[c])
        run_hook()
    while hooks:
        run_hook()


def _moe_body(off_sm, cnt_sm, sc0_ref, sc1_ref, cb0_ref, cb1_ref, wts_ref, h2_ref, x_ref,
              mod_ref, fin_ref, wg_hbm, wu_hbm, wd_hbm, o_ref, slots_a, slots_b, tm_scr,
              wg_ref, wu_ref, wd_ref, w_sem, *, n_blocks, layer, final_norm):
    r = pl.program_id(0)
    e = pl.program_id(1)

    g = r * N_EXPERTS + e
    n_steps = (n_blocks + 2) * N_EXPERTS

    def w_needed(step):
        rnd = step // N_EXPERTS
        return (rnd >= 1) & (rnd <= n_blocks) & (step < n_steps)

    def w_copies(step):
        ex = layer * N_EXPERTS + step % N_EXPERTS
        sl = step % MOE_WRING
        return [pltpu.make_async_copy(src.at[ex], dst.at[sl], w_sem.at[i, sl])
                for i, (src, dst) in enumerate(((wg_hbm, wg_ref), (wu_hbm, wu_ref),
                                                (wd_hbm, wd_ref)))]

    def w_start(step):
        @pl.when(w_needed(step))
        def _():
            for cp in w_copies(step):
                cp.start()

    @pl.when(g == 0)
    def _():
        for ahead in range(MOE_WRING - 1):
            w_start(g + ahead)

    w_start(g + MOE_WRING - 1)

    @pl.when(w_needed(g))
    def _():
        for cp in w_copies(g):
            cp.wait()

    ws = g % MOE_WRING
    R = ROWS_PER_TOKEN
    U = MOE_UNROLL
    tm = MOE_TM
    mv = MOE_MOVE
    half = N_EXPERTS // 2
    n_slots = 2 * half * mv
    n_hooks = 2 * (D_EXPERT // 256) + D_MODEL // 256
    per_hook = mv // n_hooks

    @pl.when((r == 0) & (e == 0))
    def _():
        zeros = jnp.zeros((tm * R, LANES), jnp.float32)
        slots_a[n_slots * R:(n_slots + tm) * R, :] = zeros
        slots_b[n_slots * R:(n_slots + tm) * R, :] = zeros

    do_exp = (r >= 1) & (r <= n_blocks)
    do_comb = (r >= 2) & (e < half)
    do_scat = (r < n_blocks) & (e >= half)
    b_exp = jnp.clip(r - 1, 0, n_blocks - 1)
    off = off_sm[b_exp, e]
    n = jnp.where(do_exp, cnt_sm[b_exp, e], 0)
    comb_idx0 = e * mv
    scat_idx0 = (e - half) * mv
    small = tm // 2
    first_rows = jnp.where(n > small, tm, jnp.where(n > 0, small, 0))
    fused = do_comb | do_scat
    start = jnp.where(fused, first_rows, 0)
    rest = jnp.maximum(n - start, 0)
    rest_big = rest // tm + ((rest % tm) > small).astype(jnp.int32)
    rest_tail = rest - rest_big * tm

    def round_body(exp_buf, mov_buf):
        tile = functools.partial(_expert_tile_hooked, exp_buf, wg_ref, wu_ref, wd_ref, ws)

        def comb_hooks():
            def hook(k):
                def run():
                    if k < n_hooks:
                        _combine_tokens(cb0_ref, cb1_ref, wts_ref, mov_buf, tm_scr, comb_idx0,
                                        k * per_hook, per_hook)
                    if k >= 1:
                        _combine_epilogue(tm_scr, x_ref, mod_ref, fin_ref, o_ref,
                                          (k - 1) * per_hook, per_hook, final_norm)
                return run
            return [hook(k) for k in range(n_hooks + 1)]

        def scat_hooks():
            def hook(k):
                return lambda: _scatter_tokens(sc0_ref, sc1_ref, h2_ref, mov_buf, scat_idx0,
                                               k * per_hook, per_hook)
            return [hook(k) for k in range(n_hooks)]

        for rows in (tm, small):
            @pl.when((first_rows == rows) & do_comb)
            def _():
                tile(off, n, rows, comb_hooks())

            @pl.when((first_rows == rows) & do_scat)
            def _():
                tile(off, n, rows, scat_hooks())

        @pl.when(first_rows == 0)
        def _():
            @pl.when(do_comb)
            def _():
                def body(i, carry):
                    i0 = comb_idx0 + U * i
                    for u in range(U):
                        p0 = cb0_ref[0, 0, i0 + u]
                        p1 = cb1_ref[0, 0, i0 + u]
                        w0 = wts_ref[0, 0, i0 + u]
                        za = mov_buf[pl.ds(pl.multiple_of(p0, R), R), :]
                        zb = mov_buf[pl.ds(pl.multiple_of(p1, R), R), :]
                        r0 = pl.multiple_of(i * (U * R), U * R) + u * R
                        tm_scr[pl.ds(r0, R), :] = zb + w0 * (za - zb)
                    return carry

                lax.fori_loop(0, mv // U, body, 0)
                _combine_epilogue(tm_scr, x_ref, mod_ref, fin_ref, o_ref, 0, mv, final_norm)

            @pl.when(do_scat)
            def _():
                def body(i, carry):
                    i0 = scat_idx0 + U * i
                    for u in range(U):
                        p0 = sc0_ref[0, 0, i0 + u]
                        p1 = sc1_ref[0, 0, i0 + u]
                        r0 = pl.multiple_of(i * (U * R), U * R) + u * R
                        val = h2_ref[pl.ds(r0, R), :]
                        mov_buf[pl.ds(pl.multiple_of(p0, R), R), :] = val
                        mov_buf[pl.ds(pl.multiple_of(p1, R), R), :] = val
                    return carry

                lax.fori_loop(0, mv // U, body, 0)

        def tile_body(j, carry):
            done = start + j * tm
            tile(off + done, n - done, tm, [])
            return carry

        lax.fori_loop(0, rest_big, tile_body, 0)

        @pl.when(rest_tail > 0)
        def _():
            done = start + rest_big * tm
            tile(off + done, n - done, small, [])

    @pl.when(r % 2 == 0)
    def _():
        round_body(slots_b, slots_a)

    @pl.when(r % 2 == 1)
    def _():
        round_body(slots_a, slots_b)


def _moe_call(off, cnt, pos0, pos1, w0, h2_tm, x, mod, fin_g, w_gate, w_up, w_down, layer,
              final_norm):
    bsz, seq, d = x.shape
    mv = MOE_MOVE
    half = N_EXPERTS // 2
    assert seq == half * mv
    n_slots = 2 * seq
    last = bsz - 1

    def scat_blk(r, e, *_):
        return (jnp.minimum(r, last) * half + jnp.clip(e - half, 0, half - 1), 0)

    def comb_blk(r, e, *_):
        return (jnp.where(r < 2, 0, (r - 2) * half + jnp.minimum(e, half - 1)), 0)

    def scat_row(r, e, *_):
        return (jnp.minimum(r, last), 0, 0)

    def comb_row(r, e, *_):
        return (jnp.clip(r - 2, 0, last), 0, 0)

    body = functools.partial(_moe_body, n_blocks=bsz, layer=layer, final_norm=final_norm)
    slot_rows = (n_slots + MOE_TM) * ROWS_PER_TOKEN
    out = pl.pallas_call(
        body,
        grid_spec=pltpu.PrefetchScalarGridSpec(
            num_scalar_prefetch=2,
            grid=(bsz + 2, N_EXPERTS),
            in_specs=[
                pl.BlockSpec((1, 1, seq), scat_row, memory_space=pltpu.SMEM),
                pl.BlockSpec((1, 1, seq), scat_row, memory_space=pltpu.SMEM),
                pl.BlockSpec((1, 1, seq), comb_row, memory_space=pltpu.SMEM),
                pl.BlockSpec((1, 1, seq), comb_row, memory_space=pltpu.SMEM),
                pl.BlockSpec((1, 1, seq), comb_row, memory_space=pltpu.SMEM),
                pl.BlockSpec((mv * ROWS_PER_TOKEN, LANES), scat_blk),
                pl.BlockSpec((mv, d), comb_blk),
                pl.BlockSpec((1, 6, d), comb_row),
                pl.BlockSpec((1, d), lambda r, e, *_: (0, 0)),
                pl.BlockSpec(memory_space=pl.ANY),
                pl.BlockSpec(memory_space=pl.ANY),
                pl.BlockSpec(memory_space=pl.ANY),
            ],
            out_specs=pl.BlockSpec((mv, d), comb_blk),
            scratch_shapes=[
                pltpu.VMEM((slot_rows, LANES), jnp.float32),
                pltpu.VMEM((slot_rows, LANES), jnp.float32),
                pltpu.VMEM((mv * ROWS_PER_TOKEN, LANES), jnp.float32),
                pltpu.VMEM((MOE_WRING, d, D_EXPERT), jnp.bfloat16),
                pltpu.VMEM((MOE_WRING, d, D_EXPERT), jnp.bfloat16),
                pltpu.VMEM((MOE_WRING, D_EXPERT, d), jnp.bfloat16),
                pltpu.SemaphoreType.DMA((3, MOE_WRING)),
            ],
        ),
        out_shape=jax.ShapeDtypeStruct((bsz * seq, d), jnp.float32),
        compiler_params=pltpu.CompilerParams(
            dimension_semantics=("arbitrary", "arbitrary"),
            vmem_limit_bytes=MOE_VMEM_LIMIT),
        name="moe",
    )(off, cnt, pos0, pos1, pos0, pos1, w0, h2_tm, x.reshape(bsz * seq, d), mod, fin_g,
      w_gate, w_up, w_down)
    return out.reshape(bsz, seq, d)


def _moe_layer(x, h2_tm, logits, rbias, mod, fin_g, w_gate, w_up, w_down, layer, final_norm):
    pos0, pos1, w0, off, cnt = _route_call(logits, rbias)
    return _moe_call(off[:, :, 0], cnt[:, :, 0], pos0, pos1, w0, h2_tm, x, mod, fin_g,
                     w_gate, w_up, w_down, layer, final_norm)


def kernel(x, c, gla_w_in, gla_w_gate_up, gla_b_gate, gla_norm_g, gla_w_out, lru_w_in, lru_conv_w, lru_conv_b, lru_w_r, lru_b_r, lru_w_i, lru_b_i, lru_lambda, lru_w_out, router_w, router_bias, moe_w_gate, moe_w_up, moe_w_down, norm_mix_g, norm_ffn_g, ada_w, ada_b, final_norm_g):
    bf = jnp.bfloat16
    depth = ada_w.shape[0]
    bsz = x.shape[0]
    d = D_MODEL
    mod_all = _ada_call(c, ada_w, ada_b).reshape(depth, bsz, 6, d)

    rw_t = router_w.T
    rw_hi = rw_t.astype(bf)
    rw_lo = (rw_t - rw_hi.astype(jnp.float32)).astype(bf)
    rw_cat = jnp.concatenate([rw_hi, rw_lo], axis=0)
    rbias = router_bias.reshape(N_EXPERTS, 1)
    fin_g = final_norm_g.reshape(1, d)
    wg_all = moe_w_gate.astype(bf).reshape(depth * N_EXPERTS, d, D_EXPERT)
    wu_all = moe_w_up.astype(bf).reshape(depth * N_EXPERTS, d, D_EXPERT)
    wd_all = moe_w_down.astype(bf).reshape(depth * N_EXPERTS, D_EXPERT, d)

    for i in range(depth):
        j = i // 2
        mod = mod_all[i]
        gmix = norm_mix_g[i].reshape(1, d)
        gffn = norm_ffn_g[i].reshape(1, d)
        if i % 2 == 0:
            w_in = jnp.pad(gla_w_in[j], ((0, 0), (0, GLA_RANK_PAD - GLA_GATE_RANK))).astype(bf)
            w_gu = jnp.pad(gla_w_gate_up[j], ((0, GLA_RANK_PAD - GLA_GATE_RANK), (0, 0))).astype(bf)
            x, h2_tm, logits = _gla_layer_call(
                x, mod, gmix, gffn, w_in, w_gu, gla_b_gate[j].reshape(1, GLA_QK),
                gla_norm_g[j].reshape(1, GLA_DV), gla_w_out[j].astype(bf),
                rw_cat, rw_hi)
        else:
            x, h2_tm, logits = _lru_layer_call(
                x, mod, gmix, gffn, lru_w_in[j].astype(bf), lru_conv_w[j],
                lru_conv_b[j].reshape(1, d), lru_w_r[j].astype(bf),
                lru_b_r[j].reshape(1, d), lru_w_i[j].astype(bf), lru_b_i[j].reshape(1, d),
                lru_lambda[j].reshape(1, d), lru_w_out[j].astype(bf), rw_cat, rw_hi)
        x = _moe_layer(x, h2_tm, logits, rbias, mod, fin_g, wg_all, wu_all, wd_all,
                       layer=i, final_norm=(i == depth - 1))
    return x
```

```python
import functools

import jax
import jax.numpy as jnp
from jax import lax
from jax.experimental import pallas as pl
from jax.experimental.pallas import tpu as pltpu

D_MODEL = 1024
CHUNK = 64
EPS = 1e-6

GLA_HEADS = 4
GLA_DK = 128
GLA_DV = 256
GLA_QK = GLA_HEADS * GLA_DK
GLA_VD = GLA_HEADS * GLA_DV
GLA_GATE_RANK = 16
GLA_GATE_TAU = 16.0
GLA_RANK_PAD = 128
GLA_IN_PAD = 2 * GLA_QK + 2 * GLA_VD + GLA_RANK_PAD

LRU_BLOCKS = 4
LRU_BLOCK_W = D_MODEL // LRU_BLOCKS
CONV_W = 4
LRU_C = 8.0

N_EXPERTS = 16
N_GROUPS = 4
EPG = N_EXPERTS // N_GROUPS
D_EXPERT = 512

LANES = 128
SUBLANES = 8
ROWS_PER_TOKEN = D_MODEL // LANES

MIX_TS = 256
MIX_STREAMS = 2
GLA_STAGGER = 6
ROUTE_BLOCK = 256
MOE_TM = 256
MOE_UNROLL = 8
MOE_MOVE = 256
MOE_WRING = 3
VMEM_LIMIT = 56 * 1024 * 1024
MOE_VMEM_LIMIT = 60 * 1024 * 1024


def _dot(a, b):
    return jnp.dot(a, b, preferred_element_type=jnp.float32)


def _dot_nt(a, b):
    return lax.dot_general(a, b, (((1,), (1,)), ((), ())),
                           preferred_element_type=jnp.float32)


def _dot_tn(a, b):
    return lax.dot_general(a, b, (((0,), (0,)), ((), ())),
                           preferred_element_type=jnp.float32)


def _split_bf16(x):
    hi = x.astype(jnp.bfloat16)
    lo = (x - hi.astype(jnp.float32)).astype(jnp.bfloat16)
    return hi, lo


def _sigmoid(x):
    return 0.5 * jnp.tanh(0.5 * x) + 0.5


def _sigmoid_rel(x):
    return 1.0 / (1.0 + jnp.exp(-x))


def _silu(x):
    return x * _sigmoid(x)


def _rms(x, g):
    inv = lax.rsqrt(jnp.mean(x * x, axis=-1, keepdims=True) + EPS)
    return x * inv * g


def _ada_body(c_ref, w_ref, b_ref, o_ref):
    c_hi, c_lo = _split_bf16(_silu(c_ref[...]))
    w_hi, w_lo = _split_bf16(w_ref[0])
    o_ref[0] = _dot(c_hi, w_hi) + (_dot(c_lo, w_hi) + _dot(c_hi, w_lo)) + b_ref[0]


def _ada_call(c, ada_w, ada_b):
    depth, d, n = ada_w.shape
    bsz = c.shape[0]
    tn = 1024
    return pl.pallas_call(
        _ada_body,
        grid=(depth, n // tn),
        in_specs=[
            pl.BlockSpec((bsz, d), lambda l, j: (0, 0)),
            pl.BlockSpec((1, d, tn), lambda l, j: (l, 0, j)),
            pl.BlockSpec((1, 1, tn), lambda l, j: (l, 0, j)),
        ],
        out_specs=pl.BlockSpec((1, bsz, tn), lambda l, j: (l, 0, j)),
        out_shape=jax.ShapeDtypeStruct((depth, bsz, n), jnp.float32),
        compiler_params=pltpu.CompilerParams(
            dimension_semantics=("arbitrary", "arbitrary"),
            vmem_limit_bytes=VMEM_LIMIT),
        name="ada",
    )(c, ada_w, ada_b.reshape(depth, 1, n))


def _pre_norm(x, mod_ref, g_ref, shift_row, scale_row):
    shift = mod_ref[0, shift_row:shift_row + 1, :]
    scale = mod_ref[0, scale_row:scale_row + 1, :]
    return _rms(x, g_ref[...]) * (1.0 + scale) + shift


def _group_partner(x, k, sub):
    n = x.shape[0]
    fwd = pltpu.roll(x, n - k, 0)
    back = pltpu.roll(x, EPG - k, 0)
    wrapped = (sub % EPG) + k >= EPG
    return jnp.where(wrapped, back, fwd), wrapped


def _router_logits(h2, rw_cat_ref, rw_hi_ref):
    h_hi, h_lo = _split_bf16(h2)
    p1 = _dot_nt(rw_cat_ref[...], h_hi)
    p2 = _dot_nt(rw_hi_ref[...], h_lo)
    return p1[0:N_EXPERTS] + p1[N_EXPERTS:2 * N_EXPERTS] + p2


def _route_select(s, rbias):
    n = s.shape[1]
    sel = s + rbias
    sub = lax.broadcasted_iota(jnp.int32, (N_EXPERTS, n), 0)

    pair_best = None
    rank = jnp.zeros((N_EXPERTS, n), jnp.float32)
    for k in range(1, EPG):
        p, wrapped = _group_partner(sel, k, sub)
        ps = sel + p
        pair_best = ps if pair_best is None else jnp.maximum(pair_best, ps)
        ahead = (p > sel) | ((p == sel) & wrapped)
        rank = rank + ahead.astype(jnp.float32)
    gscore = pair_best
    for k in range(1, EPG):
        p, _ = _group_partner(pair_best, k, sub)
        gscore = jnp.maximum(gscore, p)
    chosen = jnp.ones((N_EXPERTS, n), jnp.bool_)
    grp = sub // EPG
    for j in range(1, N_GROUPS):
        other = pltpu.roll(gscore, N_EXPERTS - EPG * j, 0)
        other_is_later = grp + j < N_GROUPS
        chosen = chosen & ((gscore > other) | ((gscore == other) & other_is_later))
    m = chosen & (rank < 2.0)
    sm = s * m.astype(jnp.float32)
    wgt = sm / jnp.sum(sm, axis=0, keepdims=True)
    return m, wgt, sub


def _post_mix(x, mix, mod_ref, gffn_ref, rw_cat_ref, rw_hi_ref, xo_ref, h2_ref, lg_ref):
    ts = x.shape[0]
    x_new = x + mod_ref[0, 2:3, :] * mix
    xo_ref[0] = x_new
    h2 = _pre_norm(x_new, mod_ref, gffn_ref, 3, 4)
    for c in range(ROWS_PER_TOKEN):
        h2_ref[pl.ds(c, ts, stride=ROWS_PER_TOKEN), :] = h2[:, c * LANES:(c + 1) * LANES]
    lg_ref[0] = _router_logits(h2, rw_cat_ref, rw_hi_ref)


def _mixer_io_specs(bsz, seq, ts):
    ns = MIX_STREAMS
    hb = bsz // ns
    nt = seq // ts
    in_specs = [
        pl.BlockSpec((ns, 1, ts, D_MODEL), lambda b, t: (0, b, t, 0)),
        pl.BlockSpec((ns, 1, 6, D_MODEL), lambda b, t: (0, b, 0, 0)),
    ]
    out_specs = [
        pl.BlockSpec((ns, 1, ts, D_MODEL), lambda b, t: (0, b, t, 0)),
        pl.BlockSpec((ns, ts * ROWS_PER_TOKEN, LANES), lambda b, t: (0, b * nt + t, 0)),
        pl.BlockSpec((ns, 1, N_EXPERTS, ts), lambda b, t: (0, b, 0, t)),
    ]
    out_shapes = [
        jax.ShapeDtypeStruct((ns, hb, seq, D_MODEL), jnp.float32),
        jax.ShapeDtypeStruct((ns, hb * seq * ROWS_PER_TOKEN, LANES), jnp.float32),
        jax.ShapeDtypeStruct((ns, hb, N_EXPERTS, seq), jnp.float32),
    ]
    return in_specs, out_specs, out_shapes


def _mixer_unstream(outs, bsz, seq):
    x, h2_tm, logits = outs
    return (x.reshape(bsz, seq, D_MODEL), h2_tm.reshape(bsz * seq * ROWS_PER_TOKEN, LANES),
            logits.reshape(bsz, N_EXPERTS, seq))


def _const_spec(shape):
    nd = len(shape)
    return pl.BlockSpec(shape, lambda b, t: (0,) * nd)


def _gla_body(x_ref, mod_ref, gmix_ref, gffn_ref, win_ref, wgu_ref, bg_ref, ng_ref,
              wout_ref, rw_cat_ref, rw_hi_ref, *rest, n_cast):
    cast_in = rest[:n_cast]
    xo_ref, h2_ref, lg_ref = rest[n_cast:n_cast + 3]
    cast_out = rest[n_cast + 3:2 * n_cast + 3]
    state_scr, o_scr = rest[2 * n_cast + 3:]

    @pl.when(pl.program_id(1) == 0)
    def _():
        state_scr[...] = jnp.zeros_like(state_scr)

    def caster():
        pieces = 4
        for src, dst in zip(cast_in, cast_out):
            rows = src.shape[0] // pieces
            for i in range(pieces):
                dst[i * rows:(i + 1) * rows, :] = src[i * rows:(i + 1) * rows, :].astype(jnp.bfloat16)
                yield

    gens = [_gla_stream(x_ref.at[st], mod_ref.at[st], gmix_ref, gffn_ref, win_ref, wgu_ref,
                        bg_ref, ng_ref, wout_ref, rw_cat_ref, rw_hi_ref,
                        xo_ref.at[st], h2_ref.at[st], lg_ref.at[st],
                        state_scr.at[st], o_scr.at[st])
            for st in range(MIX_STREAMS)]
    _interleave(gens + [caster()], GLA_STAGGER)


def _gla_stream(x_ref, mod_ref, gmix_ref, gffn_ref, win_ref, wgu_ref, bg_ref, ng_ref,
                wout_ref, rw_cat_ref, rw_hi_ref,
                xo_ref, h2_ref, lg_ref, state_scr, o_scr):
    ts = x_ref.shape[1]
    nchunk = ts // CHUNK
    half = GLA_VD // 2
    x = x_ref[0]
    h = _pre_norm(x, mod_ref, gmix_ref, 0, 1).astype(jnp.bfloat16)
    yield

    o_q, o_k, o_v, o_g, o_a = 0, GLA_QK, 2 * GLA_QK, 2 * GLA_QK + GLA_VD, 2 * GLA_QK + 2 * GLA_VD
    q = (_dot(h, win_ref[:, o_q:o_k]) * (GLA_DK ** -0.5)).astype(jnp.bfloat16)
    yield
    k = _dot(h, win_ref[:, o_k:o_v])
    yield
    v = jnp.concatenate(
        [_dot(h, win_ref[:, o_v + i * half:o_v + (i + 1) * half]).astype(jnp.bfloat16)
         for i in range(2)], axis=1)
    yield
    a_lr = _dot(h, win_ref[:, o_a:o_a + GLA_RANK_PAD])
    a_hi, a_lo = _split_bf16(a_lr)
    z2 = _dot(jnp.concatenate([a_hi, a_lo], axis=0), wgu_ref[...])
    z = z2[0:ts] + z2[ts:2 * ts] + bg_ref[...]
    log_a = -(jnp.maximum(-z, 0.0) + jnp.log1p(jnp.exp(-jnp.abs(z)))) * (1.0 / GLA_GATE_TAU)
    yield

    row = lax.broadcasted_iota(jnp.int32, (ts, ts), 0)
    col = lax.broadcasted_iota(jnp.int32, (ts, ts), 1)
    tri = ((row // CHUNK == col // CHUNK) & (col <= row)).astype(jnp.bfloat16)
    l_hi, l_lo = _split_bf16(log_a)
    cum2 = _dot(tri, jnp.concatenate([l_hi, l_lo], axis=1))
    cum = cum2[:, 0:GLA_QK] + cum2[:, GLA_QK:2 * GLA_QK]
    yield

    ng = ng_ref[...]
    kv_t, gammas = [], []
    for j in range(nchunk):
        r0 = j * CHUNK
        cum_j = cum[r0:r0 + CHUNK]
        total = cum_j[CHUNK - 1:CHUNK]
        k_dec = (k[r0:r0 + CHUNK] * jnp.exp(total - cum_j)).astype(jnp.bfloat16)
        gammas.append(jnp.exp(total))
        for hd in range(GLA_HEADS):
            ks = slice(hd * GLA_DK, (hd + 1) * GLA_DK)
            vs = slice(hd * GLA_DV, (hd + 1) * GLA_DV)
            kv_t.append(_dot_tn(v[r0:r0 + CHUNK, vs], k_dec[:, ks]))
        yield
    states = []
    for hd in range(GLA_HEADS):
        ks = slice(hd * GLA_DK, (hd + 1) * GLA_DK)
        s_cur = state_scr[hd]
        for j in range(nchunk):
            s_cur = s_cur * gammas[j][:, ks] + kv_t[j * GLA_HEADS + hd]
            states.append(s_cur.astype(jnp.bfloat16))
        state_scr[hd] = s_cur
        yield
    for hd in range(GLA_HEADS):
        ks = slice(hd * GLA_DK, (hd + 1) * GLA_DK)
        vs = slice(hd * GLA_DV, (hd + 1) * GLA_DV)
        for j in range(nchunk):
            r0 = j * CHUNK
            o = _dot_nt(q[r0:r0 + CHUNK, ks], states[hd * nchunk + j])
            o_scr[r0:r0 + CHUNK, vs] = _rms(o, ng)
        yield

    og = []
    for i in range(2):
        g = _dot(h, win_ref[:, o_g + i * half:o_g + (i + 1) * half])
        og.append((o_scr[:, i * half:(i + 1) * half] * _silu(g)).astype(jnp.bfloat16))
        yield
    og = jnp.concatenate(og, axis=1)
    x_new = []
    for i in range(2):
        cols = slice(i * half, (i + 1) * half)
        mix = _dot(og, wout_ref[:, i * half:(i + 1) * half])
        x_new.append(x[:, cols] + mod_ref[0, 2:3, cols] * mix)
        yield
    x_new = jnp.concatenate(x_new, axis=1)
    xo_ref[0] = x_new
    h2 = _pre_norm(x_new, mod_ref, gffn_ref, 3, 4)
    for c in range(ROWS_PER_TOKEN):
        h2_ref[pl.ds(c, ts, stride=ROWS_PER_TOKEN), :] = h2[:, c * LANES:(c + 1) * LANES]
    yield
    lg_ref[0] = _router_logits(h2, rw_cat_ref, rw_hi_ref)
    yield


def _gla_layer_call(x, mod, gmix, gffn, w_in, w_gu, b_g, n_g, w_out, rw_cat, rw_hi, cast_src):
    bsz, seq, d = x.shape
    ts = MIX_TS
    ns = MIX_STREAMS
    hb = bsz // ns
    nt = seq // ts
    io_in, out_specs, out_shapes = _mixer_io_specs(bsz, seq, ts)
    consts = (gmix, gffn, w_in, w_gu, b_g, n_g, w_out, rw_cat, rw_hi)
    cast_specs, cast_shapes = [], []
    for a in cast_src:
        rows = a.shape[0] // (hb * nt)
        assert rows * hb * nt == a.shape[0] and a.shape[1] == d
        cast_specs.append(pl.BlockSpec((rows, d), lambda b, t: (b * nt + t, 0)))
        cast_shapes.append(jax.ShapeDtypeStruct(a.shape, jnp.bfloat16))
    outs = pl.pallas_call(
        functools.partial(_gla_body, n_cast=len(cast_src)),
        grid=(hb, nt),
        in_specs=io_in + [_const_spec(a.shape) for a in consts] + cast_specs,
        out_specs=out_specs + cast_specs,
        out_shape=out_shapes + cast_shapes,
        scratch_shapes=[
            pltpu.VMEM((ns, GLA_HEADS, GLA_DV, GLA_DK), jnp.float32),
            pltpu.VMEM((ns, ts, GLA_VD), jnp.float32),
        ],
        compiler_params=pltpu.CompilerParams(
            dimension_semantics=("arbitrary", "arbitrary"),
            vmem_limit_bytes=VMEM_LIMIT),
        name="gla_layer",
    )(x.reshape(ns, hb, seq, d), mod.reshape(ns, hb, 6, d), *consts, *cast_src)
    return _mixer_unstream(outs[:3], bsz, seq), outs[3:]


def _lru_body(x_ref, mod_ref, gmix_ref, gffn_ref, win_ref, cw_ref, cb_ref, wr_ref, br_ref,
              wi_ref, bi_ref, lam_ref, wout_ref, rw_cat_ref, rw_hi_ref,
              xo_ref, h2_ref, lg_ref,
              conv_scr, hstate_scr, a_scr, b_scr, gelu_scr):
    ns = MIX_STREAMS
    ts = x_ref.shape[2]
    w = D_MODEL

    @pl.when(pl.program_id(1) == 0)
    def _():
        conv_scr[:, 0:SUBLANES, :] = jnp.zeros((ns, SUBLANES, w), jnp.float32)
        hstate_scr[...] = jnp.zeros_like(hstate_scr)

    for st in range(ns):
        _lru_pre_scan(x_ref.at[st], mod_ref.at[st], gmix_ref, win_ref, cw_ref, cb_ref, wr_ref,
                      br_ref, wi_ref, bi_ref, lam_ref,
                      conv_scr.at[st], a_scr.at[st], b_scr.at[st], gelu_scr.at[st])

    unroll = 4

    def group_step(gi, hprev):
        for u in range(unroll):
            r0 = pl.multiple_of(gi * (unroll * SUBLANES), unroll * SUBLANES) + u * SUBLANES
            nxt = []
            for st in range(ns):
                hs = (a_scr[st, pl.ds(r0, SUBLANES), :] * hprev[st]
                      + b_scr[st, pl.ds(r0, SUBLANES), :])
                b_scr[st, pl.ds(r0, SUBLANES), :] = hs
                nxt.append(hs[SUBLANES - 1:SUBLANES, :])
            hprev = tuple(nxt)
        return hprev

    hlast = lax.fori_loop(0, ts // (unroll * SUBLANES), group_step,
                          tuple(hstate_scr[st] for st in range(ns)))
    for st in range(ns):
        hstate_scr[st] = hlast[st]

    for st in range(ns):
        x = x_ref[st, 0]
        y = (b_scr[st] * gelu_scr[st]).astype(jnp.bfloat16)
        mix = _dot(y, wout_ref[...])
        _post_mix(x, mix, mod_ref.at[st], gffn_ref, rw_cat_ref, rw_hi_ref,
                  xo_ref.at[st], h2_ref.at[st], lg_ref.at[st])


def _lru_pre_scan(x_ref, mod_ref, gmix_ref, win_ref, cw_ref, cb_ref, wr_ref, br_ref,
                  wi_ref, bi_ref, lam_ref, conv_scr, a_scr, b_scr, gelu_scr):
    ts = x_ref.shape[1]
    w = D_MODEL
    x = x_ref[0]
    h = _pre_norm(x, mod_ref, gmix_ref, 0, 1).astype(jnp.bfloat16)
    gate_br = _dot(h, win_ref[:, 0:w])
    gelu_scr[...] = 0.5 * gate_br * (1.0 + jnp.tanh(
        0.7978845608028654 * (gate_br + 0.044715 * gate_br * gate_br * gate_br)))
    xb = _dot(h, win_ref[:, w:2 * w])

    conv_scr[SUBLANES:SUBLANES + ts, :] = xb
    xc = cb_ref[...] + xb * cw_ref[CONV_W - 1:CONV_W, :]
    for j in range(CONV_W - 1):
        back = CONV_W - 1 - j
        xc = xc + conv_scr[SUBLANES - back:SUBLANES - back + ts, :] * cw_ref[j:j + 1, :]
    conv_scr[0:SUBLANES, :] = conv_scr[ts:ts + SUBLANES, :]

    xcb = xc.astype(jnp.bfloat16)
    rs, iis = [], []
    for hd in range(LRU_BLOCKS):
        sl = slice(hd * LRU_BLOCK_W, (hd + 1) * LRU_BLOCK_W)
        rs.append(_dot(xcb[:, sl], wr_ref[hd]))
        iis.append(_dot(xcb[:, sl], wi_ref[hd]))
    r = _sigmoid(jnp.concatenate(rs, axis=1) + br_ref[...])
    ig = _sigmoid(jnp.concatenate(iis, axis=1) + bi_ref[...])

    lam = lam_ref[...]
    softplus_neg_lam = jnp.maximum(-lam, 0.0) + jnp.log1p(jnp.exp(-jnp.abs(lam)))
    log_a = (-LRU_C * r) * softplus_neg_lam
    a = jnp.exp(log_a)
    mult = jnp.sqrt(-jnp.tanh(log_a) * (a * a + 1.0))
    bb = (xc * ig) * mult

    a = a.reshape(ts // SUBLANES, SUBLANES, w)
    bb = bb.reshape(ts // SUBLANES, SUBLANES, w)
    rowi = lax.broadcasted_iota(jnp.int32, a.shape, 1)
    for d in (1, 2, 4):
        keep = rowi >= d
        a_sh = jnp.where(keep, pltpu.roll(a, d, 1), 1.0)
        b_sh = jnp.where(keep, pltpu.roll(bb, d, 1), 0.0)
        bb = a * b_sh + bb
        a = a * a_sh
    a_scr[...] = a.reshape(ts, w)
    b_scr[...] = bb.reshape(ts, w)


def _interleave(gens, lag):
    live = [True] * len(gens)
    tick = 0
    while any(live):
        for i, gen in enumerate(gens):
            if live[i] and tick >= i * lag:
                try:
                    next(gen)
                except StopIteration:
                    live[i] = False
        tick += 1


def _lru_layer_call(x, mod, gmix, gffn, w_in, conv_w, conv_b, w_r, b_r, w_i, b_i, lam, w_out,
                    rw_cat, rw_hi):
    bsz, seq, d = x.shape
    ts = MIX_TS
    ns = MIX_STREAMS
    hb = bsz // ns
    io_in, out_specs, out_shapes = _mixer_io_specs(bsz, seq, ts)
    consts = (gmix, gffn, w_in, conv_w, conv_b, w_r, b_r, w_i, b_i, lam, w_out,
              rw_cat, rw_hi)
    outs = pl.pallas_call(
        _lru_body,
        grid=(hb, seq // ts),
        in_specs=io_in + [_const_spec(a.shape) for a in consts],
        out_specs=out_specs,
        out_shape=out_shapes,
        scratch_shapes=[
            pltpu.VMEM((ns, SUBLANES + ts, d), jnp.float32),
            pltpu.VMEM((ns, 1, d), jnp.float32),
            pltpu.VMEM((ns, ts, d), jnp.float32),
            pltpu.VMEM((ns, ts, d), jnp.float32),
            pltpu.VMEM((ns, ts, d), jnp.float32),
        ],
        compiler_params=pltpu.CompilerParams(
            dimension_semantics=("arbitrary", "arbitrary"),
            vmem_limit_bytes=VMEM_LIMIT),
        name="lru_layer",
    )(x.reshape(ns, hb, seq, d), mod.reshape(ns, hb, 6, d), *consts)
    return _mixer_unstream(outs, bsz, seq)


def _route_body(lg_ref, rbias_ref, pos0_ref, pos1_ref, w0_ref, off_ref, cnt_ref):
    seq = lg_ref.shape[2]
    blk = ROUTE_BLOCK
    s = _sigmoid_rel(lg_ref[0])
    m, wgt, sub = _route_select(s, rbias_ref[...])
    mf = m.astype(jnp.float32)

    row = lax.broadcasted_iota(jnp.int32, (blk, blk), 0)
    col = lax.broadcasted_iota(jnp.int32, (blk, blk), 1)
    before = (row < col).astype(jnp.bfloat16)
    carry = jnp.zeros((N_EXPERTS, 1), jnp.float32)
    ranks = []
    for i in range(seq // blk):
        mb = mf[:, i * blk:(i + 1) * blk]
        ranks.append(_dot(mb.astype(jnp.bfloat16), before) + carry)
        carry = carry + jnp.sum(mb, axis=1, keepdims=True)
    rnk = jnp.concatenate(ranks, axis=1)
    cnt_ref[0] = jnp.broadcast_to(carry, (N_EXPERTS, LANES)).astype(jnp.int32)

    eidx = sub.astype(jnp.float32)
    e_lo = jnp.min(jnp.where(m, eidx, 99.0), axis=0, keepdims=True)
    e_hi = jnp.max(jnp.where(m, eidx, -1.0), axis=0, keepdims=True)
    is_lo = m & (eidx == e_lo)
    is_hi = m & (eidx == e_hi)
    p_lo = jnp.sum(jnp.where(is_lo, rnk, 0.0), axis=0, keepdims=True)
    p_hi = jnp.sum(jnp.where(is_hi, rnk, 0.0), axis=0, keepdims=True)
    w0_ref[0] = jnp.sum(jnp.where(is_lo, wgt, 0.0), axis=0, keepdims=True)

    off = jnp.zeros((1, 1), jnp.float32)
    for e in range(N_EXPERTS):
        off_ref[0, e:e + 1, :] = jnp.broadcast_to(off, (1, LANES)).astype(jnp.int32)
        p_lo = p_lo + jnp.where(e_lo == float(e), off, 0.0)
        p_hi = p_hi + jnp.where(e_hi == float(e), off, 0.0)
        off = off + carry[e:e + 1, :]
    pos0_ref[0] = p_lo.astype(jnp.int32) * ROWS_PER_TOKEN
    pos1_ref[0] = p_hi.astype(jnp.int32) * ROWS_PER_TOKEN


def _route_call(logits, rbias):
    bsz, _, seq = logits.shape
    row_spec = pl.BlockSpec((1, 1, seq), lambda b: (b, 0, 0))
    tab_spec = pl.BlockSpec((1, N_EXPERTS, LANES), lambda b: (b, 0, 0))
    return pl.pallas_call(
        _route_body,
        grid=(bsz,),
        in_specs=[
            pl.BlockSpec((1, N_EXPERTS, seq), lambda b: (b, 0, 0)),
            pl.BlockSpec((N_EXPERTS, 1), lambda b: (0, 0)),
        ],
        out_specs=[row_spec, row_spec, row_spec, tab_spec, tab_spec],
        out_shape=[
            jax.ShapeDtypeStruct((bsz, 1, seq), jnp.int32),
            jax.ShapeDtypeStruct((bsz, 1, seq), jnp.int32),
            jax.ShapeDtypeStruct((bsz, 1, seq), jnp.float32),
            jax.ShapeDtypeStruct((bsz, N_EXPERTS, LANES), jnp.int32),
            jax.ShapeDtypeStruct((bsz, N_EXPERTS, LANES), jnp.int32),
        ],
        compiler_params=pltpu.CompilerParams(dimension_semantics=("arbitrary",)),
        name="moe_route",
    )(logits, rbias)


def _scatter_tokens(pos0_ref, pos1_ref, h2_ref, slots, idx0, lo, cnt):
    R = ROWS_PER_TOKEN
    for u in range(lo, lo + cnt):
        p0 = pos0_ref[0, 0, idx0 + u]
        p1 = pos1_ref[0, 0, idx0 + u]
        val = h2_ref[u * R:(u + 1) * R, :]
        slots[pl.ds(pl.multiple_of(p0, R), R), :] = val
        slots[pl.ds(pl.multiple_of(p1, R), R), :] = val


def _combine_tokens(pos0_ref, pos1_ref, wts_ref, slots, tm_scr, idx0, lo, cnt):
    R = ROWS_PER_TOKEN
    for u in range(lo, lo + cnt):
        p0 = pos0_ref[0, 0, idx0 + u]
        p1 = pos1_ref[0, 0, idx0 + u]
        w0 = wts_ref[0, 0, idx0 + u]
        za = slots[pl.ds(pl.multiple_of(p0, R), R), :]
        zb = slots[pl.ds(pl.multiple_of(p1, R), R), :]
        tm_scr[u * R:(u + 1) * R, :] = zb + w0 * (za - zb)


def _combine_epilogue(tm_scr, x_ref, mod_ref, fin_ref, o_ref, lo, cnt, final_norm):
    R = ROWS_PER_TOKEN
    moe = jnp.concatenate(
        [tm_scr[pl.ds(lo * R + c, cnt, stride=R), :] for c in range(R)], axis=1)
    out = x_ref[lo:lo + cnt, :] + mod_ref[0, 5:6, :] * moe
    if final_norm:
        out = _rms(out, fin_ref[...])
    o_ref[lo:lo + cnt, :] = out


def _expert_tile_hooked(slots, wg_ref, wu_ref, wd_ref, ws, slot0, n_valid, tm, hooks):
    R = ROWS_PER_TOKEN
    nw = 256
    row0 = pl.multiple_of(slot0 * R, R)
    xs = [slots[pl.ds(row0 + c, tm, stride=R), :] for c in range(R)]
    xt = jnp.concatenate(xs, axis=1).astype(jnp.bfloat16)
    ok = lax.broadcasted_iota(jnp.int32, (tm, LANES), 0) < n_valid
    hooks = list(hooks)

    def run_hook():
        if hooks:
            hooks.pop(0)()

    gate, up = [], []
    for i in range(D_EXPERT // nw):
        gate.append(_dot(xt, wg_ref[ws, :, i * nw:(i + 1) * nw]))
        run_hook()
    for i in range(D_EXPERT // nw):
        up.append(_dot(xt, wu_ref[ws, :, i * nw:(i + 1) * nw]))
        run_hook()
    he = (_silu(jnp.concatenate(gate, axis=1)) * jnp.concatenate(up, axis=1)).astype(jnp.bfloat16)
    for i in range(D_MODEL // nw):
        y = _dot(he, wd_ref[ws, :, i * nw:(i + 1) * nw])
        for cc in range(nw // LANES):
            c = i * (nw // LANES) + cc
            slots[pl.ds(row0 + c, tm, stride=R), :] = jnp.where(
                ok, y[:, cc * LANES:(cc + 1) * LANES], xs[c])
        run_hook()
    while hooks:
        run_hook()


def _moe_body(off_sm, cnt_sm, sc0_ref, sc1_ref, cb0_ref, cb1_ref, wts_ref, h2_ref, x_ref,
              mod_ref, fin_ref, wg_hbm, wu_hbm, wd_hbm, o_ref, slots_a, slots_b, tm_scr,
              wg_ref, wu_ref, wd_ref, w_sem, *, n_blocks, layer, final_norm):
    r = pl.program_id(0)
    e = pl.program_id(1)

    g = r * N_EXPERTS + e
    n_steps = (n_blocks + 2) * N_EXPERTS

    def w_needed(step):
        rnd = step // N_EXPERTS
        return (rnd >= 1) & (rnd <= n_blocks) & (step < n_steps)

    def w_copies(step):
        ex = layer * N_EXPERTS + step % N_EXPERTS
        sl = step % MOE_WRING
        return [pltpu.make_async_copy(src.at[ex], dst.at[sl], w_sem.at[i, sl])
                for i, (src, dst) in enumerate(((wg_hbm, wg_ref), (wu_hbm, wu_ref),
                                                (wd_hbm, wd_ref)))]

    def w_start(step):
        @pl.when(w_needed(step))
        def _():
            for cp in w_copies(step):
                cp.start()

    @pl.when(g == 0)
    def _():
        for ahead in range(MOE_WRING - 1):
            w_start(g + ahead)

    w_start(g + MOE_WRING - 1)

    @pl.when(w_needed(g))
    def _():
        for cp in w_copies(g):
            cp.wait()

    ws = g % MOE_WRING
    R = ROWS_PER_TOKEN
    U = MOE_UNROLL
    tm = MOE_TM
    mv = MOE_MOVE
    half = N_EXPERTS // 2
    n_slots = 2 * half * mv
    n_hooks = 2 * (D_EXPERT // 256) + D_MODEL // 256
    per_hook = mv // n_hooks

    @pl.when((r == 0) & (e == 0))
    def _():
        zeros = jnp.zeros((tm * R, LANES), jnp.float32)
        slots_a[n_slots * R:(n_slots + tm) * R, :] = zeros
        slots_b[n_slots * R:(n_slots + tm) * R, :] = zeros

    do_exp = (r >= 1) & (r <= n_blocks)
    do_comb = (r >= 2) & (e < half)
    do_scat = (r < n_blocks) & (e >= half)
    b_exp = jnp.clip(r - 1, 0, n_blocks - 1)
    off = off_sm[b_exp, e]
    n = jnp.where(do_exp, cnt_sm[b_exp, e], 0)
    comb_idx0 = e * mv
    scat_idx0 = (e - half) * mv
    small = tm // 2
    first_rows = jnp.where(n > small, tm, jnp.where(n > 0, small, 0))
    fused = do_comb | do_scat
    start = jnp.where(fused, first_rows, 0)
    rest = jnp.maximum(n - start, 0)
    rest_big = rest // tm + ((rest % tm) > small).astype(jnp.int32)
    rest_tail = rest - rest_big * tm

    def round_body(exp_buf, mov_buf):
        tile = functools.partial(_expert_tile_hooked, exp_buf, wg_ref, wu_ref, wd_ref, ws)

        def comb_hooks():
            def hook(k):
                def run():
                    if k < n_hooks:
                        _combine_tokens(cb0_ref, cb1_ref, wts_ref, mov_buf, tm_scr, comb_idx0,
                                        k * per_hook, per_hook)
                    if k >= 1:
                        _combine_epilogue(tm_scr, x_ref, mod_ref, fin_ref, o_ref,
                                          (k - 1) * per_hook, per_hook, final_norm)
                return run
            return [hook(k) for k in range(n_hooks + 1)]

        def scat_hooks():
            def hook(k):
                return lambda: _scatter_tokens(sc0_ref, sc1_ref, h2_ref, mov_buf, scat_idx0,
                                               k * per_hook, per_hook)
            return [hook(k) for k in range(n_hooks)]

        for rows in (tm, small):
            @pl.when((first_rows == rows) & do_comb)
            def _():
                tile(off, n, rows, comb_hooks())

            @pl.when((first_rows == rows) & do_scat)
            def _():
                tile(off, n, rows, scat_hooks())

        @pl.when(first_rows == 0)
        def _():
            @pl.when(do_comb)
            def _():
                def body(i, carry):
                    i0 = comb_idx0 + U * i
                    for u in range(U):
                        p0 = cb0_ref[0, 0, i0 + u]
                        p1 = cb1_ref[0, 0, i0 + u]
                        w0 = wts_ref[0, 0, i0 + u]
                        za = mov_buf[pl.ds(pl.multiple_of(p0, R), R), :]
                        zb = mov_buf[pl.ds(pl.multiple_of(p1, R), R), :]
                        r0 = pl.multiple_of(i * (U * R), U * R) + u * R
                        tm_scr[pl.ds(r0, R), :] = zb + w0 * (za - zb)
                    return carry

                lax.fori_loop(0, mv // U, body, 0)
                _combine_epilogue(tm_scr, x_ref, mod_ref, fin_ref, o_ref, 0, mv, final_norm)

            @pl.when(do_scat)
            def _():
                def body(i, carry):
                    i0 = scat_idx0 + U * i
                    for u in range(U):
                        p0 = sc0_ref[0, 0, i0 + u]
                        p1 = sc1_ref[0, 0, i0 + u]
                        r0 = pl.multiple_of(i * (U * R), U * R) + u * R
                        val = h2_ref[pl.ds(r0, R), :]
                        mov_buf[pl.ds(pl.multiple_of(p0, R), R), :] = val
                        mov_buf[pl.ds(pl.multiple_of(p1, R), R), :] = val
                    return carry

                lax.fori_loop(0, mv // U, body, 0)

        def tile_body(j, carry):
            done = start + j * tm
            tile(off + done, n - done, tm, [])
            return carry

        lax.fori_loop(0, rest_big, tile_body, 0)

        @pl.when(rest_tail > 0)
        def _():
            done = start + rest_big * tm
            tile(off + done, n - done, small, [])

    @pl.when(r % 2 == 0)
    def _():
        round_body(slots_b, slots_a)

    @pl.when(r % 2 == 1)
    def _():
        round_body(slots_a, slots_b)


def _moe_call(off, cnt, pos0, pos1, w0, h2_tm, x, mod, fin_g, w_gate, w_up, w_down, layer,
              final_norm):
    bsz, seq, d = x.shape
    mv = MOE_MOVE
    half = N_EXPERTS // 2
    assert seq == half * mv
    n_slots = 2 * seq
    last = bsz - 1

    def scat_blk(r, e, *_):
        return (jnp.minimum(r, last) * half + jnp.clip(e - half, 0, half - 1), 0)

    def comb_blk(r, e, *_):
        return (jnp.where(r < 2, 0, (r - 2) * half + jnp.minimum(e, half - 1)), 0)

    def scat_row(r, e, *_):
        return (jnp.minimum(r, last), 0, 0)

    def comb_row(r, e, *_):
        return (jnp.clip(r - 2, 0, last), 0, 0)

    body = functools.partial(_moe_body, n_blocks=bsz, layer=layer, final_norm=final_norm)
    slot_rows = (n_slots + MOE_TM) * ROWS_PER_TOKEN
    out = pl.pallas_call(
        body,
        grid_spec=pltpu.PrefetchScalarGridSpec(
            num_scalar_prefetch=2,
            grid=(bsz + 2, N_EXPERTS),
            in_specs=[
                pl.BlockSpec((1, 1, seq), scat_row, memory_space=pltpu.SMEM),
                pl.BlockSpec((1, 1, seq), scat_row, memory_space=pltpu.SMEM),
                pl.BlockSpec((1, 1, seq), comb_row, memory_space=pltpu.SMEM),
                pl.BlockSpec((1, 1, seq), comb_row, memory_space=pltpu.SMEM),
                pl.BlockSpec((1, 1, seq), comb_row, memory_space=pltpu.SMEM),
                pl.BlockSpec((mv * ROWS_PER_TOKEN, LANES), scat_blk),
                pl.BlockSpec((mv, d), comb_blk),
                pl.BlockSpec((1, 6, d), comb_row),
                pl.BlockSpec((1, d), lambda r, e, *_: (0, 0)),
                pl.BlockSpec(memory_space=pl.ANY),
                pl.BlockSpec(memory_space=pl.ANY),
                pl.BlockSpec(memory_space=pl.ANY),
            ],
            out_specs=pl.BlockSpec((mv, d), comb_blk),
            scratch_shapes=[
                pltpu.VMEM((slot_rows, LANES), jnp.float32),
                pltpu.VMEM((slot_rows, LANES), jnp.float32),
                pltpu.VMEM((mv * ROWS_PER_TOKEN, LANES), jnp.float32),
                pltpu.VMEM((MOE_WRING, d, D_EXPERT), jnp.bfloat16),
                pltpu.VMEM((MOE_WRING, d, D_EXPERT), jnp.bfloat16),
                pltpu.VMEM((MOE_WRING, D_EXPERT, d), jnp.bfloat16),
                pltpu.SemaphoreType.DMA((3, MOE_WRING)),
            ],
        ),
        out_shape=jax.ShapeDtypeStruct((bsz * seq, d), jnp.float32),
        compiler_params=pltpu.CompilerParams(
            dimension_semantics=("arbitrary", "arbitrary"),
            vmem_limit_bytes=MOE_VMEM_LIMIT),
        name="moe",
    )(off, cnt, pos0, pos1, pos0, pos1, w0, h2_tm, x.reshape(bsz * seq, d), mod, fin_g,
      w_gate, w_up, w_down)
    return out.reshape(bsz, seq, d)


def _moe_layer(x, h2_tm, logits, rbias, mod, fin_g, w_gate, w_up, w_down, layer, final_norm):
    pos0, pos1, w0, off, cnt = _route_call(logits, rbias)
    return _moe_call(off[:, :, 0], cnt[:, :, 0], pos0, pos1, w0, h2_tm, x, mod, fin_g,
                     w_gate, w_up, w_down, layer, final_norm)


def kernel(x, c, gla_w_in, gla_w_gate_up, gla_b_gate, gla_norm_g, gla_w_out, lru_w_in, lru_conv_w, lru_conv_b, lru_w_r, lru_b_r, lru_w_i, lru_b_i, lru_lambda, lru_w_out, router_w, router_bias, moe_w_gate, moe_w_up, moe_w_down, norm_mix_g, norm_ffn_g, ada_w, ada_b, final_norm_g):
    bf = jnp.bfloat16
    depth = ada_w.shape[0]
    bsz = x.shape[0]
    d = D_MODEL
    mod_all = _ada_call(c, ada_w, ada_b).reshape(depth, bsz, 6, d)

    rw_t = router_w.T
    rw_hi = rw_t.astype(bf)
    rw_lo = (rw_t - rw_hi.astype(jnp.float32)).astype(bf)
    rw_cat = jnp.concatenate([rw_hi, rw_lo], axis=0)
    rbias = router_bias.reshape(N_EXPERTS, 1)
    fin_g = final_norm_g.reshape(1, d)
    expert_w = None

    for i in range(depth):
        j = i // 2
        mod = mod_all[i]
        gmix = norm_mix_g[i].reshape(1, d)
        gffn = norm_ffn_g[i].reshape(1, d)
        if i % 2 == 0:
            w_in = jnp.pad(gla_w_in[j], ((0, 0), (0, GLA_RANK_PAD - GLA_GATE_RANK))).astype(bf)
            w_gu = jnp.pad(gla_w_gate_up[j], ((0, GLA_RANK_PAD - GLA_GATE_RANK), (0, 0))).astype(bf)
            cast_src = () if expert_w is not None else tuple(
                w.reshape(-1, d) for w in (moe_w_gate, moe_w_up, moe_w_down))
            (x, h2_tm, logits), cast = _gla_layer_call(
                x, mod, gmix, gffn, w_in, w_gu, gla_b_gate[j].reshape(1, GLA_QK),
                gla_norm_g[j].reshape(1, GLA_DV), gla_w_out[j].astype(bf),
                rw_cat, rw_hi, cast_src)
            if cast:
                expert_w = (cast[0].reshape(depth * N_EXPERTS, d, D_EXPERT),
                            cast[1].reshape(depth * N_EXPERTS, d, D_EXPERT),
                            cast[2].reshape(depth * N_EXPERTS, D_EXPERT, d))
        else:
            x, h2_tm, logits = _lru_layer_call(
                x, mod, gmix, gffn, lru_w_in[j].astype(bf), lru_conv_w[j],
                lru_conv_b[j].reshape(1, d), lru_w_r[j].astype(bf),
                lru_b_r[j].reshape(1, d), lru_w_i[j].astype(bf), lru_b_i[j].reshape(1, d),
                lru_lambda[j].reshape(1, d), lru_w_out[j].astype(bf), rw_cat, rw_hi)
        x = _moe_layer(x, h2_tm, logits, rbias, mod, fin_g, *expert_w,
                       layer=i, final_norm=(i == depth - 1))
    return x
```

```python
import functools

import jax
import jax.numpy as jnp
from jax import lax
from jax.experimental import pallas as pl
from jax.experimental.pallas import tpu as pltpu

D_MODEL = 1024
CHUNK = 64
EPS = 1e-6

GLA_HEADS = 4
GLA_DK = 128
GLA_DV = 256
GLA_QK = GLA_HEADS * GLA_DK
GLA_VD = GLA_HEADS * GLA_DV
GLA_GATE_RANK = 16
GLA_GATE_TAU = 16.0
GLA_RANK_PAD = 128
GLA_IN_PAD = 2 * GLA_QK + 2 * GLA_VD + GLA_RANK_PAD

LRU_BLOCKS = 4
LRU_BLOCK_W = D_MODEL // LRU_BLOCKS
CONV_W = 4
LRU_C = 8.0

N_EXPERTS = 16
N_GROUPS = 4
EPG = N_EXPERTS // N_GROUPS
D_EXPERT = 512

LANES = 128
SUBLANES = 8
ROWS_PER_TOKEN = D_MODEL // LANES

MIX_TS = 256
MIX_STREAMS = 2
GLA_STAGGER = 6
ROUTE_BLOCK = 256
MOE_TM = 256
MOE_UNROLL = 8
MOE_MOVE = 256
MOE_WRING = 3
VMEM_LIMIT = 56 * 1024 * 1024
MOE_VMEM_LIMIT = 60 * 1024 * 1024


def _dot(a, b):
    return jnp.dot(a, b, preferred_element_type=jnp.float32)


def _dot_nt(a, b):
    return lax.dot_general(a, b, (((1,), (1,)), ((), ())),
                           preferred_element_type=jnp.float32)


def _dot_tn(a, b):
    return lax.dot_general(a, b, (((0,), (0,)), ((), ())),
                           preferred_element_type=jnp.float32)


def _split_bf16(x):
    hi = x.astype(jnp.bfloat16)
    lo = (x - hi.astype(jnp.float32)).astype(jnp.bfloat16)
    return hi, lo


def _sigmoid(x):
    return 0.5 * jnp.tanh(0.5 * x) + 0.5


def _sigmoid_rel(x):
    return 1.0 / (1.0 + jnp.exp(-x))


def _silu(x):
    return x * _sigmoid(x)


def _rms(x, g):
    inv = lax.rsqrt(jnp.mean(x * x, axis=-1, keepdims=True) + EPS)
    return x * inv * g


def _ada_body(c_ref, w_ref, b_ref, o_ref):
    c_hi, c_lo = _split_bf16(_silu(c_ref[...]))
    w_hi, w_lo = _split_bf16(w_ref[0])
    o_ref[0] = _dot(c_hi, w_hi) + (_dot(c_lo, w_hi) + _dot(c_hi, w_lo)) + b_ref[0]


def _ada_call(c, ada_w, ada_b):
    depth, d, n = ada_w.shape
    bsz = c.shape[0]
    tn = 1024
    return pl.pallas_call(
        _ada_body,
        grid=(depth, n // tn),
        in_specs=[
            pl.BlockSpec((bsz, d), lambda l, j: (0, 0)),
            pl.BlockSpec((1, d, tn), lambda l, j: (l, 0, j)),
            pl.BlockSpec((1, 1, tn), lambda l, j: (l, 0, j)),
        ],
        out_specs=pl.BlockSpec((1, bsz, tn), lambda l, j: (l, 0, j)),
        out_shape=jax.ShapeDtypeStruct((depth, bsz, n), jnp.float32),
        compiler_params=pltpu.CompilerParams(
            dimension_semantics=("arbitrary", "arbitrary"),
            vmem_limit_bytes=VMEM_LIMIT),
        name="ada",
    )(c, ada_w, ada_b.reshape(depth, 1, n))


def _pre_norm(x, mod_ref, g_ref, shift_row, scale_row):
    shift = mod_ref[0, shift_row:shift_row + 1, :]
    scale = mod_ref[0, scale_row:scale_row + 1, :]
    return _rms(x, g_ref[...]) * (1.0 + scale) + shift


def _group_partner(x, k, sub):
    n = x.shape[0]
    fwd = pltpu.roll(x, n - k, 0)
    back = pltpu.roll(x, EPG - k, 0)
    wrapped = (sub % EPG) + k >= EPG
    return jnp.where(wrapped, back, fwd), wrapped


def _router_logits(h2, rw_cat_ref, rw_hi_ref):
    h_hi, h_lo = _split_bf16(h2)
    p1 = _dot_nt(rw_cat_ref[...], h_hi)
    p2 = _dot_nt(rw_hi_ref[...], h_lo)
    return p1[0:N_EXPERTS] + p1[N_EXPERTS:2 * N_EXPERTS] + p2


def _route_select(s, rbias):
    n = s.shape[1]
    sel = s + rbias
    sub = lax.broadcasted_iota(jnp.int32, (N_EXPERTS, n), 0)

    pair_best = None
    rank = jnp.zeros((N_EXPERTS, n), jnp.float32)
    for k in range(1, EPG):
        p, wrapped = _group_partner(sel, k, sub)
        ps = sel + p
        pair_best = ps if pair_best is None else jnp.maximum(pair_best, ps)
        ahead = (p > sel) | ((p == sel) & wrapped)
        rank = rank + ahead.astype(jnp.float32)
    gscore = pair_best
    for k in range(1, EPG):
        p, _ = _group_partner(pair_best, k, sub)
        gscore = jnp.maximum(gscore, p)
    chosen = jnp.ones((N_EXPERTS, n), jnp.bool_)
    grp = sub // EPG
    for j in range(1, N_GROUPS):
        other = pltpu.roll(gscore, N_EXPERTS - EPG * j, 0)
        other_is_later = grp + j < N_GROUPS
        chosen = chosen & ((gscore > other) | ((gscore == other) & other_is_later))
    m = chosen & (rank < 2.0)
    sm = s * m.astype(jnp.float32)
    wgt = sm / jnp.sum(sm, axis=0, keepdims=True)
    return m, wgt, sub


def _post_mix(x, mix, mod_ref, gffn_ref, rw_cat_ref, rw_hi_ref, xo_ref, h2_ref, lg_ref):
    ts = x.shape[0]
    x_new = x + mod_ref[0, 2:3, :] * mix
    xo_ref[0] = x_new
    h2 = _pre_norm(x_new, mod_ref, gffn_ref, 3, 4)
    for c in range(ROWS_PER_TOKEN):
        h2_ref[pl.ds(c, ts, stride=ROWS_PER_TOKEN), :] = h2[:, c * LANES:(c + 1) * LANES]
    lg_ref[0] = _router_logits(h2, rw_cat_ref, rw_hi_ref)


def _mixer_io_specs(bsz, seq, ts):
    ns = MIX_STREAMS
    hb = bsz // ns
    nt = seq // ts
    in_specs = [
        pl.BlockSpec((ns, 1, ts, D_MODEL), lambda b, t: (0, b, t, 0)),
        pl.BlockSpec((ns, 1, 6, D_MODEL), lambda b, t: (0, b, 0, 0)),
    ]
    out_specs = [
        pl.BlockSpec((ns, 1, ts, D_MODEL), lambda b, t: (0, b, t, 0)),
        pl.BlockSpec((ns, ts * ROWS_PER_TOKEN, LANES), lambda b, t: (0, b * nt + t, 0)),
        pl.BlockSpec((ns, 1, N_EXPERTS, ts), lambda b, t: (0, b, 0, t)),
    ]
    out_shapes = [
        jax.ShapeDtypeStruct((ns, hb, seq, D_MODEL), jnp.float32),
        jax.ShapeDtypeStruct((ns, hb * seq * ROWS_PER_TOKEN, LANES), jnp.float32),
        jax.ShapeDtypeStruct((ns, hb, N_EXPERTS, seq), jnp.float32),
    ]
    return in_specs, out_specs, out_shapes


def _mixer_unstream(outs, bsz, seq):
    x, h2_tm, logits = outs
    return (x.reshape(bsz, seq, D_MODEL), h2_tm.reshape(bsz * seq * ROWS_PER_TOKEN, LANES),
            logits.reshape(bsz, N_EXPERTS, seq))


def _const_spec(shape):
    nd = len(shape)
    return pl.BlockSpec(shape, lambda b, t: (0,) * nd)


def _gla_body(x_ref, mod_ref, gmix_ref, gffn_ref, win_ref, wgu_ref, bg_ref, ng_ref,
              wout_ref, rw_cat_ref, rw_hi_ref, *rest, n_cast):
    cast_in = rest[:n_cast]
    xo_ref, h2_ref, lg_ref = rest[n_cast:n_cast + 3]
    cast_out = rest[n_cast + 3:2 * n_cast + 3]
    state_scr, o_scr = rest[2 * n_cast + 3:]

    @pl.when(pl.program_id(1) == 0)
    def _():
        state_scr[...] = jnp.zeros_like(state_scr)

    def caster():
        pieces = 4
        for src, dst in zip(cast_in, cast_out):
            rows = src.shape[0] // pieces
            for i in range(pieces):
                dst[i * rows:(i + 1) * rows, :] = src[i * rows:(i + 1) * rows, :].astype(jnp.bfloat16)
                yield

    gens = [_gla_stream(x_ref.at[st], mod_ref.at[st], gmix_ref, gffn_ref, win_ref, wgu_ref,
                        bg_ref, ng_ref, wout_ref, rw_cat_ref, rw_hi_ref,
                        xo_ref.at[st], h2_ref.at[st], lg_ref.at[st],
                        state_scr.at[st], o_scr.at[st])
            for st in range(MIX_STREAMS)]
    _interleave(gens + [caster()], GLA_STAGGER)


def _gla_stream(x_ref, mod_ref, gmix_ref, gffn_ref, win_ref, wgu_ref, bg_ref, ng_ref,
                wout_ref, rw_cat_ref, rw_hi_ref,
                xo_ref, h2_ref, lg_ref, state_scr, o_scr):
    ts = x_ref.shape[1]
    nchunk = ts // CHUNK
    half = GLA_VD // 2
    x = x_ref[0]
    h = _pre_norm(x, mod_ref, gmix_ref, 0, 1).astype(jnp.bfloat16)
    yield

    o_q, o_k, o_v, o_g, o_a = 0, GLA_QK, 2 * GLA_QK, 2 * GLA_QK + GLA_VD, 2 * GLA_QK + 2 * GLA_VD
    q = (_dot(h, win_ref[:, o_q:o_k]) * (GLA_DK ** -0.5)).astype(jnp.bfloat16)
    yield
    k = _dot(h, win_ref[:, o_k:o_v])
    yield
    v = jnp.concatenate(
        [_dot(h, win_ref[:, o_v + i * half:o_v + (i + 1) * half]).astype(jnp.bfloat16)
         for i in range(2)], axis=1)
    yield
    a_lr = _dot(h, win_ref[:, o_a:o_a + GLA_RANK_PAD])
    a_hi, a_lo = _split_bf16(a_lr)
    z2 = _dot(jnp.concatenate([a_hi, a_lo], axis=0), wgu_ref[...])
    z = z2[0:ts] + z2[ts:2 * ts] + bg_ref[...]
    log_a = -(jnp.maximum(-z, 0.0) + jnp.log1p(jnp.exp(-jnp.abs(z)))) * (1.0 / GLA_GATE_TAU)
    yield

    row = lax.broadcasted_iota(jnp.int32, (ts, ts), 0)
    col = lax.broadcasted_iota(jnp.int32, (ts, ts), 1)
    tri = ((row // CHUNK == col // CHUNK) & (col <= row)).astype(jnp.bfloat16)
    l_hi, l_lo = _split_bf16(log_a)
    cum2 = _dot(tri, jnp.concatenate([l_hi, l_lo], axis=1))
    cum = cum2[:, 0:GLA_QK] + cum2[:, GLA_QK:2 * GLA_QK]
    yield

    ng = ng_ref[...]
    kv_t, gammas = [], []
    for j in range(nchunk):
        r0 = j * CHUNK
        cum_j = cum[r0:r0 + CHUNK]
        total = cum_j[CHUNK - 1:CHUNK]
        k_dec = (k[r0:r0 + CHUNK] * jnp.exp(total - cum_j)).astype(jnp.bfloat16)
        gammas.append(jnp.exp(total))
        for hd in range(GLA_HEADS):
            ks = slice(hd * GLA_DK, (hd + 1) * GLA_DK)
            vs = slice(hd * GLA_DV, (hd + 1) * GLA_DV)
            kv_t.append(_dot_tn(v[r0:r0 + CHUNK, vs], k_dec[:, ks]))
        yield
    states = []
    for hd in range(GLA_HEADS):
        ks = slice(hd * GLA_DK, (hd + 1) * GLA_DK)
        s_cur = state_scr[hd]
        for j in range(nchunk):
            s_cur = s_cur * gammas[j][:, ks] + kv_t[j * GLA_HEADS + hd]
            states.append(s_cur.astype(jnp.bfloat16))
        state_scr[hd] = s_cur
        yield
    for hd in range(GLA_HEADS):
        ks = slice(hd * GLA_DK, (hd + 1) * GLA_DK)
        vs = slice(hd * GLA_DV, (hd + 1) * GLA_DV)
        for j in range(nchunk):
            r0 = j * CHUNK
            o = _dot_nt(q[r0:r0 + CHUNK, ks], states[hd * nchunk + j])
            o_scr[r0:r0 + CHUNK, vs] = _rms(o, ng)
        yield

    og = []
    for i in range(2):
        g = _dot(h, win_ref[:, o_g + i * half:o_g + (i + 1) * half])
        og.append((o_scr[:, i * half:(i + 1) * half] * _silu(g)).astype(jnp.bfloat16))
        yield
    og = jnp.concatenate(og, axis=1)
    x_new = []
    for i in range(2):
        cols = slice(i * half, (i + 1) * half)
        mix = _dot(og, wout_ref[:, i * half:(i + 1) * half])
        x_new.append(x[:, cols] + mod_ref[0, 2:3, cols] * mix)
        yield
    x_new = jnp.concatenate(x_new, axis=1)
    xo_ref[0] = x_new
    h2 = _pre_norm(x_new, mod_ref, gffn_ref, 3, 4)
    for c in range(ROWS_PER_TOKEN):
        h2_ref[pl.ds(c, ts, stride=ROWS_PER_TOKEN), :] = h2[:, c * LANES:(c + 1) * LANES]
    yield
    lg_ref[0] = _router_logits(h2, rw_cat_ref, rw_hi_ref)
    yield


def _gla_layer_call(x, mod, gmix, gffn, w_in, w_gu, b_g, n_g, w_out, rw_cat, rw_hi, cast_src):
    bsz, seq, d = x.shape
    ts = MIX_TS
    ns = MIX_STREAMS
    hb = bsz // ns
    nt = seq // ts
    io_in, out_specs, out_shapes = _mixer_io_specs(bsz, seq, ts)
    consts = (gmix, gffn, w_in, w_gu, b_g, n_g, w_out, rw_cat, rw_hi)
    cast_specs, cast_shapes = [], []
    for a in cast_src:
        rows = a.shape[0] // (hb * nt)
        assert rows * hb * nt == a.shape[0]
        cast_specs.append(pl.BlockSpec((rows, a.shape[1]), lambda b, t: (b * nt + t, 0)))
        cast_shapes.append(jax.ShapeDtypeStruct(a.shape, jnp.bfloat16))
    outs = pl.pallas_call(
        functools.partial(_gla_body, n_cast=len(cast_src)),
        grid=(hb, nt),
        in_specs=io_in + [_const_spec(a.shape) for a in consts] + cast_specs,
        out_specs=out_specs + cast_specs,
        out_shape=out_shapes + cast_shapes,
        scratch_shapes=[
            pltpu.VMEM((ns, GLA_HEADS, GLA_DV, GLA_DK), jnp.float32),
            pltpu.VMEM((ns, ts, GLA_VD), jnp.float32),
        ],
        compiler_params=pltpu.CompilerParams(
            dimension_semantics=("arbitrary", "arbitrary"),
            vmem_limit_bytes=VMEM_LIMIT),
        name="gla_layer",
    )(x.reshape(ns, hb, seq, d), mod.reshape(ns, hb, 6, d), *consts, *cast_src)
    return _mixer_unstream(outs[:3], bsz, seq), outs[3:]


def _lru_body(x_ref, mod_ref, gmix_ref, gffn_ref, win_ref, cw_ref, cb_ref, wr_ref, br_ref,
              wi_ref, bi_ref, lam_ref, wout_ref, rw_cat_ref, rw_hi_ref,
              xo_ref, h2_ref, lg_ref,
              conv_scr, hstate_scr, a_scr, b_scr, gelu_scr):
    ns = MIX_STREAMS
    ts = x_ref.shape[2]
    w = D_MODEL

    @pl.when(pl.program_id(1) == 0)
    def _():
        conv_scr[:, 0:SUBLANES, :] = jnp.zeros((ns, SUBLANES, w), jnp.float32)
        hstate_scr[...] = jnp.zeros_like(hstate_scr)

    for st in range(ns):
        _lru_pre_scan(x_ref.at[st], mod_ref.at[st], gmix_ref, win_ref, cw_ref, cb_ref, wr_ref,
                      br_ref, wi_ref, bi_ref, lam_ref,
                      conv_scr.at[st], a_scr.at[st], b_scr.at[st], gelu_scr.at[st])

    unroll = 4

    def group_step(gi, hprev):
        for u in range(unroll):
            r0 = pl.multiple_of(gi * (unroll * SUBLANES), unroll * SUBLANES) + u * SUBLANES
            nxt = []
            for st in range(ns):
                hs = (a_scr[st, pl.ds(r0, SUBLANES), :] * hprev[st]
                      + b_scr[st, pl.ds(r0, SUBLANES), :])
                b_scr[st, pl.ds(r0, SUBLANES), :] = hs
                nxt.append(hs[SUBLANES - 1:SUBLANES, :])
            hprev = tuple(nxt)
        return hprev

    hlast = lax.fori_loop(0, ts // (unroll * SUBLANES), group_step,
                          tuple(hstate_scr[st] for st in range(ns)))
    for st in range(ns):
        hstate_scr[st] = hlast[st]

    for st in range(ns):
        x = x_ref[st, 0]
        y = (b_scr[st] * gelu_scr[st]).astype(jnp.bfloat16)
        mix = _dot(y, wout_ref[...])
        _post_mix(x, mix, mod_ref.at[st], gffn_ref, rw_cat_ref, rw_hi_ref,
                  xo_ref.at[st], h2_ref.at[st], lg_ref.at[st])


def _lru_pre_scan(x_ref, mod_ref, gmix_ref, win_ref, cw_ref, cb_ref, wr_ref, br_ref,
                  wi_ref, bi_ref, lam_ref, conv_scr, a_scr, b_scr, gelu_scr):
    ts = x_ref.shape[1]
    w = D_MODEL
    x = x_ref[0]
    h = _pre_norm(x, mod_ref, gmix_ref, 0, 1).astype(jnp.bfloat16)
    gate_br = _dot(h, win_ref[:, 0:w])
    gelu_scr[...] = 0.5 * gate_br * (1.0 + jnp.tanh(
        0.7978845608028654 * (gate_br + 0.044715 * gate_br * gate_br * gate_br)))
    xb = _dot(h, win_ref[:, w:2 * w])

    conv_scr[SUBLANES:SUBLANES + ts, :] = xb
    xc = cb_ref[...] + xb * cw_ref[CONV_W - 1:CONV_W, :]
    for j in range(CONV_W - 1):
        back = CONV_W - 1 - j
        xc = xc + conv_scr[SUBLANES - back:SUBLANES - back + ts, :] * cw_ref[j:j + 1, :]
    conv_scr[0:SUBLANES, :] = conv_scr[ts:ts + SUBLANES, :]

    xcb = xc.astype(jnp.bfloat16)
    rs, iis = [], []
    for hd in range(LRU_BLOCKS):
        sl = slice(hd * LRU_BLOCK_W, (hd + 1) * LRU_BLOCK_W)
        rs.append(_dot(xcb[:, sl], wr_ref[hd]))
        iis.append(_dot(xcb[:, sl], wi_ref[hd]))
    r = _sigmoid(jnp.concatenate(rs, axis=1) + br_ref[...])
    ig = _sigmoid(jnp.concatenate(iis, axis=1) + bi_ref[...])

    lam = lam_ref[...]
    softplus_neg_lam = jnp.maximum(-lam, 0.0) + jnp.log1p(jnp.exp(-jnp.abs(lam)))
    log_a = (-LRU_C * r) * softplus_neg_lam
    a = jnp.exp(log_a)
    mult = jnp.sqrt(-jnp.tanh(log_a) * (a * a + 1.0))
    bb = (xc * ig) * mult

    a = a.reshape(ts // SUBLANES, SUBLANES, w)
    bb = bb.reshape(ts // SUBLANES, SUBLANES, w)
    rowi = lax.broadcasted_iota(jnp.int32, a.shape, 1)
    for d in (1, 2, 4):
        keep = rowi >= d
        a_sh = jnp.where(keep, pltpu.roll(a, d, 1), 1.0)
        b_sh = jnp.where(keep, pltpu.roll(bb, d, 1), 0.0)
        bb = a * b_sh + bb
        a = a * a_sh
    a_scr[...] = a.reshape(ts, w)
    b_scr[...] = bb.reshape(ts, w)


def _interleave(gens, lag):
    live = [True] * len(gens)
    tick = 0
    while any(live):
        for i, gen in enumerate(gens):
            if live[i] and tick >= i * lag:
                try:
                    next(gen)
                except StopIteration:
                    live[i] = False
        tick += 1


def _lru_layer_call(x, mod, gmix, gffn, w_in, conv_w, conv_b, w_r, b_r, w_i, b_i, lam, w_out,
                    rw_cat, rw_hi):
    bsz, seq, d = x.shape
    ts = MIX_TS
    ns = MIX_STREAMS
    hb = bsz // ns
    io_in, out_specs, out_shapes = _mixer_io_specs(bsz, seq, ts)
    consts = (gmix, gffn, w_in, conv_w, conv_b, w_r, b_r, w_i, b_i, lam, w_out,
              rw_cat, rw_hi)
    outs = pl.pallas_call(
        _lru_body,
        grid=(hb, seq // ts),
        in_specs=io_in + [_const_spec(a.shape) for a in consts],
        out_specs=out_specs,
        out_shape=out_shapes,
        scratch_shapes=[
            pltpu.VMEM((ns, SUBLANES + ts, d), jnp.float32),
            pltpu.VMEM((ns, 1, d), jnp.float32),
            pltpu.VMEM((ns, ts, d), jnp.float32),
            pltpu.VMEM((ns, ts, d), jnp.float32),
            pltpu.VMEM((ns, ts, d), jnp.float32),
        ],
        compiler_params=pltpu.CompilerParams(
            dimension_semantics=("arbitrary", "arbitrary"),
            vmem_limit_bytes=VMEM_LIMIT),
        name="lru_layer",
    )(x.reshape(ns, hb, seq, d), mod.reshape(ns, hb, 6, d), *consts)
    return _mixer_unstream(outs, bsz, seq)


def _route_body(lg_ref, rbias_ref, pos0_ref, pos1_ref, w0_ref, off_ref, cnt_ref):
    seq = lg_ref.shape[2]
    blk = ROUTE_BLOCK
    s = _sigmoid_rel(lg_ref[0])
    m, wgt, sub = _route_select(s, rbias_ref[...])
    mf = m.astype(jnp.float32)

    row = lax.broadcasted_iota(jnp.int32, (blk, blk), 0)
    col = lax.broadcasted_iota(jnp.int32, (blk, blk), 1)
    before = (row < col).astype(jnp.bfloat16)
    carry = jnp.zeros((N_EXPERTS, 1), jnp.float32)
    ranks = []
    for i in range(seq // blk):
        mb = mf[:, i * blk:(i + 1) * blk]
        ranks.append(_dot(mb.astype(jnp.bfloat16), before) + carry)
        carry = carry + jnp.sum(mb, axis=1, keepdims=True)
    rnk = jnp.concatenate(ranks, axis=1)
    cnt_ref[0] = jnp.broadcast_to(carry, (N_EXPERTS, LANES)).astype(jnp.int32)

    eidx = sub.astype(jnp.float32)
    e_lo = jnp.min(jnp.where(m, eidx, 99.0), axis=0, keepdims=True)
    e_hi = jnp.max(jnp.where(m, eidx, -1.0), axis=0, keepdims=True)
    is_lo = m & (eidx == e_lo)
    is_hi = m & (eidx == e_hi)
    p_lo = jnp.sum(jnp.where(is_lo, rnk, 0.0), axis=0, keepdims=True)
    p_hi = jnp.sum(jnp.where(is_hi, rnk, 0.0), axis=0, keepdims=True)
    w0_ref[0] = jnp.sum(jnp.where(is_lo, wgt, 0.0), axis=0, keepdims=True)

    off = jnp.zeros((1, 1), jnp.float32)
    for e in range(N_EXPERTS):
        off_ref[0, e:e + 1, :] = jnp.broadcast_to(off, (1, LANES)).astype(jnp.int32)
        p_lo = p_lo + jnp.where(e_lo == float(e), off, 0.0)
        p_hi = p_hi + jnp.where(e_hi == float(e), off, 0.0)
        off = off + carry[e:e + 1, :]
    pos0_ref[0] = p_lo.astype(jnp.int32) * ROWS_PER_TOKEN
    pos1_ref[0] = p_hi.astype(jnp.int32) * ROWS_PER_TOKEN


def _route_call(logits, rbias):
    bsz, _, seq = logits.shape
    row_spec = pl.BlockSpec((1, 1, seq), lambda b: (b, 0, 0))
    tab_spec = pl.BlockSpec((1, N_EXPERTS, LANES), lambda b: (b, 0, 0))
    return pl.pallas_call(
        _route_body,
        grid=(bsz,),
        in_specs=[
            pl.BlockSpec((1, N_EXPERTS, seq), lambda b: (b, 0, 0)),
            pl.BlockSpec((N_EXPERTS, 1), lambda b: (0, 0)),
        ],
        out_specs=[row_spec, row_spec, row_spec, tab_spec, tab_spec],
        out_shape=[
            jax.ShapeDtypeStruct((bsz, 1, seq), jnp.int32),
            jax.ShapeDtypeStruct((bsz, 1, seq), jnp.int32),
            jax.ShapeDtypeStruct((bsz, 1, seq), jnp.float32),
            jax.ShapeDtypeStruct((bsz, N_EXPERTS, LANES), jnp.int32),
            jax.ShapeDtypeStruct((bsz, N_EXPERTS, LANES), jnp.int32),
        ],
        compiler_params=pltpu.CompilerParams(dimension_semantics=("arbitrary",)),
        name="moe_route",
    )(logits, rbias)


def _scatter_tokens(pos0_ref, pos1_ref, h2_ref, slots, idx0, lo, cnt):
    R = ROWS_PER_TOKEN
    for u in range(lo, lo + cnt):
        p0 = pos0_ref[0, 0, idx0 + u]
        p1 = pos1_ref[0, 0, idx0 + u]
        val = h2_ref[u * R:(u + 1) * R, :]
        slots[pl.ds(pl.multiple_of(p0, R), R), :] = val
        slots[pl.ds(pl.multiple_of(p1, R), R), :] = val


def _combine_tokens(pos0_ref, pos1_ref, wts_ref, slots, tm_scr, idx0, lo, cnt):
    R = ROWS_PER_TOKEN
    for u in range(lo, lo + cnt):
        p0 = pos0_ref[0, 0, idx0 + u]
        p1 = pos1_ref[0, 0, idx0 + u]
        w0 = wts_ref[0, 0, idx0 + u]
        za = slots[pl.ds(pl.multiple_of(p0, R), R), :]
        zb = slots[pl.ds(pl.multiple_of(p1, R), R), :]
        tm_scr[u * R:(u + 1) * R, :] = zb + w0 * (za - zb)


def _combine_epilogue(tm_scr, x_ref, mod_ref, fin_ref, o_ref, lo, cnt, final_norm):
    R = ROWS_PER_TOKEN
    moe = jnp.concatenate(
        [tm_scr[pl.ds(lo * R + c, cnt, stride=R), :] for c in range(R)], axis=1)
    out = x_ref[lo:lo + cnt, :] + mod_ref[0, 5:6, :] * moe
    if final_norm:
        out = _rms(out, fin_ref[...])
    o_ref[lo:lo + cnt, :] = out


def _expert_tile_hooked(slots, wg_ref, wu_ref, wd_ref, ws, slot0, n_valid, tm, hooks):
    R = ROWS_PER_TOKEN
    nw = 256
    row0 = pl.multiple_of(slot0 * R, R)
    xs = [slots[pl.ds(row0 + c, tm, stride=R), :] for c in range(R)]
    xt = jnp.concatenate(xs, axis=1).astype(jnp.bfloat16)
    ok = lax.broadcasted_iota(jnp.int32, (tm, LANES), 0) < n_valid
    hooks = list(hooks)

    def run_hook():
        if hooks:
            hooks.pop(0)()

    gate, up = [], []
    for i in range(D_EXPERT // nw):
        gate.append(_dot(xt, wg_ref[ws, :, i * nw:(i + 1) * nw]))
        run_hook()
    for i in range(D_EXPERT // nw):
        up.append(_dot(xt, wu_ref[ws, :, i * nw:(i + 1) * nw]))
        run_hook()
    he = (_silu(jnp.concatenate(gate, axis=1)) * jnp.concatenate(up, axis=1)).astype(jnp.bfloat16)
    for i in range(D_MODEL // nw):
        y = _dot(he, wd_ref[ws, :, i * nw:(i + 1) * nw])
        for cc in range(nw // LANES):
            c = i * (nw // LANES) + cc
            slots[pl.ds(row0 + c, tm, stride=R), :] = jnp.where(
                ok, y[:, cc * LANES:(cc + 1) * LANES], xs[c])
        run_hook()
    while hooks:
        run_hook()


def _moe_body(off_sm, cnt_sm, sc0_ref, sc1_ref, cb0_ref, cb1_ref, wts_ref, h2_ref, x_ref,
              mod_ref, fin_ref, wg_hbm, wu_hbm, wd_hbm, o_ref, slots_a, slots_b, tm_scr,
              wg_ref, wu_ref, wd_ref, w_sem, *, n_blocks, layer, final_norm):
    r = pl.program_id(0)
    e = pl.program_id(1)

    g = r * N_EXPERTS + e
    n_steps = (n_blocks + 2) * N_EXPERTS

    def w_needed(step):
        rnd = step // N_EXPERTS
        return (rnd >= 1) & (rnd <= n_blocks) & (step < n_steps)

    def w_copies(step):
        ex = layer * N_EXPERTS + step % N_EXPERTS
        sl = step % MOE_WRING
        return [pltpu.make_async_copy(src.at[ex], dst.at[sl], w_sem.at[i, sl])
                for i, (src, dst) in enumerate(((wg_hbm, wg_ref), (wu_hbm, wu_ref),
                                                (wd_hbm, wd_ref)))]

    def w_start(step):
        @pl.when(w_needed(step))
        def _():
            for cp in w_copies(step):
                cp.start()

    @pl.when(g == 0)
    def _():
        for ahead in range(MOE_WRING - 1):
            w_start(g + ahead)

    w_start(g + MOE_WRING - 1)

    @pl.when(w_needed(g))
    def _():
        for cp in w_copies(g):
            cp.wait()

    ws = g % MOE_WRING
    R = ROWS_PER_TOKEN
    U = MOE_UNROLL
    tm = MOE_TM
    mv = MOE_MOVE
    half = N_EXPERTS // 2
    n_slots = 2 * half * mv
    n_hooks = 2 * (D_EXPERT // 256) + D_MODEL // 256
    per_hook = mv // n_hooks

    @pl.when((r == 0) & (e == 0))
    def _():
        zeros = jnp.zeros((tm * R, LANES), jnp.float32)
        slots_a[n_slots * R:(n_slots + tm) * R, :] = zeros
        slots_b[n_slots * R:(n_slots + tm) * R, :] = zeros

    do_exp = (r >= 1) & (r <= n_blocks)
    do_comb = (r >= 2) & (e < half)
    do_scat = (r < n_blocks) & (e >= half)
    b_exp = jnp.clip(r - 1, 0, n_blocks - 1)
    off = off_sm[b_exp, e]
    n = jnp.where(do_exp, cnt_sm[b_exp, e], 0)
    comb_idx0 = e * mv
    scat_idx0 = (e - half) * mv
    small = tm // 2
    first_rows = jnp.where(n > small, tm, jnp.where(n > 0, small, 0))
    fused = do_comb | do_scat
    start = jnp.where(fused, first_rows, 0)
    rest = jnp.maximum(n - start, 0)
    rest_big = rest // tm + ((rest % tm) > small).astype(jnp.int32)
    rest_tail = rest - rest_big * tm

    def round_body(exp_buf, mov_buf):
        tile = functools.partial(_expert_tile_hooked, exp_buf, wg_ref, wu_ref, wd_ref, ws)

        def comb_hooks():
            def hook(k):
                def run():
                    if k < n_hooks:
                        _combine_tokens(cb0_ref, cb1_ref, wts_ref, mov_buf, tm_scr, comb_idx0,
                                        k * per_hook, per_hook)
                    if k >= 1:
                        _combine_epilogue(tm_scr, x_ref, mod_ref, fin_ref, o_ref,
                                          (k - 1) * per_hook, per_hook, final_norm)
                return run
            return [hook(k) for k in range(n_hooks + 1)]

        def scat_hooks():
            def hook(k):
                return lambda: _scatter_tokens(sc0_ref, sc1_ref, h2_ref, mov_buf, scat_idx0,
                                               k * per_hook, per_hook)
            return [hook(k) for k in range(n_hooks)]

        for rows in (tm, small):
            @pl.when((first_rows == rows) & do_comb)
            def _():
                tile(off, n, rows, comb_hooks())

            @pl.when((first_rows == rows) & do_scat)
            def _():
                tile(off, n, rows, scat_hooks())

        @pl.when(first_rows == 0)
        def _():
            @pl.when(do_comb)
            def _():
                def body(i, carry):
                    i0 = comb_idx0 + U * i
                    for u in range(U):
                        p0 = cb0_ref[0, 0, i0 + u]
                        p1 = cb1_ref[0, 0, i0 + u]
                        w0 = wts_ref[0, 0, i0 + u]
                        za = mov_buf[pl.ds(pl.multiple_of(p0, R), R), :]
                        zb = mov_buf[pl.ds(pl.multiple_of(p1, R), R), :]
                        r0 = pl.multiple_of(i * (U * R), U * R) + u * R
                        tm_scr[pl.ds(r0, R), :] = zb + w0 * (za - zb)
                    return carry

                lax.fori_loop(0, mv // U, body, 0)
                _combine_epilogue(tm_scr, x_ref, mod_ref, fin_ref, o_ref, 0, mv, final_norm)

            @pl.when(do_scat)
            def _():
                def body(i, carry):
                    i0 = scat_idx0 + U * i
                    for u in range(U):
                        p0 = sc0_ref[0, 0, i0 + u]
                        p1 = sc1_ref[0, 0, i0 + u]
                        r0 = pl.multiple_of(i * (U * R), U * R) + u * R
                        val = h2_ref[pl.ds(r0, R), :]
                        mov_buf[pl.ds(pl.multiple_of(p0, R), R), :] = val
                        mov_buf[pl.ds(pl.multiple_of(p1, R), R), :] = val
                    return carry

                lax.fori_loop(0, mv // U, body, 0)

        def tile_body(j, carry):
            done = start + j * tm
            tile(off + done, n - done, tm, [])
            return carry

        lax.fori_loop(0, rest_big, tile_body, 0)

        @pl.when(rest_tail > 0)
        def _():
            done = start + rest_big * tm
            tile(off + done, n - done, small, [])

    @pl.when(r % 2 == 0)
    def _():
        round_body(slots_b, slots_a)

    @pl.when(r % 2 == 1)
    def _():
        round_body(slots_a, slots_b)


def _moe_call(off, cnt, pos0, pos1, w0, h2_tm, x, mod, fin_g, w_gate, w_up, w_down, layer,
              final_norm):
    bsz, seq, d = x.shape
    mv = MOE_MOVE
    half = N_EXPERTS // 2
    assert seq == half * mv
    n_slots = 2 * seq
    last = bsz - 1

    def scat_blk(r, e, *_):
        return (jnp.minimum(r, last) * half + jnp.clip(e - half, 0, half - 1), 0)

    def comb_blk(r, e, *_):
        return (jnp.where(r < 2, 0, (r - 2) * half + jnp.minimum(e, half - 1)), 0)

    def scat_row(r, e, *_):
        return (jnp.minimum(r, last), 0, 0)

    def comb_row(r, e, *_):
        return (jnp.clip(r - 2, 0, last), 0, 0)

    body = functools.partial(_moe_body, n_blocks=bsz, layer=layer, final_norm=final_norm)
    slot_rows = (n_slots + MOE_TM) * ROWS_PER_TOKEN
    out = pl.pallas_call(
        body,
        grid_spec=pltpu.PrefetchScalarGridSpec(
            num_scalar_prefetch=2,
            grid=(bsz + 2, N_EXPERTS),
            in_specs=[
                pl.BlockSpec((1, 1, seq), scat_row, memory_space=pltpu.SMEM),
                pl.BlockSpec((1, 1, seq), scat_row, memory_space=pltpu.SMEM),
                pl.BlockSpec((1, 1, seq), comb_row, memory_space=pltpu.SMEM),
                pl.BlockSpec((1, 1, seq), comb_row, memory_space=pltpu.SMEM),
                pl.BlockSpec((1, 1, seq), comb_row, memory_space=pltpu.SMEM),
                pl.BlockSpec((mv * ROWS_PER_TOKEN, LANES), scat_blk),
                pl.BlockSpec((mv, d), comb_blk),
                pl.BlockSpec((1, 6, d), comb_row),
                pl.BlockSpec((1, d), lambda r, e, *_: (0, 0)),
                pl.BlockSpec(memory_space=pl.ANY),
                pl.BlockSpec(memory_space=pl.ANY),
                pl.BlockSpec(memory_space=pl.ANY),
            ],
            out_specs=pl.BlockSpec((mv, d), comb_blk),
            scratch_shapes=[
                pltpu.VMEM((slot_rows, LANES), jnp.float32),
                pltpu.VMEM((slot_rows, LANES), jnp.float32),
                pltpu.VMEM((mv * ROWS_PER_TOKEN, LANES), jnp.float32),
                pltpu.VMEM((MOE_WRING, d, D_EXPERT), jnp.bfloat16),
                pltpu.VMEM((MOE_WRING, d, D_EXPERT), jnp.bfloat16),
                pltpu.VMEM((MOE_WRING, D_EXPERT, d), jnp.bfloat16),
                pltpu.SemaphoreType.DMA((3, MOE_WRING)),
            ],
        ),
        out_shape=jax.ShapeDtypeStruct((bsz * seq, d), jnp.float32),
        compiler_params=pltpu.CompilerParams(
            dimension_semantics=("arbitrary", "arbitrary"),
            vmem_limit_bytes=MOE_VMEM_LIMIT),
        name="moe",
    )(off, cnt, pos0, pos1, pos0, pos1, w0, h2_tm, x.reshape(bsz * seq, d), mod, fin_g,
      w_gate, w_up, w_down)
    return out.reshape(bsz, seq, d)


def _moe_layer(x, h2_tm, logits, rbias, mod, fin_g, w_gate, w_up, w_down, layer, final_norm):
    pos0, pos1, w0, off, cnt = _route_call(logits, rbias)
    return _moe_call(off[:, :, 0], cnt[:, :, 0], pos0, pos1, w0, h2_tm, x, mod, fin_g,
                     w_gate, w_up, w_down, layer, final_norm)


def kernel(x, c, gla_w_in, gla_w_gate_up, gla_b_gate, gla_norm_g, gla_w_out, lru_w_in, lru_conv_w, lru_conv_b, lru_w_r, lru_b_r, lru_w_i, lru_b_i, lru_lambda, lru_w_out, router_w, router_bias, moe_w_gate, moe_w_up, moe_w_down, norm_mix_g, norm_ffn_g, ada_w, ada_b, final_norm_g):
    bf = jnp.bfloat16
    depth = ada_w.shape[0]
    bsz = x.shape[0]
    d = D_MODEL
    mod_all = _ada_call(c, ada_w, ada_b).reshape(depth, bsz, 6, d)

    rw_t = router_w.T
    rw_hi = rw_t.astype(bf)
    rw_lo = (rw_t - rw_hi.astype(jnp.float32)).astype(bf)
    rw_cat = jnp.concatenate([rw_hi, rw_lo], axis=0)
    rbias = router_bias.reshape(N_EXPERTS, 1)
    fin_g = final_norm_g.reshape(1, d)
    expert_w = None

    for i in range(depth):
        j = i // 2
        mod = mod_all[i]
        gmix = norm_mix_g[i].reshape(1, d)
        gffn = norm_ffn_g[i].reshape(1, d)
        if i % 2 == 0:
            w_in = jnp.pad(gla_w_in[j], ((0, 0), (0, GLA_RANK_PAD - GLA_GATE_RANK))).astype(bf)
            w_gu = jnp.pad(gla_w_gate_up[j], ((0, GLA_RANK_PAD - GLA_GATE_RANK), (0, 0))).astype(bf)
            cast_src = () if expert_w is not None else tuple(
                w.reshape(-1, w.shape[-1]) for w in (moe_w_gate, moe_w_up, moe_w_down))
            (x, h2_tm, logits), cast = _gla_layer_call(
                x, mod, gmix, gffn, w_in, w_gu, gla_b_gate[j].reshape(1, GLA_QK),
                gla_norm_g[j].reshape(1, GLA_DV), gla_w_out[j].astype(bf),
                rw_cat, rw_hi, cast_src)
            if cast:
                expert_w = (cast[0].reshape(depth * N_EXPERTS, d, D_EXPERT),
                            cast[1].reshape(depth * N_EXPERTS, d, D_EXPERT),
                            cast[2].reshape(depth * N_EXPERTS, D_EXPERT, d))
        else:
            x, h2_tm, logits = _lru_layer_call(
                x, mod, gmix, gffn, lru_w_in[j].astype(bf), lru_conv_w[j],
                lru_conv_b[j].reshape(1, d), lru_w_r[j].astype(bf),
                lru_b_r[j].reshape(1, d), lru_w_i[j].astype(bf), lru_b_i[j].reshape(1, d),
                lru_lambda[j].reshape(1, d), lru_w_out[j].astype(bf), rw_cat, rw_hi)
        x = _moe_layer(x, h2_tm, logits, rbias, mod, fin_g, *expert_w,
                       layer=i, final_norm=(i == depth - 1))
    return x
```

```python
import functools

import jax
import jax.numpy as jnp
from jax import lax
from jax.experimental import pallas as pl
from jax.experimental.pallas import tpu as pltpu

D_MODEL = 1024
CHUNK = 64
EPS = 1e-6

GLA_HEADS = 4
GLA_DK = 128
GLA_DV = 256
GLA_QK = GLA_HEADS * GLA_DK
GLA_VD = GLA_HEADS * GLA_DV
GLA_GATE_RANK = 16
GLA_GATE_TAU = 16.0
GLA_RANK_PAD = 128
GLA_IN_PAD = 2 * GLA_QK + 2 * GLA_VD + GLA_RANK_PAD

LRU_BLOCKS = 4
LRU_BLOCK_W = D_MODEL // LRU_BLOCKS
CONV_W = 4
LRU_C = 8.0

N_EXPERTS = 16
N_GROUPS = 4
EPG = N_EXPERTS // N_GROUPS
D_EXPERT = 512

LANES = 128
SUBLANES = 8
ROWS_PER_TOKEN = D_MODEL // LANES

MIX_TS = 256
MIX_STREAMS = 2
GLA_STAGGER = 6
ROUTE_BLOCK = 256
MOE_TM = 256
MOE_UNROLL = 8
MOE_MOVE = 256
MOE_WRING = 3
VMEM_LIMIT = 56 * 1024 * 1024
MOE_VMEM_LIMIT = 60 * 1024 * 1024


def _dot(a, b):
    return jnp.dot(a, b, preferred_element_type=jnp.float32)


def _dot_nt(a, b):
    return lax.dot_general(a, b, (((1,), (1,)), ((), ())),
                           preferred_element_type=jnp.float32)


def _dot_tn(a, b):
    return lax.dot_general(a, b, (((0,), (0,)), ((), ())),
                           preferred_element_type=jnp.float32)


def _split_bf16(x):
    hi = x.astype(jnp.bfloat16)
    lo = (x - hi.astype(jnp.float32)).astype(jnp.bfloat16)
    return hi, lo


def _sigmoid(x):
    return 0.5 * jnp.tanh(0.5 * x) + 0.5


def _sigmoid_rel(x):
    return 1.0 / (1.0 + jnp.exp(-x))


def _silu(x):
    return x * _sigmoid(x)


def _rms(x, g):
    inv = lax.rsqrt(jnp.mean(x * x, axis=-1, keepdims=True) + EPS)
    return x * inv * g


def _ada_body(c_ref, w_ref, b_ref, o_ref):
    c_hi, c_lo = _split_bf16(_silu(c_ref[...]))
    w_hi, w_lo = _split_bf16(w_ref[0])
    o_ref[0] = _dot(c_hi, w_hi) + (_dot(c_lo, w_hi) + _dot(c_hi, w_lo)) + b_ref[0]


def _ada_call(c, ada_w, ada_b):
    depth, d, n = ada_w.shape
    bsz = c.shape[0]
    tn = 1024
    return pl.pallas_call(
        _ada_body,
        grid=(depth, n // tn),
        in_specs=[
            pl.BlockSpec((bsz, d), lambda l, j: (0, 0)),
            pl.BlockSpec((1, d, tn), lambda l, j: (l, 0, j)),
            pl.BlockSpec((1, 1, tn), lambda l, j: (l, 0, j)),
        ],
        out_specs=pl.BlockSpec((1, bsz, tn), lambda l, j: (l, 0, j)),
        out_shape=jax.ShapeDtypeStruct((depth, bsz, n), jnp.float32),
        compiler_params=pltpu.CompilerParams(
            dimension_semantics=("arbitrary", "arbitrary"),
            vmem_limit_bytes=VMEM_LIMIT),
        name="ada",
    )(c, ada_w, ada_b.reshape(depth, 1, n))


def _pre_norm(x, mod_ref, g_ref, shift_row, scale_row):
    shift = mod_ref[0, shift_row:shift_row + 1, :]
    scale = mod_ref[0, scale_row:scale_row + 1, :]
    inv = lax.rsqrt(jnp.mean(x * x, axis=-1, keepdims=True) + EPS)
    return (x * inv) * (g_ref[...] * (1.0 + scale)) + shift


def _group_partner(x, k, sub):
    n = x.shape[0]
    fwd = pltpu.roll(x, n - k, 0)
    back = pltpu.roll(x, EPG - k, 0)
    wrapped = (sub % EPG) + k >= EPG
    return jnp.where(wrapped, back, fwd), wrapped


def _router_logits(h2, rw_cat_ref, rw_hi_ref):
    h_hi, h_lo = _split_bf16(h2)
    p1 = _dot_nt(rw_cat_ref[...], h_hi)
    p2 = _dot_nt(rw_hi_ref[...], h_lo)
    return p1[0:N_EXPERTS] + p1[N_EXPERTS:2 * N_EXPERTS] + p2


def _route_select(s, rbias):
    n = s.shape[1]
    sel = s + rbias
    sub = lax.broadcasted_iota(jnp.int32, (N_EXPERTS, n), 0)

    pair_best = None
    rank = jnp.zeros((N_EXPERTS, n), jnp.float32)
    for k in range(1, EPG):
        p, wrapped = _group_partner(sel, k, sub)
        ps = sel + p
        pair_best = ps if pair_best is None else jnp.maximum(pair_best, ps)
        ahead = (p > sel) | ((p == sel) & wrapped)
        rank = rank + ahead.astype(jnp.float32)
    gscore = pair_best
    for k in range(1, EPG):
        p, _ = _group_partner(pair_best, k, sub)
        gscore = jnp.maximum(gscore, p)
    chosen = jnp.ones((N_EXPERTS, n), jnp.bool_)
    grp = sub // EPG
    for j in range(1, N_GROUPS):
        other = pltpu.roll(gscore, N_EXPERTS - EPG * j, 0)
        other_is_later = grp + j < N_GROUPS
        chosen = chosen & ((gscore > other) | ((gscore == other) & other_is_later))
    m = chosen & (rank < 2.0)
    sm = s * m.astype(jnp.float32)
    wgt = sm / jnp.sum(sm, axis=0, keepdims=True)
    return m, wgt, sub


def _post_mix(x, mix, mod_ref, gffn_ref, rw_cat_ref, rw_hi_ref, xo_ref, h2_ref, lg_ref):
    ts = x.shape[0]
    x_new = x + mod_ref[0, 2:3, :] * mix
    xo_ref[0] = x_new
    h2 = _pre_norm(x_new, mod_ref, gffn_ref, 3, 4)
    for c in range(ROWS_PER_TOKEN):
        h2_ref[pl.ds(c, ts, stride=ROWS_PER_TOKEN), :] = h2[:, c * LANES:(c + 1) * LANES]
    lg_ref[0] = _router_logits(h2, rw_cat_ref, rw_hi_ref)


def _mixer_io_specs(bsz, seq, ts):
    ns = MIX_STREAMS
    hb = bsz // ns
    nt = seq // ts
    in_specs = [
        pl.BlockSpec((ns, 1, ts, D_MODEL), lambda b, t: (0, b, t, 0)),
        pl.BlockSpec((ns, 1, 6, D_MODEL), lambda b, t: (0, b, 0, 0)),
    ]
    out_specs = [
        pl.BlockSpec((ns, 1, ts, D_MODEL), lambda b, t: (0, b, t, 0)),
        pl.BlockSpec((ns, ts * ROWS_PER_TOKEN, LANES), lambda b, t: (0, b * nt + t, 0)),
        pl.BlockSpec((ns, 1, N_EXPERTS, ts), lambda b, t: (0, b, 0, t)),
    ]
    out_shapes = [
        jax.ShapeDtypeStruct((ns, hb, seq, D_MODEL), jnp.float32),
        jax.ShapeDtypeStruct((ns, hb * seq * ROWS_PER_TOKEN, LANES), jnp.float32),
        jax.ShapeDtypeStruct((ns, hb, N_EXPERTS, seq), jnp.float32),
    ]
    return in_specs, out_specs, out_shapes


def _mixer_unstream(outs, bsz, seq):
    x, h2_tm, logits = outs
    return (x.reshape(bsz, seq, D_MODEL), h2_tm.reshape(bsz * seq * ROWS_PER_TOKEN, LANES),
            logits.reshape(bsz, N_EXPERTS, seq))


def _const_spec(shape):
    nd = len(shape)
    return pl.BlockSpec(shape, lambda b, t: (0,) * nd)


def _gla_body(x_ref, mod_ref, gmix_ref, gffn_ref, win_ref, wgu_ref, bg_ref, ng_ref,
              wout_ref, rw_cat_ref, rw_hi_ref, *rest, n_cast):
    cast_in = rest[:n_cast]
    xo_ref, h2_ref, lg_ref = rest[n_cast:n_cast + 3]
    cast_out = rest[n_cast + 3:2 * n_cast + 3]
    state_scr, o_scr = rest[2 * n_cast + 3:]

    @pl.when(pl.program_id(1) == 0)
    def _():
        state_scr[...] = jnp.zeros_like(state_scr)

    def caster():
        pieces = 4
        for src, dst in zip(cast_in, cast_out):
            rows = src.shape[0] // pieces
            for i in range(pieces):
                dst[i * rows:(i + 1) * rows, :] = src[i * rows:(i + 1) * rows, :].astype(jnp.bfloat16)
                yield

    gens = [_gla_stream(x_ref.at[st], mod_ref.at[st], gmix_ref, gffn_ref, win_ref, wgu_ref,
                        bg_ref, ng_ref, wout_ref, rw_cat_ref, rw_hi_ref,
                        xo_ref.at[st], h2_ref.at[st], lg_ref.at[st],
                        state_scr.at[st], o_scr.at[st])
            for st in range(MIX_STREAMS)]
    _interleave(gens + [caster()], GLA_STAGGER)


def _gla_stream(x_ref, mod_ref, gmix_ref, gffn_ref, win_ref, wgu_ref, bg_ref, ng_ref,
                wout_ref, rw_cat_ref, rw_hi_ref,
                xo_ref, h2_ref, lg_ref, state_scr, o_scr):
    ts = x_ref.shape[1]
    nchunk = ts // CHUNK
    half = GLA_VD // 2
    x = x_ref[0]
    h = _pre_norm(x, mod_ref, gmix_ref, 0, 1).astype(jnp.bfloat16)
    yield

    o_q, o_k, o_v, o_g, o_a = 0, GLA_QK, 2 * GLA_QK, 2 * GLA_QK + GLA_VD, 2 * GLA_QK + 2 * GLA_VD
    q = (_dot(h, win_ref[:, o_q:o_k]) * (GLA_DK ** -0.5)).astype(jnp.bfloat16)
    yield
    k = _dot(h, win_ref[:, o_k:o_v])
    yield
    v = jnp.concatenate(
        [_dot(h, win_ref[:, o_v + i * half:o_v + (i + 1) * half]).astype(jnp.bfloat16)
         for i in range(2)], axis=1)
    yield
    a_lr = _dot(h, win_ref[:, o_a:o_a + GLA_RANK_PAD])
    a_hi, a_lo = _split_bf16(a_lr)
    z2 = _dot(jnp.concatenate([a_hi, a_lo], axis=0), wgu_ref[...])
    z = z2[0:ts] + z2[ts:2 * ts] + bg_ref[...]
    log_a = -(jnp.maximum(-z, 0.0) + jnp.log1p(jnp.exp(-jnp.abs(z)))) * (1.0 / GLA_GATE_TAU)
    yield

    row = lax.broadcasted_iota(jnp.int32, (ts, ts), 0)
    col = lax.broadcasted_iota(jnp.int32, (ts, ts), 1)
    tri = ((row // CHUNK == col // CHUNK) & (col <= row)).astype(jnp.bfloat16)
    l_hi, l_lo = _split_bf16(log_a)
    cum2 = _dot(tri, jnp.concatenate([l_hi, l_lo], axis=1))
    cum = cum2[:, 0:GLA_QK] + cum2[:, GLA_QK:2 * GLA_QK]
    yield

    ng = ng_ref[...]
    kv_t, gammas = [], []
    for j in range(nchunk):
        r0 = j * CHUNK
        cum_j = cum[r0:r0 + CHUNK]
        total = cum_j[CHUNK - 1:CHUNK]
        k_dec = (k[r0:r0 + CHUNK] * jnp.exp(total - cum_j)).astype(jnp.bfloat16)
        gammas.append(jnp.exp(total))
        for hd in range(GLA_HEADS):
            ks = slice(hd * GLA_DK, (hd + 1) * GLA_DK)
            vs = slice(hd * GLA_DV, (hd + 1) * GLA_DV)
            kv_t.append(_dot_tn(v[r0:r0 + CHUNK, vs], k_dec[:, ks]))
        yield
    states = []
    for hd in range(GLA_HEADS):
        ks = slice(hd * GLA_DK, (hd + 1) * GLA_DK)
        s_cur = state_scr[hd]
        for j in range(nchunk):
            s_cur = s_cur * gammas[j][:, ks] + kv_t[j * GLA_HEADS + hd]
            states.append(s_cur.astype(jnp.bfloat16))
        state_scr[hd] = s_cur
        yield
    for hd in range(GLA_HEADS):
        ks = slice(hd * GLA_DK, (hd + 1) * GLA_DK)
        vs = slice(hd * GLA_DV, (hd + 1) * GLA_DV)
        for j in range(nchunk):
            r0 = j * CHUNK
            o = _dot_nt(q[r0:r0 + CHUNK, ks], states[hd * nchunk + j])
            o_scr[r0:r0 + CHUNK, vs] = _rms(o, ng)
        yield

    og = []
    for i in range(2):
        g = _dot(h, win_ref[:, o_g + i * half:o_g + (i + 1) * half])
        og.append((o_scr[:, i * half:(i + 1) * half] * _silu(g)).astype(jnp.bfloat16))
        yield
    og = jnp.concatenate(og, axis=1)
    x_new = []
    for i in range(2):
        cols = slice(i * half, (i + 1) * half)
        mix = _dot(og, wout_ref[:, i * half:(i + 1) * half])
        x_new.append(x[:, cols] + mod_ref[0, 2:3, cols] * mix)
        yield
    x_new = jnp.concatenate(x_new, axis=1)
    xo_ref[0] = x_new
    h2 = _pre_norm(x_new, mod_ref, gffn_ref, 3, 4)
    for c in range(ROWS_PER_TOKEN):
        h2_ref[pl.ds(c, ts, stride=ROWS_PER_TOKEN), :] = h2[:, c * LANES:(c + 1) * LANES]
    yield
    lg_ref[0] = _router_logits(h2, rw_cat_ref, rw_hi_ref)
    yield


def _gla_layer_call(x, mod, gmix, gffn, w_in, w_gu, b_g, n_g, w_out, rw_cat, rw_hi, cast_src):
    bsz, seq, d = x.shape
    ts = MIX_TS
    ns = MIX_STREAMS
    hb = bsz // ns
    nt = seq // ts
    io_in, out_specs, out_shapes = _mixer_io_specs(bsz, seq, ts)
    consts = (gmix, gffn, w_in, w_gu, b_g, n_g, w_out, rw_cat, rw_hi)
    cast_specs, cast_shapes = [], []
    for a in cast_src:
        rows = a.shape[0] // (hb * nt)
        assert rows * hb * nt == a.shape[0]
        cast_specs.append(pl.BlockSpec((rows, a.shape[1]), lambda b, t: (b * nt + t, 0)))
        cast_shapes.append(jax.ShapeDtypeStruct(a.shape, jnp.bfloat16))
    outs = pl.pallas_call(
        functools.partial(_gla_body, n_cast=len(cast_src)),
        grid=(hb, nt),
        in_specs=io_in + [_const_spec(a.shape) for a in consts] + cast_specs,
        out_specs=out_specs + cast_specs,
        out_shape=out_shapes + cast_shapes,
        scratch_shapes=[
            pltpu.VMEM((ns, GLA_HEADS, GLA_DV, GLA_DK), jnp.float32),
            pltpu.VMEM((ns, ts, GLA_VD), jnp.float32),
        ],
        compiler_params=pltpu.CompilerParams(
            dimension_semantics=("arbitrary", "arbitrary"),
            vmem_limit_bytes=VMEM_LIMIT),
        name="gla_layer",
    )(x.reshape(ns, hb, seq, d), mod.reshape(ns, hb, 6, d), *consts, *cast_src)
    return _mixer_unstream(outs[:3], bsz, seq), outs[3:]


def _lru_body(x_ref, mod_ref, gmix_ref, gffn_ref, win_ref, cw_ref, cb_ref, wr_ref, br_ref,
              wi_ref, bi_ref, lam_ref, wout_ref, rw_cat_ref, rw_hi_ref,
              xo_ref, h2_ref, lg_ref,
              conv_scr, hstate_scr, a_scr, b_scr, gelu_scr):
    ns = MIX_STREAMS
    ts = x_ref.shape[2]
    w = D_MODEL

    @pl.when(pl.program_id(1) == 0)
    def _():
        conv_scr[...] = jnp.zeros_like(conv_scr)
        hstate_scr[...] = jnp.zeros_like(hstate_scr)

    for st in range(ns):
        _lru_pre_scan(x_ref.at[st], mod_ref.at[st], gmix_ref, win_ref, cw_ref, cb_ref, wr_ref,
                      br_ref, wi_ref, bi_ref, lam_ref,
                      conv_scr.at[st], a_scr.at[st], b_scr.at[st], gelu_scr.at[st])

    unroll = 4

    def group_step(gi, hprev):
        for u in range(unroll):
            r0 = pl.multiple_of(gi * (unroll * SUBLANES), unroll * SUBLANES) + u * SUBLANES
            nxt = []
            for st in range(ns):
                hs = (a_scr[st, pl.ds(r0, SUBLANES), :] * hprev[st]
                      + b_scr[st, pl.ds(r0, SUBLANES), :])
                b_scr[st, pl.ds(r0, SUBLANES), :] = hs
                nxt.append(hs[SUBLANES - 1:SUBLANES, :])
            hprev = tuple(nxt)
        return hprev

    hlast = lax.fori_loop(0, ts // (unroll * SUBLANES), group_step,
                          tuple(hstate_scr[st] for st in range(ns)))
    for st in range(ns):
        hstate_scr[st] = hlast[st]

    for st in range(ns):
        x = x_ref[st, 0]
        y = (b_scr[st] * gelu_scr[st]).astype(jnp.bfloat16)
        mix = _dot(y, wout_ref[...])
        _post_mix(x, mix, mod_ref.at[st], gffn_ref, rw_cat_ref, rw_hi_ref,
                  xo_ref.at[st], h2_ref.at[st], lg_ref.at[st])


def _lru_pre_scan(x_ref, mod_ref, gmix_ref, win_ref, cw_ref, cb_ref, wr_ref, br_ref,
                  wi_ref, bi_ref, lam_ref, conv_scr, a_scr, b_scr, gelu_scr):
    ts = x_ref.shape[1]
    w = D_MODEL
    x = x_ref[0]
    h = _pre_norm(x, mod_ref, gmix_ref, 0, 1).astype(jnp.bfloat16)
    gate_br = _dot(h, win_ref[:, 0:w])
    c0 = 0.7978845608028654
    u = gate_br * (c0 + (c0 * 0.044715) * (gate_br * gate_br))
    half_g = 0.5 * gate_br
    gelu_scr[...] = half_g + half_g * jnp.tanh(u)
    xb = _dot(h, win_ref[:, w:2 * w])

    ext = jnp.concatenate([conv_scr[...], xb], axis=0)
    xc = cb_ref[...] + xb * cw_ref[CONV_W - 1:CONV_W, :]
    for j in range(CONV_W - 1):
        back = CONV_W - 1 - j
        xc = xc + pltpu.roll(ext, back, 0)[SUBLANES:SUBLANES + ts] * cw_ref[j:j + 1, :]
    conv_scr[...] = xb[ts - SUBLANES:ts]

    xcb = xc.astype(jnp.bfloat16)
    rs, iis = [], []
    for hd in range(LRU_BLOCKS):
        sl = slice(hd * LRU_BLOCK_W, (hd + 1) * LRU_BLOCK_W)
        rs.append(_dot(xcb[:, sl], wr_ref[hd]))
        iis.append(_dot(xcb[:, sl], wi_ref[hd]))
    t_r = jnp.tanh(0.5 * (jnp.concatenate(rs, axis=1) + br_ref[...]))
    t_i = jnp.tanh(0.5 * (jnp.concatenate(iis, axis=1) + bi_ref[...]))

    lam = lam_ref[...]
    softplus_neg_lam = jnp.maximum(-lam, 0.0) + jnp.log1p(jnp.exp(-jnp.abs(lam)))
    sp = (-0.5 * LRU_C) * softplus_neg_lam
    log_a = sp * t_r + sp
    a = jnp.exp(log_a)
    half_mult = jnp.sqrt(jnp.tanh(-log_a) * (0.25 * (a * a) + 0.25))
    bb = (xc * half_mult) * (t_i + 1.0)

    a = a.reshape(ts // SUBLANES, SUBLANES, w)
    bb = bb.reshape(ts // SUBLANES, SUBLANES, w)
    rowi = lax.broadcasted_iota(jnp.int32, a.shape, 1)
    for d in (1, 2, 4):
        keep = rowi >= d
        a_sh = jnp.where(keep, pltpu.roll(a, d, 1), 1.0)
        b_sh = jnp.where(keep, pltpu.roll(bb, d, 1), 0.0)
        bb = a * b_sh + bb
        a = a * a_sh
    a_scr[...] = a.reshape(ts, w)
    b_scr[...] = bb.reshape(ts, w)


def _interleave(gens, lag):
    live = [True] * len(gens)
    tick = 0
    while any(live):
        for i, gen in enumerate(gens):
            if live[i] and tick >= i * lag:
                try:
                    next(gen)
                except StopIteration:
                    live[i] = False
        tick += 1


def _lru_layer_call(x, mod, gmix, gffn, w_in, conv_w, conv_b, w_r, b_r, w_i, b_i, lam, w_out,
                    rw_cat, rw_hi):
    bsz, seq, d = x.shape
    ts = MIX_TS
    ns = MIX_STREAMS
    hb = bsz // ns
    io_in, out_specs, out_shapes = _mixer_io_specs(bsz, seq, ts)
    consts = (gmix, gffn, w_in, conv_w, conv_b, w_r, b_r, w_i, b_i, lam, w_out,
              rw_cat, rw_hi)
    outs = pl.pallas_call(
        _lru_body,
        grid=(hb, seq // ts),
        in_specs=io_in + [_const_spec(a.shape) for a in consts],
        out_specs=out_specs,
        out_shape=out_shapes,
        scratch_shapes=[
            pltpu.VMEM((ns, SUBLANES, d), jnp.float32),
            pltpu.VMEM((ns, 1, d), jnp.float32),
            pltpu.VMEM((ns, ts, d), jnp.float32),
            pltpu.VMEM((ns, ts, d), jnp.float32),
            pltpu.VMEM((ns, ts, d), jnp.float32),
        ],
        compiler_params=pltpu.CompilerParams(
            dimension_semantics=("arbitrary", "arbitrary"),
            vmem_limit_bytes=VMEM_LIMIT),
        name="lru_layer",
    )(x.reshape(ns, hb, seq, d), mod.reshape(ns, hb, 6, d), *consts)
    return _mixer_unstream(outs, bsz, seq)


def _route_body(lg_ref, rbias_ref, pos0_ref, pos1_ref, w0_ref, off_ref, cnt_ref):
    seq = lg_ref.shape[2]
    blk = ROUTE_BLOCK
    s = _sigmoid_rel(lg_ref[0])
    m, wgt, sub = _route_select(s, rbias_ref[...])
    mf = m.astype(jnp.float32)

    row = lax.broadcasted_iota(jnp.int32, (blk, blk), 0)
    col = lax.broadcasted_iota(jnp.int32, (blk, blk), 1)
    before = (row < col).astype(jnp.bfloat16)
    carry = jnp.zeros((N_EXPERTS, 1), jnp.float32)
    ranks = []
    for i in range(seq // blk):
        mb = mf[:, i * blk:(i + 1) * blk]
        ranks.append(_dot(mb.astype(jnp.bfloat16), before) + carry)
        carry = carry + jnp.sum(mb, axis=1, keepdims=True)
    rnk = jnp.concatenate(ranks, axis=1)
    cnt_ref[0] = jnp.broadcast_to(carry, (N_EXPERTS, LANES)).astype(jnp.int32)

    eidx = sub.astype(jnp.float32)
    e_lo = jnp.min(jnp.where(m, eidx, 99.0), axis=0, keepdims=True)
    e_hi = jnp.max(jnp.where(m, eidx, -1.0), axis=0, keepdims=True)
    is_lo = m & (eidx == e_lo)
    is_hi = m & (eidx == e_hi)
    p_lo = jnp.sum(jnp.where(is_lo, rnk, 0.0), axis=0, keepdims=True)
    p_hi = jnp.sum(jnp.where(is_hi, rnk, 0.0), axis=0, keepdims=True)
    w0_ref[0] = jnp.sum(jnp.where(is_lo, wgt, 0.0), axis=0, keepdims=True)

    off = jnp.zeros((1, 1), jnp.float32)
    for e in range(N_EXPERTS):
        off_ref[0, e:e + 1, :] = jnp.broadcast_to(off, (1, LANES)).astype(jnp.int32)
        p_lo = p_lo + jnp.where(e_lo == float(e), off, 0.0)
        p_hi = p_hi + jnp.where(e_hi == float(e), off, 0.0)
        off = off + carry[e:e + 1, :]
    pos0_ref[0] = p_lo.astype(jnp.int32) * ROWS_PER_TOKEN
    pos1_ref[0] = p_hi.astype(jnp.int32) * ROWS_PER_TOKEN


def _route_call(logits, rbias):
    bsz, _, seq = logits.shape
    row_spec = pl.BlockSpec((1, 1, seq), lambda b: (b, 0, 0))
    tab_spec = pl.BlockSpec((1, N_EXPERTS, LANES), lambda b: (b, 0, 0))
    return pl.pallas_call(
        _route_body,
        grid=(bsz,),
        in_specs=[
            pl.BlockSpec((1, N_EXPERTS, seq), lambda b: (b, 0, 0)),
            pl.BlockSpec((N_EXPERTS, 1), lambda b: (0, 0)),
        ],
        out_specs=[row_spec, row_spec, row_spec, tab_spec, tab_spec],
        out_shape=[
            jax.ShapeDtypeStruct((bsz, 1, seq), jnp.int32),
            jax.ShapeDtypeStruct((bsz, 1, seq), jnp.int32),
            jax.ShapeDtypeStruct((bsz, 1, seq), jnp.float32),
            jax.ShapeDtypeStruct((bsz, N_EXPERTS, LANES), jnp.int32),
            jax.ShapeDtypeStruct((bsz, N_EXPERTS, LANES), jnp.int32),
        ],
        compiler_params=pltpu.CompilerParams(dimension_semantics=("arbitrary",)),
        name="moe_route",
    )(logits, rbias)


def _scatter_tokens(pos0_ref, pos1_ref, h2_ref, slots, idx0, lo, cnt):
    R = ROWS_PER_TOKEN
    for u in range(lo, lo + cnt):
        p0 = pos0_ref[0, 0, idx0 + u]
        p1 = pos1_ref[0, 0, idx0 + u]
        val = h2_ref[u * R:(u + 1) * R, :]
        slots[pl.ds(pl.multiple_of(p0, R), R), :] = val
        slots[pl.ds(pl.multiple_of(p1, R), R), :] = val


def _combine_tokens(pos0_ref, pos1_ref, wts_ref, slots, tm_scr, idx0, lo, cnt):
    R = ROWS_PER_TOKEN
    for u in range(lo, lo + cnt):
        p0 = pos0_ref[0, 0, idx0 + u]
        p1 = pos1_ref[0, 0, idx0 + u]
        w0 = wts_ref[0, 0, idx0 + u]
        za = slots[pl.ds(pl.multiple_of(p0, R), R), :]
        zb = slots[pl.ds(pl.multiple_of(p1, R), R), :]
        tm_scr[u * R:(u + 1) * R, :] = zb + w0 * (za - zb)


def _combine_epilogue(tm_scr, x_ref, mod_ref, fin_ref, o_ref, lo, cnt, final_norm):
    R = ROWS_PER_TOKEN
    moe = jnp.concatenate(
        [tm_scr[pl.ds(lo * R + c, cnt, stride=R), :] for c in range(R)], axis=1)
    out = x_ref[lo:lo + cnt, :] + mod_ref[0, 5:6, :] * moe
    if final_norm:
        out = _rms(out, fin_ref[...])
    o_ref[lo:lo + cnt, :] = out


def _expert_tile_hooked(slots, wg_ref, wu_ref, wd_ref, ws, slot0, n_valid, tm, hooks):
    R = ROWS_PER_TOKEN
    nw = 256
    row0 = pl.multiple_of(slot0 * R, R)
    xs = [slots[pl.ds(row0 + c, tm, stride=R), :] for c in range(R)]
    xt = jnp.concatenate(xs, axis=1).astype(jnp.bfloat16)
    ok = lax.broadcasted_iota(jnp.int32, (tm, LANES), 0) < n_valid
    hooks = list(hooks)

    def run_hook():
        if hooks:
            hooks.pop(0)()

    gate, up = [], []
    for i in range(D_EXPERT // nw):
        gate.append(_dot(xt, wg_ref[ws, :, i * nw:(i + 1) * nw]))
        run_hook()
    for i in range(D_EXPERT // nw):
        up.append(_dot(xt, wu_ref[ws, :, i * nw:(i + 1) * nw]))
        run_hook()
    he = (_silu(jnp.concatenate(gate, axis=1)) * jnp.concatenate(up, axis=1)).astype(jnp.bfloat16)
    for i in range(D_MODEL // nw):
        y = _dot(he, wd_ref[ws, :, i * nw:(i + 1) * nw])
        for cc in range(nw // LANES):
            c = i * (nw // LANES) + cc
            slots[pl.ds(row0 + c, tm, stride=R), :] = jnp.where(
                ok, y[:, cc * LANES:(cc + 1) * LANES], xs[c])
        run_hook()
    while hooks:
        run_hook()


def _moe_body(off_sm, cnt_sm, sc0_ref, sc1_ref, cb0_ref, cb1_ref, wts_ref, h2_ref, x_ref,
              mod_ref, fin_ref, wg_hbm, wu_hbm, wd_hbm, o_ref, slots_a, slots_b, tm_scr,
              wg_ref, wu_ref, wd_ref, w_sem, *, n_blocks, layer, final_norm):
    r = pl.program_id(0)
    e = pl.program_id(1)

    g = r * N_EXPERTS + e
    n_steps = (n_blocks + 2) * N_EXPERTS

    def w_needed(step):
        rnd = step // N_EXPERTS
        return (rnd >= 1) & (rnd <= n_blocks) & (step < n_steps)

    def w_copies(step):
        ex = layer * N_EXPERTS + step % N_EXPERTS
        sl = step % MOE_WRING
        return [pltpu.make_async_copy(src.at[ex], dst.at[sl], w_sem.at[i, sl])
                for i, (src, dst) in enumerate(((wg_hbm, wg_ref), (wu_hbm, wu_ref),
                                                (wd_hbm, wd_ref)))]

    def w_start(step):
        @pl.when(w_needed(step))
        def _():
            for cp in w_copies(step):
                cp.start()

    @pl.when(g == 0)
    def _():
        for ahead in range(MOE_WRING - 1):
            w_start(g + ahead)

    w_start(g + MOE_WRING - 1)

    @pl.when(w_needed(g))
    def _():
        for cp in w_copies(g):
            cp.wait()

    ws = g % MOE_WRING
    R = ROWS_PER_TOKEN
    U = MOE_UNROLL
    tm = MOE_TM
    mv = MOE_MOVE
    half = N_EXPERTS // 2
    n_slots = 2 * half * mv
    n_hooks = 2 * (D_EXPERT // 256) + D_MODEL // 256
    per_hook = mv // n_hooks

    @pl.when((r == 0) & (e == 0))
    def _():
        zeros = jnp.zeros((tm * R, LANES), jnp.float32)
        slots_a[n_slots * R:(n_slots + tm) * R, :] = zeros
        slots_b[n_slots * R:(n_slots + tm) * R, :] = zeros

    do_exp = (r >= 1) & (r <= n_blocks)
    do_comb = (r >= 2) & (e < half)
    do_scat = (r < n_blocks) & (e >= half)
    b_exp = jnp.clip(r - 1, 0, n_blocks - 1)
    off = off_sm[b_exp, e]
    n = jnp.where(do_exp, cnt_sm[b_exp, e], 0)
    comb_idx0 = e * mv
    scat_idx0 = (e - half) * mv
    small = tm // 2
    first_rows = jnp.where(n > small, tm, jnp.where(n > 0, small, 0))
    fused = do_comb | do_scat
    start = jnp.where(fused, first_rows, 0)
    rest = jnp.maximum(n - start, 0)
    rest_big = rest // tm + ((rest % tm) > small).astype(jnp.int32)
    rest_tail = rest - rest_big * tm

    def round_body(exp_buf, mov_buf):
        tile = functools.partial(_expert_tile_hooked, exp_buf, wg_ref, wu_ref, wd_ref, ws)

        def comb_hooks():
            def hook(k):
                def run():
                    if k < n_hooks:
                        _combine_tokens(cb0_ref, cb1_ref, wts_ref, mov_buf, tm_scr, comb_idx0,
                                        k * per_hook, per_hook)
                    if k >= 1:
                        _combine_epilogue(tm_scr, x_ref, mod_ref, fin_ref, o_ref,
                                          (k - 1) * per_hook, per_hook, final_norm)
                return run
            return [hook(k) for k in range(n_hooks + 1)]

        def scat_hooks():
            def hook(k):
                return lambda: _scatter_tokens(sc0_ref, sc1_ref, h2_ref, mov_buf, scat_idx0,
                                               k * per_hook, per_hook)
            return [hook(k) for k in range(n_hooks)]

        for rows in (tm, small):
            @pl.when((first_rows == rows) & do_comb)
            def _():
                tile(off, n, rows, comb_hooks())

            @pl.when((first_rows == rows) & do_scat)
            def _():
                tile(off, n, rows, scat_hooks())

        @pl.when(first_rows == 0)
        def _():
            @pl.when(do_comb)
            def _():
                def body(i, carry):
                    i0 = comb_idx0 + U * i
                    for u in range(U):
                        p0 = cb0_ref[0, 0, i0 + u]
                        p1 = cb1_ref[0, 0, i0 + u]
                        w0 = wts_ref[0, 0, i0 + u]
                        za = mov_buf[pl.ds(pl.multiple_of(p0, R), R), :]
                        zb = mov_buf[pl.ds(pl.multiple_of(p1, R), R), :]
                        r0 = pl.multiple_of(i * (U * R), U * R) + u * R
                        tm_scr[pl.ds(r0, R), :] = zb + w0 * (za - zb)
                    return carry

                lax.fori_loop(0, mv // U, body, 0)
                _combine_epilogue(tm_scr, x_ref, mod_ref, fin_ref, o_ref, 0, mv, final_norm)

            @pl.when(do_scat)
            def _():
                def body(i, carry):
                    i0 = scat_idx0 + U * i
                    for u in range(U):
                        p0 = sc0_ref[0, 0, i0 + u]
                        p1 = sc1_ref[0, 0, i0 + u]
                        r0 = pl.multiple_of(i * (U * R), U * R) + u * R
                        val = h2_ref[pl.ds(r0, R), :]
                        mov_buf[pl.ds(pl.multiple_of(p0, R), R), :] = val
                        mov_buf[pl.ds(pl.multiple_of(p1, R), R), :] = val
                    return carry

                lax.fori_loop(0, mv // U, body, 0)

        def tile_body(j, carry):
            done = start + j * tm
            tile(off + done, n - done, tm, [])
            return carry

        lax.fori_loop(0, rest_big, tile_body, 0)

        @pl.when(rest_tail > 0)
        def _():
            done = start + rest_big * tm
            tile(off + done, n - done, small, [])

    @pl.when(r % 2 == 0)
    def _():
        round_body(slots_b, slots_a)

    @pl.when(r % 2 == 1)
    def _():
        round_body(slots_a, slots_b)


def _moe_call(off, cnt, pos0, pos1, w0, h2_tm, x, mod, fin_g, w_gate, w_up, w_down, layer,
              final_norm):
    bsz, seq, d = x.shape
    mv = MOE_MOVE
    half = N_EXPERTS // 2
    assert seq == half * mv
    n_slots = 2 * seq
    last = bsz - 1

    def scat_blk(r, e, *_):
        return (jnp.minimum(r, last) * half + jnp.clip(e - half, 0, half - 1), 0)

    def comb_blk(r, e, *_):
        return (jnp.where(r < 2, 0, (r - 2) * half + jnp.minimum(e, half - 1)), 0)

    def scat_row(r, e, *_):
        return (jnp.minimum(r, last), 0, 0)

    def comb_row(r, e, *_):
        return (jnp.clip(r - 2, 0, last), 0, 0)

    body = functools.partial(_moe_body, n_blocks=bsz, layer=layer, final_norm=final_norm)
    slot_rows = (n_slots + MOE_TM) * ROWS_PER_TOKEN
    out = pl.pallas_call(
        body,
        grid_spec=pltpu.PrefetchScalarGridSpec(
            num_scalar_prefetch=2,
            grid=(bsz + 2, N_EXPERTS),
            in_specs=[
                pl.BlockSpec((1, 1, seq), scat_row, memory_space=pltpu.SMEM),
                pl.BlockSpec((1, 1, seq), scat_row, memory_space=pltpu.SMEM),
                pl.BlockSpec((1, 1, seq), comb_row, memory_space=pltpu.SMEM),
                pl.BlockSpec((1, 1, seq), comb_row, memory_space=pltpu.SMEM),
                pl.BlockSpec((1, 1, seq), comb_row, memory_space=pltpu.SMEM),
                pl.BlockSpec((mv * ROWS_PER_TOKEN, LANES), scat_blk),
                pl.BlockSpec((mv, d), comb_blk),
                pl.BlockSpec((1, 6, d), comb_row),
                pl.BlockSpec((1, d), lambda r, e, *_: (0, 0)),
                pl.BlockSpec(memory_space=pl.ANY),
                pl.BlockSpec(memory_space=pl.ANY),
                pl.BlockSpec(memory_space=pl.ANY),
            ],
            out_specs=pl.BlockSpec((mv, d), comb_blk),
            scratch_shapes=[
                pltpu.VMEM((slot_rows, LANES), jnp.float32),
                pltpu.VMEM((slot_rows, LANES), jnp.float32),
                pltpu.VMEM((mv * ROWS_PER_TOKEN, LANES), jnp.float32),
                pltpu.VMEM((MOE_WRING, d, D_EXPERT), jnp.bfloat16),
                pltpu.VMEM((MOE_WRING, d, D_EXPERT), jnp.bfloat16),
                pltpu.VMEM((MOE_WRING, D_EXPERT, d), jnp.bfloat16),
                pltpu.SemaphoreType.DMA((3, MOE_WRING)),
            ],
        ),
        out_shape=jax.ShapeDtypeStruct((bsz * seq, d), jnp.float32),
        compiler_params=pltpu.CompilerParams(
            dimension_semantics=("arbitrary", "arbitrary"),
            vmem_limit_bytes=MOE_VMEM_LIMIT),
        name="moe",
    )(off, cnt, pos0, pos1, pos0, pos1, w0, h2_tm, x.reshape(bsz * seq, d), mod, fin_g,
      w_gate, w_up, w_down)
    return out.reshape(bsz, seq, d)


def _moe_layer(x, h2_tm, logits, rbias, mod, fin_g, w_gate, w_up, w_down, layer, final_norm):
    pos0, pos1, w0, off, cnt = _route_call(logits, rbias)
    return _moe_call(off[:, :, 0], cnt[:, :, 0], pos0, pos1, w0, h2_tm, x, mod, fin_g,
                     w_gate, w_up, w_down, layer, final_norm)


def kernel(x, c, gla_w_in, gla_w_gate_up, gla_b_gate, gla_norm_g, gla_w_out, lru_w_in, lru_conv_w, lru_conv_b, lru_w_r, lru_b_r, lru_w_i, lru_b_i, lru_lambda, lru_w_out, router_w, router_bias, moe_w_gate, moe_w_up, moe_w_down, norm_mix_g, norm_ffn_g, ada_w, ada_b, final_norm_g):
    bf = jnp.bfloat16
    depth = ada_w.shape[0]
    bsz = x.shape[0]
    d = D_MODEL
    mod_all = _ada_call(c, ada_w, ada_b).reshape(depth, bsz, 6, d)

    rw_t = router_w.T
    rw_hi = rw_t.astype(bf)
    rw_lo = (rw_t - rw_hi.astype(jnp.float32)).astype(bf)
    rw_cat = jnp.concatenate([rw_hi, rw_lo], axis=0)
    rbias = router_bias.reshape(N_EXPERTS, 1)
    fin_g = final_norm_g.reshape(1, d)
    expert_w = None

    for i in range(depth):
        j = i // 2
        mod = mod_all[i]
        gmix = norm_mix_g[i].reshape(1, d)
        gffn = norm_ffn_g[i].reshape(1, d)
        if i % 2 == 0:
            w_in = jnp.pad(gla_w_in[j], ((0, 0), (0, GLA_RANK_PAD - GLA_GATE_RANK))).astype(bf)
            w_gu = jnp.pad(gla_w_gate_up[j], ((0, GLA_RANK_PAD - GLA_GATE_RANK), (0, 0))).astype(bf)
            cast_src = () if expert_w is not None else tuple(
                w.reshape(-1, w.shape[-1]) for w in (moe_w_gate, moe_w_up, moe_w_down))
            (x, h2_tm, logits), cast = _gla_layer_call(
                x, mod, gmix, gffn, w_in, w_gu, gla_b_gate[j].reshape(1, GLA_QK),
                gla_norm_g[j].reshape(1, GLA_DV), gla_w_out[j].astype(bf),
                rw_cat, rw_hi, cast_src)
            if cast:
                expert_w = (cast[0].reshape(depth * N_EXPERTS, d, D_EXPERT),
                            cast[1].reshape(depth * N_EXPERTS, d, D_EXPERT),
                            cast[2].reshape(depth * N_EXPERTS, D_EXPERT, d))
        else:
            x, h2_tm, logits = _lru_layer_call(
                x, mod, gmix, gffn, lru_w_in[j].astype(bf), lru_conv_w[j],
                lru_conv_b[j].reshape(1, d), lru_w_r[j].astype(bf),
                lru_b_r[j].reshape(1, d), lru_w_i[j].astype(bf), lru_b_i[j].reshape(1, d),
                lru_lambda[j].reshape(1, d), lru_w_out[j].astype(bf), rw_cat, rw_hi)
        x = _moe_layer(x, h2_tm, logits, rbias, mod, fin_g, *expert_w,
                       layer=i, final_norm=(i == depth - 1))
    return x
```

```python
import functools

import jax
import jax.numpy as jnp
from jax import lax
from jax.experimental import pallas as pl
from jax.experimental.pallas import tpu as pltpu

D_MODEL = 1024
CHUNK = 64
EPS = 1e-6

GLA_HEADS = 4
GLA_DK = 128
GLA_DV = 256
GLA_QK = GLA_HEADS * GLA_DK
GLA_VD = GLA_HEADS * GLA_DV
GLA_GATE_RANK = 16
GLA_GATE_TAU = 16.0
GLA_RANK_PAD = 128

LRU_BLOCKS = 4
LRU_BLOCK_W = D_MODEL // LRU_BLOCKS
CONV_W = 4
LRU_C = 8.0

N_EXPERTS = 16
N_GROUPS = 4
EPG = N_EXPERTS // N_GROUPS
D_EXPERT = 512

LANES = 128
SUBLANES = 8
ROWS_PER_TOKEN = D_MODEL // LANES

MIX_TS = 256
MIX_STREAMS = 2
GLA_STAGGER = 6
ROUTE_BLOCK = 256
SLOT_ALIGN = 4
MOE_TM = 256
MOE_UNROLL = 8
MOE_MOVE = 256
MOE_WRING = 3
VMEM_LIMIT = 56 * 1024 * 1024
MOE_VMEM_LIMIT = 60 * 1024 * 1024


def _dot(a, b):
    return jnp.dot(a, b, preferred_element_type=jnp.float32)


def _dot_nt(a, b):
    return lax.dot_general(a, b, (((1,), (1,)), ((), ())),
                           preferred_element_type=jnp.float32)


def _dot_tn(a, b):
    return lax.dot_general(a, b, (((0,), (0,)), ((), ())),
                           preferred_element_type=jnp.float32)


def _split_bf16(x):
    hi = x.astype(jnp.bfloat16)
    lo = (x - hi.astype(jnp.float32)).astype(jnp.bfloat16)
    return hi, lo


def _sigmoid(x):
    return 0.5 * jnp.tanh(0.5 * x) + 0.5


def _sigmoid_rel(x):
    return 1.0 / (1.0 + jnp.exp(-x))


def _silu(x):
    return x * _sigmoid(x)


def _rms(x, g):
    inv = lax.rsqrt(jnp.mean(x * x, axis=-1, keepdims=True) + EPS)
    return x * inv * g


def _ada_body(c_ref, w_ref, b_ref, o_ref):
    c_hi, c_lo = _split_bf16(_silu(c_ref[...]))
    w_hi, w_lo = _split_bf16(w_ref[0])
    o_ref[0] = _dot(c_hi, w_hi) + (_dot(c_lo, w_hi) + _dot(c_hi, w_lo)) + b_ref[0]


def _ada_call(c, ada_w, ada_b):
    depth, d, n = ada_w.shape
    bsz = c.shape[0]
    tn = 1024
    return pl.pallas_call(
        _ada_body,
        grid=(depth, n // tn),
        in_specs=[
            pl.BlockSpec((bsz, d), lambda l, j: (0, 0)),
            pl.BlockSpec((1, d, tn), lambda l, j: (l, 0, j)),
            pl.BlockSpec((1, 1, tn), lambda l, j: (l, 0, j)),
        ],
        out_specs=pl.BlockSpec((1, bsz, tn), lambda l, j: (l, 0, j)),
        out_shape=jax.ShapeDtypeStruct((depth, bsz, n), jnp.float32),
        compiler_params=pltpu.CompilerParams(
            dimension_semantics=("arbitrary", "arbitrary"),
            vmem_limit_bytes=VMEM_LIMIT),
        name="ada",
    )(c, ada_w, ada_b.reshape(depth, 1, n))


def _pre_norm(x, mod_ref, g_ref, shift_row, scale_row):
    shift = mod_ref[0, shift_row:shift_row + 1, :]
    scale = mod_ref[0, scale_row:scale_row + 1, :]
    inv = lax.rsqrt(jnp.mean(x * x, axis=-1, keepdims=True) + EPS)
    return (x * inv) * (g_ref[...] * (1.0 + scale)) + shift


def _group_partner(x, k, sub):
    n = x.shape[0]
    fwd = pltpu.roll(x, n - k, 0)
    back = pltpu.roll(x, EPG - k, 0)
    wrapped = (sub % EPG) + k >= EPG
    return jnp.where(wrapped, back, fwd), wrapped


def _router_logits(h2, rw_cat_ref, rw_hi_ref):
    h_hi, h_lo = _split_bf16(h2)
    p1 = _dot_nt(rw_cat_ref[...], h_hi)
    p2 = _dot_nt(rw_hi_ref[...], h_lo)
    return p1[0:N_EXPERTS] + p1[N_EXPERTS:2 * N_EXPERTS] + p2


def _route_select(s, rbias):
    n = s.shape[1]
    sel = s + rbias
    sub = lax.broadcasted_iota(jnp.int32, (N_EXPERTS, n), 0)

    pair_best = None
    rank = jnp.zeros((N_EXPERTS, n), jnp.float32)
    for k in range(1, EPG):
        p, wrapped = _group_partner(sel, k, sub)
        ps = sel + p
        pair_best = ps if pair_best is None else jnp.maximum(pair_best, ps)
        ahead = (p > sel) | ((p == sel) & wrapped)
        rank = rank + ahead.astype(jnp.float32)
    gscore = pair_best
    for k in range(1, EPG):
        p, _ = _group_partner(pair_best, k, sub)
        gscore = jnp.maximum(gscore, p)
    chosen = jnp.ones((N_EXPERTS, n), jnp.bool_)
    grp = sub // EPG
    for j in range(1, N_GROUPS):
        other = pltpu.roll(gscore, N_EXPERTS - EPG * j, 0)
        other_is_later = grp + j < N_GROUPS
        chosen = chosen & ((gscore > other) | ((gscore == other) & other_is_later))
    m = chosen & (rank < 2.0)
    sm = s * m.astype(jnp.float32)
    wgt = sm / jnp.sum(sm, axis=0, keepdims=True)
    return m, wgt, sub


def _post_mix(x, mix, mod_ref, gffn_ref, rw_cat_ref, rw_hi_ref, xo_ref, h2_ref, lg_ref):
    ts = x.shape[0]
    x_new = x + mod_ref[0, 2:3, :] * mix
    xo_ref[0] = x_new
    h2 = _pre_norm(x_new, mod_ref, gffn_ref, 3, 4)
    for c in range(ROWS_PER_TOKEN):
        h2_ref[pl.ds(c, ts, stride=ROWS_PER_TOKEN), :] = h2[:, c * LANES:(c + 1) * LANES]
    lg_ref[0] = _router_logits(h2, rw_cat_ref, rw_hi_ref)


def _mixer_io_specs(bsz, seq, ts):
    ns = MIX_STREAMS
    hb = bsz // ns
    nt = seq // ts
    in_specs = [
        pl.BlockSpec((ns, 1, ts, D_MODEL), lambda b, t: (0, b, t, 0)),
        pl.BlockSpec((ns, 1, 6, D_MODEL), lambda b, t: (0, b, 0, 0)),
    ]
    out_specs = [
        pl.BlockSpec((ns, 1, ts, D_MODEL), lambda b, t: (0, b, t, 0)),
        pl.BlockSpec((ns, ts * ROWS_PER_TOKEN, LANES), lambda b, t: (0, b * nt + t, 0)),
        pl.BlockSpec((ns, 1, N_EXPERTS, ts), lambda b, t: (0, b, 0, t)),
    ]
    out_shapes = [
        jax.ShapeDtypeStruct((ns, hb, seq, D_MODEL), jnp.float32),
        jax.ShapeDtypeStruct((ns, hb * seq * ROWS_PER_TOKEN, LANES), jnp.float32),
        jax.ShapeDtypeStruct((ns, hb, N_EXPERTS, seq), jnp.float32),
    ]
    return in_specs, out_specs, out_shapes


def _mixer_unstream(outs, bsz, seq):
    x, h2_tm, logits = outs
    return (x.reshape(bsz, seq, D_MODEL), h2_tm.reshape(bsz * seq * ROWS_PER_TOKEN, LANES),
            logits.reshape(bsz, N_EXPERTS, seq))


def _const_spec(shape):
    nd = len(shape)
    return pl.BlockSpec(shape, lambda b, t: (0,) * nd)


def _gla_body(x_ref, mod_ref, gmix_ref, gffn_ref, win_ref, wgu_ref, bg_ref, ng_ref,
              wout_ref, rw_cat_ref, rw_hi_ref, *rest, n_cast):
    cast_in = rest[:n_cast]
    xo_ref, h2_ref, lg_ref = rest[n_cast:n_cast + 3]
    cast_out = rest[n_cast + 3:2 * n_cast + 3]
    state_scr, o_scr = rest[2 * n_cast + 3:]

    @pl.when(pl.program_id(1) == 0)
    def _():
        state_scr[...] = jnp.zeros_like(state_scr)

    def caster():
        pieces = 4
        for src, dst in zip(cast_in, cast_out):
            rows = src.shape[0] // pieces
            for i in range(pieces):
                dst[i * rows:(i + 1) * rows, :] = src[i * rows:(i + 1) * rows, :].astype(jnp.bfloat16)
                yield

    gens = [_gla_stream(x_ref.at[st], mod_ref.at[st], gmix_ref, gffn_ref, win_ref, wgu_ref,
                        bg_ref, ng_ref, wout_ref, rw_cat_ref, rw_hi_ref,
                        xo_ref.at[st], h2_ref.at[st], lg_ref.at[st],
                        state_scr.at[st], o_scr.at[st])
            for st in range(MIX_STREAMS)]
    _interleave(gens + [caster()], GLA_STAGGER)


def _gla_stream(x_ref, mod_ref, gmix_ref, gffn_ref, win_ref, wgu_ref, bg_ref, ng_ref,
                wout_ref, rw_cat_ref, rw_hi_ref,
                xo_ref, h2_ref, lg_ref, state_scr, o_scr):
    ts = x_ref.shape[1]
    nchunk = ts // CHUNK
    half = GLA_VD // 2
    x = x_ref[0]
    h = _pre_norm(x, mod_ref, gmix_ref, 0, 1).astype(jnp.bfloat16)
    yield

    o_q, o_k, o_v, o_g, o_a = 0, GLA_QK, 2 * GLA_QK, 2 * GLA_QK + GLA_VD, 2 * GLA_QK + 2 * GLA_VD
    q = (_dot(h, win_ref[:, o_q:o_k]) * (GLA_DK ** -0.5)).astype(jnp.bfloat16)
    yield
    k = _dot(h, win_ref[:, o_k:o_v])
    yield
    v = jnp.concatenate(
        [_dot(h, win_ref[:, o_v + i * half:o_v + (i + 1) * half]).astype(jnp.bfloat16)
         for i in range(2)], axis=1)
    yield
    a_lr = _dot(h, win_ref[:, o_a:o_a + GLA_RANK_PAD])
    a_hi, a_lo = _split_bf16(a_lr)
    z2 = _dot(jnp.concatenate([a_hi, a_lo], axis=0), wgu_ref[...])
    z = z2[0:ts] + z2[ts:2 * ts] + bg_ref[...]
    log_a = -(jnp.maximum(-z, 0.0) + jnp.log1p(jnp.exp(-jnp.abs(z)))) * (1.0 / GLA_GATE_TAU)
    yield

    row = lax.broadcasted_iota(jnp.int32, (ts, ts), 0)
    col = lax.broadcasted_iota(jnp.int32, (ts, ts), 1)
    tri = ((row // CHUNK == col // CHUNK) & (col <= row)).astype(jnp.bfloat16)
    l_hi, l_lo = _split_bf16(log_a)
    cum2 = _dot(tri, jnp.concatenate([l_hi, l_lo], axis=1))
    cum = cum2[:, 0:GLA_QK] + cum2[:, GLA_QK:2 * GLA_QK]
    yield

    ng = ng_ref[...]
    kv_t, gammas = [], []
    for j in range(nchunk):
        r0 = j * CHUNK
        cum_j = cum[r0:r0 + CHUNK]
        total = cum_j[CHUNK - 1:CHUNK]
        k_dec = (k[r0:r0 + CHUNK] * jnp.exp(total - cum_j)).astype(jnp.bfloat16)
        gammas.append(jnp.exp(total))
        for hd in range(GLA_HEADS):
            ks = slice(hd * GLA_DK, (hd + 1) * GLA_DK)
            vs = slice(hd * GLA_DV, (hd + 1) * GLA_DV)
            kv_t.append(_dot_tn(v[r0:r0 + CHUNK, vs], k_dec[:, ks]))
        yield
    states = []
    for hd in range(GLA_HEADS):
        ks = slice(hd * GLA_DK, (hd + 1) * GLA_DK)
        s_cur = state_scr[hd]
        for j in range(nchunk):
            s_cur = s_cur * gammas[j][:, ks] + kv_t[j * GLA_HEADS + hd]
            states.append(s_cur.astype(jnp.bfloat16))
        state_scr[hd] = s_cur
        yield
    for hd in range(GLA_HEADS):
        ks = slice(hd * GLA_DK, (hd + 1) * GLA_DK)
        vs = slice(hd * GLA_DV, (hd + 1) * GLA_DV)
        for j in range(nchunk):
            r0 = j * CHUNK
            o = _dot_nt(q[r0:r0 + CHUNK, ks], states[hd * nchunk + j])
            o_scr[r0:r0 + CHUNK, vs] = _rms(o, ng)
        yield

    og = []
    for i in range(2):
        g = _dot(h, win_ref[:, o_g + i * half:o_g + (i + 1) * half])
        og.append((o_scr[:, i * half:(i + 1) * half] * _silu(g)).astype(jnp.bfloat16))
        yield
    og = jnp.concatenate(og, axis=1)
    x_new = []
    for i in range(2):
        cols = slice(i * half, (i + 1) * half)
        mix = _dot(og, wout_ref[:, i * half:(i + 1) * half])
        x_new.append(x[:, cols] + mod_ref[0, 2:3, cols] * mix)
        yield
    x_new = jnp.concatenate(x_new, axis=1)
    xo_ref[0] = x_new
    h2 = _pre_norm(x_new, mod_ref, gffn_ref, 3, 4)
    for c in range(ROWS_PER_TOKEN):
        h2_ref[pl.ds(c, ts, stride=ROWS_PER_TOKEN), :] = h2[:, c * LANES:(c + 1) * LANES]
    yield
    lg_ref[0] = _router_logits(h2, rw_cat_ref, rw_hi_ref)
    yield


def _gla_layer_call(x, mod, gmix, gffn, w_in, w_gu, b_g, n_g, w_out, rw_cat, rw_hi, cast_src):
    bsz, seq, d = x.shape
    ts = MIX_TS
    ns = MIX_STREAMS
    hb = bsz // ns
    nt = seq // ts
    io_in, out_specs, out_shapes = _mixer_io_specs(bsz, seq, ts)
    consts = (gmix, gffn, w_in, w_gu, b_g, n_g, w_out, rw_cat, rw_hi)
    cast_specs, cast_shapes = [], []
    for a in cast_src:
        rows = a.shape[0] // (hb * nt)
        assert rows * hb * nt == a.shape[0]
        cast_specs.append(pl.BlockSpec((rows, a.shape[1]), lambda b, t: (b * nt + t, 0)))
        cast_shapes.append(jax.ShapeDtypeStruct(a.shape, jnp.bfloat16))
    outs = pl.pallas_call(
        functools.partial(_gla_body, n_cast=len(cast_src)),
        grid=(hb, nt),
        in_specs=io_in + [_const_spec(a.shape) for a in consts] + cast_specs,
        out_specs=out_specs + cast_specs,
        out_shape=out_shapes + cast_shapes,
        scratch_shapes=[
            pltpu.VMEM((ns, GLA_HEADS, GLA_DV, GLA_DK), jnp.float32),
            pltpu.VMEM((ns, ts, GLA_VD), jnp.float32),
        ],
        compiler_params=pltpu.CompilerParams(
            dimension_semantics=("arbitrary", "arbitrary"),
            vmem_limit_bytes=VMEM_LIMIT),
        name="gla_layer",
    )(x.reshape(ns, hb, seq, d), mod.reshape(ns, hb, 6, d), *consts, *cast_src)
    return _mixer_unstream(outs[:3], bsz, seq), outs[3:]


def _lru_body(x_ref, mod_ref, gmix_ref, gffn_ref, win_ref, cw_ref, cb_ref, wr_ref, br_ref,
              wi_ref, bi_ref, lam_ref, wout_ref, rw_cat_ref, rw_hi_ref,
              xo_ref, h2_ref, lg_ref,
              conv_scr, hstate_scr, a_scr, b_scr, gelu_scr):
    ns = MIX_STREAMS
    ts = x_ref.shape[2]
    w = D_MODEL

    @pl.when(pl.program_id(1) == 0)
    def _():
        conv_scr[...] = jnp.zeros_like(conv_scr)
        hstate_scr[...] = jnp.zeros_like(hstate_scr)

    for st in range(ns):
        _lru_pre_scan(x_ref.at[st], mod_ref.at[st], gmix_ref, win_ref, cw_ref, cb_ref, wr_ref,
                      br_ref, wi_ref, bi_ref, lam_ref,
                      conv_scr.at[st], a_scr.at[st], b_scr.at[st], gelu_scr.at[st])

    unroll = 4

    def group_step(gi, hprev):
        for u in range(unroll):
            r0 = pl.multiple_of(gi * (unroll * SUBLANES), unroll * SUBLANES) + u * SUBLANES
            nxt = []
            for st in range(ns):
                hs = (a_scr[st, pl.ds(r0, SUBLANES), :] * hprev[st]
                      + b_scr[st, pl.ds(r0, SUBLANES), :])
                b_scr[st, pl.ds(r0, SUBLANES), :] = hs
                nxt.append(hs[SUBLANES - 1:SUBLANES, :])
            hprev = tuple(nxt)
        return hprev

    hlast = lax.fori_loop(0, ts // (unroll * SUBLANES), group_step,
                          tuple(hstate_scr[st] for st in range(ns)))
    for st in range(ns):
        hstate_scr[st] = hlast[st]

    for st in range(ns):
        x = x_ref[st, 0]
        y = (b_scr[st] * gelu_scr[st]).astype(jnp.bfloat16)
        mix = _dot(y, wout_ref[...])
        _post_mix(x, mix, mod_ref.at[st], gffn_ref, rw_cat_ref, rw_hi_ref,
                  xo_ref.at[st], h2_ref.at[st], lg_ref.at[st])


def _lru_pre_scan(x_ref, mod_ref, gmix_ref, win_ref, cw_ref, cb_ref, wr_ref, br_ref,
                  wi_ref, bi_ref, lam_ref, conv_scr, a_scr, b_scr, gelu_scr):
    ts = x_ref.shape[1]
    w = D_MODEL
    x = x_ref[0]
    h = _pre_norm(x, mod_ref, gmix_ref, 0, 1).astype(jnp.bfloat16)
    gate_br = _dot(h, win_ref[:, 0:w])
    c0 = 0.7978845608028654
    u = gate_br * (c0 + (c0 * 0.044715) * (gate_br * gate_br))
    half_g = 0.5 * gate_br
    gelu_scr[...] = half_g + half_g * jnp.tanh(u)
    xb = _dot(h, win_ref[:, w:2 * w])

    ext = jnp.concatenate([conv_scr[...], xb], axis=0)
    xc = cb_ref[...] + xb * cw_ref[CONV_W - 1:CONV_W, :]
    for j in range(CONV_W - 1):
        back = CONV_W - 1 - j
        xc = xc + pltpu.roll(ext, back, 0)[SUBLANES:SUBLANES + ts] * cw_ref[j:j + 1, :]
    conv_scr[...] = xb[ts - SUBLANES:ts]

    xcb = xc.astype(jnp.bfloat16)
    rs, iis = [], []
    for hd in range(LRU_BLOCKS):
        sl = slice(hd * LRU_BLOCK_W, (hd + 1) * LRU_BLOCK_W)
        rs.append(_dot(xcb[:, sl], wr_ref[hd]))
        iis.append(_dot(xcb[:, sl], wi_ref[hd]))
    t_r = jnp.tanh(0.5 * (jnp.concatenate(rs, axis=1) + br_ref[...]))
    t_i = jnp.tanh(0.5 * (jnp.concatenate(iis, axis=1) + bi_ref[...]))

    lam = lam_ref[...]
    softplus_neg_lam = jnp.maximum(-lam, 0.0) + jnp.log1p(jnp.exp(-jnp.abs(lam)))
    sp = (-0.5 * LRU_C) * softplus_neg_lam
    log_a = sp * t_r + sp
    a = jnp.exp(log_a)
    half_mult = jnp.sqrt(jnp.tanh(-log_a) * (0.25 * (a * a) + 0.25))
    bb = (xc * half_mult) * (t_i + 1.0)

    a = a.reshape(ts // SUBLANES, SUBLANES, w)
    bb = bb.reshape(ts // SUBLANES, SUBLANES, w)
    rowi = lax.broadcasted_iota(jnp.int32, a.shape, 1)
    for d in (1, 2, 4):
        keep = rowi >= d
        a_sh = jnp.where(keep, pltpu.roll(a, d, 1), 1.0)
        b_sh = jnp.where(keep, pltpu.roll(bb, d, 1), 0.0)
        bb = a * b_sh + bb
        a = a * a_sh
    a_scr[...] = a.reshape(ts, w)
    b_scr[...] = bb.reshape(ts, w)


def _interleave(gens, lag):
    live = [True] * len(gens)
    tick = 0
    while any(live):
        for i, gen in enumerate(gens):
            if live[i] and tick >= i * lag:
                try:
                    next(gen)
                except StopIteration:
                    live[i] = False
        tick += 1


def _lru_layer_call(x, mod, gmix, gffn, w_in, conv_w, conv_b, w_r, b_r, w_i, b_i, lam, w_out,
                    rw_cat, rw_hi):
    bsz, seq, d = x.shape
    ts = MIX_TS
    ns = MIX_STREAMS
    hb = bsz // ns
    io_in, out_specs, out_shapes = _mixer_io_specs(bsz, seq, ts)
    consts = (gmix, gffn, w_in, conv_w, conv_b, w_r, b_r, w_i, b_i, lam, w_out,
              rw_cat, rw_hi)
    outs = pl.pallas_call(
        _lru_body,
        grid=(hb, seq // ts),
        in_specs=io_in + [_const_spec(a.shape) for a in consts],
        out_specs=out_specs,
        out_shape=out_shapes,
        scratch_shapes=[
            pltpu.VMEM((ns, SUBLANES, d), jnp.float32),
            pltpu.VMEM((ns, 1, d), jnp.float32),
            pltpu.VMEM((ns, ts, d), jnp.float32),
            pltpu.VMEM((ns, ts, d), jnp.float32),
            pltpu.VMEM((ns, ts, d), jnp.float32),
        ],
        compiler_params=pltpu.CompilerParams(
            dimension_semantics=("arbitrary", "arbitrary"),
            vmem_limit_bytes=VMEM_LIMIT),
        name="lru_layer",
    )(x.reshape(ns, hb, seq, d), mod.reshape(ns, hb, 6, d), *consts)
    return _mixer_unstream(outs, bsz, seq)


def _route_body(lg_ref, rbias_ref, pos0_ref, pos1_ref, w0_ref, off_ref, cnt_ref):
    seq = lg_ref.shape[2]
    blk = ROUTE_BLOCK
    s = _sigmoid_rel(lg_ref[0])
    m, wgt, sub = _route_select(s, rbias_ref[...])
    mf = m.astype(jnp.float32)

    row = lax.broadcasted_iota(jnp.int32, (blk, blk), 0)
    col = lax.broadcasted_iota(jnp.int32, (blk, blk), 1)
    before = (row < col).astype(jnp.bfloat16)
    carry = jnp.zeros((N_EXPERTS, 1), jnp.float32)
    ranks = []
    for i in range(seq // blk):
        mb = mf[:, i * blk:(i + 1) * blk]
        ranks.append(_dot(mb.astype(jnp.bfloat16), before) + carry)
        carry = carry + jnp.sum(mb, axis=1, keepdims=True)
    rnk = jnp.concatenate(ranks, axis=1)
    cnt_ref[0] = jnp.broadcast_to(carry, (N_EXPERTS, LANES)).astype(jnp.int32)

    eidx = sub.astype(jnp.float32)
    e_lo = jnp.min(jnp.where(m, eidx, 99.0), axis=0, keepdims=True)
    e_hi = jnp.max(jnp.where(m, eidx, -1.0), axis=0, keepdims=True)
    is_lo = m & (eidx == e_lo)
    is_hi = m & (eidx == e_hi)
    p_lo = jnp.sum(jnp.where(is_lo, rnk, 0.0), axis=0, keepdims=True)
    p_hi = jnp.sum(jnp.where(is_hi, rnk, 0.0), axis=0, keepdims=True)
    w0_ref[0] = jnp.sum(jnp.where(is_lo, wgt, 0.0), axis=0, keepdims=True)

    off = jnp.zeros((1, 1), jnp.float32)
    for e in range(N_EXPERTS):
        off_ref[0, e:e + 1, :] = jnp.broadcast_to(off, (1, LANES)).astype(jnp.int32)
        p_lo = p_lo + jnp.where(e_lo == float(e), off, 0.0)
        p_hi = p_hi + jnp.where(e_hi == float(e), off, 0.0)
        off = off + jnp.floor((carry[e:e + 1, :] + (SLOT_ALIGN - 1.0)) * (1.0 / SLOT_ALIGN)) * SLOT_ALIGN
    pos0_ref[0] = p_lo.astype(jnp.int32) * ROWS_PER_TOKEN
    pos1_ref[0] = p_hi.astype(jnp.int32) * ROWS_PER_TOKEN


def _route_call(logits, rbias):
    bsz, _, seq = logits.shape
    row_spec = pl.BlockSpec((1, 1, seq), lambda b: (b, 0, 0))
    tab_spec = pl.BlockSpec((1, N_EXPERTS, LANES), lambda b: (b, 0, 0))
    return pl.pallas_call(
        _route_body,
        grid=(bsz,),
        in_specs=[
            pl.BlockSpec((1, N_EXPERTS, seq), lambda b: (b, 0, 0)),
            pl.BlockSpec((N_EXPERTS, 1), lambda b: (0, 0)),
        ],
        out_specs=[row_spec, row_spec, row_spec, tab_spec, tab_spec],
        out_shape=[
            jax.ShapeDtypeStruct((bsz, 1, seq), jnp.int32),
            jax.ShapeDtypeStruct((bsz, 1, seq), jnp.int32),
            jax.ShapeDtypeStruct((bsz, 1, seq), jnp.float32),
            jax.ShapeDtypeStruct((bsz, N_EXPERTS, LANES), jnp.int32),
            jax.ShapeDtypeStruct((bsz, N_EXPERTS, LANES), jnp.int32),
        ],
        compiler_params=pltpu.CompilerParams(dimension_semantics=("arbitrary",)),
        name="moe_route",
    )(logits, rbias)


def _scatter_tokens(pos0_ref, pos1_ref, h2_ref, slots, idx0, lo, cnt):
    R = ROWS_PER_TOKEN
    for u in range(lo, lo + cnt):
        p0 = pos0_ref[0, 0, idx0 + u]
        p1 = pos1_ref[0, 0, idx0 + u]
        val = h2_ref[u * R:(u + 1) * R, :]
        slots[pl.ds(pl.multiple_of(p0, R), R), :] = val
        slots[pl.ds(pl.multiple_of(p1, R), R), :] = val


def _combine_tokens(pos0_ref, pos1_ref, wts_ref, slots, tm_scr, idx0, lo, cnt):
    R = ROWS_PER_TOKEN
    for u in range(lo, lo + cnt):
        p0 = pos0_ref[0, 0, idx0 + u]
        p1 = pos1_ref[0, 0, idx0 + u]
        w0 = wts_ref[0, 0, idx0 + u]
        za = slots[pl.ds(pl.multiple_of(p0, R), R), :]
        zb = slots[pl.ds(pl.multiple_of(p1, R), R), :]
        tm_scr[u * R:(u + 1) * R, :] = zb + w0 * (za - zb)


def _combine_epilogue(tm_scr, x_ref, mod_ref, fin_ref, o_ref, lo, cnt, final_norm):
    R = ROWS_PER_TOKEN
    moe = jnp.concatenate(
        [tm_scr[pl.ds(lo * R + c, cnt, stride=R), :] for c in range(R)], axis=1)
    out = x_ref[lo:lo + cnt, :] + mod_ref[0, 5:6, :] * moe
    if final_norm:
        out = _rms(out, fin_ref[...])
    o_ref[lo:lo + cnt, :] = out


def _expert_tile_hooked(slots, wg_ref, wu_ref, wd_ref, ws, slot0, n_valid, tm, hooks):
    R = ROWS_PER_TOKEN
    nw = 256
    row0 = pl.multiple_of(slot0 * R, SLOT_ALIGN * R)
    xs = [slots[pl.ds(row0 + c, tm, stride=R), :] for c in range(R)]
    xt = jnp.concatenate(xs, axis=1).astype(jnp.bfloat16)
    ok = lax.broadcasted_iota(jnp.int32, (tm, LANES), 0) < n_valid
    hooks = list(hooks)

    def run_hook():
        if hooks:
            hooks.pop(0)()

    gate, up = [], []
    for i in range(D_EXPERT // nw):
        gate.append(_dot(xt, wg_ref[ws, :, i * nw:(i + 1) * nw]))
        run_hook()
    for i in range(D_EXPERT // nw):
        up.append(_dot(xt, wu_ref[ws, :, i * nw:(i + 1) * nw]))
        run_hook()
    he = (_silu(jnp.concatenate(gate, axis=1)) * jnp.concatenate(up, axis=1)).astype(jnp.bfloat16)
    for i in range(D_MODEL // nw):
        y = _dot(he, wd_ref[ws, :, i * nw:(i + 1) * nw])
        for cc in range(nw // LANES):
            c = i * (nw // LANES) + cc
            slots[pl.ds(row0 + c, tm, stride=R), :] = jnp.where(
                ok, y[:, cc * LANES:(cc + 1) * LANES], xs[c])
        run_hook()
    while hooks:
        run_hook()


def _moe_body(off_sm, cnt_sm, sc0_ref, sc1_ref, cb0_ref, cb1_ref, wts_ref, h2_ref, x_ref,
              mod_ref, fin_ref, wg_hbm, wu_hbm, wd_hbm, o_ref, slots_a, slots_b, tm_scr,
              wg_ref, wu_ref, wd_ref, w_sem, *, n_blocks, layer, final_norm):
    r = pl.program_id(0)
    e = pl.program_id(1)

    g = r * N_EXPERTS + e
    n_steps = (n_blocks + 2) * N_EXPERTS

    def w_needed(step):
        rnd = step // N_EXPERTS
        return (rnd >= 1) & (rnd <= n_blocks) & (step < n_steps)

    def w_copies(step):
        ex = layer * N_EXPERTS + step % N_EXPERTS
        sl = step % MOE_WRING
        return [pltpu.make_async_copy(src.at[ex], dst.at[sl], w_sem.at[i, sl])
                for i, (src, dst) in enumerate(((wg_hbm, wg_ref), (wu_hbm, wu_ref),
                                                (wd_hbm, wd_ref)))]

    def w_start(step):
        @pl.when(w_needed(step))
        def _():
            for cp in w_copies(step):
                cp.start()

    @pl.when(g == 0)
    def _():
        for ahead in range(MOE_WRING - 1):
            w_start(g + ahead)

    w_start(g + MOE_WRING - 1)

    @pl.when(w_needed(g))
    def _():
        for cp in w_copies(g):
            cp.wait()

    ws = g % MOE_WRING
    R = ROWS_PER_TOKEN
    U = MOE_UNROLL
    tm = MOE_TM
    mv = MOE_MOVE
    half = N_EXPERTS // 2
    n_hooks = 2 * (D_EXPERT // 256) + D_MODEL // 256
    per_hook = mv // n_hooks

    @pl.when((r == 0) & (e == 0))
    def _():
        slots_a[...] = jnp.zeros_like(slots_a)
        slots_b[...] = jnp.zeros_like(slots_b)

    do_exp = (r >= 1) & (r <= n_blocks)
    do_comb = (r >= 2) & (e < half)
    do_scat = (r < n_blocks) & (e >= half)
    b_exp = jnp.clip(r - 1, 0, n_blocks - 1)
    off = off_sm[b_exp, e]
    n = jnp.where(do_exp, cnt_sm[b_exp, e], 0)
    comb_idx0 = e * mv
    scat_idx0 = (e - half) * mv
    small = tm // 2
    first_rows = jnp.where(n > small, tm, jnp.where(n > 0, small, 0))
    fused = do_comb | do_scat
    start = jnp.where(fused, first_rows, 0)
    rest = jnp.maximum(n - start, 0)
    rest_big = rest // tm + ((rest % tm) > small).astype(jnp.int32)
    rest_tail = rest - rest_big * tm

    def round_body(exp_buf, mov_buf):
        tile = functools.partial(_expert_tile_hooked, exp_buf, wg_ref, wu_ref, wd_ref, ws)

        def comb_hooks():
            def hook(k):
                def run():
                    if k < n_hooks:
                        _combine_tokens(cb0_ref, cb1_ref, wts_ref, mov_buf, tm_scr, comb_idx0,
                                        k * per_hook, per_hook)
                    if k >= 1:
                        _combine_epilogue(tm_scr, x_ref, mod_ref, fin_ref, o_ref,
                                          (k - 1) * per_hook, per_hook, final_norm)
                return run
            return [hook(k) for k in range(n_hooks + 1)]

        def scat_hooks():
            def hook(k):
                return lambda: _scatter_tokens(sc0_ref, sc1_ref, h2_ref, mov_buf, scat_idx0,
                                               k * per_hook, per_hook)
            return [hook(k) for k in range(n_hooks)]

        for rows in (tm, small):
            @pl.when((first_rows == rows) & do_comb)
            def _():
                tile(off, n, rows, comb_hooks())

            @pl.when((first_rows == rows) & do_scat)
            def _():
                tile(off, n, rows, scat_hooks())

        @pl.when(first_rows == 0)
        def _():
            @pl.when(do_comb)
            def _():
                def body(i, carry):
                    i0 = comb_idx0 + U * i
                    for u in range(U):
                        p0 = cb0_ref[0, 0, i0 + u]
                        p1 = cb1_ref[0, 0, i0 + u]
                        w0 = wts_ref[0, 0, i0 + u]
                        za = mov_buf[pl.ds(pl.multiple_of(p0, R), R), :]
                        zb = mov_buf[pl.ds(pl.multiple_of(p1, R), R), :]
                        r0 = pl.multiple_of(i * (U * R), U * R) + u * R
                        tm_scr[pl.ds(r0, R), :] = zb + w0 * (za - zb)
                    return carry

                lax.fori_loop(0, mv // U, body, 0)
                _combine_epilogue(tm_scr, x_ref, mod_ref, fin_ref, o_ref, 0, mv, final_norm)

            @pl.when(do_scat)
            def _():
                def body(i, carry):
                    i0 = scat_idx0 + U * i
                    for u in range(U):
                        p0 = sc0_ref[0, 0, i0 + u]
                        p1 = sc1_ref[0, 0, i0 + u]
                        r0 = pl.multiple_of(i * (U * R), U * R) + u * R
                        val = h2_ref[pl.ds(r0, R), :]
                        mov_buf[pl.ds(pl.multiple_of(p0, R), R), :] = val
                        mov_buf[pl.ds(pl.multiple_of(p1, R), R), :] = val
                    return carry

                lax.fori_loop(0, mv // U, body, 0)

        def tile_body(j, carry):
            done = start + j * tm
            tile(off + done, n - done, tm, [])
            return carry

        lax.fori_loop(0, rest_big, tile_body, 0)

        @pl.when(rest_tail > 0)
        def _():
            done = start + rest_big * tm
            tile(off + done, n - done, small, [])

    @pl.when(r % 2 == 0)
    def _():
        round_body(slots_b, slots_a)

    @pl.when(r % 2 == 1)
    def _():
        round_body(slots_a, slots_b)


def _moe_call(off, cnt, pos0, pos1, w0, h2_tm, x, mod, fin_g, w_gate, w_up, w_down, layer,
              final_norm):
    bsz, seq, d = x.shape
    mv = MOE_MOVE
    half = N_EXPERTS // 2
    assert seq == half * mv
    n_slots = 2 * seq
    last = bsz - 1

    def scat_blk(r, e, *_):
        return (jnp.minimum(r, last) * half + jnp.clip(e - half, 0, half - 1), 0)

    def comb_blk(r, e, *_):
        return (jnp.where(r < 2, 0, (r - 2) * half + jnp.minimum(e, half - 1)), 0)

    def scat_row(r, e, *_):
        return (jnp.minimum(r, last), 0, 0)

    def comb_row(r, e, *_):
        return (jnp.clip(r - 2, 0, last), 0, 0)

    body = functools.partial(_moe_body, n_blocks=bsz, layer=layer, final_norm=final_norm)
    slot_rows = (n_slots + N_EXPERTS * (SLOT_ALIGN - 1) + MOE_TM) * ROWS_PER_TOKEN
    out = pl.pallas_call(
        body,
        grid_spec=pltpu.PrefetchScalarGridSpec(
            num_scalar_prefetch=2,
            grid=(bsz + 2, N_EXPERTS),
            in_specs=[
                pl.BlockSpec((1, 1, seq), scat_row, memory_space=pltpu.SMEM),
                pl.BlockSpec((1, 1, seq), scat_row, memory_space=pltpu.SMEM),
                pl.BlockSpec((1, 1, seq), comb_row, memory_space=pltpu.SMEM),
                pl.BlockSpec((1, 1, seq), comb_row, memory_space=pltpu.SMEM),
                pl.BlockSpec((1, 1, seq), comb_row, memory_space=pltpu.SMEM),
                pl.BlockSpec((mv * ROWS_PER_TOKEN, LANES), scat_blk),
                pl.BlockSpec((mv, d), comb_blk),
                pl.BlockSpec((1, 6, d), comb_row),
                pl.BlockSpec((1, d), lambda r, e, *_: (0, 0)),
                pl.BlockSpec(memory_space=pl.ANY),
                pl.BlockSpec(memory_space=pl.ANY),
                pl.BlockSpec(memory_space=pl.ANY),
            ],
            out_specs=pl.BlockSpec((mv, d), comb_blk),
            scratch_shapes=[
                pltpu.VMEM((slot_rows, LANES), jnp.float32),
                pltpu.VMEM((slot_rows, LANES), jnp.float32),
                pltpu.VMEM((mv * ROWS_PER_TOKEN, LANES), jnp.float32),
                pltpu.VMEM((MOE_WRING, d, D_EXPERT), jnp.bfloat16),
                pltpu.VMEM((MOE_WRING, d, D_EXPERT), jnp.bfloat16),
                pltpu.VMEM((MOE_WRING, D_EXPERT, d), jnp.bfloat16),
                pltpu.SemaphoreType.DMA((3, MOE_WRING)),
            ],
        ),
        out_shape=jax.ShapeDtypeStruct((bsz * seq, d), jnp.float32),
        compiler_params=pltpu.CompilerParams(
            dimension_semantics=("arbitrary", "arbitrary"),
            vmem_limit_bytes=MOE_VMEM_LIMIT),
        name="moe",
    )(off, cnt, pos0, pos1, pos0, pos1, w0, h2_tm, x.reshape(bsz * seq, d), mod, fin_g,
      w_gate, w_up, w_down)
    return out.reshape(bsz, seq, d)


def _moe_layer(x, h2_tm, logits, rbias, mod, fin_g, w_gate, w_up, w_down, layer, final_norm):
    pos0, pos1, w0, off, cnt = _route_call(logits, rbias)
    return _moe_call(off[:, :, 0], cnt[:, :, 0], pos0, pos1, w0, h2_tm, x, mod, fin_g,
                     w_gate, w_up, w_down, layer, final_norm)


def kernel(x, c, gla_w_in, gla_w_gate_up, gla_b_gate, gla_norm_g, gla_w_out, lru_w_in, lru_conv_w, lru_conv_b, lru_w_r, lru_b_r, lru_w_i, lru_b_i, lru_lambda, lru_w_out, router_w, router_bias, moe_w_gate, moe_w_up, moe_w_down, norm_mix_g, norm_ffn_g, ada_w, ada_b, final_norm_g):
    bf = jnp.bfloat16
    depth = ada_w.shape[0]
    bsz = x.shape[0]
    d = D_MODEL
    mod_all = _ada_call(c, ada_w, ada_b).reshape(depth, bsz, 6, d)

    rw_t = router_w.T
    rw_hi = rw_t.astype(bf)
    rw_lo = (rw_t - rw_hi.astype(jnp.float32)).astype(bf)
    rw_cat = jnp.concatenate([rw_hi, rw_lo], axis=0)
    rbias = router_bias.reshape(N_EXPERTS, 1)
    fin_g = final_norm_g.reshape(1, d)
    expert_w = None

    for i in range(depth):
        j = i // 2
        mod = mod_all[i]
        gmix = norm_mix_g[i].reshape(1, d)
        gffn = norm_ffn_g[i].reshape(1, d)
        if i % 2 == 0:
            w_in = jnp.pad(gla_w_in[j], ((0, 0), (0, GLA_RANK_PAD - GLA_GATE_RANK))).astype(bf)
            w_gu = jnp.pad(gla_w_gate_up[j], ((0, GLA_RANK_PAD - GLA_GATE_RANK), (0, 0))).astype(bf)
            cast_src = () if expert_w is not None else tuple(
                w.reshape(-1, w.shape[-1]) for w in (moe_w_gate, moe_w_up, moe_w_down))
            (x, h2_tm, logits), cast = _gla_layer_call(
                x, mod, gmix, gffn, w_in, w_gu, gla_b_gate[j].reshape(1, GLA_QK),
                gla_norm_g[j].reshape(1, GLA_DV), gla_w_out[j].astype(bf),
                rw_cat, rw_hi, cast_src)
            if cast:
                expert_w = (cast[0].reshape(depth * N_EXPERTS, d, D_EXPERT),
                            cast[1].reshape(depth * N_EXPERTS, d, D_EXPERT),
                            cast[2].reshape(depth * N_EXPERTS, D_EXPERT, d))
        else:
            x, h2_tm, logits = _lru_layer_call(
                x, mod, gmix, gffn, lru_w_in[j].astype(bf), lru_conv_w[j],
                lru_conv_b[j].reshape(1, d), lru_w_r[j].astype(bf),
                lru_b_r[j].reshape(1, d), lru_w_i[j].astype(bf), lru_b_i[j].reshape(1, d),
                lru_lambda[j].reshape(1, d), lru_w_out[j].astype(bf), rw_cat, rw_hi)
        x = _moe_layer(x, h2_tm, logits, rbias, mod, fin_g, *expert_w,
                       layer=i, final_norm=(i == depth - 1))
    return x
```

```python
import functools

import jax
import jax.numpy as jnp
from jax import lax
from jax.experimental import pallas as pl
from jax.experimental.pallas import tpu as pltpu

D_MODEL = 1024
CHUNK = 64
EPS = 1e-6

GLA_HEADS = 4
GLA_DK = 128
GLA_DV = 256
GLA_QK = GLA_HEADS * GLA_DK
GLA_VD = GLA_HEADS * GLA_DV
GLA_GATE_RANK = 16
GLA_GATE_TAU = 16.0
GLA_RANK_PAD = 128

LRU_BLOCKS = 4
LRU_BLOCK_W = D_MODEL // LRU_BLOCKS
CONV_W = 4
LRU_C = 8.0

N_EXPERTS = 16
N_GROUPS = 4
EPG = N_EXPERTS // N_GROUPS
D_EXPERT = 512

LANES = 128
SUBLANES = 8
ROWS_PER_TOKEN = D_MODEL // LANES

MIX_TS = 256
MIX_STREAMS = 2
GLA_STAGGER = 6
ROUTE_BLOCK = 256
SLOT_ALIGN = 4
MOE_TM = 256
MOE_UNROLL = 8
MOE_MOVE = 256
MOE_WRING = 3
VMEM_LIMIT = 56 * 1024 * 1024
MOE_VMEM_LIMIT = 60 * 1024 * 1024


def _dot(a, b):
    return jnp.dot(a, b, preferred_element_type=jnp.float32)


def _dot_nt(a, b):
    return lax.dot_general(a, b, (((1,), (1,)), ((), ())),
                           preferred_element_type=jnp.float32)


def _dot_tn(a, b):
    return lax.dot_general(a, b, (((0,), (0,)), ((), ())),
                           preferred_element_type=jnp.float32)


def _split_bf16(x):
    hi = x.astype(jnp.bfloat16)
    lo = (x - hi.astype(jnp.float32)).astype(jnp.bfloat16)
    return hi, lo


def _sigmoid(x):
    return 0.5 * jnp.tanh(0.5 * x) + 0.5


def _sigmoid_rel(x):
    return 1.0 / (1.0 + jnp.exp(-x))


def _silu(x):
    return x * _sigmoid(x)


def _rms(x, g):
    inv = lax.rsqrt(jnp.mean(x * x, axis=-1, keepdims=True) + EPS)
    return x * inv * g


def _ada_body(c_ref, w_ref, b_ref, o_ref):
    c_hi, c_lo = _split_bf16(_silu(c_ref[...]))
    w_hi, w_lo = _split_bf16(w_ref[0])
    o_ref[0] = _dot(c_hi, w_hi) + (_dot(c_lo, w_hi) + _dot(c_hi, w_lo)) + b_ref[0]


def _ada_call(c, ada_w, ada_b):
    depth, d, n = ada_w.shape
    bsz = c.shape[0]
    tn = 1024
    return pl.pallas_call(
        _ada_body,
        grid=(depth, n // tn),
        in_specs=[
            pl.BlockSpec((bsz, d), lambda l, j: (0, 0)),
            pl.BlockSpec((1, d, tn), lambda l, j: (l, 0, j)),
            pl.BlockSpec((1, 1, tn), lambda l, j: (l, 0, j)),
        ],
        out_specs=pl.BlockSpec((1, bsz, tn), lambda l, j: (l, 0, j)),
        out_shape=jax.ShapeDtypeStruct((depth, bsz, n), jnp.float32),
        compiler_params=pltpu.CompilerParams(
            dimension_semantics=("arbitrary", "arbitrary"),
            vmem_limit_bytes=VMEM_LIMIT),
        name="ada",
    )(c, ada_w, ada_b.reshape(depth, 1, n))


def _pre_norm(x, mod_ref, g_ref, shift_row, scale_row):
    shift = mod_ref[0, shift_row:shift_row + 1, :]
    scale = mod_ref[0, scale_row:scale_row + 1, :]
    inv = lax.rsqrt(jnp.mean(x * x, axis=-1, keepdims=True) + EPS)
    return (x * inv) * (g_ref[...] * (1.0 + scale)) + shift


def _group_partner(x, k, sub):
    n = x.shape[0]
    fwd = pltpu.roll(x, n - k, 0)
    back = pltpu.roll(x, EPG - k, 0)
    wrapped = (sub % EPG) + k >= EPG
    return jnp.where(wrapped, back, fwd), wrapped


def _router_logits(h2, rw_cat_ref, rw_hi_ref):
    h_hi, h_lo = _split_bf16(h2)
    p1 = _dot_nt(rw_cat_ref[...], h_hi)
    p2 = _dot_nt(rw_hi_ref[...], h_lo)
    return p1[0:N_EXPERTS] + p1[N_EXPERTS:2 * N_EXPERTS] + p2


def _route_select(s, rbias):
    n = s.shape[1]
    sel = s + rbias
    sub = lax.broadcasted_iota(jnp.int32, (N_EXPERTS, n), 0)

    pair_best = None
    rank = jnp.zeros((N_EXPERTS, n), jnp.float32)
    for k in range(1, EPG):
        p, wrapped = _group_partner(sel, k, sub)
        ps = sel + p
        pair_best = ps if pair_best is None else jnp.maximum(pair_best, ps)
        ahead = (p > sel) | ((p == sel) & wrapped)
        rank = rank + ahead.astype(jnp.float32)
    gscore = pair_best
    for k in range(1, EPG):
        p, _ = _group_partner(pair_best, k, sub)
        gscore = jnp.maximum(gscore, p)
    chosen = jnp.ones((N_EXPERTS, n), jnp.bool_)
    grp = sub // EPG
    for j in range(1, N_GROUPS):
        other = pltpu.roll(gscore, N_EXPERTS - EPG * j, 0)
        other_is_later = grp + j < N_GROUPS
        chosen = chosen & ((gscore > other) | ((gscore == other) & other_is_later))
    m = chosen & (rank < 2.0)
    sm = s * m.astype(jnp.float32)
    wgt = sm / jnp.sum(sm, axis=0, keepdims=True)
    return m, wgt, sub


def _post_mix(x, mix, mod_ref, gffn_ref, rw_cat_ref, rw_hi_ref, xo_ref, h2_ref, lg_ref):
    ts = x.shape[0]
    x_new = x + mod_ref[0, 2:3, :] * mix
    xo_ref[0] = x_new
    h2 = _pre_norm(x_new, mod_ref, gffn_ref, 3, 4)
    for c in range(ROWS_PER_TOKEN):
        h2_ref[pl.ds(c, ts, stride=ROWS_PER_TOKEN), :] = h2[:, c * LANES:(c + 1) * LANES]
    lg_ref[0] = _router_logits(h2, rw_cat_ref, rw_hi_ref)


def _mixer_io_specs(bsz, seq, ts):
    ns = MIX_STREAMS
    hb = bsz // ns
    nt = seq // ts
    in_specs = [
        pl.BlockSpec((ns, 1, ts, D_MODEL), lambda b, t: (0, b, t, 0)),
        pl.BlockSpec((ns, 1, 6, D_MODEL), lambda b, t: (0, b, 0, 0)),
    ]
    out_specs = [
        pl.BlockSpec((ns, 1, ts, D_MODEL), lambda b, t: (0, b, t, 0)),
        pl.BlockSpec((ns, ts * ROWS_PER_TOKEN, LANES), lambda b, t: (0, b * nt + t, 0)),
        pl.BlockSpec((ns, 1, N_EXPERTS, ts), lambda b, t: (0, b, 0, t)),
    ]
    out_shapes = [
        jax.ShapeDtypeStruct((ns, hb, seq, D_MODEL), jnp.float32),
        jax.ShapeDtypeStruct((ns, hb * seq * ROWS_PER_TOKEN, LANES), jnp.float32),
        jax.ShapeDtypeStruct((ns, hb, N_EXPERTS, seq), jnp.float32),
    ]
    return in_specs, out_specs, out_shapes


def _mixer_unstream(outs, bsz, seq):
    x, h2_tm, logits = outs
    return (x.reshape(bsz, seq, D_MODEL), h2_tm.reshape(bsz * seq * ROWS_PER_TOKEN, LANES),
            logits.reshape(bsz, N_EXPERTS, seq))


def _const_spec(shape):
    nd = len(shape)
    return pl.BlockSpec(shape, lambda b, t: (0,) * nd)


def _gla_body(x_ref, mod_ref, gmix_ref, gffn_ref, win_ref, wgu_ref, bg_ref, ng_ref,
              wout_ref, rw_cat_ref, rw_hi_ref, *rest, n_cast):
    cast_in = rest[:n_cast]
    xo_ref, h2_ref, lg_ref = rest[n_cast:n_cast + 3]
    cast_out = rest[n_cast + 3:2 * n_cast + 3]
    state_scr, o_scr = rest[2 * n_cast + 3:]

    @pl.when(pl.program_id(1) == 0)
    def _():
        state_scr[...] = jnp.zeros_like(state_scr)

    def caster():
        pieces = 4
        for src, dst in zip(cast_in, cast_out):
            rows = src.shape[0] // pieces
            for i in range(pieces):
                dst[i * rows:(i + 1) * rows, :] = src[i * rows:(i + 1) * rows, :].astype(jnp.bfloat16)
                yield

    gens = [_gla_stream(x_ref.at[st], mod_ref.at[st], gmix_ref, gffn_ref, win_ref, wgu_ref,
                        bg_ref, ng_ref, wout_ref, rw_cat_ref, rw_hi_ref,
                        xo_ref.at[st], h2_ref.at[st], lg_ref.at[st],
                        state_scr.at[st], o_scr.at[st])
            for st in range(MIX_STREAMS)]
    _interleave(gens + [caster()], GLA_STAGGER)


def _gla_stream(x_ref, mod_ref, gmix_ref, gffn_ref, win_ref, wgu_ref, bg_ref, ng_ref,
                wout_ref, rw_cat_ref, rw_hi_ref,
                xo_ref, h2_ref, lg_ref, state_scr, o_scr):
    ts = x_ref.shape[1]
    nchunk = ts // CHUNK
    half = GLA_VD // 2
    x = x_ref[0]
    h = _pre_norm(x, mod_ref, gmix_ref, 0, 1).astype(jnp.bfloat16)
    yield

    o_q, o_k, o_v, o_g, o_a = 0, GLA_QK, 2 * GLA_QK, 2 * GLA_QK + GLA_VD, 2 * GLA_QK + 2 * GLA_VD
    q = (_dot(h, win_ref[:, o_q:o_k]) * (GLA_DK ** -0.5)).astype(jnp.bfloat16)
    yield
    k = _dot(h, win_ref[:, o_k:o_v])
    yield
    v = jnp.concatenate(
        [_dot(h, win_ref[:, o_v + i * half:o_v + (i + 1) * half]).astype(jnp.bfloat16)
         for i in range(2)], axis=1)
    yield
    a_lr = _dot(h, win_ref[:, o_a:o_a + GLA_RANK_PAD])
    a_hi, a_lo = _split_bf16(a_lr)
    z2 = _dot(jnp.concatenate([a_hi, a_lo], axis=0), wgu_ref[...])
    z = z2[0:ts] + z2[ts:2 * ts] + bg_ref[...]
    log_a = -(jnp.maximum(-z, 0.0) + jnp.log1p(jnp.exp(-jnp.abs(z)))) * (1.0 / GLA_GATE_TAU)
    yield

    row = lax.broadcasted_iota(jnp.int32, (ts, ts), 0)
    col = lax.broadcasted_iota(jnp.int32, (ts, ts), 1)
    tri = ((row // CHUNK == col // CHUNK) & (col <= row)).astype(jnp.bfloat16)
    l_hi, l_lo = _split_bf16(log_a)
    cum2 = _dot(tri, jnp.concatenate([l_hi, l_lo], axis=1))
    cum = cum2[:, 0:GLA_QK] + cum2[:, GLA_QK:2 * GLA_QK]
    yield

    ng = ng_ref[...]
    kv_t, gammas = [], []
    for j in range(nchunk):
        r0 = j * CHUNK
        cum_j = cum[r0:r0 + CHUNK]
        total = cum_j[CHUNK - 1:CHUNK]
        k_dec = (k[r0:r0 + CHUNK] * jnp.exp(total - cum_j)).astype(jnp.bfloat16)
        gammas.append(jnp.exp(total))
        for hd in range(GLA_HEADS):
            ks = slice(hd * GLA_DK, (hd + 1) * GLA_DK)
            vs = slice(hd * GLA_DV, (hd + 1) * GLA_DV)
            kv_t.append(_dot_tn(v[r0:r0 + CHUNK, vs], k_dec[:, ks]))
        yield
    states = []
    for hd in range(GLA_HEADS):
        ks = slice(hd * GLA_DK, (hd + 1) * GLA_DK)
        s_cur = state_scr[hd]
        for j in range(nchunk):
            s_cur = s_cur * gammas[j][:, ks] + kv_t[j * GLA_HEADS + hd]
            states.append(s_cur.astype(jnp.bfloat16))
        state_scr[hd] = s_cur
        yield
    for hd in range(GLA_HEADS):
        ks = slice(hd * GLA_DK, (hd + 1) * GLA_DK)
        vs = slice(hd * GLA_DV, (hd + 1) * GLA_DV)
        for j in range(nchunk):
            r0 = j * CHUNK
            o = _dot_nt(q[r0:r0 + CHUNK, ks], states[hd * nchunk + j])
            o_scr[r0:r0 + CHUNK, vs] = _rms(o, ng)
        yield

    og = []
    for i in range(2):
        g = _dot(h, win_ref[:, o_g + i * half:o_g + (i + 1) * half])
        og.append((o_scr[:, i * half:(i + 1) * half] * _silu(g)).astype(jnp.bfloat16))
        yield
    og = jnp.concatenate(og, axis=1)
    x_new = []
    for i in range(2):
        cols = slice(i * half, (i + 1) * half)
        mix = _dot(og, wout_ref[:, i * half:(i + 1) * half])
        x_new.append(x[:, cols] + mod_ref[0, 2:3, cols] * mix)
        yield
    x_new = jnp.concatenate(x_new, axis=1)
    xo_ref[0] = x_new
    h2 = _pre_norm(x_new, mod_ref, gffn_ref, 3, 4)
    for c in range(ROWS_PER_TOKEN):
        h2_ref[pl.ds(c, ts, stride=ROWS_PER_TOKEN), :] = h2[:, c * LANES:(c + 1) * LANES]
    yield
    lg_ref[0] = _router_logits(h2, rw_cat_ref, rw_hi_ref)
    yield


def _gla_layer_call(x, mod, gmix, gffn, w_in, w_gu, b_g, n_g, w_out, rw_cat, rw_hi, cast_src):
    bsz, seq, d = x.shape
    ts = MIX_TS
    ns = MIX_STREAMS
    hb = bsz // ns
    nt = seq // ts
    io_in, out_specs, out_shapes = _mixer_io_specs(bsz, seq, ts)
    consts = (gmix, gffn, w_in, w_gu, b_g, n_g, w_out, rw_cat, rw_hi)
    cast_specs, cast_shapes = [], []
    for a in cast_src:
        rows = a.shape[0] // (hb * nt)
        assert rows * hb * nt == a.shape[0]
        cast_specs.append(pl.BlockSpec((rows, a.shape[1]), lambda b, t: (b * nt + t, 0)))
        cast_shapes.append(jax.ShapeDtypeStruct(a.shape, jnp.bfloat16))
    outs = pl.pallas_call(
        functools.partial(_gla_body, n_cast=len(cast_src)),
        grid=(hb, nt),
        in_specs=io_in + [_const_spec(a.shape) for a in consts] + cast_specs,
        out_specs=out_specs + cast_specs,
        out_shape=out_shapes + cast_shapes,
        scratch_shapes=[
            pltpu.VMEM((ns, GLA_HEADS, GLA_DV, GLA_DK), jnp.float32),
            pltpu.VMEM((ns, ts, GLA_VD), jnp.float32),
        ],
        compiler_params=pltpu.CompilerParams(
            dimension_semantics=("arbitrary", "arbitrary"),
            vmem_limit_bytes=VMEM_LIMIT),
        name="gla_layer",
    )(x.reshape(ns, hb, seq, d), mod.reshape(ns, hb, 6, d), *consts, *cast_src)
    return _mixer_unstream(outs[:3], bsz, seq), outs[3:]


def _lru_body(x_ref, mod_ref, gmix_ref, gffn_ref, win_ref, cw_ref, cb_ref, wr_ref, br_ref,
              wi_ref, bi_ref, lam_ref, wout_ref, rw_cat_ref, rw_hi_ref,
              xo_ref, h2_ref, lg_ref,
              conv_scr, hstate_scr, a_scr, b_scr, gelu_scr):
    ns = MIX_STREAMS
    ts = x_ref.shape[2]
    w = D_MODEL

    @pl.when(pl.program_id(1) == 0)
    def _():
        conv_scr[...] = jnp.zeros_like(conv_scr)
        hstate_scr[...] = jnp.zeros_like(hstate_scr)

    for st in range(ns):
        _lru_pre_scan(x_ref.at[st], mod_ref.at[st], gmix_ref, win_ref, cw_ref, cb_ref, wr_ref,
                      br_ref, wi_ref, bi_ref, lam_ref,
                      conv_scr.at[st], a_scr.at[st], b_scr.at[st], gelu_scr.at[st])

    unroll = 4

    def group_step(gi, hprev):
        for u in range(unroll):
            r0 = pl.multiple_of(gi * (unroll * SUBLANES), unroll * SUBLANES) + u * SUBLANES
            nxt = []
            for st in range(ns):
                hs = (a_scr[st, pl.ds(r0, SUBLANES), :] * hprev[st]
                      + b_scr[st, pl.ds(r0, SUBLANES), :])
                b_scr[st, pl.ds(r0, SUBLANES), :] = hs
                nxt.append(hs[SUBLANES - 1:SUBLANES, :])
            hprev = tuple(nxt)
        return hprev

    hlast = tuple(hstate_scr[st] for st in range(ns))
    for gi in range(ts // (unroll * SUBLANES)):
        hlast = group_step(gi, hlast)
    for st in range(ns):
        hstate_scr[st] = hlast[st]

    for st in range(ns):
        x = x_ref[st, 0]
        y = (b_scr[st] * gelu_scr[st]).astype(jnp.bfloat16)
        mix = _dot(y, wout_ref[...])
        _post_mix(x, mix, mod_ref.at[st], gffn_ref, rw_cat_ref, rw_hi_ref,
                  xo_ref.at[st], h2_ref.at[st], lg_ref.at[st])


def _lru_pre_scan(x_ref, mod_ref, gmix_ref, win_ref, cw_ref, cb_ref, wr_ref, br_ref,
                  wi_ref, bi_ref, lam_ref, conv_scr, a_scr, b_scr, gelu_scr):
    ts = x_ref.shape[1]
    w = D_MODEL
    x = x_ref[0]
    h = _pre_norm(x, mod_ref, gmix_ref, 0, 1).astype(jnp.bfloat16)
    gate_br = _dot(h, win_ref[:, 0:w])
    c0 = 0.7978845608028654
    u = gate_br * (c0 + (c0 * 0.044715) * (gate_br * gate_br))
    half_g = 0.5 * gate_br
    gelu_scr[...] = half_g + half_g * jnp.tanh(u)
    xb = _dot(h, win_ref[:, w:2 * w])

    ext = jnp.concatenate([conv_scr[...], xb], axis=0)
    xc = cb_ref[...] + xb * cw_ref[CONV_W - 1:CONV_W, :]
    for j in range(CONV_W - 1):
        back = CONV_W - 1 - j
        xc = xc + pltpu.roll(ext, back, 0)[SUBLANES:SUBLANES + ts] * cw_ref[j:j + 1, :]
    conv_scr[...] = xb[ts - SUBLANES:ts]

    xcb = xc.astype(jnp.bfloat16)
    rs, iis = [], []
    for hd in range(LRU_BLOCKS):
        sl = slice(hd * LRU_BLOCK_W, (hd + 1) * LRU_BLOCK_W)
        rs.append(_dot(xcb[:, sl], wr_ref[hd]))
        iis.append(_dot(xcb[:, sl], wi_ref[hd]))
    t_r = jnp.tanh(0.5 * (jnp.concatenate(rs, axis=1) + br_ref[...]))
    t_i = jnp.tanh(0.5 * (jnp.concatenate(iis, axis=1) + bi_ref[...]))

    lam = lam_ref[...]
    softplus_neg_lam = jnp.maximum(-lam, 0.0) + jnp.log1p(jnp.exp(-jnp.abs(lam)))
    sp = (-0.5 * LRU_C) * softplus_neg_lam
    log_a = sp * t_r + sp
    a = jnp.exp(log_a)
    half_mult = jnp.sqrt(jnp.tanh(-log_a) * (0.25 * (a * a) + 0.25))
    bb = (xc * half_mult) * (t_i + 1.0)

    a = a.reshape(ts // SUBLANES, SUBLANES, w)
    bb = bb.reshape(ts // SUBLANES, SUBLANES, w)
    rowi = lax.broadcasted_iota(jnp.int32, a.shape, 1)
    for d in (1, 2, 4):
        keep = rowi >= d
        a_sh = jnp.where(keep, pltpu.roll(a, d, 1), 1.0)
        b_sh = jnp.where(keep, pltpu.roll(bb, d, 1), 0.0)
        bb = a * b_sh + bb
        a = a * a_sh
    a_scr[...] = a.reshape(ts, w)
    b_scr[...] = bb.reshape(ts, w)


def _interleave(gens, lag):
    live = [True] * len(gens)
    tick = 0
    while any(live):
        for i, gen in enumerate(gens):
            if live[i] and tick >= i * lag:
                try:
                    next(gen)
                except StopIteration:
                    live[i] = False
        tick += 1


def _lru_layer_call(x, mod, gmix, gffn, w_in, conv_w, conv_b, w_r, b_r, w_i, b_i, lam, w_out,
                    rw_cat, rw_hi):
    bsz, seq, d = x.shape
    ts = MIX_TS
    ns = MIX_STREAMS
    hb = bsz // ns
    io_in, out_specs, out_shapes = _mixer_io_specs(bsz, seq, ts)
    consts = (gmix, gffn, w_in, conv_w, conv_b, w_r, b_r, w_i, b_i, lam, w_out,
              rw_cat, rw_hi)
    outs = pl.pallas_call(
        _lru_body,
        grid=(hb, seq // ts),
        in_specs=io_in + [_const_spec(a.shape) for a in consts],
        out_specs=out_specs,
        out_shape=out_shapes,
        scratch_shapes=[
            pltpu.VMEM((ns, SUBLANES, d), jnp.float32),
            pltpu.VMEM((ns, 1, d), jnp.float32),
            pltpu.VMEM((ns, ts, d), jnp.float32),
            pltpu.VMEM((ns, ts, d), jnp.float32),
            pltpu.VMEM((ns, ts, d), jnp.float32),
        ],
        compiler_params=pltpu.CompilerParams(
            dimension_semantics=("arbitrary", "arbitrary"),
            vmem_limit_bytes=VMEM_LIMIT),
        name="lru_layer",
    )(x.reshape(ns, hb, seq, d), mod.reshape(ns, hb, 6, d), *consts)
    return _mixer_unstream(outs, bsz, seq)


def _route_body(lg_ref, rbias_ref, pos0_ref, pos1_ref, w0_ref, off_ref, cnt_ref):
    seq = lg_ref.shape[2]
    blk = ROUTE_BLOCK
    s = _sigmoid_rel(lg_ref[0])
    m, wgt, sub = _route_select(s, rbias_ref[...])
    mf = m.astype(jnp.float32)

    row = lax.broadcasted_iota(jnp.int32, (blk, blk), 0)
    col = lax.broadcasted_iota(jnp.int32, (blk, blk), 1)
    before = (row < col).astype(jnp.bfloat16)
    carry = jnp.zeros((N_EXPERTS, 1), jnp.float32)
    ranks = []
    for i in range(seq // blk):
        mb = mf[:, i * blk:(i + 1) * blk]
        ranks.append(_dot(mb.astype(jnp.bfloat16), before) + carry)
        carry = carry + jnp.sum(mb, axis=1, keepdims=True)
    rnk = jnp.concatenate(ranks, axis=1)
    cnt_ref[0] = jnp.broadcast_to(carry, (N_EXPERTS, LANES)).astype(jnp.int32)

    eidx = sub.astype(jnp.float32)
    e_lo = jnp.min(jnp.where(m, eidx, 99.0), axis=0, keepdims=True)
    e_hi = jnp.max(jnp.where(m, eidx, -1.0), axis=0, keepdims=True)
    is_lo = m & (eidx == e_lo)
    is_hi = m & (eidx == e_hi)
    p_lo = jnp.sum(jnp.where(is_lo, rnk, 0.0), axis=0, keepdims=True)
    p_hi = jnp.sum(jnp.where(is_hi, rnk, 0.0), axis=0, keepdims=True)
    w0_ref[0] = jnp.sum(jnp.where(is_lo, wgt, 0.0), axis=0, keepdims=True)

    off = jnp.zeros((1, 1), jnp.float32)
    for e in range(N_EXPERTS):
        off_ref[0, e:e + 1, :] = jnp.broadcast_to(off, (1, LANES)).astype(jnp.int32)
        p_lo = p_lo + jnp.where(e_lo == float(e), off, 0.0)
        p_hi = p_hi + jnp.where(e_hi == float(e), off, 0.0)
        off = off + jnp.floor((carry[e:e + 1, :] + (SLOT_ALIGN - 1.0)) * (1.0 / SLOT_ALIGN)) * SLOT_ALIGN
    pos0_ref[0] = p_lo.astype(jnp.int32) * ROWS_PER_TOKEN
    pos1_ref[0] = p_hi.astype(jnp.int32) * ROWS_PER_TOKEN


def _route_call(logits, rbias):
    bsz, _, seq = logits.shape
    row_spec = pl.BlockSpec((1, 1, seq), lambda b: (b, 0, 0))
    tab_spec = pl.BlockSpec((1, N_EXPERTS, LANES), lambda b: (b, 0, 0))
    return pl.pallas_call(
        _route_body,
        grid=(bsz,),
        in_specs=[
            pl.BlockSpec((1, N_EXPERTS, seq), lambda b: (b, 0, 0)),
            pl.BlockSpec((N_EXPERTS, 1), lambda b: (0, 0)),
        ],
        out_specs=[row_spec, row_spec, row_spec, tab_spec, tab_spec],
        out_shape=[
            jax.ShapeDtypeStruct((bsz, 1, seq), jnp.int32),
            jax.ShapeDtypeStruct((bsz, 1, seq), jnp.int32),
            jax.ShapeDtypeStruct((bsz, 1, seq), jnp.float32),
            jax.ShapeDtypeStruct((bsz, N_EXPERTS, LANES), jnp.int32),
            jax.ShapeDtypeStruct((bsz, N_EXPERTS, LANES), jnp.int32),
        ],
        compiler_params=pltpu.CompilerParams(dimension_semantics=("arbitrary",)),
        name="moe_route",
    )(logits, rbias)


def _scatter_tokens(pos0_ref, pos1_ref, h2_ref, slots, idx0, lo, cnt):
    R = ROWS_PER_TOKEN
    for u in range(lo, lo + cnt):
        p0 = pos0_ref[0, 0, idx0 + u]
        p1 = pos1_ref[0, 0, idx0 + u]
        val = h2_ref[u * R:(u + 1) * R, :]
        slots[pl.ds(pl.multiple_of(p0, R), R), :] = val
        slots[pl.ds(pl.multiple_of(p1, R), R), :] = val


def _combine_tokens(pos0_ref, pos1_ref, wts_ref, slots, tm_scr, idx0, lo, cnt):
    R = ROWS_PER_TOKEN
    for u in range(lo, lo + cnt):
        p0 = pos0_ref[0, 0, idx0 + u]
        p1 = pos1_ref[0, 0, idx0 + u]
        w0 = wts_ref[0, 0, idx0 + u]
        za = slots[pl.ds(pl.multiple_of(p0, R), R), :]
        zb = slots[pl.ds(pl.multiple_of(p1, R), R), :]
        tm_scr[u * R:(u + 1) * R, :] = zb + w0 * (za - zb)


def _combine_epilogue(tm_scr, x_ref, mod_ref, fin_ref, o_ref, lo, cnt, final_norm):
    R = ROWS_PER_TOKEN
    moe = jnp.concatenate(
        [tm_scr[pl.ds(lo * R + c, cnt, stride=R), :] for c in range(R)], axis=1)
    out = x_ref[lo:lo + cnt, :] + mod_ref[0, 5:6, :] * moe
    if final_norm:
        out = _rms(out, fin_ref[...])
    o_ref[lo:lo + cnt, :] = out


def _expert_tile_hooked(slots, wg_ref, wu_ref, wd_ref, ws, slot0, n_valid, tm, hooks):
    R = ROWS_PER_TOKEN
    nw = 256
    row0 = pl.multiple_of(slot0 * R, SLOT_ALIGN * R)
    xs = [slots[pl.ds(row0 + c, tm, stride=R), :] for c in range(R)]
    xt = jnp.concatenate(xs, axis=1).astype(jnp.bfloat16)
    ok = lax.broadcasted_iota(jnp.int32, (tm, LANES), 0) < n_valid
    hooks = list(hooks)

    def run_hook():
        if hooks:
            hooks.pop(0)()

    gate, up = [], []
    for i in range(D_EXPERT // nw):
        gate.append(_dot(xt, wg_ref[ws, :, i * nw:(i + 1) * nw]))
        run_hook()
    for i in range(D_EXPERT // nw):
        up.append(_dot(xt, wu_ref[ws, :, i * nw:(i + 1) * nw]))
        run_hook()
    he = (_silu(jnp.concatenate(gate, axis=1)) * jnp.concatenate(up, axis=1)).astype(jnp.bfloat16)
    for i in range(D_MODEL // nw):
        y = _dot(he, wd_ref[ws, :, i * nw:(i + 1) * nw])
        for cc in range(nw // LANES):
            c = i * (nw // LANES) + cc
            slots[pl.ds(row0 + c, tm, stride=R), :] = jnp.where(
                ok, y[:, cc * LANES:(cc + 1) * LANES], xs[c])
        run_hook()
    while hooks:
        run_hook()


def _moe_body(off_sm, cnt_sm, sc0_ref, sc1_ref, cb0_ref, cb1_ref, wts_ref, h2_ref, x_ref,
              mod_ref, fin_ref, wg_hbm, wu_hbm, wd_hbm, o_ref, slots_a, slots_b, tm_scr,
              wg_ref, wu_ref, wd_ref, w_sem, *, n_blocks, layer, final_norm):
    r = pl.program_id(0)
    e = pl.program_id(1)

    g = r * N_EXPERTS + e
    n_steps = (n_blocks + 2) * N_EXPERTS

    def w_needed(step):
        rnd = step // N_EXPERTS
        return (rnd >= 1) & (rnd <= n_blocks) & (step < n_steps)

    def w_copies(step):
        ex = layer * N_EXPERTS + step % N_EXPERTS
        sl = step % MOE_WRING
        return [pltpu.make_async_copy(src.at[ex], dst.at[sl], w_sem.at[i, sl])
                for i, (src, dst) in enumerate(((wg_hbm, wg_ref), (wu_hbm, wu_ref),
                                                (wd_hbm, wd_ref)))]

    def w_start(step):
        @pl.when(w_needed(step))
        def _():
            for cp in w_copies(step):
                cp.start()

    @pl.when(g == 0)
    def _():
        for ahead in range(MOE_WRING - 1):
            w_start(g + ahead)

    w_start(g + MOE_WRING - 1)

    @pl.when(w_needed(g))
    def _():
        for cp in w_copies(g):
            cp.wait()

    ws = g % MOE_WRING
    R = ROWS_PER_TOKEN
    U = MOE_UNROLL
    tm = MOE_TM
    mv = MOE_MOVE
    half = N_EXPERTS // 2
    n_hooks = 2 * (D_EXPERT // 256) + D_MODEL // 256
    per_hook = mv // n_hooks

    @pl.when((r == 0) & (e == 0))
    def _():
        slots_a[...] = jnp.zeros_like(slots_a)
        slots_b[...] = jnp.zeros_like(slots_b)

    do_exp = (r >= 1) & (r <= n_blocks)
    do_comb = (r >= 2) & (e < half)
    do_scat = (r < n_blocks) & (e >= half)
    b_exp = jnp.clip(r - 1, 0, n_blocks - 1)
    off = off_sm[b_exp, e]
    n = jnp.where(do_exp, cnt_sm[b_exp, e], 0)
    comb_idx0 = e * mv
    scat_idx0 = (e - half) * mv
    small = tm // 2
    first_rows = jnp.where(n > small, tm, jnp.where(n > 0, small, 0))
    fused = do_comb | do_scat
    start = jnp.where(fused, first_rows, 0)
    rest = jnp.maximum(n - start, 0)
    rest_big = rest // tm + ((rest % tm) > small).astype(jnp.int32)
    rest_tail = rest - rest_big * tm

    def round_body(exp_buf, mov_buf):
        tile = functools.partial(_expert_tile_hooked, exp_buf, wg_ref, wu_ref, wd_ref, ws)

        def comb_hooks():
            def hook(k):
                def run():
                    if k < n_hooks:
                        _combine_tokens(cb0_ref, cb1_ref, wts_ref, mov_buf, tm_scr, comb_idx0,
                                        k * per_hook, per_hook)
                    if k >= 1:
                        _combine_epilogue(tm_scr, x_ref, mod_ref, fin_ref, o_ref,
                                          (k - 1) * per_hook, per_hook, final_norm)
                return run
            return [hook(k) for k in range(n_hooks + 1)]

        def scat_hooks():
            def hook(k):
                return lambda: _scatter_tokens(sc0_ref, sc1_ref, h2_ref, mov_buf, scat_idx0,
                                               k * per_hook, per_hook)
            return [hook(k) for k in range(n_hooks)]

        for rows in (tm, small):
            @pl.when((first_rows == rows) & do_comb)
            def _():
                tile(off, n, rows, comb_hooks())

            @pl.when((first_rows == rows) & do_scat)
            def _():
                tile(off, n, rows, scat_hooks())

        @pl.when(first_rows == 0)
        def _():
            @pl.when(do_comb)
            def _():
                def body(i, carry):
                    i0 = comb_idx0 + U * i
                    for u in range(U):
                        p0 = cb0_ref[0, 0, i0 + u]
                        p1 = cb1_ref[0, 0, i0 + u]
                        w0 = wts_ref[0, 0, i0 + u]
                        za = mov_buf[pl.ds(pl.multiple_of(p0, R), R), :]
                        zb = mov_buf[pl.ds(pl.multiple_of(p1, R), R), :]
                        r0 = pl.multiple_of(i * (U * R), U * R) + u * R
                        tm_scr[pl.ds(r0, R), :] = zb + w0 * (za - zb)
                    return carry

                lax.fori_loop(0, mv // U, body, 0)
                _combine_epilogue(tm_scr, x_ref, mod_ref, fin_ref, o_ref, 0, mv, final_norm)

            @pl.when(do_scat)
            def _():
                def body(i, carry):
                    i0 = scat_idx0 + U * i
                    for u in range(U):
                        p0 = sc0_ref[0, 0, i0 + u]
                        p1 = sc1_ref[0, 0, i0 + u]
                        r0 = pl.multiple_of(i * (U * R), U * R) + u * R
                        val = h2_ref[pl.ds(r0, R), :]
                        mov_buf[pl.ds(pl.multiple_of(p0, R), R), :] = val
                        mov_buf[pl.ds(pl.multiple_of(p1, R), R), :] = val
                    return carry

                lax.fori_loop(0, mv // U, body, 0)

        def tile_body(j, carry):
            done = start + j * tm
            tile(off + done, n - done, tm, [])
            return carry

        lax.fori_loop(0, rest_big, tile_body, 0)

        @pl.when(rest_tail > 0)
        def _():
            done = start + rest_big * tm
            tile(off + done, n - done, small, [])

    @pl.when(r % 2 == 0)
    def _():
        round_body(slots_b, slots_a)

    @pl.when(r % 2 == 1)
    def _():
        round_body(slots_a, slots_b)


def _moe_call(off, cnt, pos0, pos1, w0, h2_tm, x, mod, fin_g, w_gate, w_up, w_down, layer,
              final_norm):
    bsz, seq, d = x.shape
    mv = MOE_MOVE
    half = N_EXPERTS // 2
    assert seq == half * mv
    n_slots = 2 * seq
    last = bsz - 1

    def scat_blk(r, e, *_):
        return (jnp.minimum(r, last) * half + jnp.clip(e - half, 0, half - 1), 0)

    def comb_blk(r, e, *_):
        return (jnp.where(r < 2, 0, (r - 2) * half + jnp.minimum(e, half - 1)), 0)

    def scat_row(r, e, *_):
        return (jnp.minimum(r, last), 0, 0)

    def comb_row(r, e, *_):
        return (jnp.clip(r - 2, 0, last), 0, 0)

    body = functools.partial(_moe_body, n_blocks=bsz, layer=layer, final_norm=final_norm)
    slot_rows = (n_slots + N_EXPERTS * (SLOT_ALIGN - 1) + MOE_TM) * ROWS_PER_TOKEN
    out = pl.pallas_call(
        body,
        grid_spec=pltpu.PrefetchScalarGridSpec(
            num_scalar_prefetch=2,
            grid=(bsz + 2, N_EXPERTS),
            in_specs=[
                pl.BlockSpec((1, 1, seq), scat_row, memory_space=pltpu.SMEM),
                pl.BlockSpec((1, 1, seq), scat_row, memory_space=pltpu.SMEM),
                pl.BlockSpec((1, 1, seq), comb_row, memory_space=pltpu.SMEM),
                pl.BlockSpec((1, 1, seq), comb_row, memory_space=pltpu.SMEM),
                pl.BlockSpec((1, 1, seq), comb_row, memory_space=pltpu.SMEM),
                pl.BlockSpec((mv * ROWS_PER_TOKEN, LANES), scat_blk),
                pl.BlockSpec((mv, d), comb_blk),
                pl.BlockSpec((1, 6, d), comb_row),
                pl.BlockSpec((1, d), lambda r, e, *_: (0, 0)),
                pl.BlockSpec(memory_space=pl.ANY),
                pl.BlockSpec(memory_space=pl.ANY),
                pl.BlockSpec(memory_space=pl.ANY),
            ],
            out_specs=pl.BlockSpec((mv, d), comb_blk),
            scratch_shapes=[
                pltpu.VMEM((slot_rows, LANES), jnp.float32),
                pltpu.VMEM((slot_rows, LANES), jnp.float32),
                pltpu.VMEM((mv * ROWS_PER_TOKEN, LANES), jnp.float32),
                pltpu.VMEM((MOE_WRING, d, D_EXPERT), jnp.bfloat16),
                pltpu.VMEM((MOE_WRING, d, D_EXPERT), jnp.bfloat16),
                pltpu.VMEM((MOE_WRING, D_EXPERT, d), jnp.bfloat16),
                pltpu.SemaphoreType.DMA((3, MOE_WRING)),
            ],
        ),
        out_shape=jax.ShapeDtypeStruct((bsz * seq, d), jnp.float32),
        compiler_params=pltpu.CompilerParams(
            dimension_semantics=("arbitrary", "arbitrary"),
            vmem_limit_bytes=MOE_VMEM_LIMIT),
        name="moe",
    )(off, cnt, pos0, pos1, pos0, pos1, w0, h2_tm, x.reshape(bsz * seq, d), mod, fin_g,
      w_gate, w_up, w_down)
    return out.reshape(bsz, seq, d)


def _moe_layer(x, h2_tm, logits, rbias, mod, fin_g, w_gate, w_up, w_down, layer, final_norm):
    pos0, pos1, w0, off, cnt = _route_call(logits, rbias)
    return _moe_call(off[:, :, 0], cnt[:, :, 0], pos0, pos1, w0, h2_tm, x, mod, fin_g,
                     w_gate, w_up, w_down, layer, final_norm)


def kernel(x, c, gla_w_in, gla_w_gate_up, gla_b_gate, gla_norm_g, gla_w_out, lru_w_in, lru_conv_w, lru_conv_b, lru_w_r, lru_b_r, lru_w_i, lru_b_i, lru_lambda, lru_w_out, router_w, router_bias, moe_w_gate, moe_w_up, moe_w_down, norm_mix_g, norm_ffn_g, ada_w, ada_b, final_norm_g):
    bf = jnp.bfloat16
    depth = ada_w.shape[0]
    bsz = x.shape[0]
    d = D_MODEL
    mod_all = _ada_call(c, ada_w, ada_b).reshape(depth, bsz, 6, d)

    rw_t = router_w.T
    rw_hi = rw_t.astype(bf)
    rw_lo = (rw_t - rw_hi.astype(jnp.float32)).astype(bf)
    rw_cat = jnp.concatenate([rw_hi, rw_lo], axis=0)
    rbias = router_bias.reshape(N_EXPERTS, 1)
    fin_g = final_norm_g.reshape(1, d)
    expert_w = None

    for i in range(depth):
        j = i // 2
        mod = mod_all[i]
        gmix = norm_mix_g[i].reshape(1, d)
        gffn = norm_ffn_g[i].reshape(1, d)
        if i % 2 == 0:
            w_in = jnp.pad(gla_w_in[j], ((0, 0), (0, GLA_RANK_PAD - GLA_GATE_RANK))).astype(bf)
            w_gu = jnp.pad(gla_w_gate_up[j], ((0, GLA_RANK_PAD - GLA_GATE_RANK), (0, 0))).astype(bf)
            cast_src = () if expert_w is not None else tuple(
                w.reshape(-1, w.shape[-1]) for w in (moe_w_gate, moe_w_up, moe_w_down))
            (x, h2_tm, logits), cast = _gla_layer_call(
                x, mod, gmix, gffn, w_in, w_gu, gla_b_gate[j].reshape(1, GLA_QK),
                gla_norm_g[j].reshape(1, GLA_DV), gla_w_out[j].astype(bf),
                rw_cat, rw_hi, cast_src)
            if cast:
                expert_w = (cast[0].reshape(depth * N_EXPERTS, d, D_EXPERT),
                            cast[1].reshape(depth * N_EXPERTS, d, D_EXPERT),
                            cast[2].reshape(depth * N_EXPERTS, D_EXPERT, d))
        else:
            x, h2_tm, logits = _lru_layer_call(
                x, mod, gmix, gffn, lru_w_in[j].astype(bf), lru_conv_w[j],
                lru_conv_b[j].reshape(1, d), lru_w_r[j].astype(bf),
                lru_b_r[j].reshape(1, d), lru_w_i[j].astype(bf), lru_b_i[j].reshape(1, d),
                lru_lambda[j].reshape(1, d), lru_w_out[j].astype(bf), rw_cat, rw_hi)
        x = _moe_layer(x, h2_tm, logits, rbias, mod, fin_g, *expert_w,
                       layer=i, final_norm=(i == depth - 1))
    return x
```

```python
import functools

import jax
import jax.numpy as jnp
from jax import lax
from jax.experimental import pallas as pl
from jax.experimental.pallas import tpu as pltpu

D_MODEL = 1024
CHUNK = 64
EPS = 1e-6

GLA_HEADS = 4
GLA_DK = 128
GLA_DV = 256
GLA_QK = GLA_HEADS * GLA_DK
GLA_VD = GLA_HEADS * GLA_DV
GLA_GATE_RANK = 16
GLA_GATE_TAU = 16.0
GLA_RANK_PAD = 128

LRU_BLOCKS = 4
LRU_BLOCK_W = D_MODEL // LRU_BLOCKS
CONV_W = 4
LRU_C = 8.0

N_EXPERTS = 16
N_GROUPS = 4
EPG = N_EXPERTS // N_GROUPS
D_EXPERT = 512

LANES = 128
SUBLANES = 8
ROWS_PER_TOKEN = D_MODEL // LANES

MIX_TS = 256
MIX_STREAMS = 2
LRU_STAGGER = 1
GLA_STAGGER = 1
ROUTE_BLOCK = 256
SLOT_ALIGN = 4
MOE_TM = 256
MOE_UNROLL = 8
MOE_MOVE = 256
MOE_WRING = 3
VMEM_LIMIT = 56 * 1024 * 1024
MOE_VMEM_LIMIT = 60 * 1024 * 1024


def _dot(a, b):
    return jnp.dot(a, b, preferred_element_type=jnp.float32)


def _dot_nt(a, b):
    return lax.dot_general(a, b, (((1,), (1,)), ((), ())),
                           preferred_element_type=jnp.float32)


def _dot_tn(a, b):
    return lax.dot_general(a, b, (((0,), (0,)), ((), ())),
                           preferred_element_type=jnp.float32)


def _split_bf16(x):
    hi = x.astype(jnp.bfloat16)
    lo = (x - hi.astype(jnp.float32)).astype(jnp.bfloat16)
    return hi, lo


def _sigmoid(x):
    return 0.5 * jnp.tanh(0.5 * x) + 0.5


def _sigmoid_rel(x):
    return 1.0 / (1.0 + jnp.exp(-x))


def _silu(x):
    return x * _sigmoid(x)


def _rms(x, g):
    inv = lax.rsqrt(jnp.mean(x * x, axis=-1, keepdims=True) + EPS)
    return x * inv * g


def _ada_body(c_ref, w_ref, b_ref, o_ref):
    c_hi, c_lo = _split_bf16(_silu(c_ref[...]))
    w_hi, w_lo = _split_bf16(w_ref[0])
    o_ref[0] = _dot(c_hi, w_hi) + (_dot(c_lo, w_hi) + _dot(c_hi, w_lo)) + b_ref[0]


def _ada_call(c, ada_w, ada_b):
    depth, d, n = ada_w.shape
    bsz = c.shape[0]
    tn = 1024
    return pl.pallas_call(
        _ada_body,
        grid=(depth, n // tn),
        in_specs=[
            pl.BlockSpec((bsz, d), lambda l, j: (0, 0)),
            pl.BlockSpec((1, d, tn), lambda l, j: (l, 0, j)),
            pl.BlockSpec((1, 1, tn), lambda l, j: (l, 0, j)),
        ],
        out_specs=pl.BlockSpec((1, bsz, tn), lambda l, j: (l, 0, j)),
        out_shape=jax.ShapeDtypeStruct((depth, bsz, n), jnp.float32),
        compiler_params=pltpu.CompilerParams(
            dimension_semantics=("arbitrary", "arbitrary"),
            vmem_limit_bytes=VMEM_LIMIT),
        name="ada",
    )(c, ada_w, ada_b.reshape(depth, 1, n))


def _pre_norm(x, mod_ref, g_ref, shift_row, scale_row):
    shift = mod_ref[0, shift_row:shift_row + 1, :]
    scale = mod_ref[0, scale_row:scale_row + 1, :]
    inv = lax.rsqrt(jnp.mean(x * x, axis=-1, keepdims=True) + EPS)
    return (x * inv) * (g_ref[...] * (1.0 + scale)) + shift


def _group_partner(x, k, sub):
    n = x.shape[0]
    fwd = pltpu.roll(x, n - k, 0)
    back = pltpu.roll(x, EPG - k, 0)
    wrapped = (sub % EPG) + k >= EPG
    return jnp.where(wrapped, back, fwd), wrapped


def _router_logits(h2, rw_cat_ref, rw_hi_ref):
    h_hi, h_lo = _split_bf16(h2)
    p1 = _dot_nt(rw_cat_ref[...], h_hi)
    p2 = _dot_nt(rw_hi_ref[...], h_lo)
    return p1[0:N_EXPERTS] + p1[N_EXPERTS:2 * N_EXPERTS] + p2


def _route_select(s, rbias):
    n = s.shape[1]
    sel = s + rbias
    sub = lax.broadcasted_iota(jnp.int32, (N_EXPERTS, n), 0)

    pair_best = None
    rank = jnp.zeros((N_EXPERTS, n), jnp.float32)
    for k in range(1, EPG):
        p, wrapped = _group_partner(sel, k, sub)
        ps = sel + p
        pair_best = ps if pair_best is None else jnp.maximum(pair_best, ps)
        ahead = (p > sel) | ((p == sel) & wrapped)
        rank = rank + ahead.astype(jnp.float32)
    gscore = pair_best
    for k in range(1, EPG):
        p, _ = _group_partner(pair_best, k, sub)
        gscore = jnp.maximum(gscore, p)
    chosen = jnp.ones((N_EXPERTS, n), jnp.bool_)
    grp = sub // EPG
    for j in range(1, N_GROUPS):
        other = pltpu.roll(gscore, N_EXPERTS - EPG * j, 0)
        other_is_later = grp + j < N_GROUPS
        chosen = chosen & ((gscore > other) | ((gscore == other) & other_is_later))
    m = chosen & (rank < 2.0)
    sm = s * m.astype(jnp.float32)
    wgt = sm / jnp.sum(sm, axis=0, keepdims=True)
    return m, wgt, sub


def _post_mix_pieces(x, mix, mod_ref, gffn_ref, rw_cat_ref, rw_hi_ref, xo_ref, h2_ref, lg_ref):
    ts = x.shape[0]
    x_new = x + mod_ref[0, 2:3, :] * mix
    xo_ref[0] = x_new
    h2 = _pre_norm(x_new, mod_ref, gffn_ref, 3, 4)
    yield
    for c in range(ROWS_PER_TOKEN):
        h2_ref[pl.ds(c, ts, stride=ROWS_PER_TOKEN), :] = h2[:, c * LANES:(c + 1) * LANES]
    yield
    lg_ref[0] = _router_logits(h2, rw_cat_ref, rw_hi_ref)
    yield


def _mixer_io_specs(bsz, seq, ts):
    ns = MIX_STREAMS
    hb = bsz // ns
    nt = seq // ts
    in_specs = [
        pl.BlockSpec((ns, 1, ts, D_MODEL), lambda b, t: (0, b, t, 0)),
        pl.BlockSpec((ns, 1, 6, D_MODEL), lambda b, t: (0, b, 0, 0)),
    ]
    out_specs = [
        pl.BlockSpec((ns, 1, ts, D_MODEL), lambda b, t: (0, b, t, 0)),
        pl.BlockSpec((ns, ts * ROWS_PER_TOKEN, LANES), lambda b, t: (0, b * nt + t, 0)),
        pl.BlockSpec((ns, 1, N_EXPERTS, ts), lambda b, t: (0, b, 0, t)),
    ]
    out_shapes = [
        jax.ShapeDtypeStruct((ns, hb, seq, D_MODEL), jnp.float32),
        jax.ShapeDtypeStruct((ns, hb * seq * ROWS_PER_TOKEN, LANES), jnp.float32),
        jax.ShapeDtypeStruct((ns, hb, N_EXPERTS, seq), jnp.float32),
    ]
    return in_specs, out_specs, out_shapes


def _mixer_unstream(outs, bsz, seq):
    x, h2_tm, logits = outs
    return (x.reshape(bsz, seq, D_MODEL), h2_tm.reshape(bsz * seq * ROWS_PER_TOKEN, LANES),
            logits.reshape(bsz, N_EXPERTS, seq))


def _const_spec(shape):
    nd = len(shape)
    return pl.BlockSpec(shape, lambda b, t: (0,) * nd)


def _gla_body(x_ref, mod_ref, gmix_ref, gffn_ref, win_ref, wgu_ref, bg_ref, ng_ref,
              wout_ref, rw_cat_ref, rw_hi_ref, *rest, n_cast):
    cast_in = rest[:n_cast]
    xo_ref, h2_ref, lg_ref = rest[n_cast:n_cast + 3]
    cast_out = rest[n_cast + 3:2 * n_cast + 3]
    state_scr, o_scr = rest[2 * n_cast + 3:]

    @pl.when(pl.program_id(1) == 0)
    def _():
        state_scr[...] = jnp.zeros_like(state_scr)

    def caster():
        pieces = 4
        for src, dst in zip(cast_in, cast_out):
            rows = src.shape[0] // pieces
            for i in range(pieces):
                dst[i * rows:(i + 1) * rows, :] = src[i * rows:(i + 1) * rows, :].astype(jnp.bfloat16)
                yield

    gens = [_gla_stream(x_ref.at[st], mod_ref.at[st], gmix_ref, gffn_ref, win_ref, wgu_ref,
                        bg_ref, ng_ref, wout_ref, rw_cat_ref, rw_hi_ref,
                        xo_ref.at[st], h2_ref.at[st], lg_ref.at[st],
                        state_scr.at[st], o_scr.at[st])
            for st in range(MIX_STREAMS)]
    _interleave(gens + [caster()], GLA_STAGGER)


def _gla_stream(x_ref, mod_ref, gmix_ref, gffn_ref, win_ref, wgu_ref, bg_ref, ng_ref,
                wout_ref, rw_cat_ref, rw_hi_ref,
                xo_ref, h2_ref, lg_ref, state_scr, o_scr):
    ts = x_ref.shape[1]
    nchunk = ts // CHUNK
    half = GLA_VD // 2
    x = x_ref[0]
    h = _pre_norm(x, mod_ref, gmix_ref, 0, 1).astype(jnp.bfloat16)
    yield

    o_q, o_k, o_v, o_g, o_a = 0, GLA_QK, 2 * GLA_QK, 2 * GLA_QK + GLA_VD, 2 * GLA_QK + 2 * GLA_VD
    q = (_dot(h, win_ref[:, o_q:o_k]) * (GLA_DK ** -0.5)).astype(jnp.bfloat16)
    yield
    k = _dot(h, win_ref[:, o_k:o_v])
    yield
    v = jnp.concatenate(
        [_dot(h, win_ref[:, o_v + i * half:o_v + (i + 1) * half]).astype(jnp.bfloat16)
         for i in range(2)], axis=1)
    yield
    a_lr = _dot(h, win_ref[:, o_a:o_a + GLA_RANK_PAD])
    a_hi, a_lo = _split_bf16(a_lr)
    z2 = _dot(jnp.concatenate([a_hi, a_lo], axis=0), wgu_ref[...])
    z = z2[0:ts] + z2[ts:2 * ts] + bg_ref[...]
    log_a = -(jnp.maximum(-z, 0.0) + jnp.log1p(jnp.exp(-jnp.abs(z)))) * (1.0 / GLA_GATE_TAU)
    yield

    row = lax.broadcasted_iota(jnp.int32, (ts, ts), 0)
    col = lax.broadcasted_iota(jnp.int32, (ts, ts), 1)
    tri = ((row // CHUNK == col // CHUNK) & (col <= row)).astype(jnp.bfloat16)
    l_hi, l_lo = _split_bf16(log_a)
    cum2 = _dot(tri, jnp.concatenate([l_hi, l_lo], axis=1))
    cum = cum2[:, 0:GLA_QK] + cum2[:, GLA_QK:2 * GLA_QK]
    yield

    ng = ng_ref[...]
    kv_t, gammas = [], []
    for j in range(nchunk):
        r0 = j * CHUNK
        cum_j = cum[r0:r0 + CHUNK]
        total = cum_j[CHUNK - 1:CHUNK]
        k_dec = (k[r0:r0 + CHUNK] * jnp.exp(total - cum_j)).astype(jnp.bfloat16)
        gammas.append(jnp.exp(total))
        for hd in range(GLA_HEADS):
            ks = slice(hd * GLA_DK, (hd + 1) * GLA_DK)
            vs = slice(hd * GLA_DV, (hd + 1) * GLA_DV)
            kv_t.append(_dot_tn(v[r0:r0 + CHUNK, vs], k_dec[:, ks]))
        yield
    states = []
    for hd in range(GLA_HEADS):
        ks = slice(hd * GLA_DK, (hd + 1) * GLA_DK)
        s_cur = state_scr[hd]
        for j in range(nchunk):
            s_cur = s_cur * gammas[j][:, ks] + kv_t[j * GLA_HEADS + hd]
            states.append(s_cur.astype(jnp.bfloat16))
        state_scr[hd] = s_cur
        yield
    for hd in range(GLA_HEADS):
        ks = slice(hd * GLA_DK, (hd + 1) * GLA_DK)
        vs = slice(hd * GLA_DV, (hd + 1) * GLA_DV)
        for j in range(nchunk):
            r0 = j * CHUNK
            o = _dot_nt(q[r0:r0 + CHUNK, ks], states[hd * nchunk + j])
            o_scr[r0:r0 + CHUNK, vs] = _rms(o, ng)
        yield

    og = []
    for i in range(2):
        g = _dot(h, win_ref[:, o_g + i * half:o_g + (i + 1) * half])
        og.append((o_scr[:, i * half:(i + 1) * half] * _silu(g)).astype(jnp.bfloat16))
        yield
    og = jnp.concatenate(og, axis=1)
    x_new = []
    for i in range(2):
        cols = slice(i * half, (i + 1) * half)
        mix = _dot(og, wout_ref[:, i * half:(i + 1) * half])
        x_new.append(x[:, cols] + mod_ref[0, 2:3, cols] * mix)
        yield
    x_new = jnp.concatenate(x_new, axis=1)
    xo_ref[0] = x_new
    h2 = _pre_norm(x_new, mod_ref, gffn_ref, 3, 4)
    for c in range(ROWS_PER_TOKEN):
        h2_ref[pl.ds(c, ts, stride=ROWS_PER_TOKEN), :] = h2[:, c * LANES:(c + 1) * LANES]
    yield
    lg_ref[0] = _router_logits(h2, rw_cat_ref, rw_hi_ref)
    yield


def _gla_layer_call(x, mod, gmix, gffn, w_in, w_gu, b_g, n_g, w_out, rw_cat, rw_hi, cast_src):
    bsz, seq, d = x.shape
    ts = MIX_TS
    ns = MIX_STREAMS
    hb = bsz // ns
    nt = seq // ts
    io_in, out_specs, out_shapes = _mixer_io_specs(bsz, seq, ts)
    consts = (gmix, gffn, w_in, w_gu, b_g, n_g, w_out, rw_cat, rw_hi)
    cast_specs, cast_shapes = [], []
    for a in cast_src:
        rows = a.shape[0] // (hb * nt)
        assert rows * hb * nt == a.shape[0]
        cast_specs.append(pl.BlockSpec((rows, a.shape[1]), lambda b, t: (b * nt + t, 0)))
        cast_shapes.append(jax.ShapeDtypeStruct(a.shape, jnp.bfloat16))
    outs = pl.pallas_call(
        functools.partial(_gla_body, n_cast=len(cast_src)),
        grid=(hb, nt),
        in_specs=io_in + [_const_spec(a.shape) for a in consts] + cast_specs,
        out_specs=out_specs + cast_specs,
        out_shape=out_shapes + cast_shapes,
        scratch_shapes=[
            pltpu.VMEM((ns, GLA_HEADS, GLA_DV, GLA_DK), jnp.float32),
            pltpu.VMEM((ns, ts, GLA_VD), jnp.float32),
        ],
        compiler_params=pltpu.CompilerParams(
            dimension_semantics=("arbitrary", "arbitrary"),
            vmem_limit_bytes=VMEM_LIMIT),
        name="gla_layer",
    )(x.reshape(ns, hb, seq, d), mod.reshape(ns, hb, 6, d), *consts, *cast_src)
    return _mixer_unstream(outs[:3], bsz, seq), outs[3:]


def _lru_body(x_ref, mod_ref, gmix_ref, gffn_ref, win_ref, cw_ref, cb_ref, wr_ref, br_ref,
              wi_ref, bi_ref, lam_ref, wout_ref, rw_cat_ref, rw_hi_ref,
              xo_ref, h2_ref, lg_ref,
              conv_scr, hstate_scr, a_scr, b_scr, gelu_scr):
    ns = MIX_STREAMS
    ts = x_ref.shape[2]
    w = D_MODEL

    @pl.when(pl.program_id(1) == 0)
    def _():
        conv_scr[...] = jnp.zeros_like(conv_scr)
        hstate_scr[...] = jnp.zeros_like(hstate_scr)

    _interleave([_lru_pre_scan(x_ref.at[st], mod_ref.at[st], gmix_ref, win_ref, cw_ref, cb_ref,
                               wr_ref, br_ref, wi_ref, bi_ref, lam_ref, conv_scr.at[st],
                               a_scr.at[st], b_scr.at[st], gelu_scr.at[st])
                 for st in range(ns)], LRU_STAGGER)

    unroll = 4

    def group_step(gi, hprev):
        for u in range(unroll):
            r0 = pl.multiple_of(gi * (unroll * SUBLANES), unroll * SUBLANES) + u * SUBLANES
            nxt = []
            for st in range(ns):
                hs = (a_scr[st, pl.ds(r0, SUBLANES), :] * hprev[st]
                      + b_scr[st, pl.ds(r0, SUBLANES), :])
                b_scr[st, pl.ds(r0, SUBLANES), :] = hs
                nxt.append(hs[SUBLANES - 1:SUBLANES, :])
            hprev = tuple(nxt)
        return hprev

    hlast = tuple(hstate_scr[st] for st in range(ns))
    for gi in range(ts // (unroll * SUBLANES)):
        hlast = group_step(gi, hlast)
    for st in range(ns):
        hstate_scr[st] = hlast[st]

    def post(st):
        y = (b_scr[st] * gelu_scr[st]).astype(jnp.bfloat16)
        yield
        mix = _dot(y, wout_ref[...])
        yield
        yield from _post_mix_pieces(x_ref[st, 0], mix, mod_ref.at[st], gffn_ref, rw_cat_ref,
                                    rw_hi_ref, xo_ref.at[st], h2_ref.at[st], lg_ref.at[st])

    _interleave([post(st) for st in range(ns)], LRU_STAGGER)


def _lru_pre_scan(x_ref, mod_ref, gmix_ref, win_ref, cw_ref, cb_ref, wr_ref, br_ref,
                  wi_ref, bi_ref, lam_ref, conv_scr, a_scr, b_scr, gelu_scr):
    ts = x_ref.shape[1]
    w = D_MODEL
    x = x_ref[0]
    h = _pre_norm(x, mod_ref, gmix_ref, 0, 1).astype(jnp.bfloat16)
    gate_br = _dot(h, win_ref[:, 0:w])
    c0 = 0.7978845608028654
    u = gate_br * (c0 + (c0 * 0.044715) * (gate_br * gate_br))
    half_g = 0.5 * gate_br
    gelu_scr[...] = half_g + half_g * jnp.tanh(u)
    yield
    xb = _dot(h, win_ref[:, w:2 * w])

    ext = jnp.concatenate([conv_scr[...], xb], axis=0)
    xc = cb_ref[...] + xb * cw_ref[CONV_W - 1:CONV_W, :]
    for j in range(CONV_W - 1):
        back = CONV_W - 1 - j
        xc = xc + pltpu.roll(ext, back, 0)[SUBLANES:SUBLANES + ts] * cw_ref[j:j + 1, :]
    conv_scr[...] = xb[ts - SUBLANES:ts]
    yield

    xcb = xc.astype(jnp.bfloat16)
    rs, iis = [], []
    for hd in range(LRU_BLOCKS):
        sl = slice(hd * LRU_BLOCK_W, (hd + 1) * LRU_BLOCK_W)
        rs.append(_dot(xcb[:, sl], wr_ref[hd]))
        iis.append(_dot(xcb[:, sl], wi_ref[hd]))
    t_r = jnp.tanh(0.5 * (jnp.concatenate(rs, axis=1) + br_ref[...]))
    t_i = jnp.tanh(0.5 * (jnp.concatenate(iis, axis=1) + bi_ref[...]))
    yield

    lam = lam_ref[...]
    softplus_neg_lam = jnp.maximum(-lam, 0.0) + jnp.log1p(jnp.exp(-jnp.abs(lam)))
    sp = (-0.5 * LRU_C) * softplus_neg_lam
    log_a = sp * t_r + sp
    a = jnp.exp(log_a)
    half_mult = jnp.sqrt(jnp.tanh(-log_a) * (0.25 * (a * a) + 0.25))
    bb = (xc * half_mult) * (t_i + 1.0)
    yield

    a = a.reshape(ts // SUBLANES, SUBLANES, w)
    bb = bb.reshape(ts // SUBLANES, SUBLANES, w)
    rowi = lax.broadcasted_iota(jnp.int32, a.shape, 1)
    for d in (1, 2, 4):
        keep = rowi >= d
        a_sh = jnp.where(keep, pltpu.roll(a, d, 1), 1.0)
        b_sh = jnp.where(keep, pltpu.roll(bb, d, 1), 0.0)
        bb = a * b_sh + bb
        a = a * a_sh
    a_scr[...] = a.reshape(ts, w)
    b_scr[...] = bb.reshape(ts, w)
    yield


def _interleave(gens, lag):
    live = [True] * len(gens)
    tick = 0
    while any(live):
        for i, gen in enumerate(gens):
            if live[i] and tick >= i * lag:
                try:
                    next(gen)
                except StopIteration:
                    live[i] = False
        tick += 1


def _lru_layer_call(x, mod, gmix, gffn, w_in, conv_w, conv_b, w_r, b_r, w_i, b_i, lam, w_out,
                    rw_cat, rw_hi):
    bsz, seq, d = x.shape
    ts = MIX_TS
    ns = MIX_STREAMS
    hb = bsz // ns
    io_in, out_specs, out_shapes = _mixer_io_specs(bsz, seq, ts)
    consts = (gmix, gffn, w_in, conv_w, conv_b, w_r, b_r, w_i, b_i, lam, w_out,
              rw_cat, rw_hi)
    outs = pl.pallas_call(
        _lru_body,
        grid=(hb, seq // ts),
        in_specs=io_in + [_const_spec(a.shape) for a in consts],
        out_specs=out_specs,
        out_shape=out_shapes,
        scratch_shapes=[
            pltpu.VMEM((ns, SUBLANES, d), jnp.float32),
            pltpu.VMEM((ns, 1, d), jnp.float32),
            pltpu.VMEM((ns, ts, d), jnp.float32),
            pltpu.VMEM((ns, ts, d), jnp.float32),
            pltpu.VMEM((ns, ts, d), jnp.float32),
        ],
        compiler_params=pltpu.CompilerParams(
            dimension_semantics=("arbitrary", "arbitrary"),
            vmem_limit_bytes=VMEM_LIMIT),
        name="lru_layer",
    )(x.reshape(ns, hb, seq, d), mod.reshape(ns, hb, 6, d), *consts)
    return _mixer_unstream(outs, bsz, seq)


def _route_body(lg_ref, rbias_ref, pos0_ref, pos1_ref, w0_ref, off_ref, cnt_ref):
    seq = lg_ref.shape[2]
    blk = ROUTE_BLOCK
    s = _sigmoid_rel(lg_ref[0])
    m, wgt, sub = _route_select(s, rbias_ref[...])
    mf = m.astype(jnp.float32)

    row = lax.broadcasted_iota(jnp.int32, (blk, blk), 0)
    col = lax.broadcasted_iota(jnp.int32, (blk, blk), 1)
    before = (row < col).astype(jnp.bfloat16)
    carry = jnp.zeros((N_EXPERTS, 1), jnp.float32)
    ranks = []
    for i in range(seq // blk):
        mb = mf[:, i * blk:(i + 1) * blk]
        ranks.append(_dot(mb.astype(jnp.bfloat16), before) + carry)
        carry = carry + jnp.sum(mb, axis=1, keepdims=True)
    rnk = jnp.concatenate(ranks, axis=1)
    cnt_ref[0] = jnp.broadcast_to(carry, (N_EXPERTS, LANES)).astype(jnp.int32)

    eidx = sub.astype(jnp.float32)
    e_lo = jnp.min(jnp.where(m, eidx, 99.0), axis=0, keepdims=True)
    e_hi = jnp.max(jnp.where(m, eidx, -1.0), axis=0, keepdims=True)
    is_lo = m & (eidx == e_lo)
    is_hi = m & (eidx == e_hi)
    p_lo = jnp.sum(jnp.where(is_lo, rnk, 0.0), axis=0, keepdims=True)
    p_hi = jnp.sum(jnp.where(is_hi, rnk, 0.0), axis=0, keepdims=True)
    w0_ref[0] = jnp.sum(jnp.where(is_lo, wgt, 0.0), axis=0, keepdims=True)

    off = jnp.zeros((1, 1), jnp.float32)
    for e in range(N_EXPERTS):
        off_ref[0, e:e + 1, :] = jnp.broadcast_to(off, (1, LANES)).astype(jnp.int32)
        p_lo = p_lo + jnp.where(e_lo == float(e), off, 0.0)
        p_hi = p_hi + jnp.where(e_hi == float(e), off, 0.0)
        off = off + jnp.floor((carry[e:e + 1, :] + (SLOT_ALIGN - 1.0)) * (1.0 / SLOT_ALIGN)) * SLOT_ALIGN
    pos0_ref[0] = p_lo.astype(jnp.int32) * ROWS_PER_TOKEN
    pos1_ref[0] = p_hi.astype(jnp.int32) * ROWS_PER_TOKEN


def _route_call(logits, rbias):
    bsz, _, seq = logits.shape
    row_spec = pl.BlockSpec((1, 1, seq), lambda b: (b, 0, 0))
    tab_spec = pl.BlockSpec((1, N_EXPERTS, LANES), lambda b: (b, 0, 0))
    return pl.pallas_call(
        _route_body,
        grid=(bsz,),
        in_specs=[
            pl.BlockSpec((1, N_EXPERTS, seq), lambda b: (b, 0, 0)),
            pl.BlockSpec((N_EXPERTS, 1), lambda b: (0, 0)),
        ],
        out_specs=[row_spec, row_spec, row_spec, tab_spec, tab_spec],
        out_shape=[
            jax.ShapeDtypeStruct((bsz, 1, seq), jnp.int32),
            jax.ShapeDtypeStruct((bsz, 1, seq), jnp.int32),
            jax.ShapeDtypeStruct((bsz, 1, seq), jnp.float32),
            jax.ShapeDtypeStruct((bsz, N_EXPERTS, LANES), jnp.int32),
            jax.ShapeDtypeStruct((bsz, N_EXPERTS, LANES), jnp.int32),
        ],
        compiler_params=pltpu.CompilerParams(dimension_semantics=("arbitrary",)),
        name="moe_route",
    )(logits, rbias)


def _scatter_tokens(pos0_ref, pos1_ref, h2_ref, slots, idx0, lo, cnt):
    R = ROWS_PER_TOKEN
    for u in range(lo, lo + cnt):
        p0 = pos0_ref[0, 0, idx0 + u]
        p1 = pos1_ref[0, 0, idx0 + u]
        val = h2_ref[u * R:(u + 1) * R, :]
        slots[pl.ds(pl.multiple_of(p0, R), R), :] = val
        slots[pl.ds(pl.multiple_of(p1, R), R), :] = val


def _combine_tokens(pos0_ref, pos1_ref, wts_ref, slots, tm_scr, idx0, lo, cnt):
    R = ROWS_PER_TOKEN
    for u in range(lo, lo + cnt):
        p0 = pos0_ref[0, 0, idx0 + u]
        p1 = pos1_ref[0, 0, idx0 + u]
        w0 = wts_ref[0, 0, idx0 + u]
        za = slots[pl.ds(pl.multiple_of(p0, R), R), :]
        zb = slots[pl.ds(pl.multiple_of(p1, R), R), :]
        tm_scr[u * R:(u + 1) * R, :] = zb + w0 * (za - zb)


def _combine_epilogue(tm_scr, x_ref, mod_ref, fin_ref, o_ref, lo, cnt, final_norm):
    R = ROWS_PER_TOKEN
    moe = jnp.concatenate(
        [tm_scr[pl.ds(lo * R + c, cnt, stride=R), :] for c in range(R)], axis=1)
    out = x_ref[lo:lo + cnt, :] + mod_ref[0, 5:6, :] * moe
    if final_norm:
        out = _rms(out, fin_ref[...])
    o_ref[lo:lo + cnt, :] = out


def _expert_tile_hooked(slots, wg_ref, wu_ref, wd_ref, ws, slot0, n_valid, tm, hooks):
    R = ROWS_PER_TOKEN
    nw = 256
    row0 = pl.multiple_of(slot0 * R, SLOT_ALIGN * R)
    xs = [slots[pl.ds(row0 + c, tm, stride=R), :] for c in range(R)]
    xt = jnp.concatenate(xs, axis=1).astype(jnp.bfloat16)
    ok = lax.broadcasted_iota(jnp.int32, (tm, LANES), 0) < n_valid
    hooks = list(hooks)

    def run_hook():
        if hooks:
            hooks.pop(0)()

    gate, up = [], []
    for i in range(D_EXPERT // nw):
        gate.append(_dot(xt, wg_ref[ws, :, i * nw:(i + 1) * nw]))
        run_hook()
    for i in range(D_EXPERT // nw):
        up.append(_dot(xt, wu_ref[ws, :, i * nw:(i + 1) * nw]))
        run_hook()
    he = (_silu(jnp.concatenate(gate, axis=1)) * jnp.concatenate(up, axis=1)).astype(jnp.bfloat16)
    for i in range(D_MODEL // nw):
        y = _dot(he, wd_ref[ws, :, i * nw:(i + 1) * nw])
        for cc in range(nw // LANES):
            c = i * (nw // LANES) + cc
            slots[pl.ds(row0 + c, tm, stride=R), :] = jnp.where(
                ok, y[:, cc * LANES:(cc + 1) * LANES], xs[c])
        run_hook()
    while hooks:
        run_hook()


def _moe_body(off_sm, cnt_sm, sc0_ref, sc1_ref, cb0_ref, cb1_ref, wts_ref, h2_ref, x_ref,
              mod_ref, fin_ref, wg_hbm, wu_hbm, wd_hbm, o_ref, slots_a, slots_b, tm_scr,
              wg_ref, wu_ref, wd_ref, w_sem, *, n_blocks, layer, final_norm):
    r = pl.program_id(0)
    e = pl.program_id(1)

    g = r * N_EXPERTS + e
    n_steps = (n_blocks + 2) * N_EXPERTS

    def w_needed(step):
        rnd = step // N_EXPERTS
        return (rnd >= 1) & (rnd <= n_blocks) & (step < n_steps)

    def w_copies(step):
        ex = layer * N_EXPERTS + step % N_EXPERTS
        sl = step % MOE_WRING
        return [pltpu.make_async_copy(src.at[ex], dst.at[sl], w_sem.at[i, sl])
                for i, (src, dst) in enumerate(((wg_hbm, wg_ref), (wu_hbm, wu_ref),
                                                (wd_hbm, wd_ref)))]

    def w_start(step):
        @pl.when(w_needed(step))
        def _():
            for cp in w_copies(step):
                cp.start()

    @pl.when(g == 0)
    def _():
        for ahead in range(MOE_WRING - 1):
            w_start(g + ahead)

    w_start(g + MOE_WRING - 1)

    @pl.when(w_needed(g))
    def _():
        for cp in w_copies(g):
            cp.wait()

    ws = g % MOE_WRING
    R = ROWS_PER_TOKEN
    U = MOE_UNROLL
    tm = MOE_TM
    mv = MOE_MOVE
    half = N_EXPERTS // 2
    n_hooks = 2 * (D_EXPERT // 256) + D_MODEL // 256
    per_hook = mv // n_hooks

    @pl.when((r == 0) & (e == 0))
    def _():
        slots_a[...] = jnp.zeros_like(slots_a)
        slots_b[...] = jnp.zeros_like(slots_b)

    do_exp = (r >= 1) & (r <= n_blocks)
    do_comb = (r >= 2) & (e < half)
    do_scat = (r < n_blocks) & (e >= half)
    b_exp = jnp.clip(r - 1, 0, n_blocks - 1)
    off = off_sm[b_exp, e]
    n = jnp.where(do_exp, cnt_sm[b_exp, e], 0)
    comb_idx0 = e * mv
    scat_idx0 = (e - half) * mv
    small = tm // 2
    first_rows = jnp.where(n > small, tm, jnp.where(n > 0, small, 0))
    fused = do_comb | do_scat
    start = jnp.where(fused, first_rows, 0)
    rest = jnp.maximum(n - start, 0)
    rest_big = rest // tm + ((rest % tm) > small).astype(jnp.int32)
    rest_tail = rest - rest_big * tm

    def round_body(exp_buf, mov_buf):
        tile = functools.partial(_expert_tile_hooked, exp_buf, wg_ref, wu_ref, wd_ref, ws)

        def comb_hooks():
            def hook(k):
                def run():
                    if k < n_hooks:
                        _combine_tokens(cb0_ref, cb1_ref, wts_ref, mov_buf, tm_scr, comb_idx0,
                                        k * per_hook, per_hook)
                    if k >= 1:
                        _combine_epilogue(tm_scr, x_ref, mod_ref, fin_ref, o_ref,
                                          (k - 1) * per_hook, per_hook, final_norm)
                return run
            return [hook(k) for k in range(n_hooks + 1)]

        def scat_hooks():
            def hook(k):
                return lambda: _scatter_tokens(sc0_ref, sc1_ref, h2_ref, mov_buf, scat_idx0,
                                               k * per_hook, per_hook)
            return [hook(k) for k in range(n_hooks)]

        for rows in (tm, small):
            @pl.when((first_rows == rows) & do_comb)
            def _():
                tile(off, n, rows, comb_hooks())

            @pl.when((first_rows == rows) & do_scat)
            def _():
                tile(off, n, rows, scat_hooks())

        @pl.when(first_rows == 0)
        def _():
            @pl.when(do_comb)
            def _():
                def body(i, carry):
                    i0 = comb_idx0 + U * i
                    for u in range(U):
                        p0 = cb0_ref[0, 0, i0 + u]
                        p1 = cb1_ref[0, 0, i0 + u]
                        w0 = wts_ref[0, 0, i0 + u]
                        za = mov_buf[pl.ds(pl.multiple_of(p0, R), R), :]
                        zb = mov_buf[pl.ds(pl.multiple_of(p1, R), R), :]
                        r0 = pl.multiple_of(i * (U * R), U * R) + u * R
                        tm_scr[pl.ds(r0, R), :] = zb + w0 * (za - zb)
                    return carry

                lax.fori_loop(0, mv // U, body, 0)
                _combine_epilogue(tm_scr, x_ref, mod_ref, fin_ref, o_ref, 0, mv, final_norm)

            @pl.when(do_scat)
            def _():
                def body(i, carry):
                    i0 = scat_idx0 + U * i
                    for u in range(U):
                        p0 = sc0_ref[0, 0, i0 + u]
                        p1 = sc1_ref[0, 0, i0 + u]
                        r0 = pl.multiple_of(i * (U * R), U * R) + u * R
                        val = h2_ref[pl.ds(r0, R), :]
                        mov_buf[pl.ds(pl.multiple_of(p0, R), R), :] = val
                        mov_buf[pl.ds(pl.multiple_of(p1, R), R), :] = val
                    return carry

                lax.fori_loop(0, mv // U, body, 0)

        def tile_body(j, carry):
            done = start + j * tm
            tile(off + done, n - done, tm, [])
            return carry

        lax.fori_loop(0, rest_big, tile_body, 0)

        @pl.when(rest_tail > 0)
        def _():
            done = start + rest_big * tm
            tile(off + done, n - done, small, [])

    @pl.when(r % 2 == 0)
    def _():
        round_body(slots_b, slots_a)

    @pl.when(r % 2 == 1)
    def _():
        round_body(slots_a, slots_b)


def _moe_call(off, cnt, pos0, pos1, w0, h2_tm, x, mod, fin_g, w_gate, w_up, w_down, layer,
              final_norm):
    bsz, seq, d = x.shape
    mv = MOE_MOVE
    half = N_EXPERTS // 2
    assert seq == half * mv
    n_slots = 2 * seq
    last = bsz - 1

    def scat_blk(r, e, *_):
        return (jnp.minimum(r, last) * half + jnp.clip(e - half, 0, half - 1), 0)

    def comb_blk(r, e, *_):
        return (jnp.where(r < 2, 0, (r - 2) * half + jnp.minimum(e, half - 1)), 0)

    def scat_row(r, e, *_):
        return (jnp.minimum(r, last), 0, 0)

    def comb_row(r, e, *_):
        return (jnp.clip(r - 2, 0, last), 0, 0)

    body = functools.partial(_moe_body, n_blocks=bsz, layer=layer, final_norm=final_norm)
    slot_rows = (n_slots + N_EXPERTS * (SLOT_ALIGN - 1) + MOE_TM) * ROWS_PER_TOKEN
    out = pl.pallas_call(
        body,
        grid_spec=pltpu.PrefetchScalarGridSpec(
            num_scalar_prefetch=2,
            grid=(bsz + 2, N_EXPERTS),
            in_specs=[
                pl.BlockSpec((1, 1, seq), scat_row, memory_space=pltpu.SMEM),
                pl.BlockSpec((1, 1, seq), scat_row, memory_space=pltpu.SMEM),
                pl.BlockSpec((1, 1, seq), comb_row, memory_space=pltpu.SMEM),
                pl.BlockSpec((1, 1, seq), comb_row, memory_space=pltpu.SMEM),
                pl.BlockSpec((1, 1, seq), comb_row, memory_space=pltpu.SMEM),
                pl.BlockSpec((mv * ROWS_PER_TOKEN, LANES), scat_blk),
                pl.BlockSpec((mv, d), comb_blk),
                pl.BlockSpec((1, 6, d), comb_row),
                pl.BlockSpec((1, d), lambda r, e, *_: (0, 0)),
                pl.BlockSpec(memory_space=pl.ANY),
                pl.BlockSpec(memory_space=pl.ANY),
                pl.BlockSpec(memory_space=pl.ANY),
            ],
            out_specs=pl.BlockSpec((mv, d), comb_blk),
            scratch_shapes=[
                pltpu.VMEM((slot_rows, LANES), jnp.float32),
                pltpu.VMEM((slot_rows, LANES), jnp.float32),
                pltpu.VMEM((mv * ROWS_PER_TOKEN, LANES), jnp.float32),
                pltpu.VMEM((MOE_WRING, d, D_EXPERT), jnp.bfloat16),
                pltpu.VMEM((MOE_WRING, d, D_EXPERT), jnp.bfloat16),
                pltpu.VMEM((MOE_WRING, D_EXPERT, d), jnp.bfloat16),
                pltpu.SemaphoreType.DMA((3, MOE_WRING)),
            ],
        ),
        out_shape=jax.ShapeDtypeStruct((bsz * seq, d), jnp.float32),
        compiler_params=pltpu.CompilerParams(
            dimension_semantics=("arbitrary", "arbitrary"),
            vmem_limit_bytes=MOE_VMEM_LIMIT),
        name="moe",
    )(off, cnt, pos0, pos1, pos0, pos1, w0, h2_tm, x.reshape(bsz * seq, d), mod, fin_g,
      w_gate, w_up, w_down)
    return out.reshape(bsz, seq, d)


def _moe_layer(x, h2_tm, logits, rbias, mod, fin_g, w_gate, w_up, w_down, layer, final_norm):
    pos0, pos1, w0, off, cnt = _route_call(logits, rbias)
    return _moe_call(off[:, :, 0], cnt[:, :, 0], pos0, pos1, w0, h2_tm, x, mod, fin_g,
                     w_gate, w_up, w_down, layer, final_norm)


def kernel(x, c, gla_w_in, gla_w_gate_up, gla_b_gate, gla_norm_g, gla_w_out, lru_w_in, lru_conv_w, lru_conv_b, lru_w_r, lru_b_r, lru_w_i, lru_b_i, lru_lambda, lru_w_out, router_w, router_bias, moe_w_gate, moe_w_up, moe_w_down, norm_mix_g, norm_ffn_g, ada_w, ada_b, final_norm_g):
    bf = jnp.bfloat16
    depth = ada_w.shape[0]
    bsz = x.shape[0]
    d = D_MODEL
    mod_all = _ada_call(c, ada_w, ada_b).reshape(depth, bsz, 6, d)

    rw_t = router_w.T
    rw_hi = rw_t.astype(bf)
    rw_lo = (rw_t - rw_hi.astype(jnp.float32)).astype(bf)
    rw_cat = jnp.concatenate([rw_hi, rw_lo], axis=0)
    rbias = router_bias.reshape(N_EXPERTS, 1)
    fin_g = final_norm_g.reshape(1, d)
    expert_w = None

    for i in range(depth):
        j = i // 2
        mod = mod_all[i]
        gmix = norm_mix_g[i].reshape(1, d)
        gffn = norm_ffn_g[i].reshape(1, d)
        if i % 2 == 0:
            w_in = jnp.pad(gla_w_in[j], ((0, 0), (0, GLA_RANK_PAD - GLA_GATE_RANK))).astype(bf)
            w_gu = jnp.pad(gla_w_gate_up[j], ((0, GLA_RANK_PAD - GLA_GATE_RANK), (0, 0))).astype(bf)
            cast_src = () if expert_w is not None else tuple(
                w.reshape(-1, w.shape[-1]) for w in (moe_w_gate, moe_w_up, moe_w_down))
            (x, h2_tm, logits), cast = _gla_layer_call(
                x, mod, gmix, gffn, w_in, w_gu, gla_b_gate[j].reshape(1, GLA_QK),
                gla_norm_g[j].reshape(1, GLA_DV), gla_w_out[j].astype(bf),
                rw_cat, rw_hi, cast_src)
            if cast:
                expert_w = (cast[0].reshape(depth * N_EXPERTS, d, D_EXPERT),
                            cast[1].reshape(depth * N_EXPERTS, d, D_EXPERT),
                            cast[2].reshape(depth * N_EXPERTS, D_EXPERT, d))
        else:
            x, h2_tm, logits = _lru_layer_call(
                x, mod, gmix, gffn, lru_w_in[j].astype(bf), lru_conv_w[j],
                lru_conv_b[j].reshape(1, d), lru_w_r[j].astype(bf),
                lru_b_r[j].reshape(1, d), lru_w_i[j].astype(bf), lru_b_i[j].reshape(1, d),
                lru_lambda[j].reshape(1, d), lru_w_out[j].astype(bf), rw_cat, rw_hi)
        x = _moe_layer(x, h2_tm, logits, rbias, mod, fin_g, *expert_w,
                       layer=i, final_norm=(i == depth - 1))
    return x
```

```python
import functools

import jax
import jax.numpy as jnp
from jax import lax
from jax.experimental import pallas as pl
from jax.experimental.pallas import tpu as pltpu

D_MODEL = 1024
CHUNK = 64
EPS = 1e-6

GLA_HEADS = 4
GLA_DK = 128
GLA_DV = 256
GLA_QK = GLA_HEADS * GLA_DK
GLA_VD = GLA_HEADS * GLA_DV
GLA_GATE_RANK = 16
GLA_GATE_TAU = 16.0
GLA_RANK_PAD = 128

LRU_BLOCKS = 4
LRU_BLOCK_W = D_MODEL // LRU_BLOCKS
CONV_W = 4
LRU_C = 8.0

N_EXPERTS = 16
N_GROUPS = 4
EPG = N_EXPERTS // N_GROUPS
D_EXPERT = 512

LANES = 128
SUBLANES = 8
ROWS_PER_TOKEN = D_MODEL // LANES

MIX_TS = 256
MIX_STREAMS = 2
LRU_STAGGER = 1
GLA_STAGGER = 1
ROUTE_BLOCK = 256
SLOT_ALIGN = 4
MOE_TM = 256
MOE_UNROLL = 8
MOE_MOVE = 256
MOE_WRING = 3
VMEM_LIMIT = 56 * 1024 * 1024
MOE_VMEM_LIMIT = 60 * 1024 * 1024


def _dot(a, b):
    return jnp.dot(a, b, preferred_element_type=jnp.float32)


def _dot_nt(a, b):
    return lax.dot_general(a, b, (((1,), (1,)), ((), ())),
                           preferred_element_type=jnp.float32)


def _dot_tn(a, b):
    return lax.dot_general(a, b, (((0,), (0,)), ((), ())),
                           preferred_element_type=jnp.float32)


def _split_bf16(x):
    hi = x.astype(jnp.bfloat16)
    lo = (x - hi.astype(jnp.float32)).astype(jnp.bfloat16)
    return hi, lo


def _sigmoid(x):
    return 0.5 * jnp.tanh(0.5 * x) + 0.5


def _sigmoid_rel(x):
    return 1.0 / (1.0 + jnp.exp(-x))


def _silu(x):
    return x * _sigmoid(x)


def _rms(x, g):
    inv = lax.rsqrt(jnp.mean(x * x, axis=-1, keepdims=True) + EPS)
    return x * inv * g


def _ada_body(c_ref, w_ref, b_ref, o_ref):
    c_hi, c_lo = _split_bf16(_silu(c_ref[...]))
    w_hi, w_lo = _split_bf16(w_ref[0])
    o_ref[0] = _dot(c_hi, w_hi) + (_dot(c_lo, w_hi) + _dot(c_hi, w_lo)) + b_ref[0]


def _ada_call(c, ada_w, ada_b):
    depth, d, n = ada_w.shape
    bsz = c.shape[0]
    tn = 1024
    return pl.pallas_call(
        _ada_body,
        grid=(depth, n // tn),
        in_specs=[
            pl.BlockSpec((bsz, d), lambda l, j: (0, 0)),
            pl.BlockSpec((1, d, tn), lambda l, j: (l, 0, j)),
            pl.BlockSpec((1, 1, tn), lambda l, j: (l, 0, j)),
        ],
        out_specs=pl.BlockSpec((1, bsz, tn), lambda l, j: (l, 0, j)),
        out_shape=jax.ShapeDtypeStruct((depth, bsz, n), jnp.float32),
        compiler_params=pltpu.CompilerParams(
            dimension_semantics=("arbitrary", "arbitrary"),
            vmem_limit_bytes=VMEM_LIMIT),
        name="ada",
    )(c, ada_w, ada_b.reshape(depth, 1, n))


def _pre_norm(x, mod_ref, g_ref, shift_row, scale_row):
    shift = mod_ref[0, shift_row:shift_row + 1, :]
    scale = mod_ref[0, scale_row:scale_row + 1, :]
    inv = lax.rsqrt(jnp.mean(x * x, axis=-1, keepdims=True) + EPS)
    return (x * inv) * (g_ref[...] * (1.0 + scale)) + shift


def _group_partner(x, k, sub):
    n = x.shape[0]
    fwd = pltpu.roll(x, n - k, 0)
    back = pltpu.roll(x, EPG - k, 0)
    wrapped = (sub % EPG) + k >= EPG
    return jnp.where(wrapped, back, fwd), wrapped


def _router_logits(h2, rw_cat_ref, rw_hi_ref):
    h_hi, h_lo = _split_bf16(h2)
    p1 = _dot_nt(rw_cat_ref[...], h_hi)
    p2 = _dot_nt(rw_hi_ref[...], h_lo)
    return p1[0:N_EXPERTS] + p1[N_EXPERTS:2 * N_EXPERTS] + p2


def _route_select(s, rbias):
    n = s.shape[1]
    sel = s + rbias
    sub = lax.broadcasted_iota(jnp.int32, (N_EXPERTS, n), 0)

    pair_best = None
    rank = jnp.zeros((N_EXPERTS, n), jnp.float32)
    for k in range(1, EPG):
        p, wrapped = _group_partner(sel, k, sub)
        ps = sel + p
        pair_best = ps if pair_best is None else jnp.maximum(pair_best, ps)
        ahead = (p > sel) | ((p == sel) & wrapped)
        rank = rank + ahead.astype(jnp.float32)
    gscore = pair_best
    for k in range(1, EPG):
        p, _ = _group_partner(pair_best, k, sub)
        gscore = jnp.maximum(gscore, p)
    chosen = jnp.ones((N_EXPERTS, n), jnp.bool_)
    grp = sub // EPG
    for j in range(1, N_GROUPS):
        other = pltpu.roll(gscore, N_EXPERTS - EPG * j, 0)
        other_is_later = grp + j < N_GROUPS
        chosen = chosen & ((gscore > other) | ((gscore == other) & other_is_later))
    m = chosen & (rank < 2.0)
    sm = s * m.astype(jnp.float32)
    wgt = sm / jnp.sum(sm, axis=0, keepdims=True)
    return m, wgt, sub


def _post_mix_pieces(x, mix, mod_ref, gffn_ref, rw_cat_ref, rw_hi_ref, xo_ref, h2_ref, lg_ref):
    ts = x.shape[0]
    x_new = x + mod_ref[0, 2:3, :] * mix
    xo_ref[0] = x_new
    h2 = _pre_norm(x_new, mod_ref, gffn_ref, 3, 4)
    yield
    for c in range(ROWS_PER_TOKEN):
        h2_ref[pl.ds(c, ts, stride=ROWS_PER_TOKEN), :] = h2[:, c * LANES:(c + 1) * LANES]
    yield
    lg_ref[0] = _router_logits(h2, rw_cat_ref, rw_hi_ref)
    yield


def _mixer_io_specs(bsz, seq, ts):
    ns = MIX_STREAMS
    hb = bsz // ns
    nt = seq // ts
    in_specs = [
        pl.BlockSpec((ns, 1, ts, D_MODEL), lambda b, t: (0, b, t, 0)),
        pl.BlockSpec((ns, 1, 6, D_MODEL), lambda b, t: (0, b, 0, 0)),
    ]
    out_specs = [
        pl.BlockSpec((ns, 1, ts, D_MODEL), lambda b, t: (0, b, t, 0)),
        pl.BlockSpec((ns, ts * ROWS_PER_TOKEN, LANES), lambda b, t: (0, b * nt + t, 0)),
        pl.BlockSpec((ns, 1, N_EXPERTS, ts), lambda b, t: (0, b, 0, t)),
    ]
    out_shapes = [
        jax.ShapeDtypeStruct((ns, hb, seq, D_MODEL), jnp.float32),
        jax.ShapeDtypeStruct((ns, hb * seq * ROWS_PER_TOKEN, LANES), jnp.float32),
        jax.ShapeDtypeStruct((ns, hb, N_EXPERTS, seq), jnp.float32),
    ]
    return in_specs, out_specs, out_shapes


def _mixer_unstream(outs, bsz, seq):
    x, h2_tm, logits = outs
    return (x.reshape(bsz, seq, D_MODEL), h2_tm.reshape(bsz * seq * ROWS_PER_TOKEN, LANES),
            logits.reshape(bsz, N_EXPERTS, seq))


def _const_spec(shape):
    nd = len(shape)
    return pl.BlockSpec(shape, lambda b, t: (0,) * nd)


def _gla_body(x_ref, mod_ref, gmix_ref, gffn_ref, win_ref, wgu_ref, bg_ref, ng_ref,
              wout_ref, rw_cat_ref, rw_hi_ref, *rest, n_cast):
    cast_in = rest[:n_cast]
    xo_ref, h2_ref, lg_ref = rest[n_cast:n_cast + 3]
    cast_out = rest[n_cast + 3:2 * n_cast + 3]
    state_scr, o_scr = rest[2 * n_cast + 3:]

    @pl.when(pl.program_id(1) == 0)
    def _():
        state_scr[...] = jnp.zeros_like(state_scr)

    def caster():
        pieces = 4
        for src, dst in zip(cast_in, cast_out):
            rows = src.shape[0] // pieces
            for i in range(pieces):
                dst[i * rows:(i + 1) * rows, :] = src[i * rows:(i + 1) * rows, :].astype(jnp.bfloat16)
                yield

    gens = [_gla_stream(x_ref.at[st], mod_ref.at[st], gmix_ref, gffn_ref, win_ref, wgu_ref,
                        bg_ref, ng_ref, wout_ref, rw_cat_ref, rw_hi_ref,
                        xo_ref.at[st], h2_ref.at[st], lg_ref.at[st],
                        state_scr.at[st], o_scr.at[st])
            for st in range(MIX_STREAMS)]
    _interleave(gens + [caster()], GLA_STAGGER)


def _gla_stream(x_ref, mod_ref, gmix_ref, gffn_ref, win_ref, wgu_ref, bg_ref, ng_ref,
                wout_ref, rw_cat_ref, rw_hi_ref,
                xo_ref, h2_ref, lg_ref, state_scr, o_scr):
    ts = x_ref.shape[1]
    nchunk = ts // CHUNK
    half = GLA_VD // 2
    x = x_ref[0]
    h = _pre_norm(x, mod_ref, gmix_ref, 0, 1).astype(jnp.bfloat16)
    yield

    o_q, o_k, o_v, o_g, o_a = 0, GLA_QK, 2 * GLA_QK, 2 * GLA_QK + GLA_VD, 2 * GLA_QK + 2 * GLA_VD
    q = (_dot(h, win_ref[:, o_q:o_k]) * (GLA_DK ** -0.5)).astype(jnp.bfloat16)
    yield
    k = _dot(h, win_ref[:, o_k:o_v])
    yield
    v = jnp.concatenate(
        [_dot(h, win_ref[:, o_v + i * half:o_v + (i + 1) * half]).astype(jnp.bfloat16)
         for i in range(2)], axis=1)
    yield
    a_lr = _dot(h, win_ref[:, o_a:o_a + GLA_RANK_PAD])
    a_hi, a_lo = _split_bf16(a_lr)
    z2 = _dot(jnp.concatenate([a_hi, a_lo], axis=0), wgu_ref[...])
    z = z2[0:ts] + z2[ts:2 * ts] + bg_ref[...]
    log_a = -(jnp.maximum(-z, 0.0) + jnp.log1p(jnp.exp(-jnp.abs(z)))) * (1.0 / GLA_GATE_TAU)
    yield

    row = lax.broadcasted_iota(jnp.int32, (ts, ts), 0)
    col = lax.broadcasted_iota(jnp.int32, (ts, ts), 1)
    tri = ((row // CHUNK == col // CHUNK) & (col <= row)).astype(jnp.bfloat16)
    l_hi, l_lo = _split_bf16(log_a)
    cum2 = _dot(tri, jnp.concatenate([l_hi, l_lo], axis=1))
    cum = cum2[:, 0:GLA_QK] + cum2[:, GLA_QK:2 * GLA_QK]
    yield

    ng = ng_ref[...]
    kv_t, gammas = [], []
    for j in range(nchunk):
        r0 = j * CHUNK
        cum_j = cum[r0:r0 + CHUNK]
        total = cum_j[CHUNK - 1:CHUNK]
        k_dec = (k[r0:r0 + CHUNK] * jnp.exp(total - cum_j)).astype(jnp.bfloat16)
        gammas.append(jnp.exp(total))
        for hd in range(GLA_HEADS):
            ks = slice(hd * GLA_DK, (hd + 1) * GLA_DK)
            vs = slice(hd * GLA_DV, (hd + 1) * GLA_DV)
            kv_t.append(_dot_tn(v[r0:r0 + CHUNK, vs], k_dec[:, ks]))
        yield
    states = []
    for hd in range(GLA_HEADS):
        ks = slice(hd * GLA_DK, (hd + 1) * GLA_DK)
        s_cur = state_scr[hd]
        for j in range(nchunk):
            s_cur = s_cur * gammas[j][:, ks] + kv_t[j * GLA_HEADS + hd]
            states.append(s_cur.astype(jnp.bfloat16))
        state_scr[hd] = s_cur
        yield
    for hd in range(GLA_HEADS):
        ks = slice(hd * GLA_DK, (hd + 1) * GLA_DK)
        vs = slice(hd * GLA_DV, (hd + 1) * GLA_DV)
        for j in range(nchunk):
            r0 = j * CHUNK
            o = _dot_nt(q[r0:r0 + CHUNK, ks], states[hd * nchunk + j])
            o_scr[r0:r0 + CHUNK, vs] = _rms(o, ng)
        yield

    og = []
    for i in range(2):
        g = _dot(h, win_ref[:, o_g + i * half:o_g + (i + 1) * half])
        og.append((o_scr[:, i * half:(i + 1) * half] * _silu(g)).astype(jnp.bfloat16))
        yield
    og = jnp.concatenate(og, axis=1)
    x_new = []
    for i in range(2):
        cols = slice(i * half, (i + 1) * half)
        mix = _dot(og, wout_ref[:, i * half:(i + 1) * half])
        x_new.append(x[:, cols] + mod_ref[0, 2:3, cols] * mix)
        yield
    x_new = jnp.concatenate(x_new, axis=1)
    xo_ref[0] = x_new
    h2 = _pre_norm(x_new, mod_ref, gffn_ref, 3, 4)
    for c in range(ROWS_PER_TOKEN):
        h2_ref[pl.ds(c, ts, stride=ROWS_PER_TOKEN), :] = h2[:, c * LANES:(c + 1) * LANES]
    yield
    lg_ref[0] = _router_logits(h2, rw_cat_ref, rw_hi_ref)
    yield


def _gla_layer_call(x, mod, gmix, gffn, w_in, w_gu, b_g, n_g, w_out, rw_cat, rw_hi, cast_src):
    bsz, seq, d = x.shape
    ts = MIX_TS
    ns = MIX_STREAMS
    hb = bsz // ns
    nt = seq // ts
    io_in, out_specs, out_shapes = _mixer_io_specs(bsz, seq, ts)
    consts = (gmix, gffn, w_in, w_gu, b_g, n_g, w_out, rw_cat, rw_hi)
    cast_specs, cast_shapes = [], []
    for a in cast_src:
        rows = a.shape[0] // (hb * nt)
        assert rows * hb * nt == a.shape[0]
        cast_specs.append(pl.BlockSpec((rows, a.shape[1]), lambda b, t: (b * nt + t, 0)))
        cast_shapes.append(jax.ShapeDtypeStruct(a.shape, jnp.bfloat16))
    outs = pl.pallas_call(
        functools.partial(_gla_body, n_cast=len(cast_src)),
        grid=(hb, nt),
        in_specs=io_in + [_const_spec(a.shape) for a in consts] + cast_specs,
        out_specs=out_specs + cast_specs,
        out_shape=out_shapes + cast_shapes,
        scratch_shapes=[
            pltpu.VMEM((ns, GLA_HEADS, GLA_DV, GLA_DK), jnp.float32),
            pltpu.VMEM((ns, ts, GLA_VD), jnp.float32),
        ],
        compiler_params=pltpu.CompilerParams(
            dimension_semantics=("arbitrary", "arbitrary"),
            vmem_limit_bytes=VMEM_LIMIT),
        name="gla_layer",
    )(x.reshape(ns, hb, seq, d), mod.reshape(ns, hb, 6, d), *consts, *cast_src)
    return _mixer_unstream(outs[:3], bsz, seq), outs[3:]


def _lru_body(x_ref, mod_ref, gmix_ref, gffn_ref, win_ref, cw_ref, cb_ref, wr_ref, br_ref,
              wi_ref, bi_ref, lam_ref, wout_ref, rw_cat_ref, rw_hi_ref,
              xo_ref, h2_ref, lg_ref,
              conv_scr, hstate_scr, a_scr, b_scr, gelu_scr):
    ns = MIX_STREAMS
    ts = x_ref.shape[2]
    w = D_MODEL

    @pl.when(pl.program_id(1) == 0)
    def _():
        conv_scr[...] = jnp.zeros_like(conv_scr)
        hstate_scr[...] = jnp.zeros_like(hstate_scr)

    _interleave([_lru_pre_scan(x_ref.at[st], mod_ref.at[st], gmix_ref, win_ref, cw_ref, cb_ref,
                               wr_ref, br_ref, wi_ref, bi_ref, lam_ref, conv_scr.at[st],
                               a_scr.at[st], b_scr.at[st], gelu_scr.at[st])
                 for st in range(ns)], LRU_STAGGER)

    unroll = 4

    def group_step(gi, hprev):
        for u in range(unroll):
            r0 = pl.multiple_of(gi * (unroll * SUBLANES), unroll * SUBLANES) + u * SUBLANES
            nxt = []
            for st in range(ns):
                hs = (a_scr[st, pl.ds(r0, SUBLANES), :] * hprev[st]
                      + b_scr[st, pl.ds(r0, SUBLANES), :])
                b_scr[st, pl.ds(r0, SUBLANES), :] = hs
                nxt.append(hs[SUBLANES - 1:SUBLANES, :])
            hprev = tuple(nxt)
        return hprev

    hlast = tuple(hstate_scr[st] for st in range(ns))
    for gi in range(ts // (unroll * SUBLANES)):
        hlast = group_step(gi, hlast)
    for st in range(ns):
        hstate_scr[st] = hlast[st]

    def post(st):
        y = (b_scr[st] * gelu_scr[st]).astype(jnp.bfloat16)
        yield
        mix = _dot(y, wout_ref[...])
        yield
        yield from _post_mix_pieces(x_ref[st, 0], mix, mod_ref.at[st], gffn_ref, rw_cat_ref,
                                    rw_hi_ref, xo_ref.at[st], h2_ref.at[st], lg_ref.at[st])

    _interleave([post(st) for st in range(ns)], LRU_STAGGER)


def _lru_pre_scan(x_ref, mod_ref, gmix_ref, win_ref, cw_ref, cb_ref, wr_ref, br_ref,
                  wi_ref, bi_ref, lam_ref, conv_scr, a_scr, b_scr, gelu_scr):
    ts = x_ref.shape[1]
    w = D_MODEL
    x = x_ref[0]
    h = _pre_norm(x, mod_ref, gmix_ref, 0, 1).astype(jnp.bfloat16)
    gate_br = _dot(h, win_ref[:, 0:w])
    c0 = 0.7978845608028654
    u = gate_br * (c0 + (c0 * 0.044715) * (gate_br * gate_br))
    half_g = 0.5 * gate_br
    gelu_scr[...] = half_g + half_g * jnp.tanh(u)
    yield
    xb = _dot(h, win_ref[:, w:2 * w])

    ext = jnp.concatenate([conv_scr[...], xb], axis=0)
    xc = cb_ref[...] + xb * cw_ref[CONV_W - 1:CONV_W, :]
    for j in range(CONV_W - 1):
        back = CONV_W - 1 - j
        xc = xc + pltpu.roll(ext, back, 0)[SUBLANES:SUBLANES + ts] * cw_ref[j:j + 1, :]
    conv_scr[...] = xb[ts - SUBLANES:ts]
    yield

    xcb = xc.astype(jnp.bfloat16)
    rs, iis = [], []
    for hd in range(LRU_BLOCKS):
        sl = slice(hd * LRU_BLOCK_W, (hd + 1) * LRU_BLOCK_W)
        rs.append(_dot(xcb[:, sl], wr_ref[hd]))
        iis.append(_dot(xcb[:, sl], wi_ref[hd]))
    t_r = jnp.tanh(0.5 * (jnp.concatenate(rs, axis=1) + br_ref[...]))
    t_i = jnp.tanh(0.5 * (jnp.concatenate(iis, axis=1) + bi_ref[...]))
    yield

    lam = lam_ref[...]
    softplus_neg_lam = jnp.maximum(-lam, 0.0) + jnp.log1p(jnp.exp(-jnp.abs(lam)))
    sp = (-0.5 * LRU_C) * softplus_neg_lam
    log_a = sp * t_r + sp
    a = jnp.exp(log_a)
    half_mult = jnp.sqrt(jnp.tanh(-log_a) * (0.25 * (a * a) + 0.25))
    bb = (xc * half_mult) * (t_i + 1.0)
    yield

    a = a.reshape(ts // SUBLANES, SUBLANES, w)
    bb = bb.reshape(ts // SUBLANES, SUBLANES, w)
    rowi = lax.broadcasted_iota(jnp.int32, a.shape, 1)
    for d in (1, 2, 4):
        keep = rowi >= d
        a_sh = jnp.where(keep, pltpu.roll(a, d, 1), 1.0)
        b_sh = jnp.where(keep, pltpu.roll(bb, d, 1), 0.0)
        bb = a * b_sh + bb
        a = a * a_sh
    a_scr[...] = a.reshape(ts, w)
    b_scr[...] = bb.reshape(ts, w)
    yield


def _interleave(gens, lag):
    live = [True] * len(gens)
    tick = 0
    while any(live):
        for i, gen in enumerate(gens):
            if live[i] and tick >= i * lag:
                try:
                    next(gen)
                except StopIteration:
                    live[i] = False
        tick += 1


def _lru_layer_call(x, mod, gmix, gffn, w_in, conv_w, conv_b, w_r, b_r, w_i, b_i, lam, w_out,
                    rw_cat, rw_hi):
    bsz, seq, d = x.shape
    ts = MIX_TS
    ns = MIX_STREAMS
    hb = bsz // ns
    io_in, out_specs, out_shapes = _mixer_io_specs(bsz, seq, ts)
    consts = (gmix, gffn, w_in, conv_w, conv_b, w_r, b_r, w_i, b_i, lam, w_out,
              rw_cat, rw_hi)
    outs = pl.pallas_call(
        _lru_body,
        grid=(hb, seq // ts),
        in_specs=io_in + [_const_spec(a.shape) for a in consts],
        out_specs=out_specs,
        out_shape=out_shapes,
        scratch_shapes=[
            pltpu.VMEM((ns, SUBLANES, d), jnp.float32),
            pltpu.VMEM((ns, 1, d), jnp.float32),
            pltpu.VMEM((ns, ts, d), jnp.float32),
            pltpu.VMEM((ns, ts, d), jnp.float32),
            pltpu.VMEM((ns, ts, d), jnp.float32),
        ],
        compiler_params=pltpu.CompilerParams(
            dimension_semantics=("arbitrary", "arbitrary"),
            vmem_limit_bytes=VMEM_LIMIT),
        name="lru_layer",
    )(x.reshape(ns, hb, seq, d), mod.reshape(ns, hb, 6, d), *consts)
    return _mixer_unstream(outs, bsz, seq)


def _route_body(lg_ref, rbias_ref, pos0_ref, pos1_ref, w0_ref, off_ref, cnt_ref):
    seq = lg_ref.shape[2]
    blk = ROUTE_BLOCK
    s = _sigmoid_rel(lg_ref[0])
    m, wgt, sub = _route_select(s, rbias_ref[...])
    mf = m.astype(jnp.float32)

    row = lax.broadcasted_iota(jnp.int32, (blk, blk), 0)
    col = lax.broadcasted_iota(jnp.int32, (blk, blk), 1)
    before = (row < col).astype(jnp.bfloat16)
    carry = jnp.zeros((N_EXPERTS, 1), jnp.float32)
    ranks = []
    for i in range(seq // blk):
        mb = mf[:, i * blk:(i + 1) * blk]
        ranks.append(_dot(mb.astype(jnp.bfloat16), before) + carry)
        carry = carry + jnp.sum(mb, axis=1, keepdims=True)
    rnk = jnp.concatenate(ranks, axis=1)
    cnt_ref[0] = jnp.broadcast_to(carry, (N_EXPERTS, LANES)).astype(jnp.int32)

    eidx = sub.astype(jnp.float32)
    e_lo = jnp.min(jnp.where(m, eidx, 99.0), axis=0, keepdims=True)
    e_hi = jnp.max(jnp.where(m, eidx, -1.0), axis=0, keepdims=True)
    is_lo = m & (eidx == e_lo)
    is_hi = m & (eidx == e_hi)
    p_lo = jnp.sum(jnp.where(is_lo, rnk, 0.0), axis=0, keepdims=True)
    p_hi = jnp.sum(jnp.where(is_hi, rnk, 0.0), axis=0, keepdims=True)
    w0_ref[0] = jnp.sum(jnp.where(is_lo, wgt, 0.0), axis=0, keepdims=True)

    off = jnp.zeros((1, 1), jnp.float32)
    for e in range(N_EXPERTS):
        off_ref[0, e:e + 1, :] = jnp.broadcast_to(off, (1, LANES)).astype(jnp.int32)
        p_lo = p_lo + jnp.where(e_lo == float(e), off, 0.0)
        p_hi = p_hi + jnp.where(e_hi == float(e), off, 0.0)
        off = off + jnp.floor((carry[e:e + 1, :] + (SLOT_ALIGN - 1.0)) * (1.0 / SLOT_ALIGN)) * SLOT_ALIGN
    pos0_ref[0] = p_lo.astype(jnp.int32) * ROWS_PER_TOKEN
    pos1_ref[0] = p_hi.astype(jnp.int32) * ROWS_PER_TOKEN


def _route_call(logits, rbias):
    bsz, _, seq = logits.shape
    row_spec = pl.BlockSpec((1, 1, seq), lambda b: (b, 0, 0))
    tab_spec = pl.BlockSpec((1, N_EXPERTS, LANES), lambda b: (b, 0, 0))
    return pl.pallas_call(
        _route_body,
        grid=(bsz,),
        in_specs=[
            pl.BlockSpec((1, N_EXPERTS, seq), lambda b: (b, 0, 0)),
            pl.BlockSpec((N_EXPERTS, 1), lambda b: (0, 0)),
        ],
        out_specs=[row_spec, row_spec, row_spec, tab_spec, tab_spec],
        out_shape=[
            jax.ShapeDtypeStruct((bsz, 1, seq), jnp.int32),
            jax.ShapeDtypeStruct((bsz, 1, seq), jnp.int32),
            jax.ShapeDtypeStruct((bsz, 1, seq), jnp.float32),
            jax.ShapeDtypeStruct((bsz, N_EXPERTS, LANES), jnp.int32),
            jax.ShapeDtypeStruct((bsz, N_EXPERTS, LANES), jnp.int32),
        ],
        compiler_params=pltpu.CompilerParams(dimension_semantics=("arbitrary",)),
        name="moe_route",
    )(logits, rbias)


def _scatter_tokens(pos0_ref, pos1_ref, h2_ref, slots, idx0, lo, cnt):
    R = ROWS_PER_TOKEN
    for u in range(lo, lo + cnt):
        p0 = pos0_ref[0, 0, idx0 + u]
        p1 = pos1_ref[0, 0, idx0 + u]
        val = h2_ref[u * R:(u + 1) * R, :]
        slots[pl.ds(pl.multiple_of(p0, R), R), :] = val
        slots[pl.ds(pl.multiple_of(p1, R), R), :] = val


def _combine_tokens(pos0_ref, pos1_ref, wts_ref, slots, tm_scr, idx0, lo, cnt):
    R = ROWS_PER_TOKEN
    for u in range(lo, lo + cnt):
        p0 = pos0_ref[0, 0, idx0 + u]
        p1 = pos1_ref[0, 0, idx0 + u]
        w0 = wts_ref[0, 0, idx0 + u]
        za = slots[pl.ds(pl.multiple_of(p0, R), R), :]
        zb = slots[pl.ds(pl.multiple_of(p1, R), R), :]
        tm_scr[u * R:(u + 1) * R, :] = zb + w0 * (za - zb)


def _combine_epilogue(tm_scr, x_ref, mod_ref, fin_ref, o_ref, lo, cnt, final_norm):
    R = ROWS_PER_TOKEN
    moe = jnp.concatenate(
        [tm_scr[pl.ds(lo * R + c, cnt, stride=R), :] for c in range(R)], axis=1)
    out = x_ref[lo:lo + cnt, :] + mod_ref[0, 5:6, :] * moe
    if final_norm:
        out = _rms(out, fin_ref[...])
    o_ref[lo:lo + cnt, :] = out


def _expert_tile_hooked(slots, wg_ref, wu_ref, wd_ref, ws, slot0, n_valid, tm, hooks):
    R = ROWS_PER_TOKEN
    nw = 256
    row0 = pl.multiple_of(slot0 * R, SLOT_ALIGN * R)
    xs = [slots[pl.ds(row0 + c, tm, stride=R), :] for c in range(R)]
    xt = jnp.concatenate(xs, axis=1).astype(jnp.bfloat16)
    ok = lax.broadcasted_iota(jnp.int32, (tm, LANES), 0) < n_valid
    hooks = list(hooks)

    def run_hook():
        if hooks:
            hooks.pop(0)()

    he = []
    for i in range(D_EXPERT // nw):
        gate = _dot(xt, wg_ref[ws, :, i * nw:(i + 1) * nw])
        run_hook()
        up = _dot(xt, wu_ref[ws, :, i * nw:(i + 1) * nw])
        run_hook()
        he.append((_silu(gate) * up).astype(jnp.bfloat16))
    he = jnp.concatenate(he, axis=1)
    for i in range(D_MODEL // nw):
        y = _dot(he, wd_ref[ws, :, i * nw:(i + 1) * nw])
        for cc in range(nw // LANES):
            c = i * (nw // LANES) + cc
            slots[pl.ds(row0 + c, tm, stride=R), :] = jnp.where(
                ok, y[:, cc * LANES:(cc + 1) * LANES], xs[c])
        run_hook()
    while hooks:
        run_hook()


def _moe_body(off_sm, cnt_sm, sc0_ref, sc1_ref, cb0_ref, cb1_ref, wts_ref, h2_ref, x_ref,
              mod_ref, fin_ref, wg_hbm, wu_hbm, wd_hbm, o_ref, slots_a, slots_b, tm_scr,
              wg_ref, wu_ref, wd_ref, w_sem, *, n_blocks, layer, final_norm):
    r = pl.program_id(0)
    e = pl.program_id(1)

    g = r * N_EXPERTS + e
    n_steps = (n_blocks + 2) * N_EXPERTS

    def w_needed(step):
        rnd = step // N_EXPERTS
        return (rnd >= 1) & (rnd <= n_blocks) & (step < n_steps)

    def w_copies(step):
        ex = layer * N_EXPERTS + step % N_EXPERTS
        sl = step % MOE_WRING
        return [pltpu.make_async_copy(src.at[ex], dst.at[sl], w_sem.at[i, sl])
                for i, (src, dst) in enumerate(((wg_hbm, wg_ref), (wu_hbm, wu_ref),
                                                (wd_hbm, wd_ref)))]

    def w_start(step):
        @pl.when(w_needed(step))
        def _():
            for cp in w_copies(step):
                cp.start()

    @pl.when(g == 0)
    def _():
        for ahead in range(MOE_WRING - 1):
            w_start(g + ahead)

    w_start(g + MOE_WRING - 1)

    @pl.when(w_needed(g))
    def _():
        for cp in w_copies(g):
            cp.wait()

    ws = g % MOE_WRING
    R = ROWS_PER_TOKEN
    U = MOE_UNROLL
    tm = MOE_TM
    mv = MOE_MOVE
    half = N_EXPERTS // 2
    n_hooks = 2 * (D_EXPERT // 256) + D_MODEL // 256
    per_hook = mv // n_hooks

    @pl.when((r == 0) & (e == 0))
    def _():
        slots_a[...] = jnp.zeros_like(slots_a)
        slots_b[...] = jnp.zeros_like(slots_b)

    do_exp = (r >= 1) & (r <= n_blocks)
    do_comb = (r >= 2) & (e < half)
    do_scat = (r < n_blocks) & (e >= half)
    b_exp = jnp.clip(r - 1, 0, n_blocks - 1)
    off = off_sm[b_exp, e]
    n = jnp.where(do_exp, cnt_sm[b_exp, e], 0)
    comb_idx0 = e * mv
    scat_idx0 = (e - half) * mv
    small = tm // 2
    first_rows = jnp.where(n > small, tm, jnp.where(n > 0, small, 0))
    fused = do_comb | do_scat
    start = jnp.where(fused, first_rows, 0)
    rest = jnp.maximum(n - start, 0)
    rest_big = rest // tm + ((rest % tm) > small).astype(jnp.int32)
    rest_tail = rest - rest_big * tm

    def round_body(exp_buf, mov_buf):
        tile = functools.partial(_expert_tile_hooked, exp_buf, wg_ref, wu_ref, wd_ref, ws)

        def comb_hooks():
            def hook(k):
                def run():
                    if k < n_hooks:
                        _combine_tokens(cb0_ref, cb1_ref, wts_ref, mov_buf, tm_scr, comb_idx0,
                                        k * per_hook, per_hook)
                    if k >= 1:
                        _combine_epilogue(tm_scr, x_ref, mod_ref, fin_ref, o_ref,
                                          (k - 1) * per_hook, per_hook, final_norm)
                return run
            return [hook(k) for k in range(n_hooks + 1)]

        def scat_hooks():
            def hook(k):
                return lambda: _scatter_tokens(sc0_ref, sc1_ref, h2_ref, mov_buf, scat_idx0,
                                               k * per_hook, per_hook)
            return [hook(k) for k in range(n_hooks)]

        for rows in (tm, small):
            @pl.when((first_rows == rows) & do_comb)
            def _():
                tile(off, n, rows, comb_hooks())

            @pl.when((first_rows == rows) & do_scat)
            def _():
                tile(off, n, rows, scat_hooks())

        @pl.when(first_rows == 0)
        def _():
            @pl.when(do_comb)
            def _():
                def body(i, carry):
                    i0 = comb_idx0 + U * i
                    for u in range(U):
                        p0 = cb0_ref[0, 0, i0 + u]
                        p1 = cb1_ref[0, 0, i0 + u]
                        w0 = wts_ref[0, 0, i0 + u]
                        za = mov_buf[pl.ds(pl.multiple_of(p0, R), R), :]
                        zb = mov_buf[pl.ds(pl.multiple_of(p1, R), R), :]
                        r0 = pl.multiple_of(i * (U * R), U * R) + u * R
                        tm_scr[pl.ds(r0, R), :] = zb + w0 * (za - zb)
                    return carry

                lax.fori_loop(0, mv // U, body, 0)
                _combine_epilogue(tm_scr, x_ref, mod_ref, fin_ref, o_ref, 0, mv, final_norm)

            @pl.when(do_scat)
            def _():
                def body(i, carry):
                    i0 = scat_idx0 + U * i
                    for u in range(U):
                        p0 = sc0_ref[0, 0, i0 + u]
                        p1 = sc1_ref[0, 0, i0 + u]
                        r0 = pl.multiple_of(i * (U * R), U * R) + u * R
                        val = h2_ref[pl.ds(r0, R), :]
                        mov_buf[pl.ds(pl.multiple_of(p0, R), R), :] = val
                        mov_buf[pl.ds(pl.multiple_of(p1, R), R), :] = val
                    return carry

                lax.fori_loop(0, mv // U, body, 0)

        def tile_body(j, carry):
            done = start + j * tm
            tile(off + done, n - done, tm, [])
            return carry

        lax.fori_loop(0, rest_big, tile_body, 0)

        @pl.when(rest_tail > 0)
        def _():
            done = start + rest_big * tm
            tile(off + done, n - done, small, [])

    @pl.when(r % 2 == 0)
    def _():
        round_body(slots_b, slots_a)

    @pl.when(r % 2 == 1)
    def _():
        round_body(slots_a, slots_b)


def _moe_call(off, cnt, pos0, pos1, w0, h2_tm, x, mod, fin_g, w_gate, w_up, w_down, layer,
              final_norm):
    bsz, seq, d = x.shape
    mv = MOE_MOVE
    half = N_EXPERTS // 2
    assert seq == half * mv
    n_slots = 2 * seq
    last = bsz - 1

    def scat_blk(r, e, *_):
        return (jnp.minimum(r, last) * half + jnp.clip(e - half, 0, half - 1), 0)

    def comb_blk(r, e, *_):
        return (jnp.where(r < 2, 0, (r - 2) * half + jnp.minimum(e, half - 1)), 0)

    def scat_row(r, e, *_):
        return (jnp.minimum(r, last), 0, 0)

    def comb_row(r, e, *_):
        return (jnp.clip(r - 2, 0, last), 0, 0)

    body = functools.partial(_moe_body, n_blocks=bsz, layer=layer, final_norm=final_norm)
    slot_rows = (n_slots + N_EXPERTS * (SLOT_ALIGN - 1) + MOE_TM) * ROWS_PER_TOKEN
    out = pl.pallas_call(
        body,
        grid_spec=pltpu.PrefetchScalarGridSpec(
            num_scalar_prefetch=2,
            grid=(bsz + 2, N_EXPERTS),
            in_specs=[
                pl.BlockSpec((1, 1, seq), scat_row, memory_space=pltpu.SMEM),
                pl.BlockSpec((1, 1, seq), scat_row, memory_space=pltpu.SMEM),
                pl.BlockSpec((1, 1, seq), comb_row, memory_space=pltpu.SMEM),
                pl.BlockSpec((1, 1, seq), comb_row, memory_space=pltpu.SMEM),
                pl.BlockSpec((1, 1, seq), comb_row, memory_space=pltpu.SMEM),
                pl.BlockSpec((mv * ROWS_PER_TOKEN, LANES), scat_blk),
                pl.BlockSpec((mv, d), comb_blk),
                pl.BlockSpec((1, 6, d), comb_row),
                pl.BlockSpec((1, d), lambda r, e, *_: (0, 0)),
                pl.BlockSpec(memory_space=pl.ANY),
                pl.BlockSpec(memory_space=pl.ANY),
                pl.BlockSpec(memory_space=pl.ANY),
            ],
            out_specs=pl.BlockSpec((mv, d), comb_blk),
            scratch_shapes=[
                pltpu.VMEM((slot_rows, LANES), jnp.float32),
                pltpu.VMEM((slot_rows, LANES), jnp.float32),
                pltpu.VMEM((mv * ROWS_PER_TOKEN, LANES), jnp.float32),
                pltpu.VMEM((MOE_WRING, d, D_EXPERT), jnp.bfloat16),
                pltpu.VMEM((MOE_WRING, d, D_EXPERT), jnp.bfloat16),
                pltpu.VMEM((MOE_WRING, D_EXPERT, d), jnp.bfloat16),
                pltpu.SemaphoreType.DMA((3, MOE_WRING)),
            ],
        ),
        out_shape=jax.ShapeDtypeStruct((bsz * seq, d), jnp.float32),
        compiler_params=pltpu.CompilerParams(
            dimension_semantics=("arbitrary", "arbitrary"),
            vmem_limit_bytes=MOE_VMEM_LIMIT),
        name="moe",
    )(off, cnt, pos0, pos1, pos0, pos1, w0, h2_tm, x.reshape(bsz * seq, d), mod, fin_g,
      w_gate, w_up, w_down)
    return out.reshape(bsz, seq, d)


def _moe_layer(x, h2_tm, logits, rbias, mod, fin_g, w_gate, w_up, w_down, layer, final_norm):
    pos0, pos1, w0, off, cnt = _route_call(logits, rbias)
    return _moe_call(off[:, :, 0], cnt[:, :, 0], pos0, pos1, w0, h2_tm, x, mod, fin_g,
                     w_gate, w_up, w_down, layer, final_norm)


def kernel(x, c, gla_w_in, gla_w_gate_up, gla_b_gate, gla_norm_g, gla_w_out, lru_w_in, lru_conv_w, lru_conv_b, lru_w_r, lru_b_r, lru_w_i, lru_b_i, lru_lambda, lru_w_out, router_w, router_bias, moe_w_gate, moe_w_up, moe_w_down, norm_mix_g, norm_ffn_g, ada_w, ada_b, final_norm_g):
    bf = jnp.bfloat16
    depth = ada_w.shape[0]
    bsz = x.shape[0]
    d = D_MODEL
    mod_all = _ada_call(c, ada_w, ada_b).reshape(depth, bsz, 6, d)

    rw_t = router_w.T
    rw_hi = rw_t.astype(bf)
    rw_lo = (rw_t - rw_hi.astype(jnp.float32)).astype(bf)
    rw_cat = jnp.concatenate([rw_hi, rw_lo], axis=0)
    rbias = router_bias.reshape(N_EXPERTS, 1)
    fin_g = final_norm_g.reshape(1, d)
    expert_w = None

    for i in range(depth):
        j = i // 2
        mod = mod_all[i]
        gmix = norm_mix_g[i].reshape(1, d)
        gffn = norm_ffn_g[i].reshape(1, d)
        if i % 2 == 0:
            w_in = jnp.pad(gla_w_in[j], ((0, 0), (0, GLA_RANK_PAD - GLA_GATE_RANK))).astype(bf)
            w_gu = jnp.pad(gla_w_gate_up[j], ((0, GLA_RANK_PAD - GLA_GATE_RANK), (0, 0))).astype(bf)
            cast_src = () if expert_w is not None else tuple(
                w.reshape(-1, w.shape[-1]) for w in (moe_w_gate, moe_w_up, moe_w_down))
            (x, h2_tm, logits), cast = _gla_layer_call(
                x, mod, gmix, gffn, w_in, w_gu, gla_b_gate[j].reshape(1, GLA_QK),
                gla_norm_g[j].reshape(1, GLA_DV), gla_w_out[j].astype(bf),
                rw_cat, rw_hi, cast_src)
            if cast:
                expert_w = (cast[0].reshape(depth * N_EXPERTS, d, D_EXPERT),
                            cast[1].reshape(depth * N_EXPERTS, d, D_EXPERT),
                            cast[2].reshape(depth * N_EXPERTS, D_EXPERT, d))
        else:
            x, h2_tm, logits = _lru_layer_call(
                x, mod, gmix, gffn, lru_w_in[j].astype(bf), lru_conv_w[j],
                lru_conv_b[j].reshape(1, d), lru_w_r[j].astype(bf),
                lru_b_r[j].reshape(1, d), lru_w_i[j].astype(bf), lru_b_i[j].reshape(1, d),
                lru_lambda[j].reshape(1, d), lru_w_out[j].astype(bf), rw_cat, rw_hi)
        x = _moe_layer(x, h2_tm, logits, rbias, mod, fin_g, *expert_w,
                       layer=i, final_norm=(i == depth - 1))
    return x
```

```python
import functools

import jax
import jax.numpy as jnp
from jax import lax
from jax.experimental import pallas as pl
from jax.experimental.pallas import tpu as pltpu

D_MODEL = 1024
CHUNK = 64
EPS = 1e-6

GLA_HEADS = 4
GLA_DK = 128
GLA_DV = 256
GLA_QK = GLA_HEADS * GLA_DK
GLA_VD = GLA_HEADS * GLA_DV
GLA_GATE_RANK = 16
GLA_GATE_TAU = 16.0
GLA_RANK_PAD = 128

LRU_BLOCKS = 4
LRU_BLOCK_W = D_MODEL // LRU_BLOCKS
CONV_W = 4
LRU_C = 8.0

N_EXPERTS = 16
N_GROUPS = 4
EPG = N_EXPERTS // N_GROUPS
D_EXPERT = 512

LANES = 128
SUBLANES = 8
ROWS_PER_TOKEN = D_MODEL // LANES

MIX_TS = 256
MIX_STREAMS = 2
LRU_STAGGER = 1
GLA_STAGGER = 1
ROUTE_BLOCK = 256
SLOT_ALIGN = 4
MOE_TM = 256
MOE_UNROLL = 8
MOE_MOVE = 256
MOE_WRING = 3
VMEM_LIMIT = 56 * 1024 * 1024
MOE_VMEM_LIMIT = 60 * 1024 * 1024


def _dot(a, b):
    return jnp.dot(a, b, preferred_element_type=jnp.float32)


def _dot_nt(a, b):
    return lax.dot_general(a, b, (((1,), (1,)), ((), ())),
                           preferred_element_type=jnp.float32)


def _dot_tn(a, b):
    return lax.dot_general(a, b, (((0,), (0,)), ((), ())),
                           preferred_element_type=jnp.float32)


def _split_bf16(x):
    hi = x.astype(jnp.bfloat16)
    lo = (x - hi.astype(jnp.float32)).astype(jnp.bfloat16)
    return hi, lo


def _sigmoid(x):
    return 0.5 * jnp.tanh(0.5 * x) + 0.5


def _sigmoid_rel(x):
    return 1.0 / (1.0 + jnp.exp(-x))


def _silu(x):
    return x * _sigmoid(x)


def _rms(x, g):
    inv = lax.rsqrt(jnp.mean(x * x, axis=-1, keepdims=True) + EPS)
    return x * inv * g


def _ada_body(c_ref, w_ref, b_ref, o_ref):
    c_hi, c_lo = _split_bf16(_silu(c_ref[...]))
    w_hi, w_lo = _split_bf16(w_ref[0])
    o_ref[0] = _dot(c_hi, w_hi) + (_dot(c_lo, w_hi) + _dot(c_hi, w_lo)) + b_ref[0]


def _ada_call(c, ada_w, ada_b):
    depth, d, n = ada_w.shape
    bsz = c.shape[0]
    tn = 1024
    return pl.pallas_call(
        _ada_body,
        grid=(depth, n // tn),
        in_specs=[
            pl.BlockSpec((bsz, d), lambda l, j: (0, 0)),
            pl.BlockSpec((1, d, tn), lambda l, j: (l, 0, j)),
            pl.BlockSpec((1, 1, tn), lambda l, j: (l, 0, j)),
        ],
        out_specs=pl.BlockSpec((1, bsz, tn), lambda l, j: (l, 0, j)),
        out_shape=jax.ShapeDtypeStruct((depth, bsz, n), jnp.float32),
        compiler_params=pltpu.CompilerParams(
            dimension_semantics=("arbitrary", "arbitrary"),
            vmem_limit_bytes=VMEM_LIMIT),
        name="ada",
    )(c, ada_w, ada_b.reshape(depth, 1, n))


def _pre_norm(x, mod_ref, g_ref, shift_row, scale_row):
    shift = mod_ref[0, shift_row:shift_row + 1, :]
    scale = mod_ref[0, scale_row:scale_row + 1, :]
    inv = lax.rsqrt(jnp.mean(x * x, axis=-1, keepdims=True) + EPS)
    return (x * inv) * (g_ref[...] * (1.0 + scale)) + shift


def _group_partner(x, k, sub):
    n = x.shape[0]
    fwd = pltpu.roll(x, n - k, 0)
    back = pltpu.roll(x, EPG - k, 0)
    wrapped = (sub % EPG) + k >= EPG
    return jnp.where(wrapped, back, fwd), wrapped


def _router_logits(h2, rw_cat_ref, rw_hi_ref):
    h_hi, h_lo = _split_bf16(h2)
    p1 = _dot_nt(rw_cat_ref[...], h_hi)
    p2 = _dot_nt(rw_hi_ref[...], h_lo)
    return p1[0:N_EXPERTS] + p1[N_EXPERTS:2 * N_EXPERTS] + p2


def _route_select(s, rbias):
    n = s.shape[1]
    sel = s + rbias
    sub = lax.broadcasted_iota(jnp.int32, (N_EXPERTS, n), 0)

    pair_best = None
    rank = jnp.zeros((N_EXPERTS, n), jnp.float32)
    for k in range(1, EPG):
        p, wrapped = _group_partner(sel, k, sub)
        ps = sel + p
        pair_best = ps if pair_best is None else jnp.maximum(pair_best, ps)
        ahead = (p > sel) | ((p == sel) & wrapped)
        rank = rank + ahead.astype(jnp.float32)
    gscore = pair_best
    for k in range(1, EPG):
        p, _ = _group_partner(pair_best, k, sub)
        gscore = jnp.maximum(gscore, p)
    chosen = jnp.ones((N_EXPERTS, n), jnp.bool_)
    grp = sub // EPG
    for j in range(1, N_GROUPS):
        other = pltpu.roll(gscore, N_EXPERTS - EPG * j, 0)
        other_is_later = grp + j < N_GROUPS
        chosen = chosen & ((gscore > other) | ((gscore == other) & other_is_later))
    m = chosen & (rank < 2.0)
    sm = s * m.astype(jnp.float32)
    wgt = sm / jnp.sum(sm, axis=0, keepdims=True)
    return m, wgt, sub


def _post_mix_pieces(x, mix, mod_ref, gffn_ref, rw_cat_ref, rw_hi_ref, xo_ref, h2_ref, lg_ref):
    ts = x.shape[0]
    x_new = x + mod_ref[0, 2:3, :] * mix
    xo_ref[0] = x_new
    h2 = _pre_norm(x_new, mod_ref, gffn_ref, 3, 4)
    yield
    for c in range(ROWS_PER_TOKEN):
        h2_ref[pl.ds(c, ts, stride=ROWS_PER_TOKEN), :] = h2[:, c * LANES:(c + 1) * LANES]
    yield
    lg_ref[0] = _router_logits(h2, rw_cat_ref, rw_hi_ref)
    yield


def _mixer_io_specs(bsz, seq, ts):
    ns = MIX_STREAMS
    hb = bsz // ns
    nt = seq // ts
    in_specs = [
        pl.BlockSpec((ns, 1, ts, D_MODEL), lambda b, t: (0, b, t, 0)),
        pl.BlockSpec((ns, 1, 6, D_MODEL), lambda b, t: (0, b, 0, 0)),
    ]
    out_specs = [
        pl.BlockSpec((ns, 1, ts, D_MODEL), lambda b, t: (0, b, t, 0)),
        pl.BlockSpec((ns, ts * ROWS_PER_TOKEN, LANES), lambda b, t: (0, b * nt + t, 0)),
        pl.BlockSpec((ns, 1, N_EXPERTS, ts), lambda b, t: (0, b, 0, t)),
    ]
    out_shapes = [
        jax.ShapeDtypeStruct((ns, hb, seq, D_MODEL), jnp.float32),
        jax.ShapeDtypeStruct((ns, hb * seq * ROWS_PER_TOKEN, LANES), jnp.float32),
        jax.ShapeDtypeStruct((ns, hb, N_EXPERTS, seq), jnp.float32),
    ]
    return in_specs, out_specs, out_shapes


def _mixer_unstream(outs, bsz, seq):
    x, h2_tm, logits = outs
    return (x.reshape(bsz, seq, D_MODEL), h2_tm.reshape(bsz * seq * ROWS_PER_TOKEN, LANES),
            logits.reshape(bsz, N_EXPERTS, seq))


def _const_spec(shape):
    nd = len(shape)
    return pl.BlockSpec(shape, lambda b, t: (0,) * nd)


def _gla_body(x_ref, mod_ref, gmix_ref, gffn_ref, win_ref, wgu_ref, bg_ref, ng_ref,
              wout_ref, rw_cat_ref, rw_hi_ref, *rest, n_cast):
    cast_in = rest[:n_cast]
    xo_ref, h2_ref, lg_ref = rest[n_cast:n_cast + 3]
    cast_out = rest[n_cast + 3:2 * n_cast + 3]
    state_scr, o_scr = rest[2 * n_cast + 3:]

    @pl.when(pl.program_id(1) == 0)
    def _():
        state_scr[...] = jnp.zeros_like(state_scr)

    def caster():
        pieces = 4
        for src, dst in zip(cast_in, cast_out):
            rows = src.shape[0] // pieces
            for i in range(pieces):
                dst[i * rows:(i + 1) * rows, :] = src[i * rows:(i + 1) * rows, :].astype(jnp.bfloat16)
                yield

    gens = [_gla_stream(x_ref.at[st], mod_ref.at[st], gmix_ref, gffn_ref, win_ref, wgu_ref,
                        bg_ref, ng_ref, wout_ref, rw_cat_ref, rw_hi_ref,
                        xo_ref.at[st], h2_ref.at[st], lg_ref.at[st],
                        state_scr.at[st], o_scr.at[st])
            for st in range(MIX_STREAMS)]
    _interleave(gens + [caster()], GLA_STAGGER)


def _gla_stream(x_ref, mod_ref, gmix_ref, gffn_ref, win_ref, wgu_ref, bg_ref, ng_ref,
                wout_ref, rw_cat_ref, rw_hi_ref,
                xo_ref, h2_ref, lg_ref, state_scr, o_scr):
    ts = x_ref.shape[1]
    nchunk = ts // CHUNK
    half = GLA_VD // 2
    x = x_ref[0]
    h = _pre_norm(x, mod_ref, gmix_ref, 0, 1).astype(jnp.bfloat16)
    yield

    o_q, o_k, o_v, o_g, o_a = 0, GLA_QK, 2 * GLA_QK, 2 * GLA_QK + GLA_VD, 2 * GLA_QK + 2 * GLA_VD
    q = (_dot(h, win_ref[:, o_q:o_k]) * (GLA_DK ** -0.5)).astype(jnp.bfloat16)
    yield
    k = _dot(h, win_ref[:, o_k:o_v])
    yield
    v = jnp.concatenate(
        [_dot(h, win_ref[:, o_v + i * half:o_v + (i + 1) * half]).astype(jnp.bfloat16)
         for i in range(2)], axis=1)
    yield
    a_lr = _dot(h, win_ref[:, o_a:o_a + GLA_RANK_PAD])
    a_hi, a_lo = _split_bf16(a_lr)
    z2 = _dot(jnp.concatenate([a_hi, a_lo], axis=0), wgu_ref[...])
    z = z2[0:ts] + z2[ts:2 * ts] + bg_ref[...]
    log_a = -(jnp.maximum(-z, 0.0) + jnp.log1p(jnp.exp(-jnp.abs(z)))) * (1.0 / GLA_GATE_TAU)
    yield

    row = lax.broadcasted_iota(jnp.int32, (ts, ts), 0)
    col = lax.broadcasted_iota(jnp.int32, (ts, ts), 1)
    tri = ((row // CHUNK == col // CHUNK) & (col <= row)).astype(jnp.bfloat16)
    l_hi, l_lo = _split_bf16(log_a)
    cum2 = _dot(tri, jnp.concatenate([l_hi, l_lo], axis=1))
    cum = cum2[:, 0:GLA_QK] + cum2[:, GLA_QK:2 * GLA_QK]
    yield

    ng = ng_ref[...]
    kv_t, gammas = [], []
    for j in range(nchunk):
        r0 = j * CHUNK
        cum_j = cum[r0:r0 + CHUNK]
        total = cum_j[CHUNK - 1:CHUNK]
        k_dec = (k[r0:r0 + CHUNK] * jnp.exp(total - cum_j)).astype(jnp.bfloat16)
        gammas.append(jnp.exp(total))
        for hd in range(GLA_HEADS):
            ks = slice(hd * GLA_DK, (hd + 1) * GLA_DK)
            vs = slice(hd * GLA_DV, (hd + 1) * GLA_DV)
            kv_t.append(_dot_tn(v[r0:r0 + CHUNK, vs], k_dec[:, ks]))
        yield
    states = []
    for hd in range(GLA_HEADS):
        ks = slice(hd * GLA_DK, (hd + 1) * GLA_DK)
        s_cur = state_scr[hd]
        for j in range(nchunk):
            s_cur = s_cur * gammas[j][:, ks] + kv_t[j * GLA_HEADS + hd]
            states.append(s_cur.astype(jnp.bfloat16))
        state_scr[hd] = s_cur
        yield
    for hd in range(GLA_HEADS):
        ks = slice(hd * GLA_DK, (hd + 1) * GLA_DK)
        vs = slice(hd * GLA_DV, (hd + 1) * GLA_DV)
        for j in range(nchunk):
            r0 = j * CHUNK
            o = _dot_nt(q[r0:r0 + CHUNK, ks], states[hd * nchunk + j])
            o_scr[r0:r0 + CHUNK, vs] = _rms(o, ng)
        yield

    og = []
    for i in range(2):
        g = _dot(h, win_ref[:, o_g + i * half:o_g + (i + 1) * half])
        og.append((o_scr[:, i * half:(i + 1) * half] * _silu(g)).astype(jnp.bfloat16))
        yield
    og = jnp.concatenate(og, axis=1)
    x_new = []
    for i in range(2):
        cols = slice(i * half, (i + 1) * half)
        mix = _dot(og, wout_ref[:, i * half:(i + 1) * half])
        x_new.append(x[:, cols] + mod_ref[0, 2:3, cols] * mix)
        yield
    x_new = jnp.concatenate(x_new, axis=1)
    xo_ref[0] = x_new
    h2 = _pre_norm(x_new, mod_ref, gffn_ref, 3, 4)
    for c in range(ROWS_PER_TOKEN):
        h2_ref[pl.ds(c, ts, stride=ROWS_PER_TOKEN), :] = h2[:, c * LANES:(c + 1) * LANES]
    yield
    lg_ref[0] = _router_logits(h2, rw_cat_ref, rw_hi_ref)
    yield


def _gla_layer_call(x, mod, gmix, gffn, w_in, w_gu, b_g, n_g, w_out, rw_cat, rw_hi, cast_src):
    bsz, seq, d = x.shape
    ts = MIX_TS
    ns = MIX_STREAMS
    hb = bsz // ns
    nt = seq // ts
    io_in, out_specs, out_shapes = _mixer_io_specs(bsz, seq, ts)
    consts = (gmix, gffn, w_in, w_gu, b_g, n_g, w_out, rw_cat, rw_hi)
    cast_specs, cast_shapes = [], []
    for a in cast_src:
        rows = a.shape[0] // (hb * nt)
        assert rows * hb * nt == a.shape[0]
        cast_specs.append(pl.BlockSpec((rows, a.shape[1]), lambda b, t: (b * nt + t, 0)))
        cast_shapes.append(jax.ShapeDtypeStruct(a.shape, jnp.bfloat16))
    outs = pl.pallas_call(
        functools.partial(_gla_body, n_cast=len(cast_src)),
        grid=(hb, nt),
        in_specs=io_in + [_const_spec(a.shape) for a in consts] + cast_specs,
        out_specs=out_specs + cast_specs,
        out_shape=out_shapes + cast_shapes,
        scratch_shapes=[
            pltpu.VMEM((ns, GLA_HEADS, GLA_DV, GLA_DK), jnp.float32),
            pltpu.VMEM((ns, ts, GLA_VD), jnp.float32),
        ],
        compiler_params=pltpu.CompilerParams(
            dimension_semantics=("arbitrary", "arbitrary"),
            vmem_limit_bytes=VMEM_LIMIT),
        name="gla_layer",
    )(x.reshape(ns, hb, seq, d), mod.reshape(ns, hb, 6, d), *consts, *cast_src)
    return _mixer_unstream(outs[:3], bsz, seq), outs[3:]


def _lru_body(x_ref, mod_ref, gmix_ref, gffn_ref, win_ref, cw_ref, cb_ref, wr_ref, br_ref,
              wi_ref, bi_ref, lam_ref, wout_ref, rw_cat_ref, rw_hi_ref,
              xo_ref, h2_ref, lg_ref,
              conv_scr, hstate_scr, a_scr, b_scr, gelu_scr):
    ns = MIX_STREAMS
    ts = x_ref.shape[2]
    w = D_MODEL

    @pl.when(pl.program_id(1) == 0)
    def _():
        conv_scr[...] = jnp.zeros_like(conv_scr)
        hstate_scr[...] = jnp.zeros_like(hstate_scr)

    _interleave([_lru_pre_scan(x_ref.at[st], mod_ref.at[st], gmix_ref, win_ref, cw_ref, cb_ref,
                               wr_ref, br_ref, wi_ref, bi_ref, lam_ref, conv_scr.at[st],
                               a_scr.at[st], b_scr.at[st], gelu_scr.at[st])
                 for st in range(ns)], LRU_STAGGER)

    unroll = 4

    def group_step(gi, hprev):
        for u in range(unroll):
            r0 = pl.multiple_of(gi * (unroll * SUBLANES), unroll * SUBLANES) + u * SUBLANES
            nxt = []
            for st in range(ns):
                hs = (a_scr[st, pl.ds(r0, SUBLANES), :] * hprev[st]
                      + b_scr[st, pl.ds(r0, SUBLANES), :])
                b_scr[st, pl.ds(r0, SUBLANES), :] = hs
                nxt.append(hs[SUBLANES - 1:SUBLANES, :])
            hprev = tuple(nxt)
        return hprev

    hlast = tuple(hstate_scr[st] for st in range(ns))
    for gi in range(ts // (unroll * SUBLANES)):
        hlast = group_step(gi, hlast)
    for st in range(ns):
        hstate_scr[st] = hlast[st]

    def post(st):
        y = (b_scr[st] * gelu_scr[st]).astype(jnp.bfloat16)
        yield
        mix = _dot(y, wout_ref[...])
        yield
        yield from _post_mix_pieces(x_ref[st, 0], mix, mod_ref.at[st], gffn_ref, rw_cat_ref,
                                    rw_hi_ref, xo_ref.at[st], h2_ref.at[st], lg_ref.at[st])

    _interleave([post(st) for st in range(ns)], LRU_STAGGER)


def _lru_pre_scan(x_ref, mod_ref, gmix_ref, win_ref, cw_ref, cb_ref, wr_ref, br_ref,
                  wi_ref, bi_ref, lam_ref, conv_scr, a_scr, b_scr, gelu_scr):
    ts = x_ref.shape[1]
    w = D_MODEL
    x = x_ref[0]
    h = _pre_norm(x, mod_ref, gmix_ref, 0, 1).astype(jnp.bfloat16)
    gate_br = _dot(h, win_ref[:, 0:w])
    c0 = 0.7978845608028654
    u = gate_br * (c0 + (c0 * 0.044715) * (gate_br * gate_br))
    half_g = 0.5 * gate_br
    gelu_scr[...] = half_g + half_g * jnp.tanh(u)
    yield
    xb = _dot(h, win_ref[:, w:2 * w])

    ext = jnp.concatenate([conv_scr[...], xb], axis=0)
    xc = cb_ref[...] + xb * cw_ref[CONV_W - 1:CONV_W, :]
    for j in range(CONV_W - 1):
        back = CONV_W - 1 - j
        xc = xc + pltpu.roll(ext, back, 0)[SUBLANES:SUBLANES + ts] * cw_ref[j:j + 1, :]
    conv_scr[...] = xb[ts - SUBLANES:ts]
    yield

    xcb = xc.astype(jnp.bfloat16)
    rs, iis = [], []
    for hd in range(LRU_BLOCKS):
        sl = slice(hd * LRU_BLOCK_W, (hd + 1) * LRU_BLOCK_W)
        rs.append(_dot(xcb[:, sl], wr_ref[hd]))
        iis.append(_dot(xcb[:, sl], wi_ref[hd]))
    t_r = jnp.tanh(0.5 * (jnp.concatenate(rs, axis=1) + br_ref[...]))
    t_i = jnp.tanh(0.5 * (jnp.concatenate(iis, axis=1) + bi_ref[...]))
    yield

    lam = lam_ref[...]
    softplus_neg_lam = jnp.maximum(-lam, 0.0) + jnp.log1p(jnp.exp(-jnp.abs(lam)))
    sp = (-0.5 * LRU_C) * softplus_neg_lam
    log_a = sp * t_r + sp
    a = jnp.exp(log_a)
    half_mult = jnp.sqrt(jnp.tanh(-log_a) * (0.25 * (a * a) + 0.25))
    bb = (xc * half_mult) * (t_i + 1.0)
    yield

    a = a.reshape(ts // SUBLANES, SUBLANES, w)
    bb = bb.reshape(ts // SUBLANES, SUBLANES, w)
    rowi = lax.broadcasted_iota(jnp.int32, a.shape, 1)
    for d in (1, 2, 4):
        keep = rowi >= d
        a_sh = jnp.where(keep, pltpu.roll(a, d, 1), 1.0)
        b_sh = jnp.where(keep, pltpu.roll(bb, d, 1), 0.0)
        bb = a * b_sh + bb
        a = a * a_sh
    a_scr[...] = a.reshape(ts, w)
    b_scr[...] = bb.reshape(ts, w)
    yield


def _interleave(gens, lag):
    live = [True] * len(gens)
    tick = 0
    while any(live):
        for i, gen in enumerate(gens):
            if live[i] and tick >= i * lag:
                try:
                    next(gen)
                except StopIteration:
                    live[i] = False
        tick += 1


def _lru_layer_call(x, mod, gmix, gffn, w_in, conv_w, conv_b, w_r, b_r, w_i, b_i, lam, w_out,
                    rw_cat, rw_hi):
    bsz, seq, d = x.shape
    ts = MIX_TS
    ns = MIX_STREAMS
    hb = bsz // ns
    io_in, out_specs, out_shapes = _mixer_io_specs(bsz, seq, ts)
    consts = (gmix, gffn, w_in, conv_w, conv_b, w_r, b_r, w_i, b_i, lam, w_out,
              rw_cat, rw_hi)
    outs = pl.pallas_call(
        _lru_body,
        grid=(hb, seq // ts),
        in_specs=io_in + [_const_spec(a.shape) for a in consts],
        out_specs=out_specs,
        out_shape=out_shapes,
        scratch_shapes=[
            pltpu.VMEM((ns, SUBLANES, d), jnp.float32),
            pltpu.VMEM((ns, 1, d), jnp.float32),
            pltpu.VMEM((ns, ts, d), jnp.float32),
            pltpu.VMEM((ns, ts, d), jnp.float32),
            pltpu.VMEM((ns, ts, d), jnp.float32),
        ],
        compiler_params=pltpu.CompilerParams(
            dimension_semantics=("arbitrary", "arbitrary"),
            vmem_limit_bytes=VMEM_LIMIT),
        name="lru_layer",
    )(x.reshape(ns, hb, seq, d), mod.reshape(ns, hb, 6, d), *consts)
    return _mixer_unstream(outs, bsz, seq)


def _route_body(lg_ref, rbias_ref, pos0_ref, pos1_ref, w0_ref, off_ref, cnt_ref):
    seq = lg_ref.shape[2]
    blk = ROUTE_BLOCK
    s = _sigmoid_rel(lg_ref[0])
    m, wgt, sub = _route_select(s, rbias_ref[...])
    mf = m.astype(jnp.float32)

    row = lax.broadcasted_iota(jnp.int32, (blk, blk), 0)
    col = lax.broadcasted_iota(jnp.int32, (blk, blk), 1)
    before = (row < col).astype(jnp.bfloat16)
    carry = jnp.zeros((N_EXPERTS, 1), jnp.float32)
    ranks = []
    for i in range(seq // blk):
        mb = mf[:, i * blk:(i + 1) * blk]
        ranks.append(_dot(mb.astype(jnp.bfloat16), before) + carry)
        carry = carry + jnp.sum(mb, axis=1, keepdims=True)
    rnk = jnp.concatenate(ranks, axis=1)
    cnt_ref[0] = jnp.broadcast_to(carry, (N_EXPERTS, LANES)).astype(jnp.int32)

    eidx = sub.astype(jnp.float32)
    e_lo = jnp.min(jnp.where(m, eidx, 99.0), axis=0, keepdims=True)
    e_hi = jnp.max(jnp.where(m, eidx, -1.0), axis=0, keepdims=True)
    is_lo = m & (eidx == e_lo)
    is_hi = m & (eidx == e_hi)
    p_lo = jnp.sum(jnp.where(is_lo, rnk, 0.0), axis=0, keepdims=True)
    p_hi = jnp.sum(jnp.where(is_hi, rnk, 0.0), axis=0, keepdims=True)
    w0_ref[0] = jnp.sum(jnp.where(is_lo, wgt, 0.0), axis=0, keepdims=True)

    off = jnp.zeros((1, 1), jnp.float32)
    for e in range(N_EXPERTS):
        off_ref[0, e:e + 1, :] = jnp.broadcast_to(off, (1, LANES)).astype(jnp.int32)
        p_lo = p_lo + jnp.where(e_lo == float(e), off, 0.0)
        p_hi = p_hi + jnp.where(e_hi == float(e), off, 0.0)
        off = off + jnp.floor((carry[e:e + 1, :] + (SLOT_ALIGN - 1.0)) * (1.0 / SLOT_ALIGN)) * SLOT_ALIGN
    pos0_ref[0] = p_lo.astype(jnp.int32) * ROWS_PER_TOKEN
    pos1_ref[0] = p_hi.astype(jnp.int32) * ROWS_PER_TOKEN


def _route_call(logits, rbias):
    bsz, _, seq = logits.shape
    row_spec = pl.BlockSpec((1, 1, seq), lambda b: (b, 0, 0))
    tab_spec = pl.BlockSpec((1, N_EXPERTS, LANES), lambda b: (b, 0, 0))
    return pl.pallas_call(
        _route_body,
        grid=(bsz,),
        in_specs=[
            pl.BlockSpec((1, N_EXPERTS, seq), lambda b: (b, 0, 0)),
            pl.BlockSpec((N_EXPERTS, 1), lambda b: (0, 0)),
        ],
        out_specs=[row_spec, row_spec, row_spec, tab_spec, tab_spec],
        out_shape=[
            jax.ShapeDtypeStruct((bsz, 1, seq), jnp.int32),
            jax.ShapeDtypeStruct((bsz, 1, seq), jnp.int32),
            jax.ShapeDtypeStruct((bsz, 1, seq), jnp.float32),
            jax.ShapeDtypeStruct((bsz, N_EXPERTS, LANES), jnp.int32),
            jax.ShapeDtypeStruct((bsz, N_EXPERTS, LANES), jnp.int32),
        ],
        compiler_params=pltpu.CompilerParams(dimension_semantics=("arbitrary",)),
        name="moe_route",
    )(logits, rbias)


def _scatter_tokens(pos0_ref, pos1_ref, h2_ref, slots, idx0, lo, cnt):
    R = ROWS_PER_TOKEN
    for u in range(lo, lo + cnt):
        p0 = pos0_ref[0, 0, idx0 + u]
        p1 = pos1_ref[0, 0, idx0 + u]
        val = h2_ref[u * R:(u + 1) * R, :]
        slots[pl.ds(pl.multiple_of(p0, R), R), :] = val
        slots[pl.ds(pl.multiple_of(p1, R), R), :] = val


def _combine_tokens(pos0_ref, pos1_ref, wts_ref, slots, tm_scr, idx0, lo, cnt):
    R = ROWS_PER_TOKEN
    for u in range(lo, lo + cnt):
        p0 = pos0_ref[0, 0, idx0 + u]
        p1 = pos1_ref[0, 0, idx0 + u]
        w0 = wts_ref[0, 0, idx0 + u]
        za = slots[pl.ds(pl.multiple_of(p0, R), R), :]
        zb = slots[pl.ds(pl.multiple_of(p1, R), R), :]
        tm_scr[u * R:(u + 1) * R, :] = zb + w0 * (za - zb)


def _combine_epilogue(tm_scr, x_ref, mod_ref, fin_ref, o_ref, lo, cnt, final_norm):
    R = ROWS_PER_TOKEN
    moe = jnp.concatenate(
        [tm_scr[pl.ds(lo * R + c, cnt, stride=R), :] for c in range(R)], axis=1)
    out = x_ref[lo:lo + cnt, :] + mod_ref[0, 5:6, :] * moe
    if final_norm:
        out = _rms(out, fin_ref[...])
    o_ref[lo:lo + cnt, :] = out


def _expert_tile_hooked(slots, wg_ref, wu_ref, wd_ref, ws, slot0, n_valid, tm, hooks):
    R = ROWS_PER_TOKEN
    nw = 256
    row0 = pl.multiple_of(slot0 * R, SLOT_ALIGN * R)
    xs = [slots[pl.ds(row0 + c, tm, stride=R), :] for c in range(R)]
    xt = jnp.concatenate(xs, axis=1).astype(jnp.bfloat16)
    ok = lax.broadcasted_iota(jnp.int32, (tm, LANES), 0) < n_valid
    hooks = list(hooks)

    def run_hook():
        if hooks:
            hooks.pop(0)()

    he = []
    for i in range(D_EXPERT // nw):
        gate = _dot(xt, wg_ref[ws, :, i * nw:(i + 1) * nw])
        run_hook()
        up = _dot(xt, wu_ref[ws, :, i * nw:(i + 1) * nw])
        run_hook()
        he.append((_silu(gate) * up).astype(jnp.bfloat16))
    he = jnp.concatenate(he, axis=1)
    for i in range(D_MODEL // nw):
        y = _dot(he, wd_ref[ws, :, i * nw:(i + 1) * nw])
        for cc in range(nw // LANES):
            c = i * (nw // LANES) + cc
            slots[pl.ds(row0 + c, tm, stride=R), :] = jnp.where(
                ok, y[:, cc * LANES:(cc + 1) * LANES], xs[c])
        run_hook()
    while hooks:
        run_hook()


def _moe_body(off_sm, cnt_sm, sc0_ref, sc1_ref, cb0_ref, cb1_ref, wts_ref, h2_ref, x_ref,
              mod_ref, fin_ref, wg_hbm, wu_hbm, wd_hbm, o_ref, slots_a, slots_b, tm_scr,
              wg_ref, wu_ref, wd_ref, w_sem, *, n_blocks, layer, final_norm):
    r = pl.program_id(0)
    e = pl.program_id(1)

    g = r * N_EXPERTS + e
    n_steps = (n_blocks + 2) * N_EXPERTS

    def w_needed(step):
        rnd = step // N_EXPERTS
        return (rnd >= 1) & (rnd <= n_blocks) & (step < n_steps)

    def w_copies(step):
        ex = layer * N_EXPERTS + step % N_EXPERTS
        sl = step % MOE_WRING
        return [pltpu.make_async_copy(src.at[ex], dst.at[sl], w_sem.at[i, sl])
                for i, (src, dst) in enumerate(((wg_hbm, wg_ref), (wu_hbm, wu_ref),
                                                (wd_hbm, wd_ref)))]

    def w_start(step):
        @pl.when(w_needed(step))
        def _():
            for cp in w_copies(step):
                cp.start(priority=1)

    @pl.when(g == 0)
    def _():
        for ahead in range(MOE_WRING - 1):
            w_start(g + ahead)

    w_start(g + MOE_WRING - 1)

    @pl.when(w_needed(g))
    def _():
        for cp in w_copies(g):
            cp.wait()

    ws = g % MOE_WRING
    R = ROWS_PER_TOKEN
    U = MOE_UNROLL
    tm = MOE_TM
    mv = MOE_MOVE
    half = N_EXPERTS // 2
    n_hooks = 2 * (D_EXPERT // 256) + D_MODEL // 256
    per_hook = mv // n_hooks

    @pl.when((r == 0) & (e == 0))
    def _():
        slots_a[...] = jnp.zeros_like(slots_a)
        slots_b[...] = jnp.zeros_like(slots_b)

    do_exp = (r >= 1) & (r <= n_blocks)
    do_comb = (r >= 2) & (e < half)
    do_scat = (r < n_blocks) & (e >= half)
    b_exp = jnp.clip(r - 1, 0, n_blocks - 1)
    off = off_sm[b_exp, e]
    n = jnp.where(do_exp, cnt_sm[b_exp, e], 0)
    comb_idx0 = e * mv
    scat_idx0 = (e - half) * mv
    small = tm // 2
    first_rows = jnp.where(n > small, tm, jnp.where(n > 0, small, 0))
    fused = do_comb | do_scat
    start = jnp.where(fused, first_rows, 0)
    rest = jnp.maximum(n - start, 0)
    rest_big = rest // tm + ((rest % tm) > small).astype(jnp.int32)
    rest_tail = rest - rest_big * tm

    def round_body(exp_buf, mov_buf):
        tile = functools.partial(_expert_tile_hooked, exp_buf, wg_ref, wu_ref, wd_ref, ws)

        def comb_hooks():
            def hook(k):
                def run():
                    if k < n_hooks:
                        _combine_tokens(cb0_ref, cb1_ref, wts_ref, mov_buf, tm_scr, comb_idx0,
                                        k * per_hook, per_hook)
                    if k >= 1:
                        _combine_epilogue(tm_scr, x_ref, mod_ref, fin_ref, o_ref,
                                          (k - 1) * per_hook, per_hook, final_norm)
                return run
            return [hook(k) for k in range(n_hooks + 1)]

        def scat_hooks():
            def hook(k):
                return lambda: _scatter_tokens(sc0_ref, sc1_ref, h2_ref, mov_buf, scat_idx0,
                                               k * per_hook, per_hook)
            return [hook(k) for k in range(n_hooks)]

        for rows in (tm, small):
            @pl.when((first_rows == rows) & do_comb)
            def _():
                tile(off, n, rows, comb_hooks())

            @pl.when((first_rows == rows) & do_scat)
            def _():
                tile(off, n, rows, scat_hooks())

        @pl.when(first_rows == 0)
        def _():
            @pl.when(do_comb)
            def _():
                def body(i, carry):
                    i0 = comb_idx0 + U * i
                    for u in range(U):
                        p0 = cb0_ref[0, 0, i0 + u]
                        p1 = cb1_ref[0, 0, i0 + u]
                        w0 = wts_ref[0, 0, i0 + u]
                        za = mov_buf[pl.ds(pl.multiple_of(p0, R), R), :]
                        zb = mov_buf[pl.ds(pl.multiple_of(p1, R), R), :]
                        r0 = pl.multiple_of(i * (U * R), U * R) + u * R
                        tm_scr[pl.ds(r0, R), :] = zb + w0 * (za - zb)
                    return carry

                lax.fori_loop(0, mv // U, body, 0)
                _combine_epilogue(tm_scr, x_ref, mod_ref, fin_ref, o_ref, 0, mv, final_norm)

            @pl.when(do_scat)
            def _():
                def body(i, carry):
                    i0 = scat_idx0 + U * i
                    for u in range(U):
                        p0 = sc0_ref[0, 0, i0 + u]
                        p1 = sc1_ref[0, 0, i0 + u]
                        r0 = pl.multiple_of(i * (U * R), U * R) + u * R
                        val = h2_ref[pl.ds(r0, R), :]
                        mov_buf[pl.ds(pl.multiple_of(p0, R), R), :] = val
                        mov_buf[pl.ds(pl.multiple_of(p1, R), R), :] = val
                    return carry

                lax.fori_loop(0, mv // U, body, 0)

        def tile_body(j, carry):
            done = start + j * tm
            tile(off + done, n - done, tm, [])
            return carry

        lax.fori_loop(0, rest_big, tile_body, 0)

        @pl.when(rest_tail > 0)
        def _():
            done = start + rest_big * tm
            tile(off + done, n - done, small, [])

    @pl.when(r % 2 == 0)
    def _():
        round_body(slots_b, slots_a)

    @pl.when(r % 2 == 1)
    def _():
        round_body(slots_a, slots_b)


def _moe_call(off, cnt, pos0, pos1, w0, h2_tm, x, mod, fin_g, w_gate, w_up, w_down, layer,
              final_norm):
    bsz, seq, d = x.shape
    mv = MOE_MOVE
    half = N_EXPERTS // 2
    assert seq == half * mv
    n_slots = 2 * seq
    last = bsz - 1

    def scat_blk(r, e, *_):
        return (jnp.minimum(r, last) * half + jnp.clip(e - half, 0, half - 1), 0)

    def comb_blk(r, e, *_):
        return (jnp.where(r < 2, 0, (r - 2) * half + jnp.minimum(e, half - 1)), 0)

    def scat_row(r, e, *_):
        return (jnp.minimum(r, last), 0, 0)

    def comb_row(r, e, *_):
        return (jnp.clip(r - 2, 0, last), 0, 0)

    body = functools.partial(_moe_body, n_blocks=bsz, layer=layer, final_norm=final_norm)
    slot_rows = (n_slots + N_EXPERTS * (SLOT_ALIGN - 1) + MOE_TM) * ROWS_PER_TOKEN
    out = pl.pallas_call(
        body,
        grid_spec=pltpu.PrefetchScalarGridSpec(
            num_scalar_prefetch=2,
            grid=(bsz + 2, N_EXPERTS),
            in_specs=[
                pl.BlockSpec((1, 1, seq), scat_row, memory_space=pltpu.SMEM),
                pl.BlockSpec((1, 1, seq), scat_row, memory_space=pltpu.SMEM),
                pl.BlockSpec((1, 1, seq), comb_row, memory_space=pltpu.SMEM),
                pl.BlockSpec((1, 1, seq), comb_row, memory_space=pltpu.SMEM),
                pl.BlockSpec((1, 1, seq), comb_row, memory_space=pltpu.SMEM),
                pl.BlockSpec((mv * ROWS_PER_TOKEN, LANES), scat_blk),
                pl.BlockSpec((mv, d), comb_blk),
                pl.BlockSpec((1, 6, d), comb_row),
                pl.BlockSpec((1, d), lambda r, e, *_: (0, 0)),
                pl.BlockSpec(memory_space=pl.ANY),
                pl.BlockSpec(memory_space=pl.ANY),
                pl.BlockSpec(memory_space=pl.ANY),
            ],
            out_specs=pl.BlockSpec((mv, d), comb_blk),
            scratch_shapes=[
                pltpu.VMEM((slot_rows, LANES), jnp.float32),
                pltpu.VMEM((slot_rows, LANES), jnp.float32),
                pltpu.VMEM((mv * ROWS_PER_TOKEN, LANES), jnp.float32),
                pltpu.VMEM((MOE_WRING, d, D_EXPERT), jnp.bfloat16),
                pltpu.VMEM((MOE_WRING, d, D_EXPERT), jnp.bfloat16),
                pltpu.VMEM((MOE_WRING, D_EXPERT, d), jnp.bfloat16),
                pltpu.SemaphoreType.DMA((3, MOE_WRING)),
            ],
        ),
        out_shape=jax.ShapeDtypeStruct((bsz * seq, d), jnp.float32),
        compiler_params=pltpu.CompilerParams(
            dimension_semantics=("arbitrary", "arbitrary"),
            vmem_limit_bytes=MOE_VMEM_LIMIT),
        name="moe",
    )(off, cnt, pos0, pos1, pos0, pos1, w0, h2_tm, x.reshape(bsz * seq, d), mod, fin_g,
      w_gate, w_up, w_down)
    return out.reshape(bsz, seq, d)


def _moe_layer(x, h2_tm, logits, rbias, mod, fin_g, w_gate, w_up, w_down, layer, final_norm):
    pos0, pos1, w0, off, cnt = _route_call(logits, rbias)
    return _moe_call(off[:, :, 0], cnt[:, :, 0], pos0, pos1, w0, h2_tm, x, mod, fin_g,
                     w_gate, w_up, w_down, layer, final_norm)


def kernel(x, c, gla_w_in, gla_w_gate_up, gla_b_gate, gla_norm_g, gla_w_out, lru_w_in, lru_conv_w, lru_conv_b, lru_w_r, lru_b_r, lru_w_i, lru_b_i, lru_lambda, lru_w_out, router_w, router_bias, moe_w_gate, moe_w_up, moe_w_down, norm_mix_g, norm_ffn_g, ada_w, ada_b, final_norm_g):
    bf = jnp.bfloat16
    depth = ada_w.shape[0]
    bsz = x.shape[0]
    d = D_MODEL
    mod_all = _ada_call(c, ada_w, ada_b).reshape(depth, bsz, 6, d)

    rw_t = router_w.T
    rw_hi = rw_t.astype(bf)
    rw_lo = (rw_t - rw_hi.astype(jnp.float32)).astype(bf)
    rw_cat = jnp.concatenate([rw_hi, rw_lo], axis=0)
    rbias = router_bias.reshape(N_EXPERTS, 1)
    fin_g = final_norm_g.reshape(1, d)
    expert_w = None

    for i in range(depth):
        j = i // 2
        mod = mod_all[i]
        gmix = norm_mix_g[i].reshape(1, d)
        gffn = norm_ffn_g[i].reshape(1, d)
        if i % 2 == 0:
            w_in = jnp.pad(gla_w_in[j], ((0, 0), (0, GLA_RANK_PAD - GLA_GATE_RANK))).astype(bf)
            w_gu = jnp.pad(gla_w_gate_up[j], ((0, GLA_RANK_PAD - GLA_GATE_RANK), (0, 0))).astype(bf)
            cast_src = () if expert_w is not None else tuple(
                w.reshape(-1, w.shape[-1]) for w in (moe_w_gate, moe_w_up, moe_w_down))
            (x, h2_tm, logits), cast = _gla_layer_call(
                x, mod, gmix, gffn, w_in, w_gu, gla_b_gate[j].reshape(1, GLA_QK),
                gla_norm_g[j].reshape(1, GLA_DV), gla_w_out[j].astype(bf),
                rw_cat, rw_hi, cast_src)
            if cast:
                expert_w = (cast[0].reshape(depth * N_EXPERTS, d, D_EXPERT),
                            cast[1].reshape(depth * N_EXPERTS, d, D_EXPERT),
                            cast[2].reshape(depth * N_EXPERTS, D_EXPERT, d))
        else:
            x, h2_tm, logits = _lru_layer_call(
                x, mod, gmix, gffn, lru_w_in[j].astype(bf), lru_conv_w[j],
                lru_conv_b[j].reshape(1, d), lru_w_r[j].astype(bf),
                lru_b_r[j].reshape(1, d), lru_w_i[j].astype(bf), lru_b_i[j].reshape(1, d),
                lru_lambda[j].reshape(1, d), lru_w_out[j].astype(bf), rw_cat, rw_hi)
        x = _moe_layer(x, h2_tm, logits, rbias, mod, fin_g, *expert_w,
                       layer=i, final_norm=(i == depth - 1))
    return x
```

```python
import functools

import jax
import jax.numpy as jnp
from jax import lax
from jax.experimental import pallas as pl
from jax.experimental.pallas import tpu as pltpu

D_MODEL = 1024
CHUNK = 64
EPS = 1e-6

GLA_HEADS = 4
GLA_DK = 128
GLA_DV = 256
GLA_QK = GLA_HEADS * GLA_DK
GLA_VD = GLA_HEADS * GLA_DV
GLA_GATE_RANK = 16
GLA_GATE_TAU = 16.0
GLA_RANK_PAD = 128

LRU_BLOCKS = 4
LRU_BLOCK_W = D_MODEL // LRU_BLOCKS
CONV_W = 4
LRU_C = 8.0

N_EXPERTS = 16
N_GROUPS = 4
EPG = N_EXPERTS // N_GROUPS
D_EXPERT = 512

LANES = 128
SUBLANES = 8
ROWS_PER_TOKEN = D_MODEL // LANES

MIX_TS = 256
MIX_STREAMS = 2
LRU_STAGGER = 1
GLA_STAGGER = 1
ROUTE_BLOCK = 256
SLOT_ALIGN = 4
MOE_TM = 256
MOE_UNROLL = 8
MOE_MOVE = 256
MOE_WRING = 4
VMEM_LIMIT = 56 * 1024 * 1024
MOE_VMEM_LIMIT = 60 * 1024 * 1024


def _dot(a, b):
    return jnp.dot(a, b, preferred_element_type=jnp.float32)


def _dot_nt(a, b):
    return lax.dot_general(a, b, (((1,), (1,)), ((), ())),
                           preferred_element_type=jnp.float32)


def _dot_tn(a, b):
    return lax.dot_general(a, b, (((0,), (0,)), ((), ())),
                           preferred_element_type=jnp.float32)


def _split_bf16(x):
    hi = x.astype(jnp.bfloat16)
    lo = (x - hi.astype(jnp.float32)).astype(jnp.bfloat16)
    return hi, lo


def _sigmoid(x):
    return 0.5 * jnp.tanh(0.5 * x) + 0.5


def _sigmoid_rel(x):
    return 1.0 / (1.0 + jnp.exp(-x))


def _silu(x):
    return x * _sigmoid(x)


def _rms(x, g):
    inv = lax.rsqrt(jnp.mean(x * x, axis=-1, keepdims=True) + EPS)
    return x * inv * g


def _ada_body(c_ref, w_ref, b_ref, o_ref):
    c_hi, c_lo = _split_bf16(_silu(c_ref[...]))
    w_hi, w_lo = _split_bf16(w_ref[0])
    o_ref[0] = _dot(c_hi, w_hi) + (_dot(c_lo, w_hi) + _dot(c_hi, w_lo)) + b_ref[0]


def _ada_call(c, ada_w, ada_b):
    depth, d, n = ada_w.shape
    bsz = c.shape[0]
    tn = 1024
    return pl.pallas_call(
        _ada_body,
        grid=(depth, n // tn),
        in_specs=[
            pl.BlockSpec((bsz, d), lambda l, j: (0, 0)),
            pl.BlockSpec((1, d, tn), lambda l, j: (l, 0, j)),
            pl.BlockSpec((1, 1, tn), lambda l, j: (l, 0, j)),
        ],
        out_specs=pl.BlockSpec((1, bsz, tn), lambda l, j: (l, 0, j)),
        out_shape=jax.ShapeDtypeStruct((depth, bsz, n), jnp.float32),
        compiler_params=pltpu.CompilerParams(
            dimension_semantics=("arbitrary", "arbitrary"),
            vmem_limit_bytes=VMEM_LIMIT),
        name="ada",
    )(c, ada_w, ada_b.reshape(depth, 1, n))


def _pre_norm(x, mod_ref, g_ref, shift_row, scale_row):
    shift = mod_ref[0, shift_row:shift_row + 1, :]
    scale = mod_ref[0, scale_row:scale_row + 1, :]
    inv = lax.rsqrt(jnp.mean(x * x, axis=-1, keepdims=True) + EPS)
    return (x * inv) * (g_ref[...] * (1.0 + scale)) + shift


def _group_partner(x, k, sub):
    n = x.shape[0]
    fwd = pltpu.roll(x, n - k, 0)
    back = pltpu.roll(x, EPG - k, 0)
    wrapped = (sub % EPG) + k >= EPG
    return jnp.where(wrapped, back, fwd), wrapped


def _router_logits(h2, rw_cat_ref, rw_hi_ref):
    h_hi, h_lo = _split_bf16(h2)
    p1 = _dot_nt(rw_cat_ref[...], h_hi)
    p2 = _dot_nt(rw_hi_ref[...], h_lo)
    return p1[0:N_EXPERTS] + p1[N_EXPERTS:2 * N_EXPERTS] + p2


def _route_select(s, rbias):
    n = s.shape[1]
    sel = s + rbias
    sub = lax.broadcasted_iota(jnp.int32, (N_EXPERTS, n), 0)

    pair_best = None
    rank = jnp.zeros((N_EXPERTS, n), jnp.float32)
    for k in range(1, EPG):
        p, wrapped = _group_partner(sel, k, sub)
        ps = sel + p
        pair_best = ps if pair_best is None else jnp.maximum(pair_best, ps)
        ahead = (p > sel) | ((p == sel) & wrapped)
        rank = rank + ahead.astype(jnp.float32)
    gscore = pair_best
    for k in range(1, EPG):
        p, _ = _group_partner(pair_best, k, sub)
        gscore = jnp.maximum(gscore, p)
    chosen = jnp.ones((N_EXPERTS, n), jnp.bool_)
    grp = sub // EPG
    for j in range(1, N_GROUPS):
        other = pltpu.roll(gscore, N_EXPERTS - EPG * j, 0)
        other_is_later = grp + j < N_GROUPS
        chosen = chosen & ((gscore > other) | ((gscore == other) & other_is_later))
    m = chosen & (rank < 2.0)
    sm = s * m.astype(jnp.float32)
    wgt = sm / jnp.sum(sm, axis=0, keepdims=True)
    return m, wgt, sub


def _post_mix_pieces(x, mix, mod_ref, gffn_ref, rw_cat_ref, rw_hi_ref, xo_ref, h2_ref, lg_ref):
    ts = x.shape[0]
    x_new = x + mod_ref[0, 2:3, :] * mix
    xo_ref[0] = x_new
    h2 = _pre_norm(x_new, mod_ref, gffn_ref, 3, 4)
    yield
    for c in range(ROWS_PER_TOKEN):
        h2_ref[pl.ds(c, ts, stride=ROWS_PER_TOKEN), :] = h2[:, c * LANES:(c + 1) * LANES]
    yield
    lg_ref[0] = _router_logits(h2, rw_cat_ref, rw_hi_ref)
    yield


def _mixer_io_specs(bsz, seq, ts):
    ns = MIX_STREAMS
    hb = bsz // ns
    nt = seq // ts
    in_specs = [
        pl.BlockSpec((ns, 1, ts, D_MODEL), lambda b, t: (0, b, t, 0)),
        pl.BlockSpec((ns, 1, 6, D_MODEL), lambda b, t: (0, b, 0, 0)),
    ]
    out_specs = [
        pl.BlockSpec((ns, 1, ts, D_MODEL), lambda b, t: (0, b, t, 0)),
        pl.BlockSpec((ns, ts * ROWS_PER_TOKEN, LANES), lambda b, t: (0, b * nt + t, 0)),
        pl.BlockSpec((ns, 1, N_EXPERTS, ts), lambda b, t: (0, b, 0, t)),
    ]
    out_shapes = [
        jax.ShapeDtypeStruct((ns, hb, seq, D_MODEL), jnp.float32),
        jax.ShapeDtypeStruct((ns, hb * seq * ROWS_PER_TOKEN, LANES), jnp.float32),
        jax.ShapeDtypeStruct((ns, hb, N_EXPERTS, seq), jnp.float32),
    ]
    return in_specs, out_specs, out_shapes


def _mixer_unstream(outs, bsz, seq):
    x, h2_tm, logits = outs
    return (x.reshape(bsz, seq, D_MODEL), h2_tm.reshape(bsz * seq * ROWS_PER_TOKEN, LANES),
            logits.reshape(bsz, N_EXPERTS, seq))


def _const_spec(shape):
    nd = len(shape)
    return pl.BlockSpec(shape, lambda b, t: (0,) * nd)


def _gla_body(x_ref, mod_ref, gmix_ref, gffn_ref, win_ref, wgu_ref, bg_ref, ng_ref,
              wout_ref, rw_cat_ref, rw_hi_ref, *rest, n_cast):
    cast_in = rest[:n_cast]
    xo_ref, h2_ref, lg_ref = rest[n_cast:n_cast + 3]
    cast_out = rest[n_cast + 3:2 * n_cast + 3]
    state_scr, o_scr = rest[2 * n_cast + 3:]

    @pl.when(pl.program_id(1) == 0)
    def _():
        state_scr[...] = jnp.zeros_like(state_scr)

    def caster():
        pieces = 4
        for src, dst in zip(cast_in, cast_out):
            rows = src.shape[0] // pieces
            for i in range(pieces):
                dst[i * rows:(i + 1) * rows, :] = src[i * rows:(i + 1) * rows, :].astype(jnp.bfloat16)
                yield

    gens = [_gla_stream(x_ref.at[st], mod_ref.at[st], gmix_ref, gffn_ref, win_ref, wgu_ref,
                        bg_ref, ng_ref, wout_ref, rw_cat_ref, rw_hi_ref,
                        xo_ref.at[st], h2_ref.at[st], lg_ref.at[st],
                        state_scr.at[st], o_scr.at[st])
            for st in range(MIX_STREAMS)]
    _interleave(gens + [caster()], GLA_STAGGER)


def _gla_stream(x_ref, mod_ref, gmix_ref, gffn_ref, win_ref, wgu_ref, bg_ref, ng_ref,
                wout_ref, rw_cat_ref, rw_hi_ref,
                xo_ref, h2_ref, lg_ref, state_scr, o_scr):
    ts = x_ref.shape[1]
    nchunk = ts // CHUNK
    half = GLA_VD // 2
    x = x_ref[0]
    h = _pre_norm(x, mod_ref, gmix_ref, 0, 1).astype(jnp.bfloat16)
    yield

    o_q, o_k, o_v, o_g, o_a = 0, GLA_QK, 2 * GLA_QK, 2 * GLA_QK + GLA_VD, 2 * GLA_QK + 2 * GLA_VD
    q = (_dot(h, win_ref[:, o_q:o_k]) * (GLA_DK ** -0.5)).astype(jnp.bfloat16)
    yield
    k = _dot(h, win_ref[:, o_k:o_v])
    yield
    v = jnp.concatenate(
        [_dot(h, win_ref[:, o_v + i * half:o_v + (i + 1) * half]).astype(jnp.bfloat16)
         for i in range(2)], axis=1)
    yield
    a_lr = _dot(h, win_ref[:, o_a:o_a + GLA_RANK_PAD])
    a_hi, a_lo = _split_bf16(a_lr)
    z2 = _dot(jnp.concatenate([a_hi, a_lo], axis=0), wgu_ref[...])
    z = z2[0:ts] + z2[ts:2 * ts] + bg_ref[...]
    log_a = -(jnp.maximum(-z, 0.0) + jnp.log1p(jnp.exp(-jnp.abs(z)))) * (1.0 / GLA_GATE_TAU)
    yield

    row = lax.broadcasted_iota(jnp.int32, (ts, ts), 0)
    col = lax.broadcasted_iota(jnp.int32, (ts, ts), 1)
    tri = ((row // CHUNK == col // CHUNK) & (col <= row)).astype(jnp.bfloat16)
    l_hi, l_lo = _split_bf16(log_a)
    cum2 = _dot(tri, jnp.concatenate([l_hi, l_lo], axis=1))
    cum = cum2[:, 0:GLA_QK] + cum2[:, GLA_QK:2 * GLA_QK]
    yield

    ng = ng_ref[...]
    kv_t, gammas = [], []
    for j in range(nchunk):
        r0 = j * CHUNK
        cum_j = cum[r0:r0 + CHUNK]
        total = cum_j[CHUNK - 1:CHUNK]
        k_dec = (k[r0:r0 + CHUNK] * jnp.exp(total - cum_j)).astype(jnp.bfloat16)
        gammas.append(jnp.exp(total))
        for hd in range(GLA_HEADS):
            ks = slice(hd * GLA_DK, (hd + 1) * GLA_DK)
            vs = slice(hd * GLA_DV, (hd + 1) * GLA_DV)
            kv_t.append(_dot_tn(v[r0:r0 + CHUNK, vs], k_dec[:, ks]))
        yield
    states = []
    for hd in range(GLA_HEADS):
        ks = slice(hd * GLA_DK, (hd + 1) * GLA_DK)
        s_cur = state_scr[hd]
        for j in range(nchunk):
            s_cur = s_cur * gammas[j][:, ks] + kv_t[j * GLA_HEADS + hd]
            states.append(s_cur.astype(jnp.bfloat16))
        state_scr[hd] = s_cur
        yield
    for hd in range(GLA_HEADS):
        ks = slice(hd * GLA_DK, (hd + 1) * GLA_DK)
        vs = slice(hd * GLA_DV, (hd + 1) * GLA_DV)
        for j in range(nchunk):
            r0 = j * CHUNK
            o = _dot_nt(q[r0:r0 + CHUNK, ks], states[hd * nchunk + j])
            o_scr[r0:r0 + CHUNK, vs] = _rms(o, ng)
        yield

    og = []
    for i in range(2):
        g = _dot(h, win_ref[:, o_g + i * half:o_g + (i + 1) * half])
        og.append((o_scr[:, i * half:(i + 1) * half] * _silu(g)).astype(jnp.bfloat16))
        yield
    og = jnp.concatenate(og, axis=1)
    x_new = []
    for i in range(2):
        cols = slice(i * half, (i + 1) * half)
        mix = _dot(og, wout_ref[:, i * half:(i + 1) * half])
        x_new.append(x[:, cols] + mod_ref[0, 2:3, cols] * mix)
        yield
    x_new = jnp.concatenate(x_new, axis=1)
    xo_ref[0] = x_new
    h2 = _pre_norm(x_new, mod_ref, gffn_ref, 3, 4)
    for c in range(ROWS_PER_TOKEN):
        h2_ref[pl.ds(c, ts, stride=ROWS_PER_TOKEN), :] = h2[:, c * LANES:(c + 1) * LANES]
    yield
    lg_ref[0] = _router_logits(h2, rw_cat_ref, rw_hi_ref)
    yield


def _gla_layer_call(x, mod, gmix, gffn, w_in, w_gu, b_g, n_g, w_out, rw_cat, rw_hi, cast_src):
    bsz, seq, d = x.shape
    ts = MIX_TS
    ns = MIX_STREAMS
    hb = bsz // ns
    nt = seq // ts
    io_in, out_specs, out_shapes = _mixer_io_specs(bsz, seq, ts)
    consts = (gmix, gffn, w_in, w_gu, b_g, n_g, w_out, rw_cat, rw_hi)
    cast_specs, cast_shapes = [], []
    for a in cast_src:
        rows = a.shape[0] // (hb * nt)
        assert rows * hb * nt == a.shape[0]
        cast_specs.append(pl.BlockSpec((rows, a.shape[1]), lambda b, t: (b * nt + t, 0)))
        cast_shapes.append(jax.ShapeDtypeStruct(a.shape, jnp.bfloat16))
    outs = pl.pallas_call(
        functools.partial(_gla_body, n_cast=len(cast_src)),
        grid=(hb, nt),
        in_specs=io_in + [_const_spec(a.shape) for a in consts] + cast_specs,
        out_specs=out_specs + cast_specs,
        out_shape=out_shapes + cast_shapes,
        scratch_shapes=[
            pltpu.VMEM((ns, GLA_HEADS, GLA_DV, GLA_DK), jnp.float32),
            pltpu.VMEM((ns, ts, GLA_VD), jnp.float32),
        ],
        compiler_params=pltpu.CompilerParams(
            dimension_semantics=("arbitrary", "arbitrary"),
            vmem_limit_bytes=VMEM_LIMIT),
        name="gla_layer",
    )(x.reshape(ns, hb, seq, d), mod.reshape(ns, hb, 6, d), *consts, *cast_src)
    return _mixer_unstream(outs[:3], bsz, seq), outs[3:]


def _lru_body(x_ref, mod_ref, gmix_ref, gffn_ref, win_ref, cw_ref, cb_ref, wr_ref, br_ref,
              wi_ref, bi_ref, lam_ref, wout_ref, rw_cat_ref, rw_hi_ref,
              xo_ref, h2_ref, lg_ref,
              conv_scr, hstate_scr, a_scr, b_scr, gelu_scr):
    ns = MIX_STREAMS
    ts = x_ref.shape[2]
    w = D_MODEL

    @pl.when(pl.program_id(1) == 0)
    def _():
        conv_scr[...] = jnp.zeros_like(conv_scr)
        hstate_scr[...] = jnp.zeros_like(hstate_scr)

    _interleave([_lru_pre_scan(x_ref.at[st], mod_ref.at[st], gmix_ref, win_ref, cw_ref, cb_ref,
                               wr_ref, br_ref, wi_ref, bi_ref, lam_ref, conv_scr.at[st],
                               a_scr.at[st], b_scr.at[st], gelu_scr.at[st])
                 for st in range(ns)], LRU_STAGGER)

    unroll = 4

    def group_step(gi, hprev):
        for u in range(unroll):
            r0 = pl.multiple_of(gi * (unroll * SUBLANES), unroll * SUBLANES) + u * SUBLANES
            nxt = []
            for st in range(ns):
                hs = (a_scr[st, pl.ds(r0, SUBLANES), :] * hprev[st]
                      + b_scr[st, pl.ds(r0, SUBLANES), :])
                b_scr[st, pl.ds(r0, SUBLANES), :] = hs
                nxt.append(hs[SUBLANES - 1:SUBLANES, :])
            hprev = tuple(nxt)
        return hprev

    hlast = tuple(hstate_scr[st] for st in range(ns))
    for gi in range(ts // (unroll * SUBLANES)):
        hlast = group_step(gi, hlast)
    for st in range(ns):
        hstate_scr[st] = hlast[st]

    def post(st):
        y = (b_scr[st] * gelu_scr[st]).astype(jnp.bfloat16)
        yield
        mix = _dot(y, wout_ref[...])
        yield
        yield from _post_mix_pieces(x_ref[st, 0], mix, mod_ref.at[st], gffn_ref, rw_cat_ref,
                                    rw_hi_ref, xo_ref.at[st], h2_ref.at[st], lg_ref.at[st])

    _interleave([post(st) for st in range(ns)], LRU_STAGGER)


def _lru_pre_scan(x_ref, mod_ref, gmix_ref, win_ref, cw_ref, cb_ref, wr_ref, br_ref,
                  wi_ref, bi_ref, lam_ref, conv_scr, a_scr, b_scr, gelu_scr):
    ts = x_ref.shape[1]
    w = D_MODEL
    x = x_ref[0]
    h = _pre_norm(x, mod_ref, gmix_ref, 0, 1).astype(jnp.bfloat16)
    gate_br = _dot(h, win_ref[:, 0:w])
    c0 = 0.7978845608028654
    u = gate_br * (c0 + (c0 * 0.044715) * (gate_br * gate_br))
    half_g = 0.5 * gate_br
    gelu_scr[...] = half_g + half_g * jnp.tanh(u)
    yield
    xb = _dot(h, win_ref[:, w:2 * w])

    ext = jnp.concatenate([conv_scr[...], xb], axis=0)
    xc = cb_ref[...] + xb * cw_ref[CONV_W - 1:CONV_W, :]
    for j in range(CONV_W - 1):
        back = CONV_W - 1 - j
        xc = xc + pltpu.roll(ext, back, 0)[SUBLANES:SUBLANES + ts] * cw_ref[j:j + 1, :]
    conv_scr[...] = xb[ts - SUBLANES:ts]
    yield

    xcb = xc.astype(jnp.bfloat16)
    rs, iis = [], []
    for hd in range(LRU_BLOCKS):
        sl = slice(hd * LRU_BLOCK_W, (hd + 1) * LRU_BLOCK_W)
        rs.append(_dot(xcb[:, sl], wr_ref[hd]))
        iis.append(_dot(xcb[:, sl], wi_ref[hd]))
    t_r = jnp.tanh(0.5 * (jnp.concatenate(rs, axis=1) + br_ref[...]))
    t_i = jnp.tanh(0.5 * (jnp.concatenate(iis, axis=1) + bi_ref[...]))
    yield

    lam = lam_ref[...]
    softplus_neg_lam = jnp.maximum(-lam, 0.0) + jnp.log1p(jnp.exp(-jnp.abs(lam)))
    sp = (-0.5 * LRU_C) * softplus_neg_lam
    log_a = sp * t_r + sp
    a = jnp.exp(log_a)
    half_mult = jnp.sqrt(jnp.tanh(-log_a) * (0.25 * (a * a) + 0.25))
    bb = (xc * half_mult) * (t_i + 1.0)
    yield

    a = a.reshape(ts // SUBLANES, SUBLANES, w)
    bb = bb.reshape(ts // SUBLANES, SUBLANES, w)
    rowi = lax.broadcasted_iota(jnp.int32, a.shape, 1)
    for d in (1, 2, 4):
        keep = rowi >= d
        a_sh = jnp.where(keep, pltpu.roll(a, d, 1), 1.0)
        b_sh = jnp.where(keep, pltpu.roll(bb, d, 1), 0.0)
        bb = a * b_sh + bb
        a = a * a_sh
    a_scr[...] = a.reshape(ts, w)
    b_scr[...] = bb.reshape(ts, w)
    yield


def _interleave(gens, lag):
    live = [True] * len(gens)
    tick = 0
    while any(live):
        for i, gen in enumerate(gens):
            if live[i] and tick >= i * lag:
                try:
                    next(gen)
                except StopIteration:
                    live[i] = False
        tick += 1


def _lru_layer_call(x, mod, gmix, gffn, w_in, conv_w, conv_b, w_r, b_r, w_i, b_i, lam, w_out,
                    rw_cat, rw_hi):
    bsz, seq, d = x.shape
    ts = MIX_TS
    ns = MIX_STREAMS
    hb = bsz // ns
    io_in, out_specs, out_shapes = _mixer_io_specs(bsz, seq, ts)
    consts = (gmix, gffn, w_in, conv_w, conv_b, w_r, b_r, w_i, b_i, lam, w_out,
              rw_cat, rw_hi)
    outs = pl.pallas_call(
        _lru_body,
        grid=(hb, seq // ts),
        in_specs=io_in + [_const_spec(a.shape) for a in consts],
        out_specs=out_specs,
        out_shape=out_shapes,
        scratch_shapes=[
            pltpu.VMEM((ns, SUBLANES, d), jnp.float32),
            pltpu.VMEM((ns, 1, d), jnp.float32),
            pltpu.VMEM((ns, ts, d), jnp.float32),
            pltpu.VMEM((ns, ts, d), jnp.float32),
            pltpu.VMEM((ns, ts, d), jnp.float32),
        ],
        compiler_params=pltpu.CompilerParams(
            dimension_semantics=("arbitrary", "arbitrary"),
            vmem_limit_bytes=VMEM_LIMIT),
        name="lru_layer",
    )(x.reshape(ns, hb, seq, d), mod.reshape(ns, hb, 6, d), *consts)
    return _mixer_unstream(outs, bsz, seq)


def _route_body(lg_ref, rbias_ref, pos0_ref, pos1_ref, w0_ref, off_ref, cnt_ref):
    seq = lg_ref.shape[2]
    blk = ROUTE_BLOCK
    s = _sigmoid_rel(lg_ref[0])
    m, wgt, sub = _route_select(s, rbias_ref[...])
    mf = m.astype(jnp.float32)

    row = lax.broadcasted_iota(jnp.int32, (blk, blk), 0)
    col = lax.broadcasted_iota(jnp.int32, (blk, blk), 1)
    before = (row < col).astype(jnp.bfloat16)
    carry = jnp.zeros((N_EXPERTS, 1), jnp.float32)
    ranks = []
    for i in range(seq // blk):
        mb = mf[:, i * blk:(i + 1) * blk]
        ranks.append(_dot(mb.astype(jnp.bfloat16), before) + carry)
        carry = carry + jnp.sum(mb, axis=1, keepdims=True)
    rnk = jnp.concatenate(ranks, axis=1)
    cnt_ref[0] = jnp.broadcast_to(carry, (N_EXPERTS, LANES)).astype(jnp.int32)

    eidx = sub.astype(jnp.float32)
    e_lo = jnp.min(jnp.where(m, eidx, 99.0), axis=0, keepdims=True)
    e_hi = jnp.max(jnp.where(m, eidx, -1.0), axis=0, keepdims=True)
    is_lo = m & (eidx == e_lo)
    is_hi = m & (eidx == e_hi)
    p_lo = jnp.sum(jnp.where(is_lo, rnk, 0.0), axis=0, keepdims=True)
    p_hi = jnp.sum(jnp.where(is_hi, rnk, 0.0), axis=0, keepdims=True)
    w0_ref[0] = jnp.sum(jnp.where(is_lo, wgt, 0.0), axis=0, keepdims=True)

    off = jnp.zeros((1, 1), jnp.float32)
    for e in range(N_EXPERTS):
        off_ref[0, e:e + 1, :] = jnp.broadcast_to(off, (1, LANES)).astype(jnp.int32)
        p_lo = p_lo + jnp.where(e_lo == float(e), off, 0.0)
        p_hi = p_hi + jnp.where(e_hi == float(e), off, 0.0)
        off = off + jnp.floor((carry[e:e + 1, :] + (SLOT_ALIGN - 1.0)) * (1.0 / SLOT_ALIGN)) * SLOT_ALIGN
    pos0_ref[0] = p_lo.astype(jnp.int32) * ROWS_PER_TOKEN
    pos1_ref[0] = p_hi.astype(jnp.int32) * ROWS_PER_TOKEN


def _route_call(logits, rbias):
    bsz, _, seq = logits.shape
    row_spec = pl.BlockSpec((1, 1, seq), lambda b: (b, 0, 0))
    tab_spec = pl.BlockSpec((1, N_EXPERTS, LANES), lambda b: (b, 0, 0))
    return pl.pallas_call(
        _route_body,
        grid=(bsz,),
        in_specs=[
            pl.BlockSpec((1, N_EXPERTS, seq), lambda b: (b, 0, 0)),
            pl.BlockSpec((N_EXPERTS, 1), lambda b: (0, 0)),
        ],
        out_specs=[row_spec, row_spec, row_spec, tab_spec, tab_spec],
        out_shape=[
            jax.ShapeDtypeStruct((bsz, 1, seq), jnp.int32),
            jax.ShapeDtypeStruct((bsz, 1, seq), jnp.int32),
            jax.ShapeDtypeStruct((bsz, 1, seq), jnp.float32),
            jax.ShapeDtypeStruct((bsz, N_EXPERTS, LANES), jnp.int32),
            jax.ShapeDtypeStruct((bsz, N_EXPERTS, LANES), jnp.int32),
        ],
        compiler_params=pltpu.CompilerParams(dimension_semantics=("arbitrary",)),
        name="moe_route",
    )(logits, rbias)


def _scatter_tokens(pos0_ref, pos1_ref, h2_ref, slots, idx0, lo, cnt):
    R = ROWS_PER_TOKEN
    for u in range(lo, lo + cnt):
        p0 = pos0_ref[0, 0, idx0 + u]
        p1 = pos1_ref[0, 0, idx0 + u]
        val = h2_ref[u * R:(u + 1) * R, :]
        slots[pl.ds(pl.multiple_of(p0, R), R), :] = val
        slots[pl.ds(pl.multiple_of(p1, R), R), :] = val


def _combine_tokens(pos0_ref, pos1_ref, wts_ref, slots, tm_scr, idx0, lo, cnt):
    R = ROWS_PER_TOKEN
    for u in range(lo, lo + cnt):
        p0 = pos0_ref[0, 0, idx0 + u]
        p1 = pos1_ref[0, 0, idx0 + u]
        w0 = wts_ref[0, 0, idx0 + u]
        za = slots[pl.ds(pl.multiple_of(p0, R), R), :]
        zb = slots[pl.ds(pl.multiple_of(p1, R), R), :]
        tm_scr[u * R:(u + 1) * R, :] = zb + w0 * (za - zb)


def _combine_epilogue(tm_scr, x_ref, mod_ref, fin_ref, o_ref, lo, cnt, final_norm):
    R = ROWS_PER_TOKEN
    moe = jnp.concatenate(
        [tm_scr[pl.ds(lo * R + c, cnt, stride=R), :] for c in range(R)], axis=1)
    out = x_ref[lo:lo + cnt, :] + mod_ref[0, 5:6, :] * moe
    if final_norm:
        out = _rms(out, fin_ref[...])
    o_ref[lo:lo + cnt, :] = out


def _expert_tile_hooked(slots, wg_ref, wu_ref, wd_ref, ws, slot0, n_valid, tm, hooks):
    R = ROWS_PER_TOKEN
    nw = 256
    row0 = pl.multiple_of(slot0 * R, SLOT_ALIGN * R)
    xs = [slots[pl.ds(row0 + c, tm, stride=R), :] for c in range(R)]
    xt = jnp.concatenate(xs, axis=1).astype(jnp.bfloat16)
    ok = lax.broadcasted_iota(jnp.int32, (tm, LANES), 0) < n_valid
    hooks = list(hooks)

    def run_hook():
        if hooks:
            hooks.pop(0)()

    he = []
    for i in range(D_EXPERT // nw):
        gate = _dot(xt, wg_ref[ws, :, i * nw:(i + 1) * nw])
        run_hook()
        up = _dot(xt, wu_ref[ws, :, i * nw:(i + 1) * nw])
        run_hook()
        he.append((_silu(gate) * up).astype(jnp.bfloat16))
    he = jnp.concatenate(he, axis=1)
    for i in range(D_MODEL // nw):
        y = _dot(he, wd_ref[ws, :, i * nw:(i + 1) * nw])
        for cc in range(nw // LANES):
            c = i * (nw // LANES) + cc
            slots[pl.ds(row0 + c, tm, stride=R), :] = jnp.where(
                ok, y[:, cc * LANES:(cc + 1) * LANES], xs[c])
        run_hook()
    while hooks:
        run_hook()


def _moe_body(off_sm, cnt_sm, sc0_ref, sc1_ref, cb0_ref, cb1_ref, wts_ref, h2_ref, x_ref,
              mod_ref, fin_ref, wg_hbm, wu_hbm, wd_hbm, o_ref, slots_a, slots_b, tm_scr,
              wg_ref, wu_ref, wd_ref, w_sem, *, n_blocks, layer, final_norm):
    r = pl.program_id(0)
    e = pl.program_id(1)

    g = r * N_EXPERTS + e
    n_steps = (n_blocks + 2) * N_EXPERTS

    def w_needed(step):
        rnd = step // N_EXPERTS
        return (rnd >= 1) & (rnd <= n_blocks) & (step < n_steps)

    def w_copies(step):
        ex = layer * N_EXPERTS + step % N_EXPERTS
        sl = step % MOE_WRING
        return [pltpu.make_async_copy(src.at[ex], dst.at[sl], w_sem.at[i, sl])
                for i, (src, dst) in enumerate(((wg_hbm, wg_ref), (wu_hbm, wu_ref),
                                                (wd_hbm, wd_ref)))]

    def w_start(step):
        @pl.when(w_needed(step))
        def _():
            for cp in w_copies(step):
                cp.start(priority=1)

    @pl.when(g == 0)
    def _():
        for ahead in range(MOE_WRING - 1):
            w_start(g + ahead)

    w_start(g + MOE_WRING - 1)

    @pl.when(w_needed(g))
    def _():
        for cp in w_copies(g):
            cp.wait()

    ws = g % MOE_WRING
    R = ROWS_PER_TOKEN
    U = MOE_UNROLL
    tm = MOE_TM
    mv = MOE_MOVE
    half = N_EXPERTS // 2
    n_hooks = 2 * (D_EXPERT // 256) + D_MODEL // 256
    per_hook = mv // n_hooks

    @pl.when((r == 0) & (e == 0))
    def _():
        slots_a[...] = jnp.zeros_like(slots_a)
        slots_b[...] = jnp.zeros_like(slots_b)

    do_exp = (r >= 1) & (r <= n_blocks)
    do_comb = (r >= 2) & (e < half)
    do_scat = (r < n_blocks) & (e >= half)
    b_exp = jnp.clip(r - 1, 0, n_blocks - 1)
    off = off_sm[b_exp, e]
    n = jnp.where(do_exp, cnt_sm[b_exp, e], 0)
    comb_idx0 = e * mv
    scat_idx0 = (e - half) * mv
    small = tm // 2
    first_rows = jnp.where(n > small, tm, jnp.where(n > 0, small, 0))
    fused = do_comb | do_scat
    start = jnp.where(fused, first_rows, 0)
    rest = jnp.maximum(n - start, 0)
    rest_big = rest // tm + ((rest % tm) > small).astype(jnp.int32)
    rest_tail = rest - rest_big * tm

    def round_body(exp_buf, mov_buf):
        tile = functools.partial(_expert_tile_hooked, exp_buf, wg_ref, wu_ref, wd_ref, ws)

        def comb_hooks():
            def hook(k):
                def run():
                    if k < n_hooks:
                        _combine_tokens(cb0_ref, cb1_ref, wts_ref, mov_buf, tm_scr, comb_idx0,
                                        k * per_hook, per_hook)
                    if k >= 1:
                        _combine_epilogue(tm_scr, x_ref, mod_ref, fin_ref, o_ref,
                                          (k - 1) * per_hook, per_hook, final_norm)
                return run
            return [hook(k) for k in range(n_hooks + 1)]

        def scat_hooks():
            def hook(k):
                return lambda: _scatter_tokens(sc0_ref, sc1_ref, h2_ref, mov_buf, scat_idx0,
                                               k * per_hook, per_hook)
            return [hook(k) for k in range(n_hooks)]

        for rows in (tm, small):
            @pl.when((first_rows == rows) & do_comb)
            def _():
                tile(off, n, rows, comb_hooks())

            @pl.when((first_rows == rows) & do_scat)
            def _():
                tile(off, n, rows, scat_hooks())

        @pl.when(first_rows == 0)
        def _():
            @pl.when(do_comb)
            def _():
                def body(i, carry):
                    i0 = comb_idx0 + U * i
                    for u in range(U):
                        p0 = cb0_ref[0, 0, i0 + u]
                        p1 = cb1_ref[0, 0, i0 + u]
                        w0 = wts_ref[0, 0, i0 + u]
                        za = mov_buf[pl.ds(pl.multiple_of(p0, R), R), :]
                        zb = mov_buf[pl.ds(pl.multiple_of(p1, R), R), :]
                        r0 = pl.multiple_of(i * (U * R), U * R) + u * R
                        tm_scr[pl.ds(r0, R), :] = zb + w0 * (za - zb)
                    return carry

                lax.fori_loop(0, mv // U, body, 0)
                _combine_epilogue(tm_scr, x_ref, mod_ref, fin_ref, o_ref, 0, mv, final_norm)

            @pl.when(do_scat)
            def _():
                def body(i, carry):
                    i0 = scat_idx0 + U * i
                    for u in range(U):
                        p0 = sc0_ref[0, 0, i0 + u]
                        p1 = sc1_ref[0, 0, i0 + u]
                        r0 = pl.multiple_of(i * (U * R), U * R) + u * R
                        val = h2_ref[pl.ds(r0, R), :]
                        mov_buf[pl.ds(pl.multiple_of(p0, R), R), :] = val
                        mov_buf[pl.ds(pl.multiple_of(p1, R), R), :] = val
                    return carry

                lax.fori_loop(0, mv // U, body, 0)

        def tile_body(j, carry):
            done = start + j * tm
            tile(off + done, n - done, tm, [])
            return carry

        lax.fori_loop(0, rest_big, tile_body, 0)

        @pl.when(rest_tail > 0)
        def _():
            done = start + rest_big * tm
            tile(off + done, n - done, small, [])

    @pl.when(r % 2 == 0)
    def _():
        round_body(slots_b, slots_a)

    @pl.when(r % 2 == 1)
    def _():
        round_body(slots_a, slots_b)


def _moe_call(off, cnt, pos0, pos1, w0, h2_tm, x, mod, fin_g, w_gate, w_up, w_down, layer,
              final_norm):
    bsz, seq, d = x.shape
    mv = MOE_MOVE
    half = N_EXPERTS // 2
    assert seq == half * mv
    n_slots = 2 * seq
    last = bsz - 1

    def scat_blk(r, e, *_):
        return (jnp.minimum(r, last) * half + jnp.clip(e - half, 0, half - 1), 0)

    def comb_blk(r, e, *_):
        return (jnp.where(r < 2, 0, (r - 2) * half + jnp.minimum(e, half - 1)), 0)

    def scat_row(r, e, *_):
        return (jnp.minimum(r, last), 0, 0)

    def comb_row(r, e, *_):
        return (jnp.clip(r - 2, 0, last), 0, 0)

    body = functools.partial(_moe_body, n_blocks=bsz, layer=layer, final_norm=final_norm)
    slot_rows = (n_slots + N_EXPERTS * (SLOT_ALIGN - 1) + MOE_TM) * ROWS_PER_TOKEN
    out = pl.pallas_call(
        body,
        grid_spec=pltpu.PrefetchScalarGridSpec(
            num_scalar_prefetch=2,
            grid=(bsz + 2, N_EXPERTS),
            in_specs=[
                pl.BlockSpec((1, 1, seq), scat_row, memory_space=pltpu.SMEM),
                pl.BlockSpec((1, 1, seq), scat_row, memory_space=pltpu.SMEM),
                pl.BlockSpec((1, 1, seq), comb_row, memory_space=pltpu.SMEM),
                pl.BlockSpec((1, 1, seq), comb_row, memory_space=pltpu.SMEM),
                pl.BlockSpec((1, 1, seq), comb_row, memory_space=pltpu.SMEM),
                pl.BlockSpec((mv * ROWS_PER_TOKEN, LANES), scat_blk),
                pl.BlockSpec((mv, d), comb_blk),
                pl.BlockSpec((1, 6, d), comb_row),
                pl.BlockSpec((1, d), lambda r, e, *_: (0, 0)),
                pl.BlockSpec(memory_space=pl.ANY),
                pl.BlockSpec(memory_space=pl.ANY),
                pl.BlockSpec(memory_space=pl.ANY),
            ],
            out_specs=pl.BlockSpec((mv, d), comb_blk),
            scratch_shapes=[
                pltpu.VMEM((slot_rows, LANES), jnp.float32),
                pltpu.VMEM((slot_rows, LANES), jnp.float32),
                pltpu.VMEM((mv * ROWS_PER_TOKEN, LANES), jnp.float32),
                pltpu.VMEM((MOE_WRING, d, D_EXPERT), jnp.bfloat16),
                pltpu.VMEM((MOE_WRING, d, D_EXPERT), jnp.bfloat16),
                pltpu.VMEM((MOE_WRING, D_EXPERT, d), jnp.bfloat16),
                pltpu.SemaphoreType.DMA((3, MOE_WRING)),
            ],
        ),
        out_shape=jax.ShapeDtypeStruct((bsz * seq, d), jnp.float32),
        compiler_params=pltpu.CompilerParams(
            dimension_semantics=("arbitrary", "arbitrary"),
            vmem_limit_bytes=MOE_VMEM_LIMIT),
        name="moe",
    )(off, cnt, pos0, pos1, pos0, pos1, w0, h2_tm, x.reshape(bsz * seq, d), mod, fin_g,
      w_gate, w_up, w_down)
    return out.reshape(bsz, seq, d)


def _moe_layer(x, h2_tm, logits, rbias, mod, fin_g, w_gate, w_up, w_down, layer, final_norm):
    pos0, pos1, w0, off, cnt = _route_call(logits, rbias)
    return _moe_call(off[:, :, 0], cnt[:, :, 0], pos0, pos1, w0, h2_tm, x, mod, fin_g,
                     w_gate, w_up, w_down, layer, final_norm)


def kernel(x, c, gla_w_in, gla_w_gate_up, gla_b_gate, gla_norm_g, gla_w_out, lru_w_in, lru_conv_w, lru_conv_b, lru_w_r, lru_b_r, lru_w_i, lru_b_i, lru_lambda, lru_w_out, router_w, router_bias, moe_w_gate, moe_w_up, moe_w_down, norm_mix_g, norm_ffn_g, ada_w, ada_b, final_norm_g):
    bf = jnp.bfloat16
    depth = ada_w.shape[0]
    bsz = x.shape[0]
    d = D_MODEL
    mod_all = _ada_call(c, ada_w, ada_b).reshape(depth, bsz, 6, d)

    rw_t = router_w.T
    rw_hi = rw_t.astype(bf)
    rw_lo = (rw_t - rw_hi.astype(jnp.float32)).astype(bf)
    rw_cat = jnp.concatenate([rw_hi, rw_lo], axis=0)
    rbias = router_bias.reshape(N_EXPERTS, 1)
    fin_g = final_norm_g.reshape(1, d)
    expert_w = None

    for i in range(depth):
        j = i // 2
        mod = mod_all[i]
        gmix = norm_mix_g[i].reshape(1, d)
        gffn = norm_ffn_g[i].reshape(1, d)
        if i % 2 == 0:
            w_in = jnp.pad(gla_w_in[j], ((0, 0), (0, GLA_RANK_PAD - GLA_GATE_RANK))).astype(bf)
            w_gu = jnp.pad(gla_w_gate_up[j], ((0, GLA_RANK_PAD - GLA_GATE_RANK), (0, 0))).astype(bf)
            cast_src = () if expert_w is not None else tuple(
                w.reshape(-1, w.shape[-1]) for w in (moe_w_gate, moe_w_up, moe_w_down))
            (x, h2_tm, logits), cast = _gla_layer_call(
                x, mod, gmix, gffn, w_in, w_gu, gla_b_gate[j].reshape(1, GLA_QK),
                gla_norm_g[j].reshape(1, GLA_DV), gla_w_out[j].astype(bf),
                rw_cat, rw_hi, cast_src)
            if cast:
                expert_w = (cast[0].reshape(depth * N_EXPERTS, d, D_EXPERT),
                            cast[1].reshape(depth * N_EXPERTS, d, D_EXPERT),
                            cast[2].reshape(depth * N_EXPERTS, D_EXPERT, d))
        else:
            x, h2_tm, logits = _lru_layer_call(
                x, mod, gmix, gffn, lru_w_in[j].astype(bf), lru_conv_w[j],
                lru_conv_b[j].reshape(1, d), lru_w_r[j].astype(bf),
                lru_b_r[j].reshape(1, d), lru_w_i[j].astype(bf), lru_b_i[j].reshape(1, d),
                lru_lambda[j].reshape(1, d), lru_w_out[j].astype(bf), rw_cat, rw_hi)
        x = _moe_layer(x, h2_tm, logits, rbias, mod, fin_g, *expert_w,
                       layer=i, final_norm=(i == depth - 1))
    return x
```
